```python
import math
import jax, jax.numpy as jnp
from jax import lax
import numpy as np

D_MODEL = 2048
BATCH = 4
SEQ = 2048
DEPTH = 2
DEC_BATCH = 128
DEC_SEQ = 1
PAST_LEN = 16384
PAGE_SIZE = 128

D_FF = 5632
EPS = 1e-6
CHUNK = 64
N_BRANCH = 3
M_WIDTH = D_MODEL // 2
M_HEADDIM = 64
M_HEADS = M_WIDTH // M_HEADDIM
M_GROUPS = 2
M_STATE = 128
M_CONV = 4
M_CONV_DIM = M_WIDTH + 2 * M_GROUPS * M_STATE
H_WIDTH = D_MODEL // 2
H_EXPAND = 128
H_HEADS = H_WIDTH // H_EXPAND
G_WIDTH = D_MODEL // 2
G_KWIDTH = G_WIDTH // 2
G_HEADS = 4
G_DK = G_KWIDTH // G_HEADS
G_DV = G_WIDTH // G_HEADS
G_RANK = 16
G_NORMALIZER = 16.0
IN_SPLIT_WIDTHS = (M_WIDTH, M_CONV_DIM, M_HEADS,
                   H_WIDTH, H_WIDTH, H_WIDTH, H_WIDTH,
                   G_KWIDTH, G_KWIDTH, G_WIDTH, G_WIDTH, G_RANK,
                   N_BRANCH * D_MODEL)
IN_COLS = (M_WIDTH + M_CONV_DIM + M_HEADS + 4 * H_WIDTH + 2 * G_KWIDTH + 2 * G_WIDTH + G_RANK
           + N_BRANCH * D_MODEL)

kernel_name = 'hybrid_ssd_hgrn2_gla_macaron_step'

F32 = jnp.float32
F32_TINY = float(np.finfo(np.float32).tiny)


def _rmsnorm(x, w):
    x32 = x.astype(F32)
    y = x32 * lax.rsqrt(jnp.mean(x32 * x32, axis=-1, keepdims=True) + EPS)
    return (y * w.astype(F32)).astype(x.dtype)


def _swiglu(h, w_gate_up, w_down):
    g, u = jnp.split(h @ w_gate_up, 2, axis=-1)
    return (jax.nn.silu(g) * u) @ w_down


def _to_chunks(a, c):
    b, L = a.shape[:2]
    n = -(-L // c)
    a = jnp.pad(a, [(0, 0), (0, n * c - L)] + [(0, 0)] * (a.ndim - 2))
    a = a.reshape((b, n, c) + a.shape[2:])
    return jnp.moveaxis(a, 1, 0)


def _from_chunks(a, L):
    a = jnp.moveaxis(a, 0, 1)
    a = a.reshape((a.shape[0], -1) + a.shape[3:])
    return a[:, :L]


def _masked_decay(diff, mask):
    return jnp.where(mask, jnp.exp(jnp.where(mask, diff, 0.0)), 0.0)


def _ssd_scan(x, dt, a_head, bm, cm, s0):
    L = x.shape[1]
    c = min(CHUNK, L)
    mask = jnp.tril(jnp.ones((c, c), dtype=bool))[None, :, :, None, None]

    def step(s, inp):
        xc, dtc, bc, cc = inp
        cum = jnp.cumsum(dtc * a_head, axis=1)
        diff = cum[:, :, None] - cum[:, None, :]
        decay = _masked_decay(diff, mask)
        cb = jnp.einsum('btgn,bsgn->btsg', cc, bc)
        y = jnp.einsum('btsg,btsgj,bsgj,bsgjp->btgjp', cb, decay, dtc, xc)
        y = y + jnp.einsum('btgn,bgjpn,btgj->btgjp', cc, s, jnp.exp(cum))
        w = dtc * jnp.exp(cum[:, -1:] - cum)
        s = jnp.exp(cum[:, -1])[..., None, None] * s + jnp.einsum('bsgn,bsgj,bsgjp->bgjpn', bc, w, xc)
        return s, y

    xs = (_to_chunks(x.astype(F32), c), _to_chunks(dt.astype(F32), c),
          _to_chunks(bm.astype(F32), c), _to_chunks(cm.astype(F32), c))
    s, y = lax.scan(step, s0.astype(F32), xs)
    return _from_chunks(y, L), s


def _gla_scan(q, k, v, log_a, s0):
    L = q.shape[1]
    c = min(CHUNK, L)
    mask = jnp.tril(jnp.ones((c, c), dtype=bool))[None, :, :, None, None]

    def step(s, inp):
        qc, kc, vc, gc = inp
        cum = jnp.cumsum(gc, axis=1)
        diff = cum[:, :, None] - cum[:, None, :]
        decay = _masked_decay(diff, mask)
        att = jnp.einsum('bthd,btshd,bshd->bhts', qc, decay, kc)
        o = jnp.einsum('bhts,bshv->bthv', att, vc) + jnp.einsum('bthd,bhdv->bthv', qc * jnp.exp(cum), s)
        s = jnp.exp(cum[:, -1])[..., None] * s + jnp.einsum('bshd,bshv->bhdv', kc * jnp.exp(cum[:, -1:] - cum), vc)
        return s, o

    xs = (_to_chunks(q.astype(F32), c), _to_chunks(k.astype(F32), c),
          _to_chunks(v.astype(F32), c), _to_chunks(log_a.astype(F32), c))
    s, o = lax.scan(step, s0.astype(F32), xs)
    return _from_chunks(o, L), s


def _mamba_branch(z, xbc, dt_raw, conv_buf, s0, conv_w, conv_b, dt_bias, a_log, d_skip, norm_w):
    b, L = xbc.shape[:2]
    j = M_HEADS // M_GROUPS
    full = jnp.concatenate([conv_buf.astype(xbc.dtype), xbc], axis=1)
    new_buf = full[:, -(M_CONV - 1):]
    conv = lax.conv_general_dilated(full, conv_w[:, None, :].astype(full.dtype), window_strides=(1,),
                                    padding='VALID', dimension_numbers=('NWC', 'WIO', 'NWC'),
                                    feature_group_count=M_CONV_DIM)
    xbc = jax.nn.silu(conv + conv_b)
    xs, bm, cm = jnp.split(xbc, [M_WIDTH, M_WIDTH + M_GROUPS * M_STATE], axis=-1)
    xs = xs.reshape(b, L, M_GROUPS, j, M_HEADDIM)
    bm = bm.reshape(b, L, M_GROUPS, M_STATE)
    cm = cm.reshape(b, L, M_GROUPS, M_STATE)
    dt = jax.nn.softplus((dt_raw + dt_bias).astype(F32)).reshape(b, L, M_GROUPS, j)
    a = -jnp.exp(a_log.astype(F32)).reshape(M_GROUPS, j)
    y, s = _ssd_scan(xs, dt, a, bm, cm, s0.reshape(b, M_GROUPS, j, M_HEADDIM, M_STATE))
    y = y + d_skip.astype(F32).reshape(M_GROUPS, j)[:, :, None] * xs.astype(F32)
    y = y.reshape(b, L, M_WIDTH) * jax.nn.silu(z.astype(F32))
    gs = M_WIDTH // M_GROUPS
    y = _rmsnorm(y.reshape(b, L, M_GROUPS, gs), norm_w.reshape(M_GROUPS, gs)).reshape(b, L, M_WIDTH)
    return y.astype(z.dtype), new_buf, s.reshape(b, M_HEADS, M_HEADDIM, M_STATE)


def _hgrn2_branch(hq, hf, hi, hg, s0, lb, norm_w):
    b, L = hq.shape[:2]
    shp = (b, L, H_HEADS, H_EXPAND)
    q = jax.nn.silu(hq.astype(F32)).reshape(shp) * H_EXPAND ** -0.5
    zf = hf.astype(F32)
    lb = lb.astype(F32)
    f = lb + (1.0 - lb) * jax.nn.sigmoid(zf)
    log_f = jnp.log(jnp.maximum(f, F32_TINY))
    k = (1.0 - lb) * jax.nn.sigmoid(-zf)
    o, s = _gla_scan(q, k.reshape(shp), hi.astype(F32).reshape(shp), log_f.reshape(shp), s0)
    o = _rmsnorm(o, norm_w) * jax.nn.silu(hg.astype(F32)).reshape(shp)
    return o.reshape(b, L, H_WIDTH).astype(hq.dtype), s


def _gla_branch(gq, gk, gv, gg, ga, s0, w_decay, b_decay, norm_w):
    b, L = gq.shape[:2]
    q = gq.astype(F32).reshape(b, L, G_HEADS, G_DK) * G_DK ** -0.5
    k = gk.astype(F32).reshape(b, L, G_HEADS, G_DK)
    v = gv.astype(F32).reshape(b, L, G_HEADS, G_DV)
    log_a = jax.nn.log_sigmoid((ga @ w_decay + b_decay).astype(F32)) / G_NORMALIZER
    o, s = _gla_scan(q, k, v, log_a.reshape(b, L, G_HEADS, G_DK), s0)
    o = _rmsnorm(o, norm_w) * jax.nn.silu(gg.astype(F32)).reshape(b, L, G_HEADS, G_DV)
    return o.reshape(b, L, G_WIDTH).astype(gq.dtype), s


def _layer(x, conv_buf, s_ssm, s_hgrn, s_gla, lb, p):
    b, L = x.shape[:2]
    x = x + 0.5 * _swiglu(_rmsnorm(x, p['ffn1_norm']), p['ffn1_w_gate_up'], p['ffn1_w_down'])
    h = _rmsnorm(x, p['mix_norm'])
    offs = np.cumsum(IN_SPLIT_WIDTHS)[:-1].tolist()
    (m_z, m_xbc, m_dt, h_q, h_f, h_i, h_g, g_q, g_k, g_v, g_g, g_a, gate) = jnp.split(h @ p['w_in'], offs, axis=-1)
    y_m, new_conv, new_ssm = _mamba_branch(m_z, m_xbc, m_dt, conv_buf, s_ssm, p['conv_w'], p['conv_b'],
                                           p['dt_bias'], p['a_log'], p['d_skip'], p['mamba_norm'])
    y_h, new_hgrn = _hgrn2_branch(h_q, h_f, h_i, h_g, s_hgrn, lb, p['hgrn_norm'])
    y_g, new_gla = _gla_branch(g_q, g_k, g_v, g_g, g_a, s_gla, p['gla_w_decay'], p['gla_b_decay'], p['gla_norm'])
    gates = jax.nn.sigmoid(gate.astype(F32)).reshape(b, L, N_BRANCH, D_MODEL).astype(x.dtype)
    merged = (gates[:, :, 0] * (y_m @ p['w_branch_mamba'])
              + gates[:, :, 1] * (y_h @ p['w_branch_hgrn'])
              + gates[:, :, 2] * (y_g @ p['w_branch_gla']))
    x = x + merged @ p['w_out']
    x = x + 0.5 * _swiglu(_rmsnorm(x, p['ffn2_norm']), p['ffn2_w_gate_up'], p['ffn2_w_down'])
    return x, new_conv, new_ssm, new_hgrn, new_gla


def setup_inputs(seed: int = 0) -> dict:
    key = jax.random.key(seed)
    ks = jax.random.split(key, 32)

    def nrm(i, shape, scale):
        return jax.random.normal(ks[i], shape, F32) * scale

    def gain(i, shape):
        return 1.0 + 0.02 * jax.random.normal(ks[i], shape, F32)

    dt0 = jnp.exp(jax.random.uniform(ks[12], (DEPTH, M_HEADS), F32)
                  * (math.log(0.1) - math.log(0.001)) + math.log(0.001))
    dt_bias = dt0 + jnp.log(-jnp.expm1(-dt0))
    a_log = jnp.log(jax.random.uniform(ks[13], (DEPTH, M_HEADS), F32, 1.0, 16.0))
    return {
        'x_prompt': nrm(0, (BATCH, SEQ, D_MODEL), 1.0),
        'x_sample': nrm(1, (DEC_BATCH, DEC_SEQ, D_MODEL), 1.0),
        'state_conv': nrm(2, (DEPTH, DEC_BATCH, M_CONV - 1, M_CONV_DIM), 1.0),
        'state_ssm': nrm(3, (DEPTH, DEC_BATCH, M_HEADS, M_HEADDIM, M_STATE), 0.5),
        'state_hgrn': nrm(4, (DEPTH, DEC_BATCH, H_HEADS, H_EXPAND, H_EXPAND), 1.0),
        'state_gla': nrm(5, (DEPTH, DEC_BATCH, G_HEADS, G_DK, G_DV), 1.0),
        'ffn1_norm': gain(6, (DEPTH, D_MODEL)),
        'ffn1_w_gate_up': nrm(7, (DEPTH, D_MODEL, 2 * D_FF), D_MODEL ** -0.5),
        'ffn1_w_down': nrm(8, (DEPTH, D_FF, D_MODEL), D_FF ** -0.5),
        'mix_norm': gain(9, (DEPTH, D_MODEL)),
        'w_in': nrm(10, (DEPTH, D_MODEL, IN_COLS), D_MODEL ** -0.5),
        'conv_w': nrm(11, (DEPTH, M_CONV, M_CONV_DIM), M_CONV ** -0.5),
        'conv_b': nrm(14, (DEPTH, M_CONV_DIM), 0.02),
        'dt_bias': dt_bias,
        'a_log': a_log,
        'd_skip': gain(15, (DEPTH, M_HEADS)),
        'mamba_norm': gain(16, (DEPTH, M_WIDTH)),
        'hgrn_lb_logits': nrm(17, (DEPTH, H_WIDTH), 1.0),
        'hgrn_norm': gain(18, (DEPTH, H_EXPAND)),
        'gla_w_decay': nrm(19, (DEPTH, G_RANK, G_KWIDTH), G_RANK ** -0.5),
        'gla_b_decay': nrm(20, (DEPTH, G_KWIDTH), 0.02),
        'gla_norm': gain(21, (DEPTH, G_DV)),
        'w_branch_mamba': nrm(22, (DEPTH, M_WIDTH, D_MODEL), M_WIDTH ** -0.5),
        'w_branch_hgrn': nrm(23, (DEPTH, H_WIDTH, D_MODEL), H_WIDTH ** -0.5),
        'w_branch_gla': nrm(24, (DEPTH, G_WIDTH, D_MODEL), G_WIDTH ** -0.5),
        'w_out': nrm(25, (DEPTH, D_MODEL, D_MODEL), D_MODEL ** -0.5),
        'ffn2_norm': gain(26, (DEPTH, D_MODEL)),
        'ffn2_w_gate_up': nrm(27, (DEPTH, D_MODEL, 2 * D_FF), D_MODEL ** -0.5),
        'ffn2_w_down': nrm(28, (DEPTH, D_FF, D_MODEL), D_FF ** -0.5),
        'final_norm': gain(29, (D_MODEL,)),
    }


def reference(x_prompt, x_sample, state_conv, state_ssm, state_hgrn, state_gla,
              ffn1_norm, ffn1_w_gate_up, ffn1_w_down, mix_norm, w_in, conv_w, conv_b,
              dt_bias, a_log, d_skip, mamba_norm, hgrn_lb_logits, hgrn_norm,
              gla_w_decay, gla_b_decay, gla_norm, w_branch_mamba, w_branch_hgrn, w_branch_gla,
              w_out, ffn2_norm, ffn2_w_gate_up, ffn2_w_down, final_norm):
    sm = jax.nn.softmax(hgrn_lb_logits.astype(F32), axis=0)
    lower_bounds = jnp.cumsum(sm, axis=0) - sm[0]

    bp = x_prompt.shape[0]
    sdt = state_ssm.dtype
    xp, xs = x_prompt, x_sample
    pc, ps, ph, pg = [], [], [], []
    sc, ss, sh, sg = [], [], [], []
    for l in range(DEPTH):
        p = {'ffn1_norm': ffn1_norm[l], 'ffn1_w_gate_up': ffn1_w_gate_up[l], 'ffn1_w_down': ffn1_w_down[l],
             'mix_norm': mix_norm[l], 'w_in': w_in[l], 'conv_w': conv_w[l], 'conv_b': conv_b[l],
             'dt_bias': dt_bias[l], 'a_log': a_log[l], 'd_skip': d_skip[l], 'mamba_norm': mamba_norm[l],
             'hgrn_norm': hgrn_norm[l], 'gla_w_decay': gla_w_decay[l], 'gla_b_decay': gla_b_decay[l],
             'gla_norm': gla_norm[l], 'w_branch_mamba': w_branch_mamba[l], 'w_branch_hgrn': w_branch_hgrn[l],
             'w_branch_gla': w_branch_gla[l], 'w_out': w_out[l], 'ffn2_norm': ffn2_norm[l],
             'ffn2_w_gate_up': ffn2_w_gate_up[l], 'ffn2_w_down': ffn2_w_down[l]}
        lb = lower_bounds[l]
        xp, c1, s1, h1, g1 = _layer(
            xp,
            jnp.zeros((bp, M_CONV - 1, M_CONV_DIM), x_prompt.dtype),
            jnp.zeros((bp, M_HEADS, M_HEADDIM, M_STATE), F32),
            jnp.zeros((bp, H_HEADS, H_EXPAND, H_EXPAND), F32),
            jnp.zeros((bp, G_HEADS, G_DK, G_DV), F32),
            lb, p)
        xs, c2, s2, h2, g2 = _layer(xs, state_conv[l], state_ssm[l], state_hgrn[l], state_gla[l], lb, p)
        pc.append(c1.astype(state_conv.dtype)); ps.append(s1.astype(sdt))
        ph.append(h1.astype(state_hgrn.dtype)); pg.append(g1.astype(state_gla.dtype))
        sc.append(c2.astype(state_conv.dtype)); ss.append(s2.astype(sdt))
        sh.append(h2.astype(state_hgrn.dtype)); sg.append(g2.astype(state_gla.dtype))
    y_prompt = _rmsnorm(xp, final_norm)
    y_sample = _rmsnorm(xs, final_norm)
    return (y_prompt, y_sample,
            jnp.stack(pc), jnp.stack(ps), jnp.stack(ph), jnp.stack(pg),
            jnp.stack(sc), jnp.stack(ss), jnp.stack(sh), jnp.stack(sg))
```

```python
import functools

import jax
import jax.numpy as jnp
import numpy as np
from jax import lax
from jax.experimental import pallas as pl
from jax.experimental.pallas import tpu as pltpu

F32 = jnp.float32
BF16 = jnp.bfloat16
EPS = 1e-6
F32_TINY = float(np.finfo(np.float32).tiny)

CHUNK = 64
LANES = 128
M_HEADDIM = 64
M_STATE = 128
M_GROUPS = 2
M_CONV = 4
G_RANK = 16
G_NORMALIZER = 16.0
N_BRANCH = 3
N_LEVELS = 6
VMEM_LIMIT = 56 * 1024 * 1024


def _cparams(n_axes):
    return pltpu.CompilerParams(dimension_semantics=("arbitrary",) * n_axes,
                                vmem_limit_bytes=VMEM_LIMIT)


def _dot(a, b):
    return jnp.dot(a, b, preferred_element_type=F32)


def _dot_nt(a, b):
    return lax.dot_general(a, b, (((1,), (1,)), ((), ())), preferred_element_type=F32)


def _dot_tn(a, b):
    return lax.dot_general(a, b, (((0,), (0,)), ((), ())), preferred_element_type=F32)


def _split3(x):
    hi = x.astype(BF16)
    r = x - hi.astype(F32)
    mid = r.astype(BF16)
    lo = (r - mid.astype(F32)).astype(BF16)
    return hi, mid, lo


def _dot_exact_lhs(p_bf16, x):
    hi, mid, lo = _split3(x)
    return _dot(p_bf16, hi) + _dot(p_bf16, mid) + _dot(p_bf16, lo)


def _dot_f32(a, b):
    ah, am, _ = _split3(a)
    bh, bm, _ = _split3(b)
    return _dot(ah, bh) + _dot(ah, bm) + _dot(am, bh)


def _sigmoid(x):
    return jax.nn.sigmoid(x)


def _silu(x):
    return x * _sigmoid(x)


def _softplus(x):
    return jnp.maximum(x, 0.0) + jnp.log1p(jnp.exp(-jnp.abs(x)))


def _log_sigmoid(x):
    return jnp.minimum(x, 0.0) - jnp.log1p(jnp.exp(-jnp.abs(x)))


def _rms(x, w):
    ms = jnp.mean(x * x, axis=-1, keepdims=True)
    return x * lax.rsqrt(ms + EPS) * w


def _pick_tm(t):
    best = 16
    for tm in range(16, min(t, 1100) + 1, 16):
        if t % tm == 0:
            best = tm
    return best


def _ffn_up_kernel(x_ref, nw_ref, wg_ref, wu_ref, o_ref, h_ref):
    @pl.when(pl.program_id(1) == 0)
    def _():
        h_ref[...] = _rms(x_ref[...], nw_ref[...]).astype(BF16)

    h = h_ref[...]
    g = _dot(h, wg_ref[...])
    u = _dot(h, wu_ref[...])
    o_ref[...] = (_silu(g) * u).astype(BF16)


def _ffn_up(x, nw, w_gu, tn=512):
    t, d = x.shape
    dff = w_gu.shape[1] // 2
    tm = _pick_tm(t)
    nj = dff // tn
    return pl.pallas_call(
        _ffn_up_kernel,
        grid=(t // tm, nj),
        in_specs=[pl.BlockSpec((tm, d), lambda i, j: (i, 0)),
                  pl.BlockSpec((1, d), lambda i, j: (0, 0)),
                  pl.BlockSpec((d, tn), lambda i, j: (0, j)),
                  pl.BlockSpec((d, tn), lambda i, j: (0, j + nj))],
        out_specs=pl.BlockSpec((tm, tn), lambda i, j: (i, j)),
        out_shape=jax.ShapeDtypeStruct((t, dff), BF16),
        scratch_shapes=[pltpu.VMEM((tm, d), BF16)],
        compiler_params=_cparams(2),
        name="ffn_up",
    )(x, nw.reshape(1, d), w_gu, w_gu)


def _resid_matmul_kernel(a_ref, w_ref, x_ref, o_ref, *, scale):
    o_ref[...] = x_ref[...] + scale * _dot(a_ref[...], w_ref[...])


def _resid_matmul(a, w, x, scale, tn=512, name="resid_matmul"):
    t, k = a.shape
    d = w.shape[1]
    tm = _pick_tm(t)
    return pl.pallas_call(
        functools.partial(_resid_matmul_kernel, scale=scale),
        grid=(t // tm, d // tn),
        in_specs=[pl.BlockSpec((tm, k), lambda i, j: (i, 0)),
                  pl.BlockSpec((k, tn), lambda i, j: (0, j)),
                  pl.BlockSpec((tm, tn), lambda i, j: (i, j))],
        out_specs=pl.BlockSpec((tm, tn), lambda i, j: (i, j)),
        out_shape=jax.ShapeDtypeStruct((t, d), F32),
        compiler_params=_cparams(2),
        name=name,
    )(a, w, x)


def _in_proj_kernel(x_ref, nw_ref, w_ref, o_ref, h_ref):
    @pl.when(pl.program_id(1) == 0)
    def _():
        h_ref[...] = _rms(x_ref[...], nw_ref[...]).astype(BF16)

    o_ref[...] = _dot(h_ref[...], w_ref[...])


def _in_proj(x, nw, w, tn=1024):
    t, d = x.shape
    n = w.shape[1]
    tm = _pick_tm(t)
    return pl.pallas_call(
        _in_proj_kernel,
        grid=(t // tm, n // tn),
        in_specs=[pl.BlockSpec((tm, d), lambda i, j: (i, 0)),
                  pl.BlockSpec((1, d), lambda i, j: (0, 0)),
                  pl.BlockSpec((d, tn), lambda i, j: (0, j))],
        out_specs=pl.BlockSpec((tm, tn), lambda i, j: (i, j)),
        out_shape=jax.ShapeDtypeStruct((t, n), F32),
        scratch_shapes=[pltpu.VMEM((tm, d), BF16)],
        compiler_params=_cparams(2),
        name="in_proj",
    )(x, nw.reshape(1, d), w)


def _merge_kernel(ym_ref, yh_ref, yg_ref, wm_ref, wh_ref, wg_ref, g0_ref, g1_ref, g2_ref, o_ref):
    acc = _sigmoid(g0_ref[...]) * _dot(ym_ref[...], wm_ref[...])
    acc = acc + _sigmoid(g1_ref[...]) * _dot(yh_ref[...], wh_ref[...])
    acc = acc + _sigmoid(g2_ref[...]) * _dot(yg_ref[...], wg_ref[...])
    o_ref[...] = acc.astype(BF16)


def _merge(ym, yh, yg, wm, wh, wg, proj, gate_off, tn=512):
    t, k = ym.shape
    d = wm.shape[1]
    tm = _pick_tm(t)
    gb = gate_off // tn
    nb = d // tn
    yspec = pl.BlockSpec((tm, k), lambda i, j: (i, 0))
    wspec = pl.BlockSpec((k, tn), lambda i, j: (0, j))

    def gspec(b):
        return pl.BlockSpec((tm, tn), lambda i, j: (i, gb + b * nb + j))

    return pl.pallas_call(
        _merge_kernel,
        grid=(t // tm, nb),
        in_specs=[yspec, yspec, yspec, wspec, wspec, wspec, gspec(0), gspec(1), gspec(2)],
        out_specs=pl.BlockSpec((tm, tn), lambda i, j: (i, j)),
        out_shape=jax.ShapeDtypeStruct((t, d), BF16),
        compiler_params=_cparams(2),
        name="merge",
    )(ym, yh, yg, wm, wh, wg, proj, proj, proj)


def _final_norm_kernel(x_ref, nw_ref, o_ref):
    o_ref[...] = _rms(x_ref[...], nw_ref[...]).reshape(o_ref.shape)


def _final_norm_prompt(x, nw, nseq, nchunks):
    d = x.shape[1]
    return pl.pallas_call(
        _final_norm_kernel,
        grid=(nchunks, nseq),
        in_specs=[pl.BlockSpec((CHUNK, d), lambda c, b: (c * nseq + b, 0)),
                  pl.BlockSpec((1, d), lambda c, b: (0, 0))],
        out_specs=pl.BlockSpec((None, None, CHUNK, d), lambda c, b: (b, c, 0, 0)),
        out_shape=jax.ShapeDtypeStruct((nseq, nchunks, CHUNK, d), F32),
        compiler_params=_cparams(2),
        name="final_norm_prompt",
    )(x, nw.reshape(1, d))


def _final_norm_sample(x, nw, row0, ns):
    d = x.shape[1]
    return pl.pallas_call(
        _final_norm_kernel,
        grid=(1,),
        in_specs=[pl.BlockSpec((ns, d), lambda i: (row0 // ns, 0)),
                  pl.BlockSpec((1, d), lambda i: (0, 0))],
        out_specs=pl.BlockSpec((ns, d), lambda i: (0, 0)),
        out_shape=jax.ShapeDtypeStruct((ns, d), F32),
        compiler_params=_cparams(1),
        name="final_norm_sample",
    )(x, nw.reshape(1, d))


def _scan_constants():
    c = CHUNK
    pm = np.zeros((8, c, c), np.float32)
    mk = np.zeros((7, c, c), np.float32)
    r = np.arange(c)
    for lvl in range(N_LEVELS):
        h = c >> (lvl + 1)
        for t in range(c):
            blk, pos = divmod(t, 2 * h)
            ridx = blk * 2 * h + h - 1
            if pos >= h:
                pm[lvl, t] = (r > ridx) & (r <= t)
                mk[lvl, t] = ((r // (2 * h)) == blk) & ((r % (2 * h)) < h)
            else:
                pm[lvl, t] = (r > t) & (r <= ridx)
    pm[6] = r[None, :] <= r[:, None]
    pm[7] = r[None, :] > r[:, None]
    mk[6] = np.eye(c)
    return jnp.asarray(pm.reshape(8 * c, c), BF16), jnp.asarray(mk, F32)


def _lower_bound(logits, layer):
    m = jnp.max(logits, axis=0, keepdims=True)
    e = jnp.exp(logits - m)
    sm = e / jnp.sum(e, axis=0, keepdims=True)
    lb = jnp.zeros_like(m)
    for i in range(1, layer + 1):
        lb = lb + sm[i:i + 1, :]
    return lb


def _gla_inputs(refs, rows, hgrn, layer):
    if hgrn:
        q_ref, f_ref, i_ref, lbl_ref = refs
        lb = _lower_bound(lbl_ref[...], layer)
        q = _silu(q_ref[rows, :]) * (LANES ** -0.5)
        zf = f_ref[rows, :]
        f = lb + (1.0 - lb) * _sigmoid(zf)
        gl = jnp.log(jnp.maximum(f, F32_TINY))
        k = (1.0 - lb) * _sigmoid(-zf)
        v = i_ref[rows, :]
    else:
        q_ref, k_ref, v_ref, a_ref, wd_ref, bd_ref = refs
        q = q_ref[rows, :] * (LANES ** -0.5)
        k = k_ref[rows, :]
        v = v_ref[rows, :]
        gl = _log_sigmoid(_dot_f32(a_ref[rows, :], wd_ref[...]) + bd_ref[...]) / G_NORMALIZER
    return q, k, v, gl


def _gla_prompt_kernel(*refs, hgrn, nseq, layer):
    n_in = 4 if hgrn else 6
    in_refs = refs[:n_in]
    g_ref, nw_ref, pm_ref, mk_ref, y_ref, so_ref, s_scr = refs[n_in:]
    c = pl.program_id(1)

    @pl.when(c == 0)
    def _():
        s_scr[...] = jnp.zeros_like(s_scr)

    pm = pm_ref[...]
    mk = mk_ref[...]
    nw = nw_ref[...]
    for b in range(nseq):
        rows = pl.ds(b * CHUNK, CHUNK)
        q, k, v, gl = _gla_inputs(in_refs, rows, hgrn, layer)
        e = jnp.exp(_dot_exact_lhs(pm, gl)).reshape(8, CHUNK, LANES)
        qs = (q[None] * e[:N_LEVELS]).astype(BF16)
        ks = (k[None] * e[:N_LEVELS]).astype(BF16)
        a = lax.dot_general(qs, ks, (((2,), (2,)), ((0,), (0,))), preferred_element_type=F32)
        att = jnp.sum(a * mk[:N_LEVELS], axis=0) + mk[N_LEVELS] * _dot_nt(q.astype(BF16), k.astype(BF16))
        vb = v.astype(BF16)
        s = s_scr[b]
        o = _dot(att.astype(BF16), vb) + _dot((q * e[6]).astype(BF16), s.astype(BF16))
        e_last = e[6].T[:, CHUNK - 1:CHUNK]
        s_scr[b] = e_last * s + _dot_tn((k * e[7]).astype(BF16), vb)
        y_ref[rows, :] = (_rms(o, nw) * _silu(g_ref[rows, :])).astype(BF16)

    @pl.when(c == pl.num_programs(1) - 1)
    def _():
        so_ref[...] = s_scr[...]


def _gla_prompt(proj, offs, extra, nw, consts, *, hgrn, nseq, nchunks, heads, dv, layer):
    rows = nseq * CHUNK
    pm, mk = consts

    def col(off, w):
        return pl.BlockSpec((rows, w), lambda h, c: (c, off // w + h))

    def full(a):
        nd = a.ndim
        return pl.BlockSpec(a.shape, lambda h, c: (0,) * nd)

    if hgrn:
        oq, of, oi, og = offs
        (lbl,) = extra
        in_specs = [col(oq, LANES), col(of, LANES), col(oi, dv),
                    pl.BlockSpec((lbl.shape[0], LANES), lambda h, c: (0, h))]
        args = [proj, proj, proj, lbl]
    else:
        oq, ok, ov, og, oa = offs
        wd, bd = extra
        in_specs = [col(oq, LANES), col(ok, LANES), col(ov, dv),
                    pl.BlockSpec((rows, LANES), lambda h, c: (c, oa // LANES)),
                    pl.BlockSpec((LANES, LANES), lambda h, c: (0, h)),
                    pl.BlockSpec((1, LANES), lambda h, c: (0, h))]
        args = [proj, proj, proj, proj, wd, bd]
    in_specs += [col(og, dv), full(nw), full(pm), full(mk)]
    args += [proj, nw, pm, mk]
    y, s = pl.pallas_call(
        functools.partial(_gla_prompt_kernel, hgrn=hgrn, nseq=nseq, layer=layer),
        grid=(heads, nchunks),
        in_specs=in_specs,
        out_specs=[pl.BlockSpec((rows, dv), lambda h, c: (c, h)),
                   pl.BlockSpec((nseq, None, LANES, dv), lambda h, c: (0, h, 0, 0))],
        out_shape=[jax.ShapeDtypeStruct((rows * nchunks, heads * dv), BF16),
                   jax.ShapeDtypeStruct((nseq, heads, LANES, dv), F32)],
        scratch_shapes=[pltpu.VMEM((nseq, LANES, dv), F32)],
        compiler_params=_cparams(2),
        name="hgrn_prompt" if hgrn else "gla_prompt",
    )(*args)
    return y, s


def _columns(x):
    nb = x.shape[0]
    if nb < LANES:
        x = jnp.concatenate([x, jnp.zeros((LANES - nb, x.shape[1]), x.dtype)], axis=0)
    return x.T


def _gla_sample_kernel(*refs, hgrn, nb, layer):
    n_in = 4 if hgrn else 6
    in_refs = refs[:n_in]
    g_ref, nw_ref, s_ref, y_ref, so_ref = refs[n_in:]
    q, k, v, gl = _gla_inputs(in_refs, slice(None), hgrn, layer)
    e_t = _columns(jnp.exp(gl))
    k_t = _columns(k)
    qb = q.astype(BF16)
    rowid = lax.broadcasted_iota(jnp.int32, v.shape, 0)
    o = jnp.zeros(v.shape, F32)
    for b in range(nb):
        s_new = e_t[:, b:b + 1] * s_ref[b] + k_t[:, b:b + 1] * v[b:b + 1, :]
        so_ref[b] = s_new
        o = jnp.where(rowid == b, _dot(qb, s_new.astype(BF16)), o)
    y_ref[...] = (_rms(o, nw_ref[...]) * _silu(g_ref[...])).astype(BF16)


def _gla_sample(proj, state, row0, offs, extra, nw, *, hgrn, heads, dv, layer, nb=16):
    ns = state.shape[0]
    rb = row0 // nb

    def col(off, w):
        return pl.BlockSpec((nb, w), lambda h, i: (rb + i, off // w + h))

    if hgrn:
        oq, of, oi, og = offs
        (lbl,) = extra
        in_specs = [col(oq, LANES), col(of, LANES), col(oi, dv),
                    pl.BlockSpec((lbl.shape[0], LANES), lambda h, i: (0, h))]
        args = [proj, proj, proj, lbl]
    else:
        oq, ok, ov, og, oa = offs
        wd, bd = extra
        in_specs = [col(oq, LANES), col(ok, LANES), col(ov, dv),
                    pl.BlockSpec((nb, LANES), lambda h, i: (rb + i, oa // LANES)),
                    pl.BlockSpec((LANES, LANES), lambda h, i: (0, h)),
                    pl.BlockSpec((1, LANES), lambda h, i: (0, h))]
        args = [proj, proj, proj, proj, wd, bd]
    sspec = pl.BlockSpec((nb, None, LANES, dv), lambda h, i: (i, h, 0, 0))
    in_specs += [col(og, dv), pl.BlockSpec(nw.shape, lambda h, i: (0, 0)), sspec]
    args += [proj, nw, state]
    y, s = pl.pallas_call(
        functools.partial(_gla_sample_kernel, hgrn=hgrn, nb=nb, layer=layer),
        grid=(heads, ns // nb),
        in_specs=in_specs,
        out_specs=[pl.BlockSpec((nb, dv), lambda h, i: (i, h)), sspec],
        out_shape=[jax.ShapeDtypeStruct((ns, heads * dv), BF16),
                   jax.ShapeDtypeStruct(state.shape, F32)],
        compiler_params=_cparams(2),
        name="hgrn_sample" if hgrn else "gla_sample",
    )(*args)
    return y, s


def _mamba_post(y, xs, z, dskip, nw):
    y = (y + dskip * xs) * _silu(z)
    gs = y.shape[1] // M_GROUPS
    outs = [_rms(y[:, g * gs:(g + 1) * gs], nw[:, g * gs:(g + 1) * gs]) for g in range(M_GROUPS)]
    return jnp.concatenate(outs, axis=1)


def _mamba_prompt_kernel(z_ref, x_ref, bc_ref, sm_ref, cwx_ref, cwb_ref, cbx_ref, cbb_ref,
                         dtb_ref, alog_ref, dsk_ref, nw_ref, tril_ref, mask_ref,
                         y_ref, cox_ref, cob_ref, so_ref, ex_scr, eb_scr, s_scr):
    c = pl.program_id(1)
    nc = pl.num_programs(1)
    tail = 8

    @pl.when(c == 0)
    def _():
        ex_scr[0:tail, :] = jnp.zeros((tail, ex_scr.shape[1]), F32)
        eb_scr[0:tail, :] = jnp.zeros((tail, eb_scr.shape[1]), F32)
        s_scr[...] = jnp.zeros_like(s_scr)

    ex_scr[tail:tail + CHUNK, :] = x_ref[...]
    eb_scr[tail:tail + CHUNK, :] = bc_ref[...]

    def conv(scr, cw_ref, cb_ref):
        acc = cb_ref[...]
        for w in range(M_CONV):
            sh = M_CONV - 1 - w
            acc = acc + cw_ref[w:w + 1, :] * scr[tail - sh:tail - sh + CHUNK, :]
        return _silu(acc)

    xs = conv(ex_scr, cwx_ref, cbx_ref)
    bcm = conv(eb_scr, cwb_ref, cbb_ref)

    @pl.when(c == nc - 1)
    def _():
        cox_ref[...] = ex_scr[tail + CHUNK - (M_CONV - 1):tail + CHUNK, :]
        cob_ref[...] = eb_scr[tail + CHUNK - (M_CONV - 1):tail + CHUNK, :]

    ex_scr[0:tail, :] = ex_scr[CHUNK:CHUNK + tail, :]
    eb_scr[0:tail, :] = eb_scr[CHUNK:CHUNK + tail, :]

    dt = _softplus(sm_ref[...] + dtb_ref[...])
    a = -jnp.exp(alog_ref[...])
    cum = _dot_exact_lhs(tril_ref[...], dt * a)
    cum_t = cum.T
    cl = cum[CHUNK - 1:CHUNK, :]
    mask = mask_ref[...] > 0.5
    gw = M_GROUPS * M_STATE
    lo = lax.broadcasted_iota(jnp.int32, (CHUNK, LANES), 1) < M_HEADDIM
    lo_r = lax.broadcasted_iota(jnp.int32, (LANES, LANES), 0) < M_HEADDIM
    n_pairs = xs.shape[1] // LANES
    per_group = n_pairs // M_GROUPS
    ys = []
    for j in range(n_pairs):
        g = j // per_group
        h0, h1 = 2 * j, 2 * j + 1
        bg = bcm[:, g * M_STATE:(g + 1) * M_STATE].astype(BF16)
        cg = bcm[:, gw + g * M_STATE:gw + (g + 1) * M_STATE].astype(BF16)
        cb = _dot_nt(cg, bg)

        def dec(h):
            diff = cum[:, h:h + 1] - cum_t[h:h + 1, :]
            return jnp.where(mask, jnp.exp(jnp.where(mask, diff, 0.0)), 0.0)

        x2 = xs[:, j * LANES:(j + 1) * LANES]
        dt2 = jnp.where(lo, dt[:, h0:h0 + 1], dt[:, h1:h1 + 1])
        xdt = (x2 * dt2).astype(BF16)
        y_in = jnp.where(lo, _dot((cb * dec(h0)).astype(BF16), xdt),
                         _dot((cb * dec(h1)).astype(BF16), xdt))
        ec2 = jnp.where(lo, jnp.exp(cum[:, h0:h0 + 1]), jnp.exp(cum[:, h1:h1 + 1]))
        s = s_scr[j]
        ys.append(y_in + _dot_nt(cg, s.astype(BF16)) * ec2)
        w2 = dt2 * jnp.where(lo, jnp.exp(cl[:, h0:h0 + 1] - cum[:, h0:h0 + 1]),
                             jnp.exp(cl[:, h1:h1 + 1] - cum[:, h1:h1 + 1]))
        el2 = jnp.where(lo_r, jnp.exp(cl[:, h0:h0 + 1]), jnp.exp(cl[:, h1:h1 + 1]))
        s_scr[j] = el2 * s + _dot_tn((x2 * w2).astype(BF16), bg)
    y = jnp.concatenate(ys, axis=1)
    y_ref[...] = _mamba_post(y, xs, z_ref[...], dsk_ref[...], nw_ref[...]).astype(BF16)

    @pl.when(c == nc - 1)
    def _():
        so_ref[...] = s_scr[...]


def _mamba_prompt(proj, offs, p, consts, *, nseq, nchunks):
    oz, ox, obc, osm = offs
    mw = p["dskip"].shape[1]
    bcw = p["cwb"].shape[1]
    n_pairs = mw // LANES

    def col(off, w):
        return pl.BlockSpec((CHUNK, w), lambda b, c: (c * nseq + b, off // w))

    def full(a):
        nd = a.ndim
        return pl.BlockSpec(a.shape, lambda b, c: (0,) * nd)

    small = [p["cwx"], p["cwb"], p["cbx"], p["cbb"], p["dtb"], p["alog"], p["dskip"], p["nw"],
             consts[0], consts[1]]
    y, cox, cob, s = pl.pallas_call(
        _mamba_prompt_kernel,
        grid=(nseq, nchunks),
        in_specs=[col(oz, mw), col(ox, mw), col(obc, bcw), col(osm, LANES)] + [full(a) for a in small],
        out_specs=[pl.BlockSpec((CHUNK, mw), lambda b, c: (c * nseq + b, 0)),
                   pl.BlockSpec((None, M_CONV - 1, mw), lambda b, c: (b, 0, 0)),
                   pl.BlockSpec((None, M_CONV - 1, bcw), lambda b, c: (b, 0, 0)),
                   pl.BlockSpec((None, n_pairs, LANES, M_STATE), lambda b, c: (b, 0, 0, 0))],
        out_shape=[jax.ShapeDtypeStruct((nseq * nchunks * CHUNK, mw), BF16),
                   jax.ShapeDtypeStruct((nseq, M_CONV - 1, mw), F32),
                   jax.ShapeDtypeStruct((nseq, M_CONV - 1, bcw), F32),
                   jax.ShapeDtypeStruct((nseq, n_pairs, LANES, M_STATE), F32)],
        scratch_shapes=[pltpu.VMEM((CHUNK + 8, mw), F32), pltpu.VMEM((CHUNK + 8, bcw), F32),
                        pltpu.VMEM((n_pairs, LANES, M_STATE), F32)],
        compiler_params=_cparams(2),
        name="mamba_prompt",
    )(proj, proj, proj, proj, *small)
    return y, jnp.concatenate([cox, cob], axis=-1), s


def _mamba_sample_prep_kernel(cs_ref, x_ref, bc_ref, sm_ref, cw_ref, cb_ref, dtb_ref, alog_ref, exp_ref,
                              co_ref, act_ref, dte_ref, ee_ref):
    cd = cw_ref.shape[1]
    new = jnp.concatenate([x_ref[...], bc_ref[...]], axis=1)
    acc = cb_ref[...] + cw_ref[M_CONV - 1:M_CONV, :] * new
    for w in range(M_CONV - 1):
        acc = acc + cw_ref[w:w + 1, :] * cs_ref[:, w * cd:(w + 1) * cd]
    act_ref[...] = _silu(acc)
    for w in range(1, M_CONV - 1):
        co_ref[:, (w - 1) * cd:w * cd] = cs_ref[:, w * cd:(w + 1) * cd]
    co_ref[:, (M_CONV - 2) * cd:(M_CONV - 1) * cd] = new
    dt = _softplus(sm_ref[...] + dtb_ref[...])
    a = -jnp.exp(alog_ref[...])
    dte_ref[...] = _dot_exact_lhs_rhs(dt, exp_ref[...])
    ee_ref[...] = jnp.exp(_dot_exact_lhs_rhs(dt * a, exp_ref[...]))


def _dot_exact_lhs_rhs(x, p_bf16):
    hi, mid, lo = _split3(x)
    return _dot(hi, p_bf16) + _dot(mid, p_bf16) + _dot(lo, p_bf16)


def _mamba_sample_state_kernel(x_ref, b_ref, c_ref, dte_ref, ee_ref, s_ref, y_ref, so_ref, *, nb):
    xdt_t = _columns(x_ref[...] * dte_ref[...])
    e_t = _columns(ee_ref[...])
    bv = b_ref[...]
    cb = c_ref[...].astype(BF16)
    rowid = lax.broadcasted_iota(jnp.int32, (nb, LANES), 0)
    y = jnp.zeros((nb, LANES), F32)
    for b in range(nb):
        s_new = e_t[:, b:b + 1] * s_ref[b] + xdt_t[:, b:b + 1] * bv[b:b + 1, :]
        so_ref[b] = s_new
        y = jnp.where(rowid == b, _dot_nt(cb, s_new.astype(BF16)), y)
    y_ref[...] = y


def _mamba_sample_post_kernel(y_ref, x_ref, z_ref, dsk_ref, nw_ref, o_ref):
    o_ref[...] = _mamba_post(y_ref[...], x_ref[...], z_ref[...], dsk_ref[...], nw_ref[...]).astype(BF16)


def _mamba_sample(proj, conv_state, ssm_state, row0, offs, p, expand, nb=16):
    oz, ox, obc, osm = offs
    ns = conv_state.shape[0]
    mw = p["dskip"].shape[1]
    bcw = p["cwb"].shape[1]
    cd = mw + bcw
    n_pairs = mw // LANES
    per_group = n_pairs // M_GROUPS
    rb = row0 // ns
    cw = jnp.concatenate([p["cwx"], p["cwb"]], axis=1)
    cb = jnp.concatenate([p["cbx"], p["cbb"]], axis=1)

    def full1(a):
        nd = a.ndim
        return pl.BlockSpec(a.shape, lambda i: (0,) * nd)

    cs2 = conv_state.reshape(ns, (M_CONV - 1) * cd)
    small = [cw, cb, p["dtb"], p["alog"], expand]
    co, act, dte, ee = pl.pallas_call(
        _mamba_sample_prep_kernel,
        grid=(1,),
        in_specs=[full1(cs2),
                  pl.BlockSpec((ns, mw), lambda i: (rb, ox // mw)),
                  pl.BlockSpec((ns, bcw), lambda i: (rb, obc // bcw)),
                  pl.BlockSpec((ns, LANES), lambda i: (rb, osm // LANES))] + [full1(a) for a in small],
        out_specs=[pl.BlockSpec((ns, (M_CONV - 1) * cd), lambda i: (0, 0)),
                   pl.BlockSpec((ns, cd), lambda i: (0, 0)),
                   pl.BlockSpec((ns, mw), lambda i: (0, 0)),
                   pl.BlockSpec((ns, mw), lambda i: (0, 0))],
        out_shape=[jax.ShapeDtypeStruct((ns, (M_CONV - 1) * cd), F32),
                   jax.ShapeDtypeStruct((ns, cd), F32),
                   jax.ShapeDtypeStruct((ns, mw), F32),
                   jax.ShapeDtypeStruct((ns, mw), F32)],
        compiler_params=_cparams(1),
        name="mamba_sample_prep",
    )(cs2, proj, proj, proj, *small)

    s4 = ssm_state.reshape(ns, n_pairs, LANES, M_STATE)
    bblk = mw // LANES
    cblk = bblk + M_GROUPS * M_STATE // LANES
    sspec = pl.BlockSpec((nb, None, LANES, M_STATE), lambda j, i: (i, j, 0, 0))
    y, s_new = pl.pallas_call(
        functools.partial(_mamba_sample_state_kernel, nb=nb),
        grid=(n_pairs, ns // nb),
        in_specs=[pl.BlockSpec((nb, LANES), lambda j, i: (i, j)),
                  pl.BlockSpec((nb, LANES), lambda j, i: (i, bblk + j // per_group)),
                  pl.BlockSpec((nb, LANES), lambda j, i: (i, cblk + j // per_group)),
                  pl.BlockSpec((nb, LANES), lambda j, i: (i, j)),
                  pl.BlockSpec((nb, LANES), lambda j, i: (i, j)),
                  sspec],
        out_specs=[pl.BlockSpec((nb, LANES), lambda j, i: (i, j)), sspec],
        out_shape=[jax.ShapeDtypeStruct((ns, mw), F32),
                   jax.ShapeDtypeStruct(s4.shape, F32)],
        compiler_params=_cparams(2),
        name="mamba_sample_state",
    )(act, act, act, dte, ee, s4)

    ym = pl.pallas_call(
        _mamba_sample_post_kernel,
        grid=(1,),
        in_specs=[full1(y),
                  pl.BlockSpec((ns, mw), lambda i: (0, 0)),
                  pl.BlockSpec((ns, mw), lambda i: (rb, oz // mw)),
                  full1(p["dskip"]), full1(p["nw"])],
        out_specs=pl.BlockSpec((ns, mw), lambda i: (0, 0)),
        out_shape=jax.ShapeDtypeStruct((ns, mw), BF16),
        compiler_params=_cparams(1),
        name="mamba_sample_post",
    )(y, act, proj, p["dskip"], p["nw"])
    return ym, co.reshape(ns, M_CONV - 1, cd), s_new.reshape(ssm_state.shape)


def _pad_lanes(v, n=LANES):
    v = v.reshape(1, -1)
    return jnp.pad(v, ((0, 0), (0, n - v.shape[1])))


def kernel(x_prompt, x_sample, state_conv, state_ssm, state_hgrn, state_gla, ffn1_norm, ffn1_w_gate_up, ffn1_w_down, mix_norm, w_in, conv_w, conv_b, dt_bias, a_log, d_skip, mamba_norm, hgrn_lb_logits, hgrn_norm, gla_w_decay, gla_b_decay, gla_norm, w_branch_mamba, w_branch_hgrn, w_branch_gla, w_out, ffn2_norm, ffn2_w_gate_up, ffn2_w_down, final_norm):
    nseq, seq, d = x_prompt.shape
    ns = x_sample.shape[0]
    depth = w_in.shape[0]
    nchunks = seq // CHUNK
    n_prompt = nseq * seq
    mw = w_branch_mamba.shape[1]
    hw = w_branch_hgrn.shape[1]
    gw = w_branch_gla.shape[1]
    gk = gla_w_decay.shape[2]
    m_heads = dt_bias.shape[1]
    h_heads = state_hgrn.shape[2]
    g_heads = state_gla.shape[2]
    g_dv = state_gla.shape[4]
    bcw = 2 * M_GROUPS * M_STATE
    assert seq % CHUNK == 0 and n_prompt % ns == 0 and ns % 16 == 0
    assert m_heads <= G_RANK + m_heads <= LANES and gk // g_heads == LANES and hw // h_heads == LANES

    widths = (mw, mw + bcw, m_heads, hw, hw, hw, hw, gk, gk, gw, gw, G_RANK, N_BRANCH * d)
    src = np.concatenate([[0], np.cumsum(widths)])
    order = (0, 1, 3, 4, 5, 6, 7, 8, 9, 10, 12, 2, 11)
    off = {}
    pos = 0
    for i in order:
        off[i] = pos
        pos += widths[i]
    n_used = pos
    n_cols = -(-n_used // 1024) * 1024
    oz, ox, obc = off[0], off[1], off[1] + mw
    osm = off[2]
    assert off[11] == osm + m_heads and osm % LANES == 0

    consts = _scan_constants()
    tril = consts[0][6 * CHUNK:7 * CHUNK]
    mconsts = (tril, jnp.asarray(np.tril(np.ones((CHUNK, CHUNK), np.float32))))
    expand_np = np.zeros((LANES, mw), np.float32)
    for h in range(m_heads):
        expand_np[h, h * M_HEADDIM:(h + 1) * M_HEADDIM] = 1.0
    expand = jnp.asarray(expand_np, BF16)

    xp = x_prompt.reshape(nseq, nchunks, CHUNK, d).transpose(1, 0, 2, 3).reshape(n_prompt, d)
    x = jnp.concatenate([xp, x_sample.reshape(ns, d)], axis=0)

    outs = {k: [] for k in ("pc", "ps", "ph", "pg", "sc", "ss", "sh", "sg")}
    for l in range(depth):
        w_perm = jnp.concatenate(
            [w_in[l][:, src[i]:src[i + 1]].astype(BF16) for i in order]
            + [jnp.zeros((d, n_cols - n_used), BF16)], axis=1)
        mp = {
            "cwx": conv_w[l][:, :mw], "cwb": conv_w[l][:, mw:],
            "cbx": conv_b[l][:mw].reshape(1, mw), "cbb": conv_b[l][mw:].reshape(1, bcw),
            "dtb": _pad_lanes(dt_bias[l]), "alog": _pad_lanes(a_log[l]),
            "dskip": jnp.repeat(d_skip[l], M_HEADDIM).reshape(1, mw),
            "nw": mamba_norm[l].reshape(1, mw),
        }
        wd = jnp.zeros((LANES, gk), F32).at[m_heads:m_heads + G_RANK].set(gla_w_decay[l])
        bd = gla_b_decay[l].reshape(1, gk)
        hnw = hgrn_norm[l].reshape(1, LANES)
        gnw = gla_norm[l].reshape(1, g_dv)

        act = _ffn_up(x, ffn1_norm[l], ffn1_w_gate_up[l].astype(BF16))
        x = _resid_matmul(act, ffn1_w_down[l].astype(BF16), x, 0.5, name="ffn_down")
        proj = _in_proj(x, mix_norm[l], w_perm)

        moffs = (oz, ox, obc, osm)
        hoffs = (off[3], off[4], off[5], off[6])
        goffs = (off[7], off[8], off[9], off[10], osm)
        ym_p, c1, s1 = _mamba_prompt(proj, moffs, mp, mconsts, nseq=nseq, nchunks=nchunks)
        yh_p, h1 = _gla_prompt(proj, hoffs, (hgrn_lb_logits,), hnw, consts, hgrn=True, nseq=nseq,
                               nchunks=nchunks, heads=h_heads, dv=LANES, layer=l)
        yg_p, g1 = _gla_prompt(proj, goffs, (wd, bd), gnw, consts, hgrn=False, nseq=nseq,
                               nchunks=nchunks, heads=g_heads, dv=g_dv, layer=l)
        ym_s, c2, s2 = _mamba_sample(proj, state_conv[l], state_ssm[l], n_prompt, moffs, mp, expand)
        yh_s, h2 = _gla_sample(proj, state_hgrn[l], n_prompt, hoffs, (hgrn_lb_logits,), hnw,
                               hgrn=True, heads=h_heads, dv=LANES, layer=l)
        yg_s, g2 = _gla_sample(proj, state_gla[l], n_prompt, goffs, (wd, bd), gnw,
                               hgrn=False, heads=g_heads, dv=g_dv, layer=l)
        ym = jnp.concatenate([ym_p, ym_s], axis=0)
        yh = jnp.concatenate([yh_p, yh_s], axis=0)
        yg = jnp.concatenate([yg_p, yg_s], axis=0)
        merged = _merge(ym, yh, yg, w_branch_mamba[l].astype(BF16), w_branch_hgrn[l].astype(BF16),
                        w_branch_gla[l].astype(BF16), proj, off[12])
        x = _resid_matmul(merged, w_out[l].astype(BF16), x, 1.0, name="out_proj")

        act = _ffn_up(x, ffn2_norm[l], ffn2_w_gate_up[l].astype(BF16))
        x = _resid_matmul(act, ffn2_w_down[l].astype(BF16), x, 0.5, name="ffn_down")

        outs["pc"].append(c1)
        outs["ps"].append(s1.reshape(nseq, m_heads, M_HEADDIM, M_STATE))
        outs["ph"].append(h1)
        outs["pg"].append(g1)
        outs["sc"].append(c2)
        outs["ss"].append(s2)
        outs["sh"].append(h2)
        outs["sg"].append(g2)

    y_prompt = _final_norm_prompt(x, final_norm, nseq, nchunks).reshape(nseq, seq, d)
    y_sample = _final_norm_sample(x, final_norm, n_prompt, ns).reshape(ns, 1, d)
    return (y_prompt, y_sample) + tuple(jnp.stack(outs[k]) for k in ("pc", "ps", "ph", "pg", "sc", "ss", "sh", "sg"))
```

```python
import functools

import jax
import jax.numpy as jnp
import numpy as np
from jax import lax
from jax.experimental import pallas as pl
from jax.experimental.pallas import tpu as pltpu

F32 = jnp.float32
BF16 = jnp.bfloat16
EPS = 1e-6
F32_TINY = float(np.finfo(np.float32).tiny)

CHUNK = 256
LANES = 128
SUBLANES = 8
M_HEADDIM = 64
M_STATE = 128
M_GROUPS = 2
M_CONV = 4
G_RANK = 16
G_NORMALIZER = 16.0
N_BRANCH = 3
N_LEVELS = int(np.log2(CHUNK))
VMEM_LIMIT = 56 * 1024 * 1024


def _cparams(n_axes):
    return pltpu.CompilerParams(dimension_semantics=("arbitrary",) * n_axes,
                                vmem_limit_bytes=VMEM_LIMIT)


def _dot(a, b):
    return jnp.dot(a, b, preferred_element_type=F32)


def _dot_nt(a, b):
    return lax.dot_general(a, b, (((1,), (1,)), ((), ())), preferred_element_type=F32)


def _dot_tn(a, b):
    return lax.dot_general(a, b, (((0,), (0,)), ((), ())), preferred_element_type=F32)


def _split3(x):
    hi = x.astype(BF16)
    r = x - hi.astype(F32)
    mid = r.astype(BF16)
    lo = (r - mid.astype(F32)).astype(BF16)
    return hi, mid, lo


def _dot_exact_lhs(p_bf16, x):
    hi, mid, lo = _split3(x)
    return _dot(p_bf16, hi) + _dot(p_bf16, mid) + _dot(p_bf16, lo)


def _dot_exact_rhs(x, p_bf16):
    hi, mid, lo = _split3(x)
    return _dot(hi, p_bf16) + _dot(mid, p_bf16) + _dot(lo, p_bf16)


def _dot_f32(a, b):
    ah, am, _ = _split3(a)
    bh, bm, _ = _split3(b)
    return _dot(ah, bh) + _dot(ah, bm) + _dot(am, bh)


def _sigmoid(x):
    return jax.nn.sigmoid(x)


def _silu(x):
    return x * _sigmoid(x)


def _softplus(x):
    return jnp.maximum(x, 0.0) + jnp.log1p(jnp.exp(-jnp.abs(x)))


def _log_sigmoid(x):
    return jnp.minimum(x, 0.0) - jnp.log1p(jnp.exp(-jnp.abs(x)))


def _rms(x, w):
    ms = jnp.mean(x * x, axis=-1, keepdims=True)
    return x * lax.rsqrt(ms + EPS) * w


def _pick_tm(t, cap=1100):
    best = 16
    for tm in range(16, min(t, cap) + 1, 16):
        if t % tm == 0:
            best = tm
    return best


def _ffn_up_kernel(x_ref, nw_ref, wg_ref, wu_ref, o_ref, h_ref):
    @pl.when(pl.program_id(1) == 0)
    def _():
        h_ref[...] = _rms(x_ref[...], nw_ref[...]).astype(BF16)

    h = h_ref[...]
    g = _dot(h, wg_ref[...].astype(BF16))
    u = _dot(h, wu_ref[...].astype(BF16))
    o_ref[...] = (_silu(g) * u).astype(BF16)


def _ffn_up(x, nw, w_gu, tn=512):
    t, d = x.shape
    dff = w_gu.shape[1] // 2
    tm = _pick_tm(t)
    nj = dff // tn
    return pl.pallas_call(
        _ffn_up_kernel,
        grid=(t // tm, nj),
        in_specs=[pl.BlockSpec((tm, d), lambda i, j: (i, 0)),
                  pl.BlockSpec((1, d), lambda i, j: (0, 0)),
                  pl.BlockSpec((d, tn), lambda i, j: (0, j)),
                  pl.BlockSpec((d, tn), lambda i, j: (0, j + nj))],
        out_specs=pl.BlockSpec((tm, tn), lambda i, j: (i, j)),
        out_shape=jax.ShapeDtypeStruct((t, dff), BF16),
        scratch_shapes=[pltpu.VMEM((tm, d), BF16)],
        compiler_params=_cparams(2),
        name="ffn_up",
    )(x, nw.reshape(1, d), w_gu, w_gu)


def _resid_matmul_kernel(a_ref, w_ref, x_ref, o_ref, *, scale):
    o_ref[...] = x_ref[...] + scale * _dot(a_ref[...], w_ref[...].astype(BF16))


def _resid_matmul(a, w, x, scale, tn, name):
    t, k = a.shape
    d = w.shape[1]
    tm = _pick_tm(t)
    return pl.pallas_call(
        functools.partial(_resid_matmul_kernel, scale=scale),
        grid=(t // tm, d // tn),
        in_specs=[pl.BlockSpec((tm, k), lambda i, j: (i, 0)),
                  pl.BlockSpec((k, tn), lambda i, j: (0, j)),
                  pl.BlockSpec((tm, tn), lambda i, j: (i, j))],
        out_specs=pl.BlockSpec((tm, tn), lambda i, j: (i, j)),
        out_shape=jax.ShapeDtypeStruct((t, d), F32),
        compiler_params=_cparams(2),
        name=name,
    )(a, w, x)


def _pack_w_in_kernel(w_ref, o_ref, *, copies, n_used):
    for src, dst, width in copies:
        o_ref[:, dst:dst + width] = w_ref[:, src:src + width].astype(BF16)
    pad = o_ref.shape[1] - n_used
    if pad:
        o_ref[:, n_used:] = jnp.zeros((o_ref.shape[0], pad), BF16)


def _pack_w_in(w, copies, n_used, n_cols, tk=128):
    d, n_src = w.shape
    return pl.pallas_call(
        functools.partial(_pack_w_in_kernel, copies=copies, n_used=n_used),
        grid=(d // tk,),
        in_specs=[pl.BlockSpec((tk, n_src), lambda i: (i, 0))],
        out_specs=pl.BlockSpec((tk, n_cols), lambda i: (i, 0)),
        out_shape=jax.ShapeDtypeStruct((d, n_cols), BF16),
        compiler_params=_cparams(1),
        name="pack_w_in",
    )(w)


def _in_proj_kernel(x_ref, nw_ref, w_ref, o_ref, h_ref):
    @pl.when(pl.program_id(1) == 0)
    def _():
        h_ref[...] = _rms(x_ref[...], nw_ref[...]).astype(BF16)

    o_ref[...] = _dot(h_ref[...], w_ref[...])


def _in_proj(x, nw, w, tn=1024):
    t, d = x.shape
    n = w.shape[1]
    tm = _pick_tm(t)
    return pl.pallas_call(
        _in_proj_kernel,
        grid=(t // tm, n // tn),
        in_specs=[pl.BlockSpec((tm, d), lambda i, j: (i, 0)),
                  pl.BlockSpec((1, d), lambda i, j: (0, 0)),
                  pl.BlockSpec((d, tn), lambda i, j: (0, j))],
        out_specs=pl.BlockSpec((tm, tn), lambda i, j: (i, j)),
        out_shape=jax.ShapeDtypeStruct((t, n), F32),
        scratch_shapes=[pltpu.VMEM((tm, d), BF16)],
        compiler_params=_cparams(2),
        name="in_proj",
    )(x, nw.reshape(1, d), w)


def _merge_kernel(ym_ref, yh_ref, yg_ref, wm_ref, wh_ref, wg_ref, g0_ref, g1_ref, g2_ref, o_ref):
    acc = _sigmoid(g0_ref[...]) * _dot(ym_ref[...], wm_ref[...].astype(BF16))
    acc = acc + _sigmoid(g1_ref[...]) * _dot(yh_ref[...], wh_ref[...].astype(BF16))
    acc = acc + _sigmoid(g2_ref[...]) * _dot(yg_ref[...], wg_ref[...].astype(BF16))
    o_ref[...] = acc.astype(BF16)


def _merge(ym, yh, yg, wm, wh, wg, proj, gate_off, tn=512):
    t, k = ym.shape
    d = wm.shape[1]
    tm = _pick_tm(t)
    gb = gate_off // tn
    nb = d // tn
    yspec = pl.BlockSpec((tm, k), lambda i, j: (i, 0))
    wspec = pl.BlockSpec((k, tn), lambda i, j: (0, j))

    def gspec(b):
        return pl.BlockSpec((tm, tn), lambda i, j: (i, gb + b * nb + j))

    return pl.pallas_call(
        _merge_kernel,
        grid=(t // tm, nb),
        in_specs=[yspec, yspec, yspec, wspec, wspec, wspec, gspec(0), gspec(1), gspec(2)],
        out_specs=pl.BlockSpec((tm, tn), lambda i, j: (i, j)),
        out_shape=jax.ShapeDtypeStruct((t, d), BF16),
        compiler_params=_cparams(2),
        name="merge",
    )(ym, yh, yg, wm, wh, wg, proj, proj, proj)


def _final_norm_kernel(x_ref, nw_ref, o_ref):
    o_ref[...] = _rms(x_ref[...], nw_ref[...])


def _final_norm(x, nw, row0, rows, name):
    d = x.shape[1]
    tm = _pick_tm(rows)
    assert row0 % tm == 0
    return pl.pallas_call(
        _final_norm_kernel,
        grid=(rows // tm,),
        in_specs=[pl.BlockSpec((tm, d), lambda i: (row0 // tm + i, 0)),
                  pl.BlockSpec((1, d), lambda i: (0, 0))],
        out_specs=pl.BlockSpec((tm, d), lambda i: (i, 0)),
        out_shape=jax.ShapeDtypeStruct((rows, d), F32),
        compiler_params=_cparams(1),
        name=name,
    )(x, nw.reshape(1, d))


def _scan_constants():
    c = CHUNK
    pm = np.zeros((2, c, c), np.float32)
    sg = np.zeros((N_LEVELS - 1, c, LANES), np.float32)
    mk = np.zeros((N_LEVELS + 1, c, c), np.float32)
    r = np.arange(c)
    for lvl in range(N_LEVELS):
        h = c >> (lvl + 1)
        for t in range(c):
            blk, pos = divmod(t, 2 * h)
            ridx = blk * 2 * h + h - 1
            upper = pos >= h
            if upper:
                mk[lvl, t] = ((r // (2 * h)) == blk) & ((r % (2 * h)) < h)
            if h >= 4:
                sg[lvl, t] = 1.0 if upper else -1.0
            elif h == 2:
                pm[0, t] = ((r > ridx) & (r <= t)) if upper else ((r > t) & (r <= ridx))
            else:
                sg[N_LEVELS - 2, t] = 1.0 if upper else 0.0
    pm[1] = r[None, :] <= r[:, None]
    mk[N_LEVELS] = np.eye(c)
    return jnp.asarray(pm.reshape(2 * c, c), BF16), jnp.asarray(sg), jnp.asarray(mk)


def _lower_bound(logits, layer):
    m = jnp.max(logits, axis=0, keepdims=True)
    e = jnp.exp(logits - m)
    sm = e / jnp.sum(e, axis=0, keepdims=True)
    lb = jnp.zeros_like(m)
    for i in range(1, layer + 1):
        lb = lb + sm[i:i + 1, :]
    return lb


def _gla_inputs(refs, hgrn, layer):
    if hgrn:
        q_ref, f_ref, i_ref, lbl_ref = refs
        lb = _lower_bound(lbl_ref[...], layer)
        q = _silu(q_ref[...]) * (LANES ** -0.5)
        zf = f_ref[...]
        f = lb + (1.0 - lb) * _sigmoid(zf)
        gl = jnp.log(jnp.maximum(f, F32_TINY))
        k = (1.0 - lb) * _sigmoid(-zf)
        v = i_ref[...]
    else:
        q_ref, k_ref, v_ref, a_ref, wd_ref, bd_ref = refs
        q = q_ref[...] * (LANES ** -0.5)
        k = k_ref[...]
        v = v_ref[...]
        gl = _log_sigmoid(_dot_f32(a_ref[...], wd_ref[...]) + bd_ref[...]) / G_NORMALIZER
    return q, k, v, gl


def _level_factor(lvl, gl, cum, z_h2, sg_ref):
    h = CHUNK >> (lvl + 1)
    if h >= 4:
        c3 = cum.reshape(CHUNK // (2 * h), 2 * h, LANES)
        d = (c3 - c3[:, h - 1:h, :]).reshape(CHUNK, LANES)
        return jnp.exp(sg_ref[lvl] * d)
    if h == 2:
        return jnp.exp(z_h2)
    return jnp.exp(sg_ref[N_LEVELS - 2] * gl)


def _gla_prompt_kernel(*refs, hgrn, layer, hp, dv):
    n_in = 4 if hgrn else 6
    in_refs = refs[:n_in]
    g_ref, nw_ref, pm_ref, sg_ref, mk_ref, y_ref, so_ref, s_scr = refs[n_in:]
    c = pl.program_id(2)
    widths = (LANES, LANES, dv, LANES) if hgrn else (LANES, LANES, dv, None, LANES, LANES)

    @pl.when(c == 0)
    def _():
        s_scr[...] = jnp.zeros_like(s_scr)

    finals = []
    for u in range(hp):
        sub = [r if w is None else r.at[:, u * w:(u + 1) * w] for r, w in zip(in_refs, widths)]
        q, k, v, gl = _gla_inputs(sub, hgrn, layer)
        zz = _dot_exact_lhs(pm_ref[...], gl)
        z_h2, cum = zz[:CHUNK], zz[CHUNK:]
        att = mk_ref[N_LEVELS] * _dot_nt(q.astype(BF16), k.astype(BF16))
        for lvl in range(N_LEVELS):
            e = _level_factor(lvl, gl, cum, z_h2, sg_ref)
            att = att + mk_ref[lvl] * _dot_nt((q * e).astype(BF16), (k * e).astype(BF16))
        e_cum = jnp.exp(cum)
        e_tail = jnp.exp(cum[CHUNK - 1:CHUNK, :] - cum)
        vb = v.astype(BF16)
        s = s_scr[u]
        o = _dot(att.astype(BF16), vb) + _dot((q * e_cum).astype(BF16), s.astype(BF16))
        e_last = e_cum.T[:, CHUNK - 1:CHUNK]
        s_new = e_last * s + _dot_tn((k * e_tail).astype(BF16), vb)
        s_scr[u] = s_new
        finals.append(s_new)
        gate = g_ref[:, u * dv:(u + 1) * dv]
        y_ref[:, u * dv:(u + 1) * dv] = (_rms(o, nw_ref[...]) * _silu(gate)).astype(BF16)

    @pl.when(c == pl.num_programs(2) - 1)
    def _():
        for u in range(hp):
            so_ref[u] = finals[u]


def _gla_prompt(proj, offs, extra, nw, consts, *, hgrn, nseq, nchunks, heads, dv, layer, hp=4):
    pm, sg, mk = consts
    assert heads % hp == 0

    def col(off, w):
        assert off % (hp * w) == 0
        return pl.BlockSpec((CHUNK, hp * w), lambda h, b, c: (b * nchunks + c, off // (hp * w) + h))

    def full(a):
        nd = a.ndim
        return pl.BlockSpec(a.shape, lambda h, b, c: (0,) * nd)

    if hgrn:
        oq, of, oi, og = offs
        (lbl,) = extra
        in_specs = [col(oq, LANES), col(of, LANES), col(oi, dv),
                    pl.BlockSpec((lbl.shape[0], hp * LANES), lambda h, b, c: (0, h))]
        args = [proj, proj, proj, lbl]
    else:
        oq, ok, ov, og, oa = offs
        wd, bd = extra
        in_specs = [col(oq, LANES), col(ok, LANES), col(ov, dv),
                    pl.BlockSpec((CHUNK, LANES), lambda h, b, c: (b * nchunks + c, oa // LANES)),
                    pl.BlockSpec((LANES, hp * LANES), lambda h, b, c: (0, h)),
                    pl.BlockSpec((1, hp * LANES), lambda h, b, c: (0, h))]
        args = [proj, proj, proj, proj, wd, bd]
    in_specs += [col(og, dv), full(nw), full(pm), full(sg), full(mk)]
    args += [proj, nw, pm, sg, mk]
    y, s = pl.pallas_call(
        functools.partial(_gla_prompt_kernel, hgrn=hgrn, layer=layer, hp=hp, dv=dv),
        grid=(heads // hp, nseq, nchunks),
        in_specs=in_specs,
        out_specs=[pl.BlockSpec((CHUNK, hp * dv), lambda h, b, c: (b * nchunks + c, h)),
                   pl.BlockSpec((None, hp, LANES, dv), lambda h, b, c: (b, h, 0, 0))],
        out_shape=[jax.ShapeDtypeStruct((nseq * nchunks * CHUNK, heads * dv), BF16),
                   jax.ShapeDtypeStruct((nseq, heads, LANES, dv), F32)],
        scratch_shapes=[pltpu.VMEM((hp, LANES, dv), F32)],
        compiler_params=_cparams(3),
        name="hgrn_prompt" if hgrn else "gla_prompt",
    )(*args)
    return y, s


def _columns(x):
    nb = x.shape[0]
    if nb < LANES:
        x = jnp.concatenate([x, jnp.zeros((LANES - nb, x.shape[1]), x.dtype)], axis=0)
    return x.T


def _store_state(so_ref, layer, b, s_new, first):
    if first:
        for l in range(so_ref.shape[0]):
            so_ref[l, b] = s_new if l == layer else jnp.zeros_like(s_new)
    else:
        so_ref[b] = s_new


def _gla_sample_kernel(*refs, hgrn, nb, layer, first):
    n_in = 4 if hgrn else 6
    in_refs = refs[:n_in]
    g_ref, nw_ref, s_ref = refs[n_in:n_in + 3]
    y_ref, so_ref = refs[-2:]
    q, k, v, gl = _gla_inputs(in_refs, hgrn, layer)
    e_t = _columns(jnp.exp(gl))
    k_t = _columns(k)
    qb = q.astype(BF16)
    rowid = lax.broadcasted_iota(jnp.int32, v.shape, 0)
    o = jnp.zeros(v.shape, F32)
    for b in range(nb):
        s_new = e_t[:, b:b + 1] * s_ref[b] + k_t[:, b:b + 1] * v[b:b + 1, :]
        _store_state(so_ref, layer, b, s_new, first)
        o = jnp.where(rowid == b, _dot(qb, s_new.astype(BF16)), o)
    y_ref[...] = (_rms(o, nw_ref[...]) * _silu(g_ref[...])).astype(BF16)


def _state_specs(depth, layer, nb, dv, first):
    ispec = pl.BlockSpec((None, nb, None, LANES, dv), lambda h, i: (layer, i, h, 0, 0))
    if first:
        ospec = pl.BlockSpec((depth, nb, None, LANES, dv), lambda h, i: (0, i, h, 0, 0))
    else:
        ospec = ispec
    return ispec, ospec


def _gla_sample(proj, state, prev, row0, offs, extra, nw, *, hgrn, heads, dv, layer, nb=16):
    depth, ns = state.shape[:2]
    rb = row0 // nb
    first = prev is None

    def col(off, w):
        return pl.BlockSpec((nb, w), lambda h, i: (rb + i, off // w + h))

    if hgrn:
        oq, of, oi, og = offs
        (lbl,) = extra
        in_specs = [col(oq, LANES), col(of, LANES), col(oi, dv),
                    pl.BlockSpec((lbl.shape[0], LANES), lambda h, i: (0, h))]
        args = [proj, proj, proj, lbl]
    else:
        oq, ok, ov, og, oa = offs
        wd, bd = extra
        in_specs = [col(oq, LANES), col(ok, LANES), col(ov, dv),
                    pl.BlockSpec((nb, LANES), lambda h, i: (rb + i, oa // LANES)),
                    pl.BlockSpec((LANES, LANES), lambda h, i: (0, h)),
                    pl.BlockSpec((1, LANES), lambda h, i: (0, h))]
        args = [proj, proj, proj, proj, wd, bd]
    ispec, ospec = _state_specs(depth, layer, nb, dv, first)
    in_specs += [col(og, dv), pl.BlockSpec(nw.shape, lambda h, i: (0, 0)), ispec]
    args += [proj, nw, state]
    aliases = {}
    if not first:
        in_specs.append(pl.BlockSpec(memory_space=pl.ANY))
        args.append(prev)
        aliases = {len(args) - 1: 1}
    y, s = pl.pallas_call(
        functools.partial(_gla_sample_kernel, hgrn=hgrn, nb=nb, layer=layer, first=first),
        grid=(heads, ns // nb),
        in_specs=in_specs,
        out_specs=[pl.BlockSpec((nb, dv), lambda h, i: (i, h)), ospec],
        out_shape=[jax.ShapeDtypeStruct((ns, heads * dv), BF16),
                   jax.ShapeDtypeStruct(state.shape, F32)],
        input_output_aliases=aliases,
        compiler_params=_cparams(2),
        name="hgrn_sample" if hgrn else "gla_sample",
    )(*args)
    return y, s


def _mamba_post(y, xs, z, dskip, nw):
    y = (y + dskip * xs) * _silu(z)
    gs = y.shape[1] // M_GROUPS
    outs = [_rms(y[:, g * gs:(g + 1) * gs], nw[:, g * gs:(g + 1) * gs]) for g in range(M_GROUPS)]
    return jnp.concatenate(outs, axis=1)


def _mamba_prompt_kernel(z_ref, x_ref, bc_ref, sm_ref, cwx_ref, cwb_ref, cbx_ref, cbb_ref,
                         dtb_ref, alog_ref, dsk_ref, nw_ref, tril_ref, mask_ref,
                         y_ref, cox_ref, cob_ref, so_ref, ex_scr, eb_scr, s_scr):
    c = pl.program_id(1)
    nc = pl.num_programs(1)
    tail = SUBLANES

    @pl.when(c == 0)
    def _():
        ex_scr[0:tail, :] = jnp.zeros((tail, ex_scr.shape[1]), F32)
        eb_scr[0:tail, :] = jnp.zeros((tail, eb_scr.shape[1]), F32)
        s_scr[...] = jnp.zeros_like(s_scr)

    ex_scr[tail:tail + CHUNK, :] = x_ref[...]
    eb_scr[tail:tail + CHUNK, :] = bc_ref[...]

    def conv(scr, cw_ref, cb_ref):
        acc = cb_ref[...]
        for w in range(M_CONV):
            sh = M_CONV - 1 - w
            acc = acc + cw_ref[w:w + 1, :] * scr[tail - sh:tail - sh + CHUNK, :]
        return _silu(acc)

    xs = conv(ex_scr, cwx_ref, cbx_ref)
    bcm = conv(eb_scr, cwb_ref, cbb_ref)

    @pl.when(c == nc - 1)
    def _():
        cox_ref[...] = ex_scr[tail + CHUNK - (M_CONV - 1):tail + CHUNK, :]
        cob_ref[...] = eb_scr[tail + CHUNK - (M_CONV - 1):tail + CHUNK, :]

    ex_scr[0:tail, :] = ex_scr[CHUNK:CHUNK + tail, :]
    eb_scr[0:tail, :] = eb_scr[CHUNK:CHUNK + tail, :]

    dt = _softplus(sm_ref[...] + dtb_ref[...])
    a = -jnp.exp(alog_ref[...])
    cum = _dot_exact_lhs(tril_ref[...], dt * a)
    cum_t = cum.T
    cl = cum[CHUNK - 1:CHUNK, :]
    mask = mask_ref[...] > 0.5
    gw = M_GROUPS * M_STATE
    lo = lax.broadcasted_iota(jnp.int32, (CHUNK, LANES), 1) < M_HEADDIM
    lo_r = lax.broadcasted_iota(jnp.int32, (LANES, LANES), 0) < M_HEADDIM
    n_pairs = xs.shape[1] // LANES
    per_group = n_pairs // M_GROUPS
    ys = []
    for j in range(n_pairs):
        g = j // per_group
        h0, h1 = 2 * j, 2 * j + 1
        if j % per_group == 0:
            bg = bcm[:, g * M_STATE:(g + 1) * M_STATE].astype(BF16)
            cg = bcm[:, gw + g * M_STATE:gw + (g + 1) * M_STATE].astype(BF16)
            cb = jnp.where(mask, _dot_nt(cg, bg), 0.0)

        def dec(h):
            return jnp.exp(jnp.minimum(cum[:, h:h + 1] - cum_t[h:h + 1, :], 0.0))

        x2 = xs[:, j * LANES:(j + 1) * LANES]
        dt2 = jnp.where(lo, dt[:, h0:h0 + 1], dt[:, h1:h1 + 1])
        xdt = (x2 * dt2).astype(BF16)
        y_in = jnp.where(lo, _dot((cb * dec(h0)).astype(BF16), xdt),
                         _dot((cb * dec(h1)).astype(BF16), xdt))
        ec2 = jnp.where(lo, jnp.exp(cum[:, h0:h0 + 1]), jnp.exp(cum[:, h1:h1 + 1]))
        s = s_scr[j]
        ys.append(y_in + _dot_nt(cg, s.astype(BF16)) * ec2)
        w2 = dt2 * jnp.where(lo, jnp.exp(cl[:, h0:h0 + 1] - cum[:, h0:h0 + 1]),
                             jnp.exp(cl[:, h1:h1 + 1] - cum[:, h1:h1 + 1]))
        el2 = jnp.where(lo_r, jnp.exp(cl[:, h0:h0 + 1]), jnp.exp(cl[:, h1:h1 + 1]))
        s_scr[j] = el2 * s + _dot_tn((x2 * w2).astype(BF16), bg)
    y = jnp.concatenate(ys, axis=1)
    y_ref[...] = _mamba_post(y, xs, z_ref[...], dsk_ref[...], nw_ref[...]).astype(BF16)

    @pl.when(c == nc - 1)
    def _():
        so_ref[...] = s_scr[...]


def _mamba_prompt(proj, offs, p, consts, *, nseq, nchunks):
    oz, ox, obc, osm = offs
    mw = p["dskip"].shape[1]
    bcw = p["cwb"].shape[1]
    n_pairs = mw // LANES

    def col(off, w):
        return pl.BlockSpec((CHUNK, w), lambda b, c: (b * nchunks + c, off // w))

    def full(a):
        nd = a.ndim
        return pl.BlockSpec(a.shape, lambda b, c: (0,) * nd)

    small = [p["cwx"], p["cwb"], p["cbx"], p["cbb"], p["dtb"], p["alog"], p["dskip"], p["nw"],
             consts[0], consts[1]]
    y, cox, cob, s = pl.pallas_call(
        _mamba_prompt_kernel,
        grid=(nseq, nchunks),
        in_specs=[col(oz, mw), col(ox, mw), col(obc, bcw), col(osm, LANES)] + [full(a) for a in small],
        out_specs=[pl.BlockSpec((CHUNK, mw), lambda b, c: (b * nchunks + c, 0)),
                   pl.BlockSpec((None, M_CONV - 1, mw), lambda b, c: (b, 0, 0)),
                   pl.BlockSpec((None, M_CONV - 1, bcw), lambda b, c: (b, 0, 0)),
                   pl.BlockSpec((None, n_pairs, LANES, M_STATE), lambda b, c: (b, 0, 0, 0))],
        out_shape=[jax.ShapeDtypeStruct((nseq * nchunks * CHUNK, mw), BF16),
                   jax.ShapeDtypeStruct((nseq, M_CONV - 1, mw), F32),
                   jax.ShapeDtypeStruct((nseq, M_CONV - 1, bcw), F32),
                   jax.ShapeDtypeStruct((nseq, n_pairs, LANES, M_STATE), F32)],
        scratch_shapes=[pltpu.VMEM((CHUNK + SUBLANES, mw), F32), pltpu.VMEM((CHUNK + SUBLANES, bcw), F32),
                        pltpu.VMEM((n_pairs, LANES, M_STATE), F32)],
        compiler_params=_cparams(2),
        name="mamba_prompt",
    )(proj, proj, proj, proj, *small)
    return y, jnp.concatenate([cox, cob], axis=-1), s


def _mamba_sample_prep_kernel(cs_ref, x_ref, bc_ref, sm_ref, cw_ref, cb_ref, dtb_ref, alog_ref, exp_ref,
                              co_ref, act_ref, dte_ref, ee_ref):
    cd = cw_ref.shape[1]
    new = jnp.concatenate([x_ref[...], bc_ref[...]], axis=1)
    acc = cb_ref[...] + cw_ref[M_CONV - 1:M_CONV, :] * new
    for w in range(M_CONV - 1):
        acc = acc + cw_ref[w:w + 1, :] * cs_ref[:, w * cd:(w + 1) * cd]
    act_ref[...] = _silu(acc)
    for w in range(1, M_CONV - 1):
        co_ref[:, (w - 1) * cd:w * cd] = cs_ref[:, w * cd:(w + 1) * cd]
    co_ref[:, (M_CONV - 2) * cd:(M_CONV - 1) * cd] = new
    dt = _softplus(sm_ref[...] + dtb_ref[...])
    a = -jnp.exp(alog_ref[...])
    dte_ref[...] = _dot_exact_rhs(dt, exp_ref[...])
    ee_ref[...] = jnp.exp(_dot_exact_rhs(dt * a, exp_ref[...]))


def _mamba_sample_state_kernel(x_ref, b_ref, c_ref, dte_ref, ee_ref, s_ref, *out_refs, nb, layer, first):
    y_ref, so_ref = out_refs[-2:]
    xdt_t = _columns(x_ref[...] * dte_ref[...])
    e_t = _columns(ee_ref[...])
    bv = b_ref[...]
    cb = c_ref[...].astype(BF16)
    rowid = lax.broadcasted_iota(jnp.int32, (nb, LANES), 0)
    y = jnp.zeros((nb, LANES), F32)
    for b in range(nb):
        s_new = e_t[:, b:b + 1] * s_ref[b] + xdt_t[:, b:b + 1] * bv[b:b + 1, :]
        _store_state(so_ref, layer, b, s_new, first)
        y = jnp.where(rowid == b, _dot_nt(cb, s_new.astype(BF16)), y)
    y_ref[...] = y


def _mamba_sample_post_kernel(y_ref, x_ref, z_ref, dsk_ref, nw_ref, o_ref):
    o_ref[...] = _mamba_post(y_ref[...], x_ref[...], z_ref[...], dsk_ref[...], nw_ref[...]).astype(BF16)


def _mamba_sample(proj, conv_state, ssm_state, prev, row0, offs, p, expand, layer, nb=16):
    oz, ox, obc, osm = offs
    depth, ns = ssm_state.shape[:2]
    mw = p["dskip"].shape[1]
    bcw = p["cwb"].shape[1]
    cd = mw + bcw
    n_pairs = mw // LANES
    per_group = n_pairs // M_GROUPS
    rb = row0 // ns
    first = prev is None
    cw = jnp.concatenate([p["cwx"], p["cwb"]], axis=1)
    cb = jnp.concatenate([p["cbx"], p["cbb"]], axis=1)

    def full1(a):
        nd = a.ndim
        return pl.BlockSpec(a.shape, lambda i: (0,) * nd)

    cs2 = conv_state.reshape(ns, (M_CONV - 1) * cd)
    small = [cw, cb, p["dtb"], p["alog"], expand]
    co, act, dte, ee = pl.pallas_call(
        _mamba_sample_prep_kernel,
        grid=(1,),
        in_specs=[full1(cs2),
                  pl.BlockSpec((ns, mw), lambda i: (rb, ox // mw)),
                  pl.BlockSpec((ns, bcw), lambda i: (rb, obc // bcw)),
                  pl.BlockSpec((ns, LANES), lambda i: (rb, osm // LANES))] + [full1(a) for a in small],
        out_specs=[pl.BlockSpec((ns, (M_CONV - 1) * cd), lambda i: (0, 0)),
                   pl.BlockSpec((ns, cd), lambda i: (0, 0)),
                   pl.BlockSpec((ns, mw), lambda i: (0, 0)),
                   pl.BlockSpec((ns, mw), lambda i: (0, 0))],
        out_shape=[jax.ShapeDtypeStruct((ns, (M_CONV - 1) * cd), F32),
                   jax.ShapeDtypeStruct((ns, cd), F32),
                   jax.ShapeDtypeStruct((ns, mw), F32),
                   jax.ShapeDtypeStruct((ns, mw), F32)],
        compiler_params=_cparams(1),
        name="mamba_sample_prep",
    )(cs2, proj, proj, proj, *small)

    bblk = mw // LANES
    cblk = bblk + M_GROUPS * M_STATE // LANES
    ispec, ospec = _state_specs(depth, layer, nb, M_STATE, first)
    in_specs = [pl.BlockSpec((nb, LANES), lambda j, i: (i, j)),
                pl.BlockSpec((nb, LANES), lambda j, i: (i, bblk + j // per_group)),
                pl.BlockSpec((nb, LANES), lambda j, i: (i, cblk + j // per_group)),
                pl.BlockSpec((nb, LANES), lambda j, i: (i, j)),
                pl.BlockSpec((nb, LANES), lambda j, i: (i, j)),
                ispec]
    args = [act, act, act, dte, ee, ssm_state]
    aliases = {}
    if not first:
        in_specs.append(pl.BlockSpec(memory_space=pl.ANY))
        args.append(prev)
        aliases = {len(args) - 1: 1}
    y, s_new = pl.pallas_call(
        functools.partial(_mamba_sample_state_kernel, nb=nb, layer=layer, first=first),
        grid=(n_pairs, ns // nb),
        in_specs=in_specs,
        out_specs=[pl.BlockSpec((nb, LANES), lambda j, i: (i, j)), ospec],
        out_shape=[jax.ShapeDtypeStruct((ns, mw), F32),
                   jax.ShapeDtypeStruct(ssm_state.shape, F32)],
        input_output_aliases=aliases,
        compiler_params=_cparams(2),
        name="mamba_sample_state",
    )(*args)

    ym = pl.pallas_call(
        _mamba_sample_post_kernel,
        grid=(1,),
        in_specs=[full1(y),
                  pl.BlockSpec((ns, mw), lambda i: (0, 0)),
                  pl.BlockSpec((ns, mw), lambda i: (rb, oz // mw)),
                  full1(p["dskip"]), full1(p["nw"])],
        out_specs=pl.BlockSpec((ns, mw), lambda i: (0, 0)),
        out_shape=jax.ShapeDtypeStruct((ns, mw), BF16),
        compiler_params=_cparams(1),
        name="mamba_sample_post",
    )(y, act, proj, p["dskip"], p["nw"])
    return ym, co.reshape(ns, M_CONV - 1, cd), s_new


def _pad_lanes(v, n=LANES):
    v = v.reshape(1, -1)
    return jnp.pad(v, ((0, 0), (0, n - v.shape[1])))


def kernel(x_prompt, x_sample, state_conv, state_ssm, state_hgrn, state_gla, ffn1_norm, ffn1_w_gate_up, ffn1_w_down, mix_norm, w_in, conv_w, conv_b, dt_bias, a_log, d_skip, mamba_norm, hgrn_lb_logits, hgrn_norm, gla_w_decay, gla_b_decay, gla_norm, w_branch_mamba, w_branch_hgrn, w_branch_gla, w_out, ffn2_norm, ffn2_w_gate_up, ffn2_w_down, final_norm):
    nseq, seq, d = x_prompt.shape
    ns = x_sample.shape[0]
    depth = w_in.shape[0]
    nchunks = seq // CHUNK
    n_prompt = nseq * seq
    mw = w_branch_mamba.shape[1]
    hw = w_branch_hgrn.shape[1]
    gw = w_branch_gla.shape[1]
    gk = gla_w_decay.shape[2]
    m_heads = dt_bias.shape[1]
    h_heads = state_hgrn.shape[2]
    g_heads = state_gla.shape[2]
    g_dv = state_gla.shape[4]
    bcw = 2 * M_GROUPS * M_STATE
    n_pairs = mw // LANES
    assert seq % CHUNK == 0 and n_prompt % ns == 0 and ns % 16 == 0
    assert m_heads <= G_RANK + m_heads <= LANES and gk // g_heads == LANES and hw // h_heads == LANES

    seg_w = {"z": mw, "xs": mw, "bc": bcw, "dt": m_heads, "hq": hw, "hf": hw, "hi": hw, "hg": hw,
             "gq": gk, "gk": gk, "gv": gw, "gg": gw, "ga": G_RANK, "gate": N_BRANCH * d}
    src_order = ("z", "xs", "bc", "dt", "hq", "hf", "hi", "hg", "gq", "gk", "gv", "gg", "ga", "gate")
    dst_order = ("z", "xs", "hq", "hf", "hi", "hg", "gv", "gg", "gq", "gk", "bc", "gate", "dt", "ga")
    src, off = {}, {}
    pos = 0
    for name in src_order:
        src[name] = pos
        pos += seg_w[name]
    pos = 0
    for name in dst_order:
        off[name] = pos
        pos += seg_w[name]
    n_used = pos
    n_cols = -(-n_used // 1024) * 1024
    copies = tuple((src[name], off[name], seg_w[name]) for name in dst_order)
    oz, ox, obc, osm = off["z"], off["xs"], off["bc"], off["dt"]
    assert off["ga"] == osm + m_heads and osm % LANES == 0

    consts = _scan_constants()
    tril = consts[0][CHUNK:]
    mconsts = (tril, jnp.asarray(np.tril(np.ones((CHUNK, CHUNK), np.float32))))
    expand_np = np.zeros((LANES, mw), np.float32)
    for h in range(m_heads):
        expand_np[h, h * M_HEADDIM:(h + 1) * M_HEADDIM] = 1.0
    expand = jnp.asarray(expand_np, BF16)

    x = jnp.concatenate([x_prompt.reshape(n_prompt, d), x_sample.reshape(ns, d)], axis=0)
    ssm5 = state_ssm.reshape(depth, ns, n_pairs, LANES, M_STATE)

    pc, ps, ph, pg, sc = [], [], [], [], []
    ss = sh = sg = None
    for l in range(depth):
        w_perm = _pack_w_in(w_in[l], copies, n_used, n_cols)
        mp = {
            "cwx": conv_w[l][:, :mw], "cwb": conv_w[l][:, mw:],
            "cbx": conv_b[l][:mw].reshape(1, mw), "cbb": conv_b[l][mw:].reshape(1, bcw),
            "dtb": _pad_lanes(dt_bias[l]), "alog": _pad_lanes(a_log[l]),
            "dskip": jnp.repeat(d_skip[l], M_HEADDIM).reshape(1, mw),
            "nw": mamba_norm[l].reshape(1, mw),
        }
        wd = jnp.zeros((LANES, gk), F32).at[m_heads:m_heads + G_RANK].set(gla_w_decay[l])
        bd = gla_b_decay[l].reshape(1, gk)
        hnw = hgrn_norm[l].reshape(1, LANES)
        gnw = gla_norm[l].reshape(1, g_dv)

        act = _ffn_up(x, ffn1_norm[l], ffn1_w_gate_up[l])
        x = _resid_matmul(act, ffn1_w_down[l], x, 0.5, 256, "ffn_down")
        proj = _in_proj(x, mix_norm[l], w_perm)

        moffs = (oz, ox, obc, osm)
        hoffs = (off["hq"], off["hf"], off["hi"], off["hg"])
        goffs = (off["gq"], off["gk"], off["gv"], off["gg"], osm)
        ym_p, c1, s1 = _mamba_prompt(proj, moffs, mp, mconsts, nseq=nseq, nchunks=nchunks)
        yh_p, h1 = _gla_prompt(proj, hoffs, (hgrn_lb_logits,), hnw, consts, hgrn=True, nseq=nseq,
                               nchunks=nchunks, heads=h_heads, dv=LANES, layer=l)
        yg_p, g1 = _gla_prompt(proj, goffs, (wd, bd), gnw, consts, hgrn=False, nseq=nseq,
                               nchunks=nchunks, heads=g_heads, dv=g_dv, layer=l)
        ym_s, c2, ss = _mamba_sample(proj, state_conv[l], ssm5, ss, n_prompt, moffs, mp, expand, l)
        yh_s, sh = _gla_sample(proj, state_hgrn, sh, n_prompt, hoffs, (hgrn_lb_logits,), hnw,
                               hgrn=True, heads=h_heads, dv=LANES, layer=l)
        yg_s, sg = _gla_sample(proj, state_gla, sg, n_prompt, goffs, (wd, bd), gnw,
                               hgrn=False, heads=g_heads, dv=g_dv, layer=l)
        ym = jnp.concatenate([ym_p, ym_s], axis=0)
        yh = jnp.concatenate([yh_p, yh_s], axis=0)
        yg = jnp.concatenate([yg_p, yg_s], axis=0)
        merged = _merge(ym, yh, yg, w_branch_mamba[l], w_branch_hgrn[l], w_branch_gla[l], proj, off["gate"])
        x = _resid_matmul(merged, w_out[l], x, 1.0, 512, "out_proj")

        act = _ffn_up(x, ffn2_norm[l], ffn2_w_gate_up[l])
        x = _resid_matmul(act, ffn2_w_down[l], x, 0.5, 256, "ffn_down")

        pc.append(c1)
        ps.append(s1.reshape(nseq, m_heads, M_HEADDIM, M_STATE))
        ph.append(h1)
        pg.append(g1)
        sc.append(c2)

    y_prompt = _final_norm(x, final_norm, 0, n_prompt, "final_norm_prompt").reshape(nseq, seq, d)
    y_sample = _final_norm(x, final_norm, n_prompt, ns, "final_norm_sample").reshape(ns, 1, d)
    return (y_prompt, y_sample, jnp.stack(pc), jnp.stack(ps), jnp.stack(ph), jnp.stack(pg),
            jnp.stack(sc), ss.reshape(state_ssm.shape), sh, sg)
```

```python
import functools

import jax
import jax.numpy as jnp
import numpy as np
from jax import lax
from jax.experimental import pallas as pl
from jax.experimental.pallas import tpu as pltpu

F32 = jnp.float32
BF16 = jnp.bfloat16
EPS = 1e-6
F32_TINY = float(np.finfo(np.float32).tiny)

CHUNK = 256
LANES = 128
SUBLANES = 8
M_HEADDIM = 64
M_STATE = 128
M_GROUPS = 2
M_CONV = 4
G_RANK = 16
G_NORMALIZER = 16.0
N_BRANCH = 3
N_LEVELS = int(np.log2(CHUNK))
VMEM_LIMIT = 56 * 1024 * 1024
STATE_BLOCK_BYTES = 4 * 1024 * 1024


def _cparams(n_axes):
    return pltpu.CompilerParams(dimension_semantics=("arbitrary",) * n_axes,
                                vmem_limit_bytes=VMEM_LIMIT)


def _dot(a, b):
    return jnp.dot(a, b, preferred_element_type=F32)


def _dot_nt(a, b):
    return lax.dot_general(a, b, (((1,), (1,)), ((), ())), preferred_element_type=F32)


def _dot_tn(a, b):
    return lax.dot_general(a, b, (((0,), (0,)), ((), ())), preferred_element_type=F32)


def _split3(x):
    hi = x.astype(BF16)
    r = x - hi.astype(F32)
    mid = r.astype(BF16)
    lo = (r - mid.astype(F32)).astype(BF16)
    return hi, mid, lo


def _dot_exact_lhs(p_bf16, x):
    hi, mid, lo = _split3(x)
    return _dot(p_bf16, hi) + _dot(p_bf16, mid) + _dot(p_bf16, lo)


def _dot_exact_rhs(x, p_bf16):
    hi, mid, lo = _split3(x)
    return _dot(hi, p_bf16) + _dot(mid, p_bf16) + _dot(lo, p_bf16)


def _dot_f32(a, b):
    ah, am, _ = _split3(a)
    bh, bm, _ = _split3(b)
    return _dot(ah, bh) + _dot(ah, bm) + _dot(am, bh)


def _sigmoid(x):
    return jax.nn.sigmoid(x)


def _silu(x):
    return x * _sigmoid(x)


def _softplus(x):
    return jnp.maximum(x, 0.0) + jnp.log1p(jnp.exp(-jnp.abs(x)))


def _log_sigmoid(x):
    return jnp.minimum(x, 0.0) - jnp.log1p(jnp.exp(-jnp.abs(x)))


def _rms(x, w):
    ms = jnp.mean(x * x, axis=-1, keepdims=True)
    return x * lax.rsqrt(ms + EPS) * w


def _pick_tm(t, cap=1100):
    best = 16
    for tm in range(16, min(t, cap) + 1, 16):
        if t % tm == 0:
            best = tm
    return best


def _ffn_up_kernel(x_ref, nw_ref, wg_ref, wu_ref, o_ref, h_ref):
    @pl.when(pl.program_id(1) == 0)
    def _():
        h_ref[...] = _rms(x_ref[...], nw_ref[...]).astype(BF16)

    h = h_ref[...]
    g = _dot(h, wg_ref[...].astype(BF16))
    u = _dot(h, wu_ref[...].astype(BF16))
    o_ref[...] = (_silu(g) * u).astype(BF16)


def _ffn_up(x, nw, w_gu, layer, tn=512):
    t, d = x.shape
    dff = w_gu.shape[2] // 2
    tm = _pick_tm(t)
    nj = dff // tn
    return pl.pallas_call(
        _ffn_up_kernel,
        grid=(t // tm, nj),
        in_specs=[pl.BlockSpec((tm, d), lambda i, j: (i, 0)),
                  pl.BlockSpec((1, d), lambda i, j: (0, 0)),
                  pl.BlockSpec((None, d, tn), lambda i, j: (layer, 0, j)),
                  pl.BlockSpec((None, d, tn), lambda i, j: (layer, 0, j + nj))],
        out_specs=pl.BlockSpec((tm, tn), lambda i, j: (i, j)),
        out_shape=jax.ShapeDtypeStruct((t, dff), BF16),
        scratch_shapes=[pltpu.VMEM((tm, d), BF16)],
        compiler_params=_cparams(2),
        name="ffn_up",
    )(x, nw.reshape(1, d), w_gu, w_gu)


def _resid_matmul_kernel(a_ref, w_ref, x_ref, o_ref, *, scale):
    o_ref[...] = x_ref[...] + scale * _dot(a_ref[...], w_ref[...].astype(BF16))


def _resid_matmul(a, w, layer, x, scale, tn, name):
    t, k = a.shape
    d = w.shape[2]
    tm = _pick_tm(t)
    return pl.pallas_call(
        functools.partial(_resid_matmul_kernel, scale=scale),
        grid=(t // tm, d // tn),
        in_specs=[pl.BlockSpec((tm, k), lambda i, j: (i, 0)),
                  pl.BlockSpec((None, k, tn), lambda i, j: (layer, 0, j)),
                  pl.BlockSpec((tm, tn), lambda i, j: (i, j))],
        out_specs=pl.BlockSpec((tm, tn), lambda i, j: (i, j)),
        out_shape=jax.ShapeDtypeStruct((t, d), F32),
        compiler_params=_cparams(2),
        name=name,
    )(a, w, x)


def _pack_w_in_kernel(w_ref, o_ref, *, copies, n_used):
    for src, dst, width in copies:
        o_ref[:, dst:dst + width] = w_ref[:, src:src + width].astype(BF16)
    pad = o_ref.shape[1] - n_used
    if pad:
        o_ref[:, n_used:] = jnp.zeros((o_ref.shape[0], pad), BF16)


def _pack_w_in(w, layer, copies, n_used, n_cols, tk=128):
    _, d, n_src = w.shape
    return pl.pallas_call(
        functools.partial(_pack_w_in_kernel, copies=copies, n_used=n_used),
        grid=(d // tk,),
        in_specs=[pl.BlockSpec((None, tk, n_src), lambda i: (layer, i, 0))],
        out_specs=pl.BlockSpec((tk, n_cols), lambda i: (i, 0)),
        out_shape=jax.ShapeDtypeStruct((d, n_cols), BF16),
        compiler_params=_cparams(1),
        name="pack_w_in",
    )(w)


def _in_proj_kernel(x_ref, nw_ref, w_ref, o_ref, h_ref):
    @pl.when(pl.program_id(1) == 0)
    def _():
        h_ref[...] = _rms(x_ref[...], nw_ref[...]).astype(BF16)

    o_ref[...] = _dot(h_ref[...], w_ref[...])


def _in_proj(x, nw, w, tn=1024):
    t, d = x.shape
    n = w.shape[1]
    tm = _pick_tm(t)
    return pl.pallas_call(
        _in_proj_kernel,
        grid=(t // tm, n // tn),
        in_specs=[pl.BlockSpec((tm, d), lambda i, j: (i, 0)),
                  pl.BlockSpec((1, d), lambda i, j: (0, 0)),
                  pl.BlockSpec((d, tn), lambda i, j: (0, j))],
        out_specs=pl.BlockSpec((tm, tn), lambda i, j: (i, j)),
        out_shape=jax.ShapeDtypeStruct((t, n), F32),
        scratch_shapes=[pltpu.VMEM((tm, d), BF16)],
        compiler_params=_cparams(2),
        name="in_proj",
    )(x, nw.reshape(1, d), w)


def _merge_kernel(ym_ref, yh_ref, yg_ref, wm_ref, wh_ref, wg_ref, g0_ref, g1_ref, g2_ref, o_ref):
    acc = _sigmoid(g0_ref[...]) * _dot(ym_ref[...], wm_ref[...].astype(BF16))
    acc = acc + _sigmoid(g1_ref[...]) * _dot(yh_ref[...], wh_ref[...].astype(BF16))
    acc = acc + _sigmoid(g2_ref[...]) * _dot(yg_ref[...], wg_ref[...].astype(BF16))
    o_ref[...] = acc.astype(BF16)


def _merge(ym, yh, yg, wm, wh, wg, layer, proj, gate_off, tn=512):
    t, k = ym.shape
    d = wm.shape[2]
    tm = _pick_tm(t)
    gb = gate_off // tn
    nb = d // tn
    yspec = pl.BlockSpec((tm, k), lambda i, j: (i, 0))
    wspec = pl.BlockSpec((None, k, tn), lambda i, j: (layer, 0, j))

    def gspec(b):
        return pl.BlockSpec((tm, tn), lambda i, j: (i, gb + b * nb + j))

    return pl.pallas_call(
        _merge_kernel,
        grid=(t // tm, nb),
        in_specs=[yspec, yspec, yspec, wspec, wspec, wspec, gspec(0), gspec(1), gspec(2)],
        out_specs=pl.BlockSpec((tm, tn), lambda i, j: (i, j)),
        out_shape=jax.ShapeDtypeStruct((t, d), BF16),
        compiler_params=_cparams(2),
        name="merge",
    )(ym, yh, yg, wm, wh, wg, proj, proj, proj)


def _final_norm_kernel(x_ref, nw_ref, o_ref):
    o_ref[...] = _rms(x_ref[...], nw_ref[...])


def _final_norm(x, nw, row0, rows, name):
    d = x.shape[1]
    tm = _pick_tm(rows)
    assert row0 % tm == 0
    return pl.pallas_call(
        _final_norm_kernel,
        grid=(rows // tm,),
        in_specs=[pl.BlockSpec((tm, d), lambda i: (row0 // tm + i, 0)),
                  pl.BlockSpec((1, d), lambda i: (0, 0))],
        out_specs=pl.BlockSpec((tm, d), lambda i: (i, 0)),
        out_shape=jax.ShapeDtypeStruct((rows, d), F32),
        compiler_params=_cparams(1),
        name=name,
    )(x, nw.reshape(1, d))


def _scan_constants():
    c = CHUNK
    pm = np.zeros((2, c, c), np.float32)
    sg = np.zeros((N_LEVELS - 1, c, LANES), np.float32)
    mk = np.zeros((N_LEVELS + 1, c, c), np.float32)
    r = np.arange(c)
    for lvl in range(N_LEVELS):
        h = c >> (lvl + 1)
        for t in range(c):
            blk, pos = divmod(t, 2 * h)
            ridx = blk * 2 * h + h - 1
            upper = pos >= h
            if upper:
                mk[lvl, t] = ((r // (2 * h)) == blk) & ((r % (2 * h)) < h)
            if h >= 4:
                sg[lvl, t] = 1.0 if upper else -1.0
            elif h == 2:
                pm[0, t] = ((r > ridx) & (r <= t)) if upper else ((r > t) & (r <= ridx))
            else:
                sg[N_LEVELS - 2, t] = 1.0 if upper else 0.0
    pm[1] = r[None, :] <= r[:, None]
    mk[N_LEVELS] = np.eye(c)
    return jnp.asarray(pm.reshape(2 * c, c), BF16), jnp.asarray(sg), jnp.asarray(mk)


def _lower_bound(logits, layer):
    m = jnp.max(logits, axis=0, keepdims=True)
    e = jnp.exp(logits - m)
    sm = e / jnp.sum(e, axis=0, keepdims=True)
    lb = jnp.zeros_like(m)
    for i in range(1, layer + 1):
        lb = lb + sm[i:i + 1, :]
    return lb


def _gla_inputs(refs, hgrn, layer):
    if hgrn:
        q_ref, f_ref, i_ref, lbl_ref = refs
        lb = _lower_bound(lbl_ref[...], layer)
        q = _silu(q_ref[...]) * (LANES ** -0.5)
        zf = f_ref[...]
        f = lb + (1.0 - lb) * _sigmoid(zf)
        gl = jnp.log(jnp.maximum(f, F32_TINY))
        k = (1.0 - lb) * _sigmoid(-zf)
        v = i_ref[...]
    else:
        q_ref, k_ref, v_ref, a_ref, wd_ref, bd_ref = refs
        q = q_ref[...] * (LANES ** -0.5)
        k = k_ref[...]
        v = v_ref[...]
        gl = _log_sigmoid(_dot_f32(a_ref[...], wd_ref[...]) + bd_ref[...]) / G_NORMALIZER
    return q, k, v, gl


def _level_factor(lvl, gl, cum, z_h2, sg_ref):
    h = CHUNK >> (lvl + 1)
    if h >= 4:
        c3 = cum.reshape(CHUNK // (2 * h), 2 * h, LANES)
        d = (c3 - c3[:, h - 1:h, :]).reshape(CHUNK, LANES)
        return jnp.exp(sg_ref[lvl] * d)
    if h == 2:
        return jnp.exp(z_h2)
    return jnp.exp(sg_ref[N_LEVELS - 2] * gl)


def _gla_prompt_kernel(*refs, hgrn, layer, hp, dv):
    n_in = 4 if hgrn else 6
    in_refs = refs[:n_in]
    g_ref, nw_ref, pm_ref, sg_ref, mk_ref, _, y_ref, so_ref, s_scr = refs[n_in:]
    c = pl.program_id(2)
    widths = (LANES, LANES, dv, LANES) if hgrn else (LANES, LANES, dv, None, LANES, LANES)

    @pl.when(c == 0)
    def _():
        s_scr[...] = jnp.zeros_like(s_scr)

    finals = []
    for u in range(hp):
        sub = [r if w is None else r.at[:, u * w:(u + 1) * w] for r, w in zip(in_refs, widths)]
        q, k, v, gl = _gla_inputs(sub, hgrn, layer)
        zz = _dot_exact_lhs(pm_ref[...], gl)
        z_h2, cum = zz[:CHUNK], zz[CHUNK:]
        att = mk_ref[N_LEVELS] * _dot_nt(q.astype(BF16), k.astype(BF16))
        for lvl in range(N_LEVELS):
            e = _level_factor(lvl, gl, cum, z_h2, sg_ref)
            att = att + mk_ref[lvl] * _dot_nt((q * e).astype(BF16), (k * e).astype(BF16))
        e_cum = jnp.exp(cum)
        e_tail = jnp.exp(cum[CHUNK - 1:CHUNK, :] - cum)
        vb = v.astype(BF16)
        s = s_scr[u]
        o = _dot(att.astype(BF16), vb) + _dot((q * e_cum).astype(BF16), s.astype(BF16))
        e_last = e_cum.T[:, CHUNK - 1:CHUNK]
        s_new = e_last * s + _dot_tn((k * e_tail).astype(BF16), vb)
        s_scr[u] = s_new
        finals.append(s_new)
        gate = g_ref[:, u * dv:(u + 1) * dv]
        y_ref[:, u * dv:(u + 1) * dv] = (_rms(o, nw_ref[...]) * _silu(gate)).astype(BF16)

    @pl.when(c == pl.num_programs(2) - 1)
    def _():
        for u in range(hp):
            so_ref[u] = finals[u]


def _gla_prompt(proj, offs, extra, nw, consts, y_init, *, hgrn, nseq, nchunks, heads, dv, layer, hp=4):
    pm, sg, mk = consts
    assert heads % hp == 0

    def col(off, w):
        assert off % (hp * w) == 0
        return pl.BlockSpec((CHUNK, hp * w), lambda h, b, c: (b * nchunks + c, off // (hp * w) + h))

    def full(a):
        nd = a.ndim
        return pl.BlockSpec(a.shape, lambda h, b, c: (0,) * nd)

    if hgrn:
        oq, of, oi, og = offs
        (lbl,) = extra
        in_specs = [col(oq, LANES), col(of, LANES), col(oi, dv),
                    pl.BlockSpec((lbl.shape[0], hp * LANES), lambda h, b, c: (0, h))]
        args = [proj, proj, proj, lbl]
    else:
        oq, ok, ov, og, oa = offs
        wd, bd = extra
        in_specs = [col(oq, LANES), col(ok, LANES), col(ov, dv),
                    pl.BlockSpec((CHUNK, LANES), lambda h, b, c: (b * nchunks + c, oa // LANES)),
                    pl.BlockSpec((LANES, hp * LANES), lambda h, b, c: (0, h)),
                    pl.BlockSpec((1, hp * LANES), lambda h, b, c: (0, h))]
        args = [proj, proj, proj, proj, wd, bd]
    in_specs += [col(og, dv), full(nw), full(pm), full(sg), full(mk), pl.BlockSpec(memory_space=pl.ANY)]
    args += [proj, nw, pm, sg, mk, y_init]
    y, s = pl.pallas_call(
        functools.partial(_gla_prompt_kernel, hgrn=hgrn, layer=layer, hp=hp, dv=dv),
        grid=(heads // hp, nseq, nchunks),
        in_specs=in_specs,
        out_specs=[pl.BlockSpec((CHUNK, hp * dv), lambda h, b, c: (b * nchunks + c, h)),
                   pl.BlockSpec((None, hp, LANES, dv), lambda h, b, c: (b, h, 0, 0))],
        out_shape=[jax.ShapeDtypeStruct(y_init.shape, BF16),
                   jax.ShapeDtypeStruct((nseq, heads, LANES, dv), F32)],
        input_output_aliases={len(args) - 1: 0},
        scratch_shapes=[pltpu.VMEM((hp, LANES, dv), F32)],
        compiler_params=_cparams(3),
        name="hgrn_prompt" if hgrn else "gla_prompt",
    )(*args)
    return y, s


def _columns(x):
    nb = x.shape[0]
    if nb < LANES:
        x = jnp.concatenate([x, jnp.zeros((LANES - nb, x.shape[1]), x.dtype)], axis=0)
    return x.T


def _store_state(so_ref, layer, b, s_new, first):
    if first:
        for l in range(so_ref.shape[0]):
            so_ref[l, b] = s_new if l == layer else jnp.zeros_like(s_new)
    else:
        so_ref[b] = s_new


def _gla_sample_kernel(*refs, hgrn, nb, layer, first):
    n_in = 4 if hgrn else 6
    in_refs = refs[:n_in]
    g_ref, nw_ref, s_ref = refs[n_in:n_in + 3]
    y_ref, so_ref = refs[-2:]
    q, k, v, gl = _gla_inputs(in_refs, hgrn, layer)
    e_t = _columns(jnp.exp(gl))
    k_t = _columns(k)
    qb = q.astype(BF16)
    rowid = lax.broadcasted_iota(jnp.int32, v.shape, 0)
    o = jnp.zeros(v.shape, F32)
    for b in range(nb):
        s_new = e_t[:, b:b + 1] * s_ref[b] + k_t[:, b:b + 1] * v[b:b + 1, :]
        _store_state(so_ref, layer, b, s_new, first)
        o = jnp.where(rowid == b, _dot(qb, s_new.astype(BF16)), o)
    y_ref[...] = (_rms(o, nw_ref[...]) * _silu(g_ref[...])).astype(BF16)


def _state_specs(depth, layer, nb, dv, first):
    ispec = pl.BlockSpec((None, nb, None, LANES, dv), lambda h, i: (layer, i, h, 0, 0))
    if first:
        ospec = pl.BlockSpec((depth, nb, None, LANES, dv), lambda h, i: (0, i, h, 0, 0))
    else:
        ospec = ispec
    return ispec, ospec


def _sample_batch_block(ns, dv):
    nb = min(ns, STATE_BLOCK_BYTES // (LANES * dv * 4))
    assert ns % nb == 0
    return nb


def _gla_sample(proj, state, prev, y_all, row0, offs, extra, nw, *, hgrn, heads, dv, layer):
    depth, ns = state.shape[:2]
    nb = _sample_batch_block(ns, dv)
    rb = row0 // nb
    first = prev is None

    def col(off, w):
        return pl.BlockSpec((nb, w), lambda h, i: (rb + i, off // w + h))

    if hgrn:
        oq, of, oi, og = offs
        (lbl,) = extra
        in_specs = [col(oq, LANES), col(of, LANES), col(oi, dv),
                    pl.BlockSpec((lbl.shape[0], LANES), lambda h, i: (0, h))]
        args = [proj, proj, proj, lbl]
    else:
        oq, ok, ov, og, oa = offs
        wd, bd = extra
        in_specs = [col(oq, LANES), col(ok, LANES), col(ov, dv),
                    pl.BlockSpec((nb, LANES), lambda h, i: (rb + i, oa // LANES)),
                    pl.BlockSpec((LANES, LANES), lambda h, i: (0, h)),
                    pl.BlockSpec((1, LANES), lambda h, i: (0, h))]
        args = [proj, proj, proj, proj, wd, bd]
    ispec, ospec = _state_specs(depth, layer, nb, dv, first)
    in_specs += [col(og, dv), pl.BlockSpec(nw.shape, lambda h, i: (0, 0)), ispec,
                 pl.BlockSpec(memory_space=pl.ANY)]
    args += [proj, nw, state, y_all]
    aliases = {len(args) - 1: 0}
    if not first:
        in_specs.append(pl.BlockSpec(memory_space=pl.ANY))
        args.append(prev)
        aliases[len(args) - 1] = 1
    y, s = pl.pallas_call(
        functools.partial(_gla_sample_kernel, hgrn=hgrn, nb=nb, layer=layer, first=first),
        grid=(heads, ns // nb),
        in_specs=in_specs,
        out_specs=[pl.BlockSpec((nb, dv), lambda h, i: (rb + i, h)), ospec],
        out_shape=[jax.ShapeDtypeStruct(y_all.shape, BF16),
                   jax.ShapeDtypeStruct(state.shape, F32)],
        input_output_aliases=aliases,
        compiler_params=_cparams(2),
        name="hgrn_sample" if hgrn else "gla_sample",
    )(*args)
    return y, s


def _mamba_post(y, xs, z, dskip, nw):
    y = (y + dskip * xs) * _silu(z)
    gs = y.shape[1] // M_GROUPS
    outs = [_rms(y[:, g * gs:(g + 1) * gs], nw[:, g * gs:(g + 1) * gs]) for g in range(M_GROUPS)]
    return jnp.concatenate(outs, axis=1)


def _mamba_prompt_kernel(z_ref, x_ref, bc_ref, sm_ref, cwx_ref, cwb_ref, cbx_ref, cbb_ref,
                         dtb_ref, alog_ref, dsk_ref, nw_ref, tril_ref, mask_ref, _,
                         y_ref, cox_ref, cob_ref, so_ref, ex_scr, eb_scr, s_scr):
    c = pl.program_id(1)
    nc = pl.num_programs(1)
    tail = SUBLANES

    @pl.when(c == 0)
    def _():
        ex_scr[0:tail, :] = jnp.zeros((tail, ex_scr.shape[1]), F32)
        eb_scr[0:tail, :] = jnp.zeros((tail, eb_scr.shape[1]), F32)
        s_scr[...] = jnp.zeros_like(s_scr)

    ex_scr[tail:tail + CHUNK, :] = x_ref[...]
    eb_scr[tail:tail + CHUNK, :] = bc_ref[...]

    def conv(scr, cw_ref, cb_ref):
        acc = cb_ref[...]
        for w in range(M_CONV):
            sh = M_CONV - 1 - w
            acc = acc + cw_ref[w:w + 1, :] * scr[tail - sh:tail - sh + CHUNK, :]
        return _silu(acc)

    xs = conv(ex_scr, cwx_ref, cbx_ref)
    bcm = conv(eb_scr, cwb_ref, cbb_ref)

    @pl.when(c == nc - 1)
    def _():
        cox_ref[...] = ex_scr[tail + CHUNK - (M_CONV - 1):tail + CHUNK, :]
        cob_ref[...] = eb_scr[tail + CHUNK - (M_CONV - 1):tail + CHUNK, :]

    ex_scr[0:tail, :] = ex_scr[CHUNK:CHUNK + tail, :]
    eb_scr[0:tail, :] = eb_scr[CHUNK:CHUNK + tail, :]

    dt = _softplus(sm_ref[...] + dtb_ref[...])
    a = -jnp.exp(alog_ref[...])
    cum = _dot_exact_lhs(tril_ref[...], dt * a)
    cum_t = cum.T
    cl = cum[CHUNK - 1:CHUNK, :]
    mask = mask_ref[...] > 0.5
    gw = M_GROUPS * M_STATE
    lo = lax.broadcasted_iota(jnp.int32, (CHUNK, LANES), 1) < M_HEADDIM
    lo_r = lax.broadcasted_iota(jnp.int32, (LANES, LANES), 0) < M_HEADDIM
    n_pairs = xs.shape[1] // LANES
    per_group = n_pairs // M_GROUPS
    ys = []
    for j in range(n_pairs):
        g = j // per_group
        h0, h1 = 2 * j, 2 * j + 1
        if j % per_group == 0:
            bg = bcm[:, g * M_STATE:(g + 1) * M_STATE].astype(BF16)
            cg = bcm[:, gw + g * M_STATE:gw + (g + 1) * M_STATE].astype(BF16)
            cb = jnp.where(mask, _dot_nt(cg, bg), 0.0)

        def dec(h):
            return jnp.exp(jnp.minimum(cum[:, h:h + 1] - cum_t[h:h + 1, :], 0.0))

        x2 = xs[:, j * LANES:(j + 1) * LANES]
        dt2 = jnp.where(lo, dt[:, h0:h0 + 1], dt[:, h1:h1 + 1])
        xdt = (x2 * dt2).astype(BF16)
        y_in = jnp.where(lo, _dot((cb * dec(h0)).astype(BF16), xdt),
                         _dot((cb * dec(h1)).astype(BF16), xdt))
        ec2 = jnp.where(lo, jnp.exp(cum[:, h0:h0 + 1]), jnp.exp(cum[:, h1:h1 + 1]))
        s = s_scr[j]
        ys.append(y_in + _dot_nt(cg, s.astype(BF16)) * ec2)
        w2 = dt2 * jnp.where(lo, jnp.exp(cl[:, h0:h0 + 1] - cum[:, h0:h0 + 1]),
                             jnp.exp(cl[:, h1:h1 + 1] - cum[:, h1:h1 + 1]))
        el2 = jnp.where(lo_r, jnp.exp(cl[:, h0:h0 + 1]), jnp.exp(cl[:, h1:h1 + 1]))
        s_scr[j] = el2 * s + _dot_tn((x2 * w2).astype(BF16), bg)
    y = jnp.concatenate(ys, axis=1)
    y_ref[...] = _mamba_post(y, xs, z_ref[...], dsk_ref[...], nw_ref[...]).astype(BF16)

    @pl.when(c == nc - 1)
    def _():
        so_ref[...] = s_scr[...]


def _mamba_prompt(proj, offs, p, consts, y_init, *, nseq, nchunks):
    oz, ox, obc, osm = offs
    mw = p["dskip"].shape[1]
    bcw = p["cwb"].shape[1]
    n_pairs = mw // LANES

    def col(off, w):
        return pl.BlockSpec((CHUNK, w), lambda b, c: (b * nchunks + c, off // w))

    def full(a):
        nd = a.ndim
        return pl.BlockSpec(a.shape, lambda b, c: (0,) * nd)

    small = [p["cwx"], p["cwb"], p["cbx"], p["cbb"], p["dtb"], p["alog"], p["dskip"], p["nw"],
             consts[0], consts[1]]
    y, cox, cob, s = pl.pallas_call(
        _mamba_prompt_kernel,
        grid=(nseq, nchunks),
        in_specs=[col(oz, mw), col(ox, mw), col(obc, bcw), col(osm, LANES)] + [full(a) for a in small]
        + [pl.BlockSpec(memory_space=pl.ANY)],
        out_specs=[pl.BlockSpec((CHUNK, mw), lambda b, c: (b * nchunks + c, 0)),
                   pl.BlockSpec((None, M_CONV - 1, mw), lambda b, c: (b, 0, 0)),
                   pl.BlockSpec((None, M_CONV - 1, bcw), lambda b, c: (b, 0, 0)),
                   pl.BlockSpec((None, n_pairs, LANES, M_STATE), lambda b, c: (b, 0, 0, 0))],
        input_output_aliases={4 + len(small): 0},
        out_shape=[jax.ShapeDtypeStruct(y_init.shape, BF16),
                   jax.ShapeDtypeStruct((nseq, M_CONV - 1, mw), F32),
                   jax.ShapeDtypeStruct((nseq, M_CONV - 1, bcw), F32),
                   jax.ShapeDtypeStruct((nseq, n_pairs, LANES, M_STATE), F32)],
        scratch_shapes=[pltpu.VMEM((CHUNK + SUBLANES, mw), F32), pltpu.VMEM((CHUNK + SUBLANES, bcw), F32),
                        pltpu.VMEM((n_pairs, LANES, M_STATE), F32)],
        compiler_params=_cparams(2),
        name="mamba_prompt",
    )(proj, proj, proj, proj, *small, y_init)
    return y, jnp.concatenate([cox, cob], axis=-1), s


def _mamba_sample_prep_kernel(cs_ref, x_ref, bc_ref, sm_ref, cw_ref, cb_ref, dtb_ref, alog_ref, exp_ref,
                              co_ref, act_ref, dte_ref, ee_ref):
    cd = cw_ref.shape[1]
    new = jnp.concatenate([x_ref[...], bc_ref[...]], axis=1)
    acc = cb_ref[...] + cw_ref[M_CONV - 1:M_CONV, :] * new
    for w in range(M_CONV - 1):
        acc = acc + cw_ref[w:w + 1, :] * cs_ref[:, w * cd:(w + 1) * cd]
    act_ref[...] = _silu(acc)
    for w in range(1, M_CONV - 1):
        co_ref[:, (w - 1) * cd:w * cd] = cs_ref[:, w * cd:(w + 1) * cd]
    co_ref[:, (M_CONV - 2) * cd:(M_CONV - 1) * cd] = new
    dt = _softplus(sm_ref[...] + dtb_ref[...])
    a = -jnp.exp(alog_ref[...])
    dte_ref[...] = _dot_exact_rhs(dt, exp_ref[...])
    ee_ref[...] = jnp.exp(_dot_exact_rhs(dt * a, exp_ref[...]))


def _mamba_sample_state_kernel(x_ref, b_ref, c_ref, dte_ref, ee_ref, s_ref, *out_refs, nb, layer, first):
    y_ref, so_ref = out_refs[-2:]
    xdt_t = _columns(x_ref[...] * dte_ref[...])
    e_t = _columns(ee_ref[...])
    bv = b_ref[...]
    cb = c_ref[...].astype(BF16)
    rowid = lax.broadcasted_iota(jnp.int32, (nb, LANES), 0)
    y = jnp.zeros((nb, LANES), F32)
    for b in range(nb):
        s_new = e_t[:, b:b + 1] * s_ref[b] + xdt_t[:, b:b + 1] * bv[b:b + 1, :]
        _store_state(so_ref, layer, b, s_new, first)
        y = jnp.where(rowid == b, _dot_nt(cb, s_new.astype(BF16)), y)
    y_ref[...] = y


def _mamba_sample_post_kernel(y_ref, x_ref, z_ref, dsk_ref, nw_ref, _, o_ref):
    o_ref[...] = _mamba_post(y_ref[...], x_ref[...], z_ref[...], dsk_ref[...], nw_ref[...]).astype(BF16)


def _mamba_sample(proj, conv_state, ssm_state, prev, y_all, row0, offs, p, expand, layer):
    oz, ox, obc, osm = offs
    depth, ns = ssm_state.shape[:2]
    nb = _sample_batch_block(ns, M_STATE)
    mw = p["dskip"].shape[1]
    bcw = p["cwb"].shape[1]
    cd = mw + bcw
    n_pairs = mw // LANES
    per_group = n_pairs // M_GROUPS
    rb = row0 // ns
    first = prev is None
    cw = jnp.concatenate([p["cwx"], p["cwb"]], axis=1)
    cb = jnp.concatenate([p["cbx"], p["cbb"]], axis=1)

    def full1(a):
        nd = a.ndim
        return pl.BlockSpec(a.shape, lambda i: (0,) * nd)

    cs2 = conv_state.reshape(ns, (M_CONV - 1) * cd)
    small = [cw, cb, p["dtb"], p["alog"], expand]
    co, act, dte, ee = pl.pallas_call(
        _mamba_sample_prep_kernel,
        grid=(1,),
        in_specs=[full1(cs2),
                  pl.BlockSpec((ns, mw), lambda i: (rb, ox // mw)),
                  pl.BlockSpec((ns, bcw), lambda i: (rb, obc // bcw)),
                  pl.BlockSpec((ns, LANES), lambda i: (rb, osm // LANES))] + [full1(a) for a in small],
        out_specs=[pl.BlockSpec((ns, (M_CONV - 1) * cd), lambda i: (0, 0)),
                   pl.BlockSpec((ns, cd), lambda i: (0, 0)),
                   pl.BlockSpec((ns, mw), lambda i: (0, 0)),
                   pl.BlockSpec((ns, mw), lambda i: (0, 0))],
        out_shape=[jax.ShapeDtypeStruct((ns, (M_CONV - 1) * cd), F32),
                   jax.ShapeDtypeStruct((ns, cd), F32),
                   jax.ShapeDtypeStruct((ns, mw), F32),
                   jax.ShapeDtypeStruct((ns, mw), F32)],
        compiler_params=_cparams(1),
        name="mamba_sample_prep",
    )(cs2, proj, proj, proj, *small)

    bblk = mw // LANES
    cblk = bblk + M_GROUPS * M_STATE // LANES
    ispec, ospec = _state_specs(depth, layer, nb, M_STATE, first)
    in_specs = [pl.BlockSpec((nb, LANES), lambda j, i: (i, j)),
                pl.BlockSpec((nb, LANES), lambda j, i: (i, bblk + j // per_group)),
                pl.BlockSpec((nb, LANES), lambda j, i: (i, cblk + j // per_group)),
                pl.BlockSpec((nb, LANES), lambda j, i: (i, j)),
                pl.BlockSpec((nb, LANES), lambda j, i: (i, j)),
                ispec]
    args = [act, act, act, dte, ee, ssm_state]
    aliases = {}
    if not first:
        in_specs.append(pl.BlockSpec(memory_space=pl.ANY))
        args.append(prev)
        aliases = {len(args) - 1: 1}
    y, s_new = pl.pallas_call(
        functools.partial(_mamba_sample_state_kernel, nb=nb, layer=layer, first=first),
        grid=(n_pairs, ns // nb),
        in_specs=in_specs,
        out_specs=[pl.BlockSpec((nb, LANES), lambda j, i: (i, j)), ospec],
        out_shape=[jax.ShapeDtypeStruct((ns, mw), F32),
                   jax.ShapeDtypeStruct(ssm_state.shape, F32)],
        input_output_aliases=aliases,
        compiler_params=_cparams(2),
        name="mamba_sample_state",
    )(*args)

    ym = pl.pallas_call(
        _mamba_sample_post_kernel,
        grid=(1,),
        in_specs=[full1(y),
                  pl.BlockSpec((ns, mw), lambda i: (0, 0)),
                  pl.BlockSpec((ns, mw), lambda i: (rb, oz // mw)),
                  full1(p["dskip"]), full1(p["nw"]), pl.BlockSpec(memory_space=pl.ANY)],
        out_specs=pl.BlockSpec((ns, mw), lambda i: (rb, 0)),
        out_shape=jax.ShapeDtypeStruct(y_all.shape, BF16),
        input_output_aliases={5: 0},
        compiler_params=_cparams(1),
        name="mamba_sample_post",
    )(y, act, proj, p["dskip"], p["nw"], y_all)
    return ym, co.reshape(ns, M_CONV - 1, cd), s_new


def _pad_lanes(v, n=LANES):
    v = v.reshape(1, -1)
    return jnp.pad(v, ((0, 0), (0, n - v.shape[1])))


def kernel(x_prompt, x_sample, state_conv, state_ssm, state_hgrn, state_gla, ffn1_norm, ffn1_w_gate_up, ffn1_w_down, mix_norm, w_in, conv_w, conv_b, dt_bias, a_log, d_skip, mamba_norm, hgrn_lb_logits, hgrn_norm, gla_w_decay, gla_b_decay, gla_norm, w_branch_mamba, w_branch_hgrn, w_branch_gla, w_out, ffn2_norm, ffn2_w_gate_up, ffn2_w_down, final_norm):
    nseq, seq, d = x_prompt.shape
    ns = x_sample.shape[0]
    depth = w_in.shape[0]
    nchunks = seq // CHUNK
    n_prompt = nseq * seq
    mw = w_branch_mamba.shape[1]
    hw = w_branch_hgrn.shape[1]
    gw = w_branch_gla.shape[1]
    gk = gla_w_decay.shape[2]
    m_heads = dt_bias.shape[1]
    h_heads = state_hgrn.shape[2]
    g_heads = state_gla.shape[2]
    g_dv = state_gla.shape[4]
    bcw = 2 * M_GROUPS * M_STATE
    n_pairs = mw // LANES
    assert seq % CHUNK == 0 and n_prompt % ns == 0 and ns % 16 == 0
    assert m_heads <= G_RANK + m_heads <= LANES and gk // g_heads == LANES and hw // h_heads == LANES

    seg_w = {"z": mw, "xs": mw, "bc": bcw, "dt": m_heads, "hq": hw, "hf": hw, "hi": hw, "hg": hw,
             "gq": gk, "gk": gk, "gv": gw, "gg": gw, "ga": G_RANK, "gate": N_BRANCH * d}
    src_order = ("z", "xs", "bc", "dt", "hq", "hf", "hi", "hg", "gq", "gk", "gv", "gg", "ga", "gate")
    dst_order = ("z", "xs", "hq", "hf", "hi", "hg", "gv", "gg", "gq", "gk", "bc", "gate", "dt", "ga")
    src, off = {}, {}
    pos = 0
    for name in src_order:
        src[name] = pos
        pos += seg_w[name]
    pos = 0
    for name in dst_order:
        off[name] = pos
        pos += seg_w[name]
    n_used = pos
    n_cols = -(-n_used // 1024) * 1024
    copies = tuple((src[name], off[name], seg_w[name]) for name in dst_order)
    oz, ox, obc, osm = off["z"], off["xs"], off["bc"], off["dt"]
    assert off["ga"] == osm + m_heads and osm % LANES == 0

    consts = _scan_constants()
    tril = consts[0][CHUNK:]
    mconsts = (tril, jnp.asarray(np.tril(np.ones((CHUNK, CHUNK), np.float32))))
    expand_np = np.zeros((LANES, mw), np.float32)
    for h in range(m_heads):
        expand_np[h, h * M_HEADDIM:(h + 1) * M_HEADDIM] = 1.0
    expand = jnp.asarray(expand_np, BF16)

    x = jnp.concatenate([x_prompt.reshape(n_prompt, d), x_sample.reshape(ns, d)], axis=0)
    t_all = n_prompt + ns
    ssm5 = state_ssm.reshape(depth, ns, n_pairs, LANES, M_STATE)

    pc, ps, ph, pg, sc = [], [], [], [], []
    ss = sh = sg = None
    for l in range(depth):
        w_perm = _pack_w_in(w_in, l, copies, n_used, n_cols)
        mp = {
            "cwx": conv_w[l][:, :mw], "cwb": conv_w[l][:, mw:],
            "cbx": conv_b[l][:mw].reshape(1, mw), "cbb": conv_b[l][mw:].reshape(1, bcw),
            "dtb": _pad_lanes(dt_bias[l]), "alog": _pad_lanes(a_log[l]),
            "dskip": jnp.repeat(d_skip[l], M_HEADDIM).reshape(1, mw),
            "nw": mamba_norm[l].reshape(1, mw),
        }
        wd = jnp.zeros((LANES, gk), F32).at[m_heads:m_heads + G_RANK].set(gla_w_decay[l])
        bd = gla_b_decay[l].reshape(1, gk)
        hnw = hgrn_norm[l].reshape(1, LANES)
        gnw = gla_norm[l].reshape(1, g_dv)

        act = _ffn_up(x, ffn1_norm[l], ffn1_w_gate_up, l)
        x = _resid_matmul(act, ffn1_w_down, l, x, 0.5, 256, "ffn_down")
        proj = _in_proj(x, mix_norm[l], w_perm)

        moffs = (oz, ox, obc, osm)
        hoffs = (off["hq"], off["hf"], off["hi"], off["hg"])
        goffs = (off["gq"], off["gk"], off["gv"], off["gg"], osm)
        ym, c1, s1 = _mamba_prompt(proj, moffs, mp, mconsts, jnp.zeros((t_all, mw), BF16),
                                   nseq=nseq, nchunks=nchunks)
        yh, h1 = _gla_prompt(proj, hoffs, (hgrn_lb_logits,), hnw, consts, jnp.zeros((t_all, hw), BF16),
                             hgrn=True, nseq=nseq, nchunks=nchunks, heads=h_heads, dv=LANES, layer=l)
        yg, g1 = _gla_prompt(proj, goffs, (wd, bd), gnw, consts, jnp.zeros((t_all, gw), BF16),
                             hgrn=False, nseq=nseq, nchunks=nchunks, heads=g_heads, dv=g_dv, layer=l)
        ym, c2, ss = _mamba_sample(proj, state_conv[l], ssm5, ss, ym, n_prompt, moffs, mp, expand, l)
        yh, sh = _gla_sample(proj, state_hgrn, sh, yh, n_prompt, hoffs, (hgrn_lb_logits,), hnw,
                             hgrn=True, heads=h_heads, dv=LANES, layer=l)
        yg, sg = _gla_sample(proj, state_gla, sg, yg, n_prompt, goffs, (wd, bd), gnw,
                             hgrn=False, heads=g_heads, dv=g_dv, layer=l)
        merged = _merge(ym, yh, yg, w_branch_mamba, w_branch_hgrn, w_branch_gla, l, proj, off["gate"])
        x = _resid_matmul(merged, w_out, l, x, 1.0, 512, "out_proj")

        act = _ffn_up(x, ffn2_norm[l], ffn2_w_gate_up, l)
        x = _resid_matmul(act, ffn2_w_down, l, x, 0.5, 256, "ffn_down")

        pc.append(c1)
        ps.append(s1.reshape(nseq, m_heads, M_HEADDIM, M_STATE))
        ph.append(h1)
        pg.append(g1)
        sc.append(c2)

    y_prompt = _final_norm(x, final_norm, 0, n_prompt, "final_norm_prompt").reshape(nseq, seq, d)
    y_sample = _final_norm(x, final_norm, n_prompt, ns, "final_norm_sample").reshape(ns, 1, d)
    return (y_prompt, y_sample, jnp.stack(pc), jnp.stack(ps), jnp.stack(ph), jnp.stack(pg),
            jnp.stack(sc), ss.reshape(state_ssm.shape), sh, sg)
```

```python
import functools

import jax
import jax.numpy as jnp
import numpy as np
from jax import lax
from jax.experimental import pallas as pl
from jax.experimental.pallas import tpu as pltpu

F32 = jnp.float32
BF16 = jnp.bfloat16
EPS = 1e-6
F32_TINY = float(np.finfo(np.float32).tiny)

CHUNK = 256
LANES = 128
SUBLANES = 8
M_HEADDIM = 64
M_STATE = 128
M_GROUPS = 2
M_CONV = 4
G_RANK = 16
G_NORMALIZER = 16.0
N_BRANCH = 3
N_LEVELS = int(np.log2(CHUNK))
VMEM_LIMIT = 56 * 1024 * 1024
STATE_BLOCK_BYTES = 4 * 1024 * 1024


def _cparams(n_axes):
    return pltpu.CompilerParams(dimension_semantics=("arbitrary",) * n_axes,
                                vmem_limit_bytes=VMEM_LIMIT)


def _dot(a, b):
    return jnp.dot(a, b, preferred_element_type=F32)


def _dot_nt(a, b):
    return lax.dot_general(a, b, (((1,), (1,)), ((), ())), preferred_element_type=F32)


def _dot_tn(a, b):
    return lax.dot_general(a, b, (((0,), (0,)), ((), ())), preferred_element_type=F32)


def _split3(x):
    hi = x.astype(BF16)
    r = x - hi.astype(F32)
    mid = r.astype(BF16)
    lo = (r - mid.astype(F32)).astype(BF16)
    return hi, mid, lo


def _dot_exact_lhs(p_bf16, x):
    hi, mid, lo = _split3(x)
    return _dot(p_bf16, hi) + _dot(p_bf16, mid) + _dot(p_bf16, lo)


def _dot_exact_rhs(x, p_bf16):
    hi, mid, lo = _split3(x)
    return _dot(hi, p_bf16) + _dot(mid, p_bf16) + _dot(lo, p_bf16)


def _dot_f32(a, b):
    ah, am, _ = _split3(a)
    bh, bm, _ = _split3(b)
    return _dot(ah, bh) + _dot(ah, bm) + _dot(am, bh)


def _sigmoid(x):
    return jax.nn.sigmoid(x)


def _silu(x):
    return x * _sigmoid(x)


def _softplus(x):
    return jnp.maximum(x, 0.0) + jnp.log1p(jnp.exp(-jnp.abs(x)))


def _log_sigmoid(x):
    return jnp.minimum(x, 0.0) - jnp.log1p(jnp.exp(-jnp.abs(x)))


def _rms(x, w):
    ms = jnp.mean(x * x, axis=-1, keepdims=True)
    return x * lax.rsqrt(ms + EPS) * w


def _pick_tm(t, cap=1100):
    best = 16
    for tm in range(16, min(t, cap) + 1, 16):
        if t % tm == 0:
            best = tm
    return best


def _ffn_up_kernel(x_ref, nw_ref, wg_ref, wu_ref, o_ref, h_ref):
    @pl.when(pl.program_id(1) == 0)
    def _():
        h_ref[...] = _rms(x_ref[...], nw_ref[...]).astype(BF16)

    h = h_ref[...]
    g = _dot(h, wg_ref[...].astype(BF16))
    u = _dot(h, wu_ref[...].astype(BF16))
    o_ref[...] = (_silu(g) * u).astype(BF16)


def _ffn_up(x, nw, w_gu, layer, tn=512):
    t, d = x.shape
    dff = w_gu.shape[2] // 2
    tm = _pick_tm(t)
    nj = dff // tn
    return pl.pallas_call(
        _ffn_up_kernel,
        grid=(t // tm, nj),
        in_specs=[pl.BlockSpec((tm, d), lambda i, j: (i, 0)),
                  pl.BlockSpec((1, d), lambda i, j: (0, 0)),
                  pl.BlockSpec((None, d, tn), lambda i, j: (layer, 0, j)),
                  pl.BlockSpec((None, d, tn), lambda i, j: (layer, 0, j + nj))],
        out_specs=pl.BlockSpec((tm, tn), lambda i, j: (i, j)),
        out_shape=jax.ShapeDtypeStruct((t, dff), BF16),
        scratch_shapes=[pltpu.VMEM((tm, d), BF16)],
        compiler_params=_cparams(2),
        name="ffn_up",
    )(x, nw.reshape(1, d), w_gu, w_gu)


def _resid_matmul_kernel(a_ref, w_ref, x_ref, o_ref, *, scale):
    o_ref[...] = x_ref[...] + scale * _dot(a_ref[...], w_ref[...].astype(BF16))


def _resid_matmul(a, w, layer, x, scale, tn, name):
    t, k = a.shape
    d = w.shape[2]
    tm = _pick_tm(t)
    return pl.pallas_call(
        functools.partial(_resid_matmul_kernel, scale=scale),
        grid=(t // tm, d // tn),
        in_specs=[pl.BlockSpec((tm, k), lambda i, j: (i, 0)),
                  pl.BlockSpec((None, k, tn), lambda i, j: (layer, 0, j)),
                  pl.BlockSpec((tm, tn), lambda i, j: (i, j))],
        out_specs=pl.BlockSpec((tm, tn), lambda i, j: (i, j)),
        out_shape=jax.ShapeDtypeStruct((t, d), F32),
        compiler_params=_cparams(2),
        name=name,
    )(a, w, x)


def _pack_w_in_kernel(w_ref, o_ref, *, copies, n_used):
    for src, dst, width in copies:
        if width % LANES == 0:
            o_ref[:, dst:dst + width] = w_ref[src:src + width, :].T.astype(BF16)
        else:
            base = src // LANES * LANES
            lo = src - base
            assert dst % LANES == lo and lo + width <= LANES
            t = w_ref[base:base + LANES, :].T
            o_ref[:, dst:dst + width] = t[:, lo:lo + width].astype(BF16)
    pad = o_ref.shape[1] - n_used
    if pad:
        o_ref[:, n_used:] = jnp.zeros((o_ref.shape[0], pad), BF16)


def _pack_w_in(w_t, layer, copies, n_used, n_cols, tk=128):
    _, n_src, d = w_t.shape
    return pl.pallas_call(
        functools.partial(_pack_w_in_kernel, copies=copies, n_used=n_used),
        grid=(d // tk,),
        in_specs=[pl.BlockSpec((None, n_src, tk), lambda i: (layer, 0, i))],
        out_specs=pl.BlockSpec((tk, n_cols), lambda i: (i, 0)),
        out_shape=jax.ShapeDtypeStruct((d, n_cols), BF16),
        compiler_params=_cparams(1),
        name="pack_w_in",
    )(w_t)


def _in_proj_kernel(x_ref, nw_ref, w_ref, o_ref, h_ref):
    @pl.when(pl.program_id(1) == 0)
    def _():
        h_ref[...] = _rms(x_ref[...], nw_ref[...]).astype(BF16)

    o_ref[...] = _dot(h_ref[...], w_ref[...])


def _in_proj(x, nw, w, tn=1024):
    t, d = x.shape
    n = w.shape[1]
    tm = _pick_tm(t)
    return pl.pallas_call(
        _in_proj_kernel,
        grid=(t // tm, n // tn),
        in_specs=[pl.BlockSpec((tm, d), lambda i, j: (i, 0)),
                  pl.BlockSpec((1, d), lambda i, j: (0, 0)),
                  pl.BlockSpec((d, tn), lambda i, j: (0, j))],
        out_specs=pl.BlockSpec((tm, tn), lambda i, j: (i, j)),
        out_shape=jax.ShapeDtypeStruct((t, n), F32),
        scratch_shapes=[pltpu.VMEM((tm, d), BF16)],
        compiler_params=_cparams(2),
        name="in_proj",
    )(x, nw.reshape(1, d), w)


def _merge_kernel(ym_ref, yh_ref, yg_ref, wm_ref, wh_ref, wg_ref, g0_ref, g1_ref, g2_ref, o_ref):
    acc = _sigmoid(g0_ref[...]) * _dot(ym_ref[...], wm_ref[...].astype(BF16))
    acc = acc + _sigmoid(g1_ref[...]) * _dot(yh_ref[...], wh_ref[...].astype(BF16))
    acc = acc + _sigmoid(g2_ref[...]) * _dot(yg_ref[...], wg_ref[...].astype(BF16))
    o_ref[...] = acc.astype(BF16)


def _merge(ym, yh, yg, wm, wh, wg, layer, proj, gate_off, tn=512):
    t, k = ym.shape
    d = wm.shape[2]
    tm = _pick_tm(t)
    gb = gate_off // tn
    nb = d // tn
    yspec = pl.BlockSpec((tm, k), lambda i, j: (i, 0))
    wspec = pl.BlockSpec((None, k, tn), lambda i, j: (layer, 0, j))

    def gspec(b):
        return pl.BlockSpec((tm, tn), lambda i, j: (i, gb + b * nb + j))

    return pl.pallas_call(
        _merge_kernel,
        grid=(t // tm, nb),
        in_specs=[yspec, yspec, yspec, wspec, wspec, wspec, gspec(0), gspec(1), gspec(2)],
        out_specs=pl.BlockSpec((tm, tn), lambda i, j: (i, j)),
        out_shape=jax.ShapeDtypeStruct((t, d), BF16),
        compiler_params=_cparams(2),
        name="merge",
    )(ym, yh, yg, wm, wh, wg, proj, proj, proj)


def _final_norm_kernel(x_ref, nw_ref, o_ref):
    o_ref[...] = _rms(x_ref[...], nw_ref[...])


def _final_norm(x, nw, row0, rows, name):
    d = x.shape[1]
    tm = _pick_tm(rows)
    assert row0 % tm == 0
    return pl.pallas_call(
        _final_norm_kernel,
        grid=(rows // tm,),
        in_specs=[pl.BlockSpec((tm, d), lambda i: (row0 // tm + i, 0)),
                  pl.BlockSpec((1, d), lambda i: (0, 0))],
        out_specs=pl.BlockSpec((tm, d), lambda i: (i, 0)),
        out_shape=jax.ShapeDtypeStruct((rows, d), F32),
        compiler_params=_cparams(1),
        name=name,
    )(x, nw.reshape(1, d))


def _scan_constants():
    c = CHUNK
    pm = np.zeros((2, c, c), np.float32)
    sg = np.zeros((N_LEVELS - 1, c, LANES), np.float32)
    mk = np.zeros((N_LEVELS + 1, c, c), np.float32)
    r = np.arange(c)
    for lvl in range(N_LEVELS):
        h = c >> (lvl + 1)
        for t in range(c):
            blk, pos = divmod(t, 2 * h)
            ridx = blk * 2 * h + h - 1
            upper = pos >= h
            if upper:
                mk[lvl, t] = ((r // (2 * h)) == blk) & ((r % (2 * h)) < h)
            if h >= 4:
                sg[lvl, t] = 1.0 if upper else -1.0
            elif h == 2:
                pm[0, t] = ((r > ridx) & (r <= t)) if upper else ((r > t) & (r <= ridx))
            else:
                sg[N_LEVELS - 2, t] = 1.0 if upper else 0.0
    pm[1] = r[None, :] <= r[:, None]
    mk[N_LEVELS] = np.eye(c)
    return jnp.asarray(pm.reshape(2 * c, c), BF16), jnp.asarray(sg), jnp.asarray(mk)


def _lower_bound(logits, layer):
    m = jnp.max(logits, axis=0, keepdims=True)
    e = jnp.exp(logits - m)
    sm = e / jnp.sum(e, axis=0, keepdims=True)
    lb = jnp.zeros_like(m)
    for i in range(1, layer + 1):
        lb = lb + sm[i:i + 1, :]
    return lb


def _gla_inputs(refs, hgrn, layer):
    if hgrn:
        q_ref, f_ref, i_ref, lbl_ref = refs
        lb = _lower_bound(lbl_ref[...], layer)
        q = _silu(q_ref[...]) * (LANES ** -0.5)
        zf = f_ref[...]
        f = lb + (1.0 - lb) * _sigmoid(zf)
        gl = jnp.log(jnp.maximum(f, F32_TINY))
        k = (1.0 - lb) * _sigmoid(-zf)
        v = i_ref[...]
    else:
        q_ref, k_ref, v_ref, a_ref, wd_ref, bd_ref = refs
        q = q_ref[...] * (LANES ** -0.5)
        k = k_ref[...]
        v = v_ref[...]
        gl = _log_sigmoid(_dot_f32(a_ref[...], wd_ref[...]) + bd_ref[...]) / G_NORMALIZER
    return q, k, v, gl


def _level_factor(lvl, gl, cum, z_h2, sg_ref):
    h = CHUNK >> (lvl + 1)
    if h >= 4:
        c3 = cum.reshape(CHUNK // (2 * h), 2 * h, LANES)
        d = (c3 - c3[:, h - 1:h, :]).reshape(CHUNK, LANES)
        return jnp.exp(sg_ref[lvl] * d)
    if h == 2:
        return jnp.exp(z_h2)
    return jnp.exp(sg_ref[N_LEVELS - 2] * gl)


def _gla_prompt_kernel(*refs, hgrn, layer, hp, dv):
    n_in = 4 if hgrn else 6
    in_refs = refs[:n_in]
    g_ref, nw_ref, pm_ref, sg_ref, mk_ref, _, y_ref, so_ref, s_scr = refs[n_in:]
    c = pl.program_id(2)
    widths = (LANES, LANES, dv, LANES) if hgrn else (LANES, LANES, dv, None, LANES, LANES)

    @pl.when(c == 0)
    def _():
        s_scr[...] = jnp.zeros_like(s_scr)

    finals = []
    for u in range(hp):
        sub = [r if w is None else r.at[:, u * w:(u + 1) * w] for r, w in zip(in_refs, widths)]
        q, k, v, gl = _gla_inputs(sub, hgrn, layer)
        zz = _dot_exact_lhs(pm_ref[...], gl)
        z_h2, cum = zz[:CHUNK], zz[CHUNK:]
        qb, kb = q.astype(BF16), k.astype(BF16)
        att = mk_ref[N_LEVELS] * _dot_nt(qb, kb)
        for lvl in range(N_LEVELS):
            eb = _level_factor(lvl, gl, cum, z_h2, sg_ref).astype(BF16)
            att = att + mk_ref[lvl] * _dot_nt(qb * eb, kb * eb)
        e_cum = jnp.exp(cum)
        e_tail = jnp.exp(cum[CHUNK - 1:CHUNK, :] - cum)
        vb = v.astype(BF16)
        s = s_scr[u]
        o = _dot(att.astype(BF16), vb) + _dot((q * e_cum).astype(BF16), s.astype(BF16))
        e_last = e_cum.T[:, CHUNK - 1:CHUNK]
        s_new = e_last * s + _dot_tn((k * e_tail).astype(BF16), vb)
        s_scr[u] = s_new
        finals.append(s_new)
        gate = g_ref[:, u * dv:(u + 1) * dv]
        y_ref[:, u * dv:(u + 1) * dv] = (_rms(o, nw_ref[...]) * _silu(gate)).astype(BF16)

    @pl.when(c == pl.num_programs(2) - 1)
    def _():
        for u in range(hp):
            so_ref[u] = finals[u]


def _gla_prompt(proj, offs, extra, nw, consts, y_init, *, hgrn, nseq, nchunks, heads, dv, layer, hp=4):
    pm, sg, mk = consts
    assert heads % hp == 0

    def col(off, w):
        assert off % (hp * w) == 0
        return pl.BlockSpec((CHUNK, hp * w), lambda h, b, c: (b * nchunks + c, off // (hp * w) + h))

    def full(a):
        nd = a.ndim
        return pl.BlockSpec(a.shape, lambda h, b, c: (0,) * nd)

    if hgrn:
        oq, of, oi, og = offs
        (lbl,) = extra
        in_specs = [col(oq, LANES), col(of, LANES), col(oi, dv),
                    pl.BlockSpec((lbl.shape[0], hp * LANES), lambda h, b, c: (0, h))]
        args = [proj, proj, proj, lbl]
    else:
        oq, ok, ov, og, oa = offs
        wd, bd = extra
        in_specs = [col(oq, LANES), col(ok, LANES), col(ov, dv),
                    pl.BlockSpec((CHUNK, LANES), lambda h, b, c: (b * nchunks + c, oa // LANES)),
                    pl.BlockSpec((LANES, hp * LANES), lambda h, b, c: (0, h)),
                    pl.BlockSpec((1, hp * LANES), lambda h, b, c: (0, h))]
        args = [proj, proj, proj, proj, wd, bd]
    in_specs += [col(og, dv), full(nw), full(pm), full(sg), full(mk), pl.BlockSpec(memory_space=pl.ANY)]
    args += [proj, nw, pm, sg, mk, y_init]
    y, s = pl.pallas_call(
        functools.partial(_gla_prompt_kernel, hgrn=hgrn, layer=layer, hp=hp, dv=dv),
        grid=(heads // hp, nseq, nchunks),
        in_specs=in_specs,
        out_specs=[pl.BlockSpec((CHUNK, hp * dv), lambda h, b, c: (b * nchunks + c, h)),
                   pl.BlockSpec((None, hp, LANES, dv), lambda h, b, c: (b, h, 0, 0))],
        out_shape=[jax.ShapeDtypeStruct(y_init.shape, BF16),
                   jax.ShapeDtypeStruct((nseq, heads, LANES, dv), F32)],
        input_output_aliases={len(args) - 1: 0},
        scratch_shapes=[pltpu.VMEM((hp, LANES, dv), F32)],
        compiler_params=_cparams(3),
        name="hgrn_prompt" if hgrn else "gla_prompt",
    )(*args)
    return y, s


def _columns(x):
    nb = x.shape[0]
    if nb < LANES:
        x = jnp.concatenate([x, jnp.zeros((LANES - nb, x.shape[1]), x.dtype)], axis=0)
    return x.T


def _store_state(so_ref, layer, b, s_new, first):
    if first:
        for l in range(so_ref.shape[0]):
            so_ref[l, b] = s_new if l == layer else jnp.zeros_like(s_new)
    else:
        so_ref[b] = s_new


def _gla_sample_kernel(*refs, hgrn, nb, layer, first):
    n_in = 4 if hgrn else 6
    in_refs = refs[:n_in]
    g_ref, nw_ref, s_ref = refs[n_in:n_in + 3]
    y_ref, so_ref = refs[-2:]
    q, k, v, gl = _gla_inputs(in_refs, hgrn, layer)
    e_t = _columns(jnp.exp(gl))
    k_t = _columns(k)
    qb = q.astype(BF16)
    rowid = lax.broadcasted_iota(jnp.int32, v.shape, 0)
    o = jnp.zeros(v.shape, F32)
    for b in range(nb):
        s_new = e_t[:, b:b + 1] * s_ref[b] + k_t[:, b:b + 1] * v[b:b + 1, :]
        _store_state(so_ref, layer, b, s_new, first)
        o = jnp.where(rowid == b, _dot(qb, s_new.astype(BF16)), o)
    y_ref[...] = (_rms(o, nw_ref[...]) * _silu(g_ref[...])).astype(BF16)


def _state_specs(depth, layer, nb, dv, first):
    ispec = pl.BlockSpec((None, nb, None, LANES, dv), lambda h, i: (layer, i, h, 0, 0))
    if first:
        ospec = pl.BlockSpec((depth, nb, None, LANES, dv), lambda h, i: (0, i, h, 0, 0))
    else:
        ospec = ispec
    return ispec, ospec


def _sample_batch_block(ns, dv):
    nb = min(ns, STATE_BLOCK_BYTES // (LANES * dv * 4))
    assert ns % nb == 0
    return nb


def _gla_sample(proj, state, prev, y_all, row0, offs, extra, nw, *, hgrn, heads, dv, layer):
    depth, ns = state.shape[:2]
    nb = _sample_batch_block(ns, dv)
    rb = row0 // nb
    first = prev is None

    def col(off, w):
        return pl.BlockSpec((nb, w), lambda h, i: (rb + i, off // w + h))

    if hgrn:
        oq, of, oi, og = offs
        (lbl,) = extra
        in_specs = [col(oq, LANES), col(of, LANES), col(oi, dv),
                    pl.BlockSpec((lbl.shape[0], LANES), lambda h, i: (0, h))]
        args = [proj, proj, proj, lbl]
    else:
        oq, ok, ov, og, oa = offs
        wd, bd = extra
        in_specs = [col(oq, LANES), col(ok, LANES), col(ov, dv),
                    pl.BlockSpec((nb, LANES), lambda h, i: (rb + i, oa // LANES)),
                    pl.BlockSpec((LANES, LANES), lambda h, i: (0, h)),
                    pl.BlockSpec((1, LANES), lambda h, i: (0, h))]
        args = [proj, proj, proj, proj, wd, bd]
    ispec, ospec = _state_specs(depth, layer, nb, dv, first)
    in_specs += [col(og, dv), pl.BlockSpec(nw.shape, lambda h, i: (0, 0)), ispec,
                 pl.BlockSpec(memory_space=pl.ANY)]
    args += [proj, nw, state, y_all]
    aliases = {len(args) - 1: 0}
    if not first:
        in_specs.append(pl.BlockSpec(memory_space=pl.ANY))
        args.append(prev)
        aliases[len(args) - 1] = 1
    y, s = pl.pallas_call(
        functools.partial(_gla_sample_kernel, hgrn=hgrn, nb=nb, layer=layer, first=first),
        grid=(heads, ns // nb),
        in_specs=in_specs,
        out_specs=[pl.BlockSpec((nb, dv), lambda h, i: (rb + i, h)), ospec],
        out_shape=[jax.ShapeDtypeStruct(y_all.shape, BF16),
                   jax.ShapeDtypeStruct(state.shape, F32)],
        input_output_aliases=aliases,
        compiler_params=_cparams(2),
        name="hgrn_sample" if hgrn else "gla_sample",
    )(*args)
    return y, s


def _mamba_post(y, xs, z, dskip, nw):
    y = (y + dskip * xs) * _silu(z)
    gs = y.shape[1] // M_GROUPS
    outs = [_rms(y[:, g * gs:(g + 1) * gs], nw[:, g * gs:(g + 1) * gs]) for g in range(M_GROUPS)]
    return jnp.concatenate(outs, axis=1)


def _mamba_prompt_kernel(z_ref, x_ref, bc_ref, sm_ref, cwx_ref, cwb_ref, cbx_ref, cbb_ref,
                         dtb_ref, alog_ref, dsk_ref, nw_ref, tril_ref, mask_ref, _,
                         y_ref, cox_ref, cob_ref, so_ref, ex_scr, eb_scr, s_scr):
    c = pl.program_id(1)
    nc = pl.num_programs(1)
    tail = SUBLANES

    @pl.when(c == 0)
    def _():
        ex_scr[0:tail, :] = jnp.zeros((tail, ex_scr.shape[1]), F32)
        eb_scr[0:tail, :] = jnp.zeros((tail, eb_scr.shape[1]), F32)
        s_scr[...] = jnp.zeros_like(s_scr)

    ex_scr[tail:tail + CHUNK, :] = x_ref[...]
    eb_scr[tail:tail + CHUNK, :] = bc_ref[...]

    def conv(scr, cw_ref, cb_ref):
        acc = cb_ref[...]
        for w in range(M_CONV):
            sh = M_CONV - 1 - w
            acc = acc + cw_ref[w:w + 1, :] * scr[tail - sh:tail - sh + CHUNK, :]
        return _silu(acc)

    xs = conv(ex_scr, cwx_ref, cbx_ref)
    bcm = conv(eb_scr, cwb_ref, cbb_ref)

    @pl.when(c == nc - 1)
    def _():
        cox_ref[...] = ex_scr[tail + CHUNK - (M_CONV - 1):tail + CHUNK, :]
        cob_ref[...] = eb_scr[tail + CHUNK - (M_CONV - 1):tail + CHUNK, :]

    ex_scr[0:tail, :] = ex_scr[CHUNK:CHUNK + tail, :]
    eb_scr[0:tail, :] = eb_scr[CHUNK:CHUNK + tail, :]

    dt = _softplus(sm_ref[...] + dtb_ref[...])
    a = -jnp.exp(alog_ref[...])
    cum = _dot_exact_lhs(tril_ref[...], dt * a)
    cum_t = cum.T
    cl = cum[CHUNK - 1:CHUNK, :]
    mask = mask_ref[...] > 0.5
    gw = M_GROUPS * M_STATE
    lo = lax.broadcasted_iota(jnp.int32, (CHUNK, LANES), 1) < M_HEADDIM
    lo_r = lax.broadcasted_iota(jnp.int32, (LANES, LANES), 0) < M_HEADDIM
    n_pairs = xs.shape[1] // LANES
    per_group = n_pairs // M_GROUPS
    ys = []
    for j in range(n_pairs):
        g = j // per_group
        h0, h1 = 2 * j, 2 * j + 1
        if j % per_group == 0:
            bg = bcm[:, g * M_STATE:(g + 1) * M_STATE].astype(BF16)
            cg = bcm[:, gw + g * M_STATE:gw + (g + 1) * M_STATE].astype(BF16)
            cb = jnp.where(mask, _dot_nt(cg, bg), 0.0)

        def dec(h):
            return jnp.exp(jnp.minimum(cum[:, h:h + 1] - cum_t[h:h + 1, :], 0.0))

        x2 = xs[:, j * LANES:(j + 1) * LANES]
        dt2 = jnp.where(lo, dt[:, h0:h0 + 1], dt[:, h1:h1 + 1])
        xdt = (x2 * dt2).astype(BF16)
        y_in = jnp.where(lo, _dot((cb * dec(h0)).astype(BF16), xdt),
                         _dot((cb * dec(h1)).astype(BF16), xdt))
        ec2 = jnp.where(lo, jnp.exp(cum[:, h0:h0 + 1]), jnp.exp(cum[:, h1:h1 + 1]))
        s = s_scr[j]
        ys.append(y_in + _dot_nt(cg, s.astype(BF16)) * ec2)
        w2 = dt2 * jnp.where(lo, jnp.exp(cl[:, h0:h0 + 1] - cum[:, h0:h0 + 1]),
                             jnp.exp(cl[:, h1:h1 + 1] - cum[:, h1:h1 + 1]))
        el2 = jnp.where(lo_r, jnp.exp(cl[:, h0:h0 + 1]), jnp.exp(cl[:, h1:h1 + 1]))
        s_scr[j] = el2 * s + _dot_tn((x2 * w2).astype(BF16), bg)
    y = jnp.concatenate(ys, axis=1)
    y_ref[...] = _mamba_post(y, xs, z_ref[...], dsk_ref[...], nw_ref[...]).astype(BF16)

    @pl.when(c == nc - 1)
    def _():
        so_ref[...] = s_scr[...]


def _mamba_prompt(proj, offs, p, consts, y_init, *, nseq, nchunks):
    oz, ox, obc, osm = offs
    mw = p["dskip"].shape[1]
    bcw = p["cwb"].shape[1]
    n_pairs = mw // LANES

    def col(off, w):
        return pl.BlockSpec((CHUNK, w), lambda b, c: (b * nchunks + c, off // w))

    def full(a):
        nd = a.ndim
        return pl.BlockSpec(a.shape, lambda b, c: (0,) * nd)

    small = [p["cwx"], p["cwb"], p["cbx"], p["cbb"], p["dtb"], p["alog"], p["dskip"], p["nw"],
             consts[0], consts[1]]
    y, cox, cob, s = pl.pallas_call(
        _mamba_prompt_kernel,
        grid=(nseq, nchunks),
        in_specs=[col(oz, mw), col(ox, mw), col(obc, bcw), col(osm, LANES)] + [full(a) for a in small]
        + [pl.BlockSpec(memory_space=pl.ANY)],
        out_specs=[pl.BlockSpec((CHUNK, mw), lambda b, c: (b * nchunks + c, 0)),
                   pl.BlockSpec((None, M_CONV - 1, mw), lambda b, c: (b, 0, 0)),
                   pl.BlockSpec((None, M_CONV - 1, bcw), lambda b, c: (b, 0, 0)),
                   pl.BlockSpec((None, n_pairs, LANES, M_STATE), lambda b, c: (b, 0, 0, 0))],
        input_output_aliases={4 + len(small): 0},
        out_shape=[jax.ShapeDtypeStruct(y_init.shape, BF16),
                   jax.ShapeDtypeStruct((nseq, M_CONV - 1, mw), F32),
                   jax.ShapeDtypeStruct((nseq, M_CONV - 1, bcw), F32),
                   jax.ShapeDtypeStruct((nseq, n_pairs, LANES, M_STATE), F32)],
        scratch_shapes=[pltpu.VMEM((CHUNK + SUBLANES, mw), F32), pltpu.VMEM((CHUNK + SUBLANES, bcw), F32),
                        pltpu.VMEM((n_pairs, LANES, M_STATE), F32)],
        compiler_params=_cparams(2),
        name="mamba_prompt",
    )(proj, proj, proj, proj, *small, y_init)
    return y, jnp.concatenate([cox, cob], axis=-1), s


def _mamba_sample_prep_kernel(cs_ref, x_ref, bc_ref, sm_ref, cw_ref, cb_ref, dtb_ref, alog_ref, exp_ref,
                              co_ref, act_ref, dte_ref, ee_ref):
    cd = cw_ref.shape[1]
    new = jnp.concatenate([x_ref[...], bc_ref[...]], axis=1)
    acc = cb_ref[...] + cw_ref[M_CONV - 1:M_CONV, :] * new
    for w in range(M_CONV - 1):
        acc = acc + cw_ref[w:w + 1, :] * cs_ref[:, w * cd:(w + 1) * cd]
    act_ref[...] = _silu(acc)
    for w in range(1, M_CONV - 1):
        co_ref[:, (w - 1) * cd:w * cd] = cs_ref[:, w * cd:(w + 1) * cd]
    co_ref[:, (M_CONV - 2) * cd:(M_CONV - 1) * cd] = new
    dt = _softplus(sm_ref[...] + dtb_ref[...])
    a = -jnp.exp(alog_ref[...])
    dte_ref[...] = _dot_exact_rhs(dt, exp_ref[...])
    ee_ref[...] = jnp.exp(_dot_exact_rhs(dt * a, exp_ref[...]))


def _mamba_sample_state_kernel(x_ref, b_ref, c_ref, dte_ref, ee_ref, s_ref, *out_refs, nb, layer, first):
    y_ref, so_ref = out_refs[-2:]
    xdt_t = _columns(x_ref[...] * dte_ref[...])
    e_t = _columns(ee_ref[...])
    bv = b_ref[...]
    cb = c_ref[...].astype(BF16)
    rowid = lax.broadcasted_iota(jnp.int32, (nb, LANES), 0)
    y = jnp.zeros((nb, LANES), F32)
    for b in range(nb):
        s_new = e_t[:, b:b + 1] * s_ref[b] + xdt_t[:, b:b + 1] * bv[b:b + 1, :]
        _store_state(so_ref, layer, b, s_new, first)
        y = jnp.where(rowid == b, _dot_nt(cb, s_new.astype(BF16)), y)
    y_ref[...] = y


def _mamba_sample_post_kernel(y_ref, x_ref, z_ref, dsk_ref, nw_ref, _, o_ref):
    o_ref[...] = _mamba_post(y_ref[...], x_ref[...], z_ref[...], dsk_ref[...], nw_ref[...]).astype(BF16)


def _mamba_sample(proj, conv_state, ssm_state, prev, y_all, row0, offs, p, expand, layer):
    oz, ox, obc, osm = offs
    depth, ns = ssm_state.shape[:2]
    nb = _sample_batch_block(ns, M_STATE)
    mw = p["dskip"].shape[1]
    bcw = p["cwb"].shape[1]
    cd = mw + bcw
    n_pairs = mw // LANES
    per_group = n_pairs // M_GROUPS
    rb = row0 // ns
    first = prev is None
    cw = jnp.concatenate([p["cwx"], p["cwb"]], axis=1)
    cb = jnp.concatenate([p["cbx"], p["cbb"]], axis=1)

    def full1(a):
        nd = a.ndim
        return pl.BlockSpec(a.shape, lambda i: (0,) * nd)

    cs2 = conv_state.reshape(ns, (M_CONV - 1) * cd)
    small = [cw, cb, p["dtb"], p["alog"], expand]
    co, act, dte, ee = pl.pallas_call(
        _mamba_sample_prep_kernel,
        grid=(1,),
        in_specs=[full1(cs2),
                  pl.BlockSpec((ns, mw), lambda i: (rb, ox // mw)),
                  pl.BlockSpec((ns, bcw), lambda i: (rb, obc // bcw)),
                  pl.BlockSpec((ns, LANES), lambda i: (rb, osm // LANES))] + [full1(a) for a in small],
        out_specs=[pl.BlockSpec((ns, (M_CONV - 1) * cd), lambda i: (0, 0)),
                   pl.BlockSpec((ns, cd), lambda i: (0, 0)),
                   pl.BlockSpec((ns, mw), lambda i: (0, 0)),
                   pl.BlockSpec((ns, mw), lambda i: (0, 0))],
        out_shape=[jax.ShapeDtypeStruct((ns, (M_CONV - 1) * cd), F32),
                   jax.ShapeDtypeStruct((ns, cd), F32),
                   jax.ShapeDtypeStruct((ns, mw), F32),
                   jax.ShapeDtypeStruct((ns, mw), F32)],
        compiler_params=_cparams(1),
        name="mamba_sample_prep",
    )(cs2, proj, proj, proj, *small)

    bblk = mw // LANES
    cblk = bblk + M_GROUPS * M_STATE // LANES
    ispec, ospec = _state_specs(depth, layer, nb, M_STATE, first)
    in_specs = [pl.BlockSpec((nb, LANES), lambda j, i: (i, j)),
                pl.BlockSpec((nb, LANES), lambda j, i: (i, bblk + j // per_group)),
                pl.BlockSpec((nb, LANES), lambda j, i: (i, cblk + j // per_group)),
                pl.BlockSpec((nb, LANES), lambda j, i: (i, j)),
                pl.BlockSpec((nb, LANES), lambda j, i: (i, j)),
                ispec]
    args = [act, act, act, dte, ee, ssm_state]
    aliases = {}
    if not first:
        in_specs.append(pl.BlockSpec(memory_space=pl.ANY))
        args.append(prev)
        aliases = {len(args) - 1: 1}
    y, s_new = pl.pallas_call(
        functools.partial(_mamba_sample_state_kernel, nb=nb, layer=layer, first=first),
        grid=(n_pairs, ns // nb),
        in_specs=in_specs,
        out_specs=[pl.BlockSpec((nb, LANES), lambda j, i: (i, j)), ospec],
        out_shape=[jax.ShapeDtypeStruct((ns, mw), F32),
                   jax.ShapeDtypeStruct(ssm_state.shape, F32)],
        input_output_aliases=aliases,
        compiler_params=_cparams(2),
        name="mamba_sample_state",
    )(*args)

    ym = pl.pallas_call(
        _mamba_sample_post_kernel,
        grid=(1,),
        in_specs=[full1(y),
                  pl.BlockSpec((ns, mw), lambda i: (0, 0)),
                  pl.BlockSpec((ns, mw), lambda i: (rb, oz // mw)),
                  full1(p["dskip"]), full1(p["nw"]), pl.BlockSpec(memory_space=pl.ANY)],
        out_specs=pl.BlockSpec((ns, mw), lambda i: (rb, 0)),
        out_shape=jax.ShapeDtypeStruct(y_all.shape, BF16),
        input_output_aliases={5: 0},
        compiler_params=_cparams(1),
        name="mamba_sample_post",
    )(y, act, proj, p["dskip"], p["nw"], y_all)
    return ym, co.reshape(ns, M_CONV - 1, cd), s_new


def _pad_lanes(v, n=LANES):
    v = v.reshape(1, -1)
    return jnp.pad(v, ((0, 0), (0, n - v.shape[1])))


def kernel(x_prompt, x_sample, state_conv, state_ssm, state_hgrn, state_gla, ffn1_norm, ffn1_w_gate_up, ffn1_w_down, mix_norm, w_in, conv_w, conv_b, dt_bias, a_log, d_skip, mamba_norm, hgrn_lb_logits, hgrn_norm, gla_w_decay, gla_b_decay, gla_norm, w_branch_mamba, w_branch_hgrn, w_branch_gla, w_out, ffn2_norm, ffn2_w_gate_up, ffn2_w_down, final_norm):
    nseq, seq, d = x_prompt.shape
    ns = x_sample.shape[0]
    depth = w_in.shape[0]
    nchunks = seq // CHUNK
    n_prompt = nseq * seq
    mw = w_branch_mamba.shape[1]
    hw = w_branch_hgrn.shape[1]
    gw = w_branch_gla.shape[1]
    gk = gla_w_decay.shape[2]
    m_heads = dt_bias.shape[1]
    h_heads = state_hgrn.shape[2]
    g_heads = state_gla.shape[2]
    g_dv = state_gla.shape[4]
    bcw = 2 * M_GROUPS * M_STATE
    n_pairs = mw // LANES
    assert seq % CHUNK == 0 and n_prompt % ns == 0 and ns % 16 == 0
    assert m_heads <= G_RANK + m_heads <= LANES and gk // g_heads == LANES and hw // h_heads == LANES

    seg_w = {"z": mw, "xs": mw, "bc": bcw, "dt": m_heads, "hq": hw, "hf": hw, "hi": hw, "hg": hw,
             "gq": gk, "gk": gk, "gv": gw, "gg": gw, "ga": G_RANK, "gate": N_BRANCH * d}
    src_order = ("z", "xs", "bc", "dt", "hq", "hf", "hi", "hg", "gq", "gk", "gv", "gg", "ga", "gate")
    dst_order = ("z", "xs", "hq", "hf", "hi", "hg", "gv", "gg", "gq", "gk", "bc", "gate", "dt", "ga")
    src, off = {}, {}
    pos = 0
    for name in src_order:
        src[name] = pos
        pos += seg_w[name]
    pos = 0
    for name in dst_order:
        off[name] = pos
        pos += seg_w[name]
    n_used = pos
    n_cols = -(-n_used // 1024) * 1024
    copies = tuple((src[name], off[name], seg_w[name]) for name in dst_order)
    oz, ox, obc, osm = off["z"], off["xs"], off["bc"], off["dt"]
    assert off["ga"] == osm + m_heads and osm % LANES == 0

    consts = _scan_constants()
    tril = consts[0][CHUNK:]
    mconsts = (tril, jnp.asarray(np.tril(np.ones((CHUNK, CHUNK), np.float32))))
    expand_np = np.zeros((LANES, mw), np.float32)
    for h in range(m_heads):
        expand_np[h, h * M_HEADDIM:(h + 1) * M_HEADDIM] = 1.0
    expand = jnp.asarray(expand_np, BF16)

    x = jnp.concatenate([x_prompt.reshape(n_prompt, d), x_sample.reshape(ns, d)], axis=0)
    t_all = n_prompt + ns
    w_in_t = jnp.swapaxes(w_in, 1, 2)
    ssm5 = state_ssm.reshape(depth, ns, n_pairs, LANES, M_STATE)

    pc, ps, ph, pg, sc = [], [], [], [], []
    ss = sh = sg = None
    for l in range(depth):
        w_perm = _pack_w_in(w_in_t, l, copies, n_used, n_cols)
        mp = {
            "cwx": conv_w[l][:, :mw], "cwb": conv_w[l][:, mw:],
            "cbx": conv_b[l][:mw].reshape(1, mw), "cbb": conv_b[l][mw:].reshape(1, bcw),
            "dtb": _pad_lanes(dt_bias[l]), "alog": _pad_lanes(a_log[l]),
            "dskip": jnp.repeat(d_skip[l], M_HEADDIM).reshape(1, mw),
            "nw": mamba_norm[l].reshape(1, mw),
        }
        wd = jnp.zeros((LANES, gk), F32).at[m_heads:m_heads + G_RANK].set(gla_w_decay[l])
        bd = gla_b_decay[l].reshape(1, gk)
        hnw = hgrn_norm[l].reshape(1, LANES)
        gnw = gla_norm[l].reshape(1, g_dv)

        act = _ffn_up(x, ffn1_norm[l], ffn1_w_gate_up, l)
        x = _resid_matmul(act, ffn1_w_down, l, x, 0.5, 256, "ffn_down")
        proj = _in_proj(x, mix_norm[l], w_perm)

        moffs = (oz, ox, obc, osm)
        hoffs = (off["hq"], off["hf"], off["hi"], off["hg"])
        goffs = (off["gq"], off["gk"], off["gv"], off["gg"], osm)
        ym, c1, s1 = _mamba_prompt(proj, moffs, mp, mconsts, jnp.zeros((t_all, mw), BF16),
                                   nseq=nseq, nchunks=nchunks)
        yh, h1 = _gla_prompt(proj, hoffs, (hgrn_lb_logits,), hnw, consts, jnp.zeros((t_all, hw), BF16),
                             hgrn=True, nseq=nseq, nchunks=nchunks, heads=h_heads, dv=LANES, layer=l)
        yg, g1 = _gla_prompt(proj, goffs, (wd, bd), gnw, consts, jnp.zeros((t_all, gw), BF16),
                             hgrn=False, nseq=nseq, nchunks=nchunks, heads=g_heads, dv=g_dv, layer=l)
        ym, c2, ss = _mamba_sample(proj, state_conv[l], ssm5, ss, ym, n_prompt, moffs, mp, expand, l)
        yh, sh = _gla_sample(proj, state_hgrn, sh, yh, n_prompt, hoffs, (hgrn_lb_logits,), hnw,
                             hgrn=True, heads=h_heads, dv=LANES, layer=l)
        yg, sg = _gla_sample(proj, state_gla, sg, yg, n_prompt, goffs, (wd, bd), gnw,
                             hgrn=False, heads=g_heads, dv=g_dv, layer=l)
        merged = _merge(ym, yh, yg, w_branch_mamba, w_branch_hgrn, w_branch_gla, l, proj, off["gate"])
        x = _resid_matmul(merged, w_out, l, x, 1.0, 512, "out_proj")

        act = _ffn_up(x, ffn2_norm[l], ffn2_w_gate_up, l)
        x = _resid_matmul(act, ffn2_w_down, l, x, 0.5, 256, "ffn_down")

        pc.append(c1)
        ps.append(s1.reshape(nseq, m_heads, M_HEADDIM, M_STATE))
        ph.append(h1)
        pg.append(g1)
        sc.append(c2)

    y_prompt = _final_norm(x, final_norm, 0, n_prompt, "final_norm_prompt").reshape(nseq, seq, d)
    y_sample = _final_norm(x, final_norm, n_prompt, ns, "final_norm_sample").reshape(ns, 1, d)
    return (y_prompt, y_sample, jnp.stack(pc), jnp.stack(ps), jnp.stack(ph), jnp.stack(pg),
            jnp.stack(sc), ss.reshape(state_ssm.shape), sh, sg)
```

```python
import functools

import jax
import jax.numpy as jnp
import numpy as np
from jax import lax
from jax.experimental import pallas as pl
from jax.experimental.pallas import tpu as pltpu

F32 = jnp.float32
BF16 = jnp.bfloat16
EPS = 1e-6
F32_TINY = float(np.finfo(np.float32).tiny)

CHUNK = 256
LANES = 128
SUBLANES = 8
M_HEADDIM = 64
M_STATE = 128
M_GROUPS = 2
M_CONV = 4
G_RANK = 16
G_NORMALIZER = 16.0
N_BRANCH = 3
N_LEVELS = int(np.log2(CHUNK))
VMEM_LIMIT = 56 * 1024 * 1024
STATE_BLOCK_BYTES = 4 * 1024 * 1024
WIDE_TM_CAP = 2100


def _cparams(n_axes):
    return pltpu.CompilerParams(dimension_semantics=("arbitrary",) * n_axes,
                                vmem_limit_bytes=VMEM_LIMIT)


def _dot(a, b):
    return jnp.dot(a, b, preferred_element_type=F32)


def _dot_nt(a, b):
    return lax.dot_general(a, b, (((1,), (1,)), ((), ())), preferred_element_type=F32)


def _dot_tn(a, b):
    return lax.dot_general(a, b, (((0,), (0,)), ((), ())), preferred_element_type=F32)


def _split3(x):
    hi = x.astype(BF16)
    r = x - hi.astype(F32)
    mid = r.astype(BF16)
    lo = (r - mid.astype(F32)).astype(BF16)
    return hi, mid, lo


def _dot_exact_lhs(p_bf16, x):
    hi, mid, lo = _split3(x)
    return _dot(p_bf16, hi) + _dot(p_bf16, mid) + _dot(p_bf16, lo)


def _dot_exact_rhs(x, p_bf16):
    hi, mid, lo = _split3(x)
    return _dot(hi, p_bf16) + _dot(mid, p_bf16) + _dot(lo, p_bf16)


def _dot_f32(a, b):
    ah, am, _ = _split3(a)
    bh, bm, _ = _split3(b)
    return _dot(ah, bh) + _dot(ah, bm) + _dot(am, bh)


def _sigmoid(x):
    return jax.nn.sigmoid(x)


def _silu(x):
    return x * _sigmoid(x)


def _softplus(x):
    return jnp.maximum(x, 0.0) + jnp.log1p(jnp.exp(-jnp.abs(x)))


def _log_sigmoid(x):
    return jnp.minimum(x, 0.0) - jnp.log1p(jnp.exp(-jnp.abs(x)))


def _rms(x, w):
    ms = jnp.mean(x * x, axis=-1, keepdims=True)
    return x * lax.rsqrt(ms + EPS) * w


def _pick_tm(t, cap=1100):
    best = 16
    for tm in range(16, min(t, cap) + 1, 16):
        if t % tm == 0:
            best = tm
    return best


def _row_scale(ssq_ref, d):
    return lax.rsqrt(ssq_ref[:, 0:1] * (1.0 / d) + EPS)


def _emit_normed(xn, nw_ref, xw_ref, ssq_ref, j=None):
    xw_ref[...] = (xn * nw_ref[...]).astype(BF16)
    part = jnp.broadcast_to(jnp.sum(xn * xn, axis=1, keepdims=True), ssq_ref.shape)
    if j is None:
        ssq_ref[...] = part
        return

    @pl.when(j == 0)
    def _():
        ssq_ref[...] = part

    @pl.when(j != 0)
    def _():
        ssq_ref[...] = ssq_ref[...] + part


def _prep_kernel(xp_ref, xs_ref, nw_ref, x_ref, xw_ref, ssq_ref, *, n_prompt_blocks):
    x = jnp.where(pl.program_id(0) < n_prompt_blocks, xp_ref[...], xs_ref[...])
    x_ref[...] = x
    _emit_normed(x, nw_ref, xw_ref, ssq_ref)


def _prep(xp, xs, nw):
    n_prompt, d = xp.shape
    rb = xs.shape[0]
    assert n_prompt % rb == 0
    npb = n_prompt // rb
    t = n_prompt + rb
    row = pl.BlockSpec((rb, d), lambda i: (i, 0))
    return pl.pallas_call(
        functools.partial(_prep_kernel, n_prompt_blocks=npb),
        grid=(npb + 1,),
        in_specs=[pl.BlockSpec((rb, d), lambda i: (jnp.minimum(i, npb - 1), 0)),
                  pl.BlockSpec((rb, d), lambda i: (0, 0)),
                  pl.BlockSpec((1, d), lambda i: (0, 0))],
        out_specs=[row, row, pl.BlockSpec((rb, LANES), lambda i: (i, 0))],
        out_shape=[jax.ShapeDtypeStruct((t, d), F32), jax.ShapeDtypeStruct((t, d), BF16),
                   jax.ShapeDtypeStruct((t, LANES), F32)],
        compiler_params=_cparams(1),
        name="prep",
    )(xp, xs, nw.reshape(1, d))


def _ffn_up_kernel(xw_ref, ssq_ref, wg_ref, wu_ref, o_ref, *, row_parts):
    wg = wg_ref[...].astype(BF16)
    wu = wu_ref[...].astype(BF16)
    tm, d = xw_ref.shape
    rows = tm // row_parts
    for p in range(row_parts):
        sl = pl.ds(p * rows, rows)
        r = lax.rsqrt(ssq_ref[sl, 0:1] * (1.0 / d) + EPS)
        h = xw_ref[sl, :]
        g = r * _dot(h, wg)
        u = r * _dot(h, wu)
        o_ref[sl, :] = (_silu(g) * u).astype(BF16)


def _ffn_up(xw, ssq, w_gu, layer, tn=512):
    t, d = xw.shape
    dff = w_gu.shape[2] // 2
    tm = _pick_tm(t, WIDE_TM_CAP)
    nj = dff // tn
    row_parts = 2 if tm % 32 == 0 else 1
    return pl.pallas_call(
        functools.partial(_ffn_up_kernel, row_parts=row_parts),
        grid=(t // tm, nj),
        in_specs=[pl.BlockSpec((tm, d), lambda i, j: (i, 0)),
                  pl.BlockSpec((tm, LANES), lambda i, j: (i, 0)),
                  pl.BlockSpec((None, d, tn), lambda i, j: (layer, 0, j)),
                  pl.BlockSpec((None, d, tn), lambda i, j: (layer, 0, j + nj))],
        out_specs=pl.BlockSpec((tm, tn), lambda i, j: (i, j)),
        out_shape=jax.ShapeDtypeStruct((t, dff), BF16),
        compiler_params=_cparams(2),
        name="ffn_up",
    )(xw, ssq, w_gu, w_gu)


def _resid_matmul_kernel(a_ref, w_ref, x_ref, *rest, scale, emit, row_parts):
    wb = w_ref[...].astype(BF16)
    rows = a_ref.shape[0] // row_parts
    for p in range(row_parts):
        sl = pl.ds(p * rows, rows)
        xn = x_ref[sl, :] + scale * _dot(a_ref[sl, :], wb)
        if emit:
            nw_ref, o_ref, xw_ref, ssq_ref = rest
            _emit_normed(xn, nw_ref, xw_ref.at[sl, :], ssq_ref.at[sl, :], pl.program_id(1))
        else:
            (o_ref,) = rest
        o_ref[sl, :] = xn


def _resid_matmul(a, w, layer, x, scale, tn, name, next_nw=None, tm_cap=1100):
    t, k = a.shape
    d = w.shape[2]
    tm = _pick_tm(t, tm_cap)
    row_parts = 2 if tm % 32 == 0 else 1
    emit = next_nw is not None
    tile = pl.BlockSpec((tm, tn), lambda i, j: (i, j))
    in_specs = [pl.BlockSpec((tm, k), lambda i, j: (i, 0)),
                pl.BlockSpec((None, k, tn), lambda i, j: (layer, 0, j)),
                tile]
    args = [a, w, x]
    out_specs = [tile]
    out_shape = [jax.ShapeDtypeStruct((t, d), F32)]
    if emit:
        in_specs.append(pl.BlockSpec((1, tn), lambda i, j: (0, j)))
        args.append(next_nw.reshape(1, d))
        out_specs += [tile, pl.BlockSpec((tm, LANES), lambda i, j: (i, 0))]
        out_shape += [jax.ShapeDtypeStruct((t, d), BF16), jax.ShapeDtypeStruct((t, LANES), F32)]
    outs = pl.pallas_call(
        functools.partial(_resid_matmul_kernel, scale=scale, emit=emit, row_parts=row_parts),
        grid=(t // tm, d // tn),
        in_specs=in_specs,
        out_specs=out_specs,
        out_shape=out_shape,
        compiler_params=_cparams(2),
        name=name,
    )(*args)
    return outs if emit else outs[0]


def _pack_w_in_kernel(w_ref, o_ref, *, copies, n_used):
    for src, dst, width in copies:
        if width % LANES == 0:
            o_ref[:, dst:dst + width] = w_ref[src:src + width, :].T.astype(BF16)
        else:
            base = src // LANES * LANES
            lo = src - base
            assert dst % LANES == lo and lo + width <= LANES
            t = w_ref[base:base + LANES, :].T
            o_ref[:, dst:dst + width] = t[:, lo:lo + width].astype(BF16)
    pad = o_ref.shape[1] - n_used
    if pad:
        o_ref[:, n_used:] = jnp.zeros((o_ref.shape[0], pad), BF16)


def _pack_w_in(w_t, layer, copies, n_used, n_cols, tk=128):
    _, n_src, d = w_t.shape
    return pl.pallas_call(
        functools.partial(_pack_w_in_kernel, copies=copies, n_used=n_used),
        grid=(d // tk,),
        in_specs=[pl.BlockSpec((None, n_src, tk), lambda i: (layer, 0, i))],
        out_specs=pl.BlockSpec((tk, n_cols), lambda i: (i, 0)),
        out_shape=jax.ShapeDtypeStruct((d, n_cols), BF16),
        compiler_params=_cparams(1),
        name="pack_w_in",
    )(w_t)


def _in_proj_kernel(xw_ref, ssq_ref, w_ref, o_ref):
    o_ref[...] = _row_scale(ssq_ref, xw_ref.shape[1]) * _dot(xw_ref[...], w_ref[...])


def _in_proj(xw, ssq, w, tn=1024):
    t, d = xw.shape
    n = w.shape[1]
    tm = _pick_tm(t, WIDE_TM_CAP)
    return pl.pallas_call(
        _in_proj_kernel,
        grid=(t // tm, n // tn),
        in_specs=[pl.BlockSpec((tm, d), lambda i, j: (i, 0)),
                  pl.BlockSpec((tm, LANES), lambda i, j: (i, 0)),
                  pl.BlockSpec((d, tn), lambda i, j: (0, j))],
        out_specs=pl.BlockSpec((tm, tn), lambda i, j: (i, j)),
        out_shape=jax.ShapeDtypeStruct((t, n), F32),
        compiler_params=_cparams(2),
        name="in_proj",
    )(xw, ssq, w)


def _merge_kernel(ym_ref, yh_ref, yg_ref, wm_ref, wh_ref, wg_ref, g0_ref, g1_ref, g2_ref, o_ref):
    acc = _sigmoid(g0_ref[...]) * _dot(ym_ref[...], wm_ref[...].astype(BF16))
    acc = acc + _sigmoid(g1_ref[...]) * _dot(yh_ref[...], wh_ref[...].astype(BF16))
    acc = acc + _sigmoid(g2_ref[...]) * _dot(yg_ref[...], wg_ref[...].astype(BF16))
    o_ref[...] = acc.astype(BF16)


def _merge(ym, yh, yg, wm, wh, wg, layer, proj, gate_off, tn=512):
    t, k = ym.shape
    d = wm.shape[2]
    tm = _pick_tm(t)
    gb = gate_off // tn
    nb = d // tn
    yspec = pl.BlockSpec((tm, k), lambda i, j: (i, 0))
    wspec = pl.BlockSpec((None, k, tn), lambda i, j: (layer, 0, j))

    def gspec(b):
        return pl.BlockSpec((tm, tn), lambda i, j: (i, gb + b * nb + j))

    return pl.pallas_call(
        _merge_kernel,
        grid=(t // tm, nb),
        in_specs=[yspec, yspec, yspec, wspec, wspec, wspec, gspec(0), gspec(1), gspec(2)],
        out_specs=pl.BlockSpec((tm, tn), lambda i, j: (i, j)),
        out_shape=jax.ShapeDtypeStruct((t, d), BF16),
        compiler_params=_cparams(2),
        name="merge",
    )(ym, yh, yg, wm, wh, wg, proj, proj, proj)


def _final_norm_kernel(x_ref, nw_ref, o_ref):
    o_ref[...] = _rms(x_ref[...], nw_ref[...])


def _final_norm(x, nw, row0, rows, name):
    d = x.shape[1]
    tm = _pick_tm(rows)
    assert row0 % tm == 0
    return pl.pallas_call(
        _final_norm_kernel,
        grid=(rows // tm,),
        in_specs=[pl.BlockSpec((tm, d), lambda i: (row0 // tm + i, 0)),
                  pl.BlockSpec((1, d), lambda i: (0, 0))],
        out_specs=pl.BlockSpec((tm, d), lambda i: (i, 0)),
        out_shape=jax.ShapeDtypeStruct((rows, d), F32),
        compiler_params=_cparams(1),
        name=name,
    )(x, nw.reshape(1, d))


def _scan_constants():
    c = CHUNK
    pm = np.zeros((2, c, c), np.float32)
    sg = np.zeros((N_LEVELS - 1, c, LANES), np.float32)
    mk = np.zeros((N_LEVELS + 1, c, c), np.float32)
    r = np.arange(c)
    for lvl in range(N_LEVELS):
        h = c >> (lvl + 1)
        for t in range(c):
            blk, pos = divmod(t, 2 * h)
            ridx = blk * 2 * h + h - 1
            upper = pos >= h
            if upper:
                mk[lvl, t] = ((r // (2 * h)) == blk) & ((r % (2 * h)) < h)
            if h >= 4:
                sg[lvl, t] = 1.0 if upper else -1.0
            elif h == 2:
                pm[0, t] = ((r > ridx) & (r <= t)) if upper else ((r > t) & (r <= ridx))
            else:
                sg[N_LEVELS - 2, t] = 1.0 if upper else 0.0
    pm[1] = r[None, :] <= r[:, None]
    mk[N_LEVELS] = np.eye(c)
    return jnp.asarray(pm.reshape(2 * c, c), BF16), jnp.asarray(sg), jnp.asarray(mk)


def _lower_bound(logits, layer):
    m = jnp.max(logits, axis=0, keepdims=True)
    e = jnp.exp(logits - m)
    sm = e / jnp.sum(e, axis=0, keepdims=True)
    lb = jnp.zeros_like(m)
    for i in range(1, layer + 1):
        lb = lb + sm[i:i + 1, :]
    return lb


def _gla_inputs(refs, hgrn, layer):
    if hgrn:
        q_ref, f_ref, i_ref, lbl_ref = refs
        lb = _lower_bound(lbl_ref[...], layer)
        q = _silu(q_ref[...]) * (LANES ** -0.5)
        zf = f_ref[...]
        f = lb + (1.0 - lb) * _sigmoid(zf)
        gl = jnp.log(jnp.maximum(f, F32_TINY))
        k = (1.0 - lb) * _sigmoid(-zf)
        v = i_ref[...]
    else:
        q_ref, k_ref, v_ref, a_ref, wd_ref, bd_ref = refs
        q = q_ref[...] * (LANES ** -0.5)
        k = k_ref[...]
        v = v_ref[...]
        gl = _log_sigmoid(_dot_f32(a_ref[...], wd_ref[...]) + bd_ref[...]) / G_NORMALIZER
    return q, k, v, gl


def _level_factor(lvl, gl, cum, z_h2, sg_ref):
    h = CHUNK >> (lvl + 1)
    if h >= 4:
        c3 = cum.reshape(CHUNK // (2 * h), 2 * h, LANES)
        d = (c3 - c3[:, h - 1:h, :]).reshape(CHUNK, LANES)
        return jnp.exp(sg_ref[lvl] * d)
    if h == 2:
        return jnp.exp(z_h2)
    return jnp.exp(sg_ref[N_LEVELS - 2] * gl)


def _gla_prompt_kernel(*refs, hgrn, layer, hp, dv):
    n_in = 4 if hgrn else 6
    in_refs = refs[:n_in]
    g_ref, nw_ref, pm_ref, sg_ref, mk_ref, _, y_ref, so_ref, s_scr = refs[n_in:]
    c = pl.program_id(2)
    widths = (LANES, LANES, dv, LANES) if hgrn else (LANES, LANES, dv, None, LANES, LANES)

    @pl.when(c == 0)
    def _():
        s_scr[...] = jnp.zeros_like(s_scr)

    finals = []
    for u in range(hp):
        sub = [r if w is None else r.at[:, u * w:(u + 1) * w] for r, w in zip(in_refs, widths)]
        q, k, v, gl = _gla_inputs(sub, hgrn, layer)
        zz = _dot_exact_lhs(pm_ref[...], gl)
        z_h2, cum = zz[:CHUNK], zz[CHUNK:]
        qb, kb = q.astype(BF16), k.astype(BF16)
        att = mk_ref[N_LEVELS] * _dot_nt(qb, kb)
        for lvl in range(N_LEVELS):
            eb = _level_factor(lvl, gl, cum, z_h2, sg_ref).astype(BF16)
            att = att + mk_ref[lvl] * _dot_nt(qb * eb, kb * eb)
        e_cum = jnp.exp(cum)
        e_tail = jnp.exp(cum[CHUNK - 1:CHUNK, :] - cum)
        vb = v.astype(BF16)
        s = s_scr[u]
        o = _dot(att.astype(BF16), vb) + _dot((q * e_cum).astype(BF16), s.astype(BF16))
        e_last = e_cum.T[:, CHUNK - 1:CHUNK]
        s_new = e_last * s + _dot_tn((k * e_tail).astype(BF16), vb)
        s_scr[u] = s_new
        finals.append(s_new)
        gate = g_ref[:, u * dv:(u + 1) * dv]
        y_ref[:, u * dv:(u + 1) * dv] = (_rms(o, nw_ref[...]) * _silu(gate)).astype(BF16)

    @pl.when(c == pl.num_programs(2) - 1)
    def _():
        for u in range(hp):
            so_ref[u] = finals[u]


def _gla_prompt(proj, offs, extra, nw, consts, y_init, *, hgrn, nseq, nchunks, heads, dv, layer, hp=4):
    pm, sg, mk = consts
    assert heads % hp == 0

    def col(off, w):
        assert off % (hp * w) == 0
        return pl.BlockSpec((CHUNK, hp * w), lambda h, b, c: (b * nchunks + c, off // (hp * w) + h))

    def full(a):
        nd = a.ndim
        return pl.BlockSpec(a.shape, lambda h, b, c: (0,) * nd)

    if hgrn:
        oq, of, oi, og = offs
        (lbl,) = extra
        in_specs = [col(oq, LANES), col(of, LANES), col(oi, dv),
                    pl.BlockSpec((lbl.shape[0], hp * LANES), lambda h, b, c: (0, h))]
        args = [proj, proj, proj, lbl]
    else:
        oq, ok, ov, og, oa = offs
        wd, bd = extra
        in_specs = [col(oq, LANES), col(ok, LANES), col(ov, dv),
                    pl.BlockSpec((CHUNK, LANES), lambda h, b, c: (b * nchunks + c, oa // LANES)),
                    pl.BlockSpec((LANES, hp * LANES), lambda h, b, c: (0, h)),
                    pl.BlockSpec((1, hp * LANES), lambda h, b, c: (0, h))]
        args = [proj, proj, proj, proj, wd, bd]
    in_specs += [col(og, dv), full(nw), full(pm), full(sg), full(mk), pl.BlockSpec(memory_space=pl.ANY)]
    args += [proj, nw, pm, sg, mk, y_init]
    y, s = pl.pallas_call(
        functools.partial(_gla_prompt_kernel, hgrn=hgrn, layer=layer, hp=hp, dv=dv),
        grid=(heads // hp, nseq, nchunks),
        in_specs=in_specs,
        out_specs=[pl.BlockSpec((CHUNK, hp * dv), lambda h, b, c: (b * nchunks + c, h)),
                   pl.BlockSpec((None, hp, LANES, dv), lambda h, b, c: (b, h, 0, 0))],
        out_shape=[jax.ShapeDtypeStruct(y_init.shape, BF16),
                   jax.ShapeDtypeStruct((nseq, heads, LANES, dv), F32)],
        input_output_aliases={len(args) - 1: 0},
        scratch_shapes=[pltpu.VMEM((hp, LANES, dv), F32)],
        compiler_params=_cparams(3),
        name="hgrn_prompt" if hgrn else "gla_prompt",
    )(*args)
    return y, s


def _columns(x):
    nb = x.shape[0]
    if nb < LANES:
        x = jnp.concatenate([x, jnp.zeros((LANES - nb, x.shape[1]), x.dtype)], axis=0)
    return x.T


def _store_state(so_ref, layer, b, s_new, first):
    if first:
        for l in range(so_ref.shape[0]):
            so_ref[l, b] = s_new if l == layer else jnp.zeros_like(s_new)
    else:
        so_ref[b] = s_new


def _gla_sample_kernel(*refs, hgrn, nb, layer, first):
    n_in = 4 if hgrn else 6
    in_refs = refs[:n_in]
    g_ref, nw_ref, s_ref = refs[n_in:n_in + 3]
    y_ref, so_ref = refs[-2:]
    q, k, v, gl = _gla_inputs(in_refs, hgrn, layer)
    e_t = _columns(jnp.exp(gl))
    k_t = _columns(k)
    qb = q.astype(BF16)
    rowid = lax.broadcasted_iota(jnp.int32, v.shape, 0)
    o = jnp.zeros(v.shape, F32)
    for b in range(nb):
        s_new = e_t[:, b:b + 1] * s_ref[b] + k_t[:, b:b + 1] * v[b:b + 1, :]
        _store_state(so_ref, layer, b, s_new, first)
        o = jnp.where(rowid == b, _dot(qb, s_new.astype(BF16)), o)
    y_ref[...] = (_rms(o, nw_ref[...]) * _silu(g_ref[...])).astype(BF16)


def _state_specs(depth, layer, nb, dv, first):
    ispec = pl.BlockSpec((None, nb, None, LANES, dv), lambda h, i: (layer, i, h, 0, 0))
    if first:
        ospec = pl.BlockSpec((depth, nb, None, LANES, dv), lambda h, i: (0, i, h, 0, 0))
    else:
        ospec = ispec
    return ispec, ospec


def _sample_batch_block(ns, dv):
    nb = min(ns, STATE_BLOCK_BYTES // (LANES * dv * 4))
    assert ns % nb == 0
    return nb


def _gla_sample(proj, state, prev, y_all, row0, offs, extra, nw, *, hgrn, heads, dv, layer):
    depth, ns = state.shape[:2]
    nb = _sample_batch_block(ns, dv)
    rb = row0 // nb
    first = prev is None

    def col(off, w):
        return pl.BlockSpec((nb, w), lambda h, i: (rb + i, off // w + h))

    if hgrn:
        oq, of, oi, og = offs
        (lbl,) = extra
        in_specs = [col(oq, LANES), col(of, LANES), col(oi, dv),
                    pl.BlockSpec((lbl.shape[0], LANES), lambda h, i: (0, h))]
        args = [proj, proj, proj, lbl]
    else:
        oq, ok, ov, og, oa = offs
        wd, bd = extra
        in_specs = [col(oq, LANES), col(ok, LANES), col(ov, dv),
                    pl.BlockSpec((nb, LANES), lambda h, i: (rb + i, oa // LANES)),
                    pl.BlockSpec((LANES, LANES), lambda h, i: (0, h)),
                    pl.BlockSpec((1, LANES), lambda h, i: (0, h))]
        args = [proj, proj, proj, proj, wd, bd]
    ispec, ospec = _state_specs(depth, layer, nb, dv, first)
    in_specs += [col(og, dv), pl.BlockSpec(nw.shape, lambda h, i: (0, 0)), ispec,
                 pl.BlockSpec(memory_space=pl.ANY)]
    args += [proj, nw, state, y_all]
    aliases = {len(args) - 1: 0}
    if not first:
        in_specs.append(pl.BlockSpec(memory_space=pl.ANY))
        args.append(prev)
        aliases[len(args) - 1] = 1
    y, s = pl.pallas_call(
        functools.partial(_gla_sample_kernel, hgrn=hgrn, nb=nb, layer=layer, first=first),
        grid=(heads, ns // nb),
        in_specs=in_specs,
        out_specs=[pl.BlockSpec((nb, dv), lambda h, i: (rb + i, h)), ospec],
        out_shape=[jax.ShapeDtypeStruct(y_all.shape, BF16),
                   jax.ShapeDtypeStruct(state.shape, F32)],
        input_output_aliases=aliases,
        compiler_params=_cparams(2),
        name="hgrn_sample" if hgrn else "gla_sample",
    )(*args)
    return y, s


def _mamba_post(y, xs, z, dskip, nw):
    y = (y + dskip * xs) * _silu(z)
    gs = y.shape[1] // M_GROUPS
    outs = [_rms(y[:, g * gs:(g + 1) * gs], nw[:, g * gs:(g + 1) * gs]) for g in range(M_GROUPS)]
    return jnp.concatenate(outs, axis=1)


def _mamba_prompt_kernel(z_ref, x_ref, bc_ref, sm_ref, cwx_ref, cwb_ref, cbx_ref, cbb_ref,
                         dtb_ref, alog_ref, dsk_ref, nw_ref, tril_ref, mask_ref, _,
                         y_ref, cox_ref, cob_ref, so_ref, ex_scr, eb_scr, s_scr):
    c = pl.program_id(1)
    nc = pl.num_programs(1)
    tail = SUBLANES

    @pl.when(c == 0)
    def _():
        ex_scr[0:tail, :] = jnp.zeros((tail, ex_scr.shape[1]), F32)
        eb_scr[0:tail, :] = jnp.zeros((tail, eb_scr.shape[1]), F32)
        s_scr[...] = jnp.zeros_like(s_scr)

    ex_scr[tail:tail + CHUNK, :] = x_ref[...]
    eb_scr[tail:tail + CHUNK, :] = bc_ref[...]

    def conv(scr, cw_ref, cb_ref):
        acc = cb_ref[...]
        for w in range(M_CONV):
            sh = M_CONV - 1 - w
            acc = acc + cw_ref[w:w + 1, :] * scr[tail - sh:tail - sh + CHUNK, :]
        return _silu(acc)

    xs = conv(ex_scr, cwx_ref, cbx_ref)
    bcm = conv(eb_scr, cwb_ref, cbb_ref)

    @pl.when(c == nc - 1)
    def _():
        cox_ref[...] = ex_scr[tail + CHUNK - (M_CONV - 1):tail + CHUNK, :]
        cob_ref[...] = eb_scr[tail + CHUNK - (M_CONV - 1):tail + CHUNK, :]

    ex_scr[0:tail, :] = ex_scr[CHUNK:CHUNK + tail, :]
    eb_scr[0:tail, :] = eb_scr[CHUNK:CHUNK + tail, :]

    dt = _softplus(sm_ref[...] + dtb_ref[...])
    a = -jnp.exp(alog_ref[...])
    cum = _dot_exact_lhs(tril_ref[...], dt * a)
    cum_t = cum.T
    cl = cum[CHUNK - 1:CHUNK, :]
    mask = mask_ref[...] > 0.5
    gw = M_GROUPS * M_STATE
    lo = lax.broadcasted_iota(jnp.int32, (CHUNK, LANES), 1) < M_HEADDIM
    lo_r = lax.broadcasted_iota(jnp.int32, (LANES, LANES), 0) < M_HEADDIM
    n_pairs = xs.shape[1] // LANES
    per_group = n_pairs // M_GROUPS
    ys = []
    for j in range(n_pairs):
        g = j // per_group
        h0, h1 = 2 * j, 2 * j + 1
        if j % per_group == 0:
            bg = bcm[:, g * M_STATE:(g + 1) * M_STATE].astype(BF16)
            cg = bcm[:, gw + g * M_STATE:gw + (g + 1) * M_STATE].astype(BF16)
            cb = jnp.where(mask, _dot_nt(cg, bg), 0.0)

        def dec(h):
            return jnp.exp(jnp.minimum(cum[:, h:h + 1] - cum_t[h:h + 1, :], 0.0))

        x2 = xs[:, j * LANES:(j + 1) * LANES]
        dt2 = jnp.where(lo, dt[:, h0:h0 + 1], dt[:, h1:h1 + 1])
        xdt = (x2 * dt2).astype(BF16)
        y_in = jnp.where(lo, _dot((cb * dec(h0)).astype(BF16), xdt),
                         _dot((cb * dec(h1)).astype(BF16), xdt))
        ec2 = jnp.where(lo, jnp.exp(cum[:, h0:h0 + 1]), jnp.exp(cum[:, h1:h1 + 1]))
        s = s_scr[j]
        ys.append(y_in + _dot_nt(cg, s.astype(BF16)) * ec2)
        w2 = dt2 * jnp.where(lo, jnp.exp(cl[:, h0:h0 + 1] - cum[:, h0:h0 + 1]),
                             jnp.exp(cl[:, h1:h1 + 1] - cum[:, h1:h1 + 1]))
        el2 = jnp.where(lo_r, jnp.exp(cl[:, h0:h0 + 1]), jnp.exp(cl[:, h1:h1 + 1]))
        s_scr[j] = el2 * s + _dot_tn((x2 * w2).astype(BF16), bg)
    y = jnp.concatenate(ys, axis=1)
    y_ref[...] = _mamba_post(y, xs, z_ref[...], dsk_ref[...], nw_ref[...]).astype(BF16)

    @pl.when(c == nc - 1)
    def _():
        so_ref[...] = s_scr[...]


def _mamba_prompt(proj, offs, p, consts, y_init, *, nseq, nchunks):
    oz, ox, obc, osm = offs
    mw = p["dskip"].shape[1]
    bcw = p["cwb"].shape[1]
    n_pairs = mw // LANES

    def col(off, w):
        return pl.BlockSpec((CHUNK, w), lambda b, c: (b * nchunks + c, off // w))

    def full(a):
        nd = a.ndim
        return pl.BlockSpec(a.shape, lambda b, c: (0,) * nd)

    small = [p["cwx"], p["cwb"], p["cbx"], p["cbb"], p["dtb"], p["alog"], p["dskip"], p["nw"],
             consts[0], consts[1]]
    y, cox, cob, s = pl.pallas_call(
        _mamba_prompt_kernel,
        grid=(nseq, nchunks),
        in_specs=[col(oz, mw), col(ox, mw), col(obc, bcw), col(osm, LANES)] + [full(a) for a in small]
        + [pl.BlockSpec(memory_space=pl.ANY)],
        out_specs=[pl.BlockSpec((CHUNK, mw), lambda b, c: (b * nchunks + c, 0)),
                   pl.BlockSpec((None, M_CONV - 1, mw), lambda b, c: (b, 0, 0)),
                   pl.BlockSpec((None, M_CONV - 1, bcw), lambda b, c: (b, 0, 0)),
                   pl.BlockSpec((None, n_pairs, LANES, M_STATE), lambda b, c: (b, 0, 0, 0))],
        input_output_aliases={4 + len(small): 0},
        out_shape=[jax.ShapeDtypeStruct(y_init.shape, BF16),
                   jax.ShapeDtypeStruct((nseq, M_CONV - 1, mw), F32),
                   jax.ShapeDtypeStruct((nseq, M_CONV - 1, bcw), F32),
                   jax.ShapeDtypeStruct((nseq, n_pairs, LANES, M_STATE), F32)],
        scratch_shapes=[pltpu.VMEM((CHUNK + SUBLANES, mw), F32), pltpu.VMEM((CHUNK + SUBLANES, bcw), F32),
                        pltpu.VMEM((n_pairs, LANES, M_STATE), F32)],
        compiler_params=_cparams(2),
        name="mamba_prompt",
    )(proj, proj, proj, proj, *small, y_init)
    return y, jnp.concatenate([cox, cob], axis=-1), s


def _mamba_sample_prep_kernel(cs_ref, x_ref, bc_ref, sm_ref, cw_ref, cb_ref, dtb_ref, alog_ref, exp_ref,
                              co_ref, act_ref, dte_ref, ee_ref):
    cd = cw_ref.shape[1]
    new = jnp.concatenate([x_ref[...], bc_ref[...]], axis=1)
    acc = cb_ref[...] + cw_ref[M_CONV - 1:M_CONV, :] * new
    for w in range(M_CONV - 1):
        acc = acc + cw_ref[w:w + 1, :] * cs_ref[:, w * cd:(w + 1) * cd]
    act_ref[...] = _silu(acc)
    for w in range(1, M_CONV - 1):
        co_ref[:, (w - 1) * cd:w * cd] = cs_ref[:, w * cd:(w + 1) * cd]
    co_ref[:, (M_CONV - 2) * cd:(M_CONV - 1) * cd] = new
    dt = _softplus(sm_ref[...] + dtb_ref[...])
    a = -jnp.exp(alog_ref[...])
    dte_ref[...] = _dot_exact_rhs(dt, exp_ref[...])
    ee_ref[...] = jnp.exp(_dot_exact_rhs(dt * a, exp_ref[...]))


def _mamba_sample_state_kernel(x_ref, b_ref, c_ref, dte_ref, ee_ref, s_ref, *out_refs, nb, layer, first):
    y_ref, so_ref = out_refs[-2:]
    xdt_t = _columns(x_ref[...] * dte_ref[...])
    e_t = _columns(ee_ref[...])
    bv = b_ref[...]
    cb = c_ref[...].astype(BF16)
    rowid = lax.broadcasted_iota(jnp.int32, (nb, LANES), 0)
    y = jnp.zeros((nb, LANES), F32)
    for b in range(nb):
        s_new = e_t[:, b:b + 1] * s_ref[b] + xdt_t[:, b:b + 1] * bv[b:b + 1, :]
        _store_state(so_ref, layer, b, s_new, first)
        y = jnp.where(rowid == b, _dot_nt(cb, s_new.astype(BF16)), y)
    y_ref[...] = y


def _mamba_sample_post_kernel(y_ref, x_ref, z_ref, dsk_ref, nw_ref, _, o_ref):
    o_ref[...] = _mamba_post(y_ref[...], x_ref[...], z_ref[...], dsk_ref[...], nw_ref[...]).astype(BF16)


def _mamba_sample(proj, conv_state, ssm_state, prev, y_all, row0, offs, p, expand, layer):
    oz, ox, obc, osm = offs
    depth, ns = ssm_state.shape[:2]
    nb = _sample_batch_block(ns, M_STATE)
    mw = p["dskip"].shape[1]
    bcw = p["cwb"].shape[1]
    cd = mw + bcw
    n_pairs = mw // LANES
    per_group = n_pairs // M_GROUPS
    rb = row0 // ns
    first = prev is None
    cw = jnp.concatenate([p["cwx"], p["cwb"]], axis=1)
    cb = jnp.concatenate([p["cbx"], p["cbb"]], axis=1)

    def full1(a):
        nd = a.ndim
        return pl.BlockSpec(a.shape, lambda i: (0,) * nd)

    cs2 = conv_state.reshape(ns, (M_CONV - 1) * cd)
    small = [cw, cb, p["dtb"], p["alog"], expand]
    co, act, dte, ee = pl.pallas_call(
        _mamba_sample_prep_kernel,
        grid=(1,),
        in_specs=[full1(cs2),
                  pl.BlockSpec((ns, mw), lambda i: (rb, ox // mw)),
                  pl.BlockSpec((ns, bcw), lambda i: (rb, obc // bcw)),
                  pl.BlockSpec((ns, LANES), lambda i: (rb, osm // LANES))] + [full1(a) for a in small],
        out_specs=[pl.BlockSpec((ns, (M_CONV - 1) * cd), lambda i: (0, 0)),
                   pl.BlockSpec((ns, cd), lambda i: (0, 0)),
                   pl.BlockSpec((ns, mw), lambda i: (0, 0)),
                   pl.BlockSpec((ns, mw), lambda i: (0, 0))],
        out_shape=[jax.ShapeDtypeStruct((ns, (M_CONV - 1) * cd), F32),
                   jax.ShapeDtypeStruct((ns, cd), F32),
                   jax.ShapeDtypeStruct((ns, mw), F32),
                   jax.ShapeDtypeStruct((ns, mw), F32)],
        compiler_params=_cparams(1),
        name="mamba_sample_prep",
    )(cs2, proj, proj, proj, *small)

    bblk = mw // LANES
    cblk = bblk + M_GROUPS * M_STATE // LANES
    ispec, ospec = _state_specs(depth, layer, nb, M_STATE, first)
    in_specs = [pl.BlockSpec((nb, LANES), lambda j, i: (i, j)),
                pl.BlockSpec((nb, LANES), lambda j, i: (i, bblk + j // per_group)),
                pl.BlockSpec((nb, LANES), lambda j, i: (i, cblk + j // per_group)),
                pl.BlockSpec((nb, LANES), lambda j, i: (i, j)),
                pl.BlockSpec((nb, LANES), lambda j, i: (i, j)),
                ispec]
    args = [act, act, act, dte, ee, ssm_state]
    aliases = {}
    if not first:
        in_specs.append(pl.BlockSpec(memory_space=pl.ANY))
        args.append(prev)
        aliases = {len(args) - 1: 1}
    y, s_new = pl.pallas_call(
        functools.partial(_mamba_sample_state_kernel, nb=nb, layer=layer, first=first),
        grid=(n_pairs, ns // nb),
        in_specs=in_specs,
        out_specs=[pl.BlockSpec((nb, LANES), lambda j, i: (i, j)), ospec],
        out_shape=[jax.ShapeDtypeStruct((ns, mw), F32),
                   jax.ShapeDtypeStruct(ssm_state.shape, F32)],
        input_output_aliases=aliases,
        compiler_params=_cparams(2),
        name="mamba_sample_state",
    )(*args)

    ym = pl.pallas_call(
        _mamba_sample_post_kernel,
        grid=(1,),
        in_specs=[full1(y),
                  pl.BlockSpec((ns, mw), lambda i: (0, 0)),
                  pl.BlockSpec((ns, mw), lambda i: (rb, oz // mw)),
                  full1(p["dskip"]), full1(p["nw"]), pl.BlockSpec(memory_space=pl.ANY)],
        out_specs=pl.BlockSpec((ns, mw), lambda i: (rb, 0)),
        out_shape=jax.ShapeDtypeStruct(y_all.shape, BF16),
        input_output_aliases={5: 0},
        compiler_params=_cparams(1),
        name="mamba_sample_post",
    )(y, act, proj, p["dskip"], p["nw"], y_all)
    return ym, co.reshape(ns, M_CONV - 1, cd), s_new


def _pad_lanes(v, n=LANES):
    v = v.reshape(1, -1)
    return jnp.pad(v, ((0, 0), (0, n - v.shape[1])))


def kernel(x_prompt, x_sample, state_conv, state_ssm, state_hgrn, state_gla, ffn1_norm, ffn1_w_gate_up, ffn1_w_down, mix_norm, w_in, conv_w, conv_b, dt_bias, a_log, d_skip, mamba_norm, hgrn_lb_logits, hgrn_norm, gla_w_decay, gla_b_decay, gla_norm, w_branch_mamba, w_branch_hgrn, w_branch_gla, w_out, ffn2_norm, ffn2_w_gate_up, ffn2_w_down, final_norm):
    nseq, seq, d = x_prompt.shape
    ns = x_sample.shape[0]
    depth = w_in.shape[0]
    nchunks = seq // CHUNK
    n_prompt = nseq * seq
    mw = w_branch_mamba.shape[1]
    hw = w_branch_hgrn.shape[1]
    gw = w_branch_gla.shape[1]
    gk = gla_w_decay.shape[2]
    m_heads = dt_bias.shape[1]
    h_heads = state_hgrn.shape[2]
    g_heads = state_gla.shape[2]
    g_dv = state_gla.shape[4]
    bcw = 2 * M_GROUPS * M_STATE
    n_pairs = mw // LANES
    assert seq % CHUNK == 0 and n_prompt % ns == 0 and ns % 16 == 0
    assert m_heads <= G_RANK + m_heads <= LANES and gk // g_heads == LANES and hw // h_heads == LANES

    seg_w = {"z": mw, "xs": mw, "bc": bcw, "dt": m_heads, "hq": hw, "hf": hw, "hi": hw, "hg": hw,
             "gq": gk, "gk": gk, "gv": gw, "gg": gw, "ga": G_RANK, "gate": N_BRANCH * d}
    src_order = ("z", "xs", "bc", "dt", "hq", "hf", "hi", "hg", "gq", "gk", "gv", "gg", "ga", "gate")
    dst_order = ("z", "xs", "hq", "hf", "hi", "hg", "gv", "gg", "gq", "gk", "bc", "gate", "dt", "ga")
    src, off = {}, {}
    pos = 0
    for name in src_order:
        src[name] = pos
        pos += seg_w[name]
    pos = 0
    for name in dst_order:
        off[name] = pos
        pos += seg_w[name]
    n_used = pos
    n_cols = -(-n_used // 1024) * 1024
    copies = tuple((src[name], off[name], seg_w[name]) for name in dst_order)
    oz, ox, obc, osm = off["z"], off["xs"], off["bc"], off["dt"]
    assert off["ga"] == osm + m_heads and osm % LANES == 0

    consts = _scan_constants()
    tril = consts[0][CHUNK:]
    mconsts = (tril, jnp.asarray(np.tril(np.ones((CHUNK, CHUNK), np.float32))))
    expand_np = np.zeros((LANES, mw), np.float32)
    for h in range(m_heads):
        expand_np[h, h * M_HEADDIM:(h + 1) * M_HEADDIM] = 1.0
    expand = jnp.asarray(expand_np, BF16)

    x, xw, ssq = _prep(x_prompt.reshape(n_prompt, d), x_sample.reshape(ns, d), ffn1_norm[0])
    t_all = n_prompt + ns
    w_in_t = jnp.swapaxes(w_in, 1, 2)
    ssm5 = state_ssm.reshape(depth, ns, n_pairs, LANES, M_STATE)

    pc, ps, ph, pg, sc = [], [], [], [], []
    ss = sh = sg = None
    for l in range(depth):
        w_perm = _pack_w_in(w_in_t, l, copies, n_used, n_cols)
        mp = {
            "cwx": conv_w[l][:, :mw], "cwb": conv_w[l][:, mw:],
            "cbx": conv_b[l][:mw].reshape(1, mw), "cbb": conv_b[l][mw:].reshape(1, bcw),
            "dtb": _pad_lanes(dt_bias[l]), "alog": _pad_lanes(a_log[l]),
            "dskip": jnp.repeat(d_skip[l], M_HEADDIM).reshape(1, mw),
            "nw": mamba_norm[l].reshape(1, mw),
        }
        wd = jnp.zeros((LANES, gk), F32).at[m_heads:m_heads + G_RANK].set(gla_w_decay[l])
        bd = gla_b_decay[l].reshape(1, gk)
        hnw = hgrn_norm[l].reshape(1, LANES)
        gnw = gla_norm[l].reshape(1, g_dv)

        act = _ffn_up(xw, ssq, ffn1_w_gate_up, l)
        x, xw, ssq = _resid_matmul(act, ffn1_w_down, l, x, 0.5, 256, "ffn_down", mix_norm[l])
        proj = _in_proj(xw, ssq, w_perm)

        moffs = (oz, ox, obc, osm)
        hoffs = (off["hq"], off["hf"], off["hi"], off["hg"])
        goffs = (off["gq"], off["gk"], off["gv"], off["gg"], osm)
        ym, c1, s1 = _mamba_prompt(proj, moffs, mp, mconsts, jnp.zeros((t_all, mw), BF16),
                                   nseq=nseq, nchunks=nchunks)
        yh, h1 = _gla_prompt(proj, hoffs, (hgrn_lb_logits,), hnw, consts, jnp.zeros((t_all, hw), BF16),
                             hgrn=True, nseq=nseq, nchunks=nchunks, heads=h_heads, dv=LANES, layer=l)
        yg, g1 = _gla_prompt(proj, goffs, (wd, bd), gnw, consts, jnp.zeros((t_all, gw), BF16),
                             hgrn=False, nseq=nseq, nchunks=nchunks, heads=g_heads, dv=g_dv, layer=l)
        ym, c2, ss = _mamba_sample(proj, state_conv[l], ssm5, ss, ym, n_prompt, moffs, mp, expand, l)
        yh, sh = _gla_sample(proj, state_hgrn, sh, yh, n_prompt, hoffs, (hgrn_lb_logits,), hnw,
                             hgrn=True, heads=h_heads, dv=LANES, layer=l)
        yg, sg = _gla_sample(proj, state_gla, sg, yg, n_prompt, goffs, (wd, bd), gnw,
                             hgrn=False, heads=g_heads, dv=g_dv, layer=l)
        merged = _merge(ym, yh, yg, w_branch_mamba, w_branch_hgrn, w_branch_gla, l, proj, off["gate"])
        x, xw, ssq = _resid_matmul(merged, w_out, l, x, 1.0, 512, "out_proj", ffn2_norm[l], WIDE_TM_CAP)

        act = _ffn_up(xw, ssq, ffn2_w_gate_up, l)
        if l + 1 < depth:
            x, xw, ssq = _resid_matmul(act, ffn2_w_down, l, x, 0.5, 256, "ffn_down", ffn1_norm[l + 1])
        else:
            x = _resid_matmul(act, ffn2_w_down, l, x, 0.5, 256, "ffn_down")

        pc.append(c1)
        ps.append(s1.reshape(nseq, m_heads, M_HEADDIM, M_STATE))
        ph.append(h1)
        pg.append(g1)
        sc.append(c2)

    y_prompt = _final_norm(x, final_norm, 0, n_prompt, "final_norm_prompt").reshape(nseq, seq, d)
    y_sample = _final_norm(x, final_norm, n_prompt, ns, "final_norm_sample").reshape(ns, 1, d)
    return (y_prompt, y_sample, jnp.stack(pc), jnp.stack(ps), jnp.stack(ph), jnp.stack(pg),
            jnp.stack(sc), ss.reshape(state_ssm.shape), sh, sg)
```

```python
import functools

import jax
import jax.numpy as jnp
import numpy as np
from jax import lax
from jax.experimental import pallas as pl
from jax.experimental.pallas import tpu as pltpu

F32 = jnp.float32
BF16 = jnp.bfloat16
EPS = 1e-6
F32_TINY = float(np.finfo(np.float32).tiny)

CHUNK = 256
LANES = 128
SUBLANES = 8
M_HEADDIM = 64
M_STATE = 128
M_GROUPS = 2
M_CONV = 4
G_RANK = 16
G_NORMALIZER = 16.0
N_BRANCH = 3
N_LEVELS = int(np.log2(CHUNK))
VMEM_LIMIT = 56 * 1024 * 1024
STATE_BLOCK_BYTES = 4 * 1024 * 1024
WIDE_TM_CAP = 2100


def _cparams(n_axes):
    return pltpu.CompilerParams(dimension_semantics=("arbitrary",) * n_axes,
                                vmem_limit_bytes=VMEM_LIMIT)


def _dot(a, b):
    return jnp.dot(a, b, preferred_element_type=F32)


def _dot_nt(a, b):
    return lax.dot_general(a, b, (((1,), (1,)), ((), ())), preferred_element_type=F32)


def _dot_tn(a, b):
    return lax.dot_general(a, b, (((0,), (0,)), ((), ())), preferred_element_type=F32)


def _split3(x):
    hi = x.astype(BF16)
    r = x - hi.astype(F32)
    mid = r.astype(BF16)
    lo = (r - mid.astype(F32)).astype(BF16)
    return hi, mid, lo


def _dot_exact_lhs(p_bf16, x):
    hi, mid, lo = _split3(x)
    return _dot(p_bf16, hi) + _dot(p_bf16, mid) + _dot(p_bf16, lo)


def _dot_exact_rhs(x, p_bf16):
    hi, mid, lo = _split3(x)
    return _dot(hi, p_bf16) + _dot(mid, p_bf16) + _dot(lo, p_bf16)


def _dot_f32(a, b):
    ah, am, _ = _split3(a)
    bh, bm, _ = _split3(b)
    return _dot(ah, bh) + _dot(ah, bm) + _dot(am, bh)


def _sigmoid(x):
    return jax.nn.sigmoid(x)


def _silu(x):
    return x * _sigmoid(x)


def _softplus(x):
    return jnp.maximum(x, 0.0) + jnp.log1p(jnp.exp(-jnp.abs(x)))


def _log_sigmoid(x):
    return jnp.minimum(x, 0.0) - jnp.log1p(jnp.exp(-jnp.abs(x)))


def _rms(x, w):
    ms = jnp.mean(x * x, axis=-1, keepdims=True)
    return x * lax.rsqrt(ms + EPS) * w


def _pick_tm(t, cap=1100):
    best = 16
    for tm in range(16, min(t, cap) + 1, 16):
        if t % tm == 0:
            best = tm
    return best


def _row_scale(ssq_ref, d):
    return lax.rsqrt(ssq_ref[:, 0:1] * (1.0 / d) + EPS)


def _emit_normed(xn, nw_ref, xw_ref, ssq_ref, j=None):
    xw_ref[...] = (xn * nw_ref[...]).astype(BF16)
    part = jnp.broadcast_to(jnp.sum(xn * xn, axis=1, keepdims=True), ssq_ref.shape)
    if j is None:
        ssq_ref[...] = part
        return

    @pl.when(j == 0)
    def _():
        ssq_ref[...] = part

    @pl.when(j != 0)
    def _():
        ssq_ref[...] = ssq_ref[...] + part


def _prep_kernel(xp_ref, xs_ref, nw_ref, x_ref, xw_ref, ssq_ref, *, n_prompt_blocks):
    x = jnp.where(pl.program_id(0) < n_prompt_blocks, xp_ref[...], xs_ref[...])
    x_ref[...] = x
    _emit_normed(x, nw_ref, xw_ref, ssq_ref)


def _prep(xp, xs, nw):
    n_prompt, d = xp.shape
    rb = xs.shape[0]
    assert n_prompt % rb == 0
    npb = n_prompt // rb
    t = n_prompt + rb
    row = pl.BlockSpec((rb, d), lambda i: (i, 0))
    return pl.pallas_call(
        functools.partial(_prep_kernel, n_prompt_blocks=npb),
        grid=(npb + 1,),
        in_specs=[pl.BlockSpec((rb, d), lambda i: (jnp.minimum(i, npb - 1), 0)),
                  pl.BlockSpec((rb, d), lambda i: (0, 0)),
                  pl.BlockSpec((1, d), lambda i: (0, 0))],
        out_specs=[row, row, pl.BlockSpec((rb, LANES), lambda i: (i, 0))],
        out_shape=[jax.ShapeDtypeStruct((t, d), F32), jax.ShapeDtypeStruct((t, d), BF16),
                   jax.ShapeDtypeStruct((t, LANES), F32)],
        compiler_params=_cparams(1),
        name="prep",
    )(xp, xs, nw.reshape(1, d))


def _row_parts(tm):
    half = -(-(tm // 2) // 16) * 16
    return ((0, half), (half, tm - half)) if 0 < half < tm else ((0, tm),)


def _side_casts(ws, layer, n_i, nj):
    steps = n_i * nj
    ins, outs, shapes = [], [], []
    for w in ws:
        _, k, d = w.shape
        rows = k // steps
        if rows * steps != k or rows % 16:
            return None
        ins.append(pl.BlockSpec((None, rows, d), lambda i, j: (layer, i * nj + j, 0)))
        outs.append(pl.BlockSpec((rows, d), lambda i, j: (i * nj + j, 0)))
        shapes.append(jax.ShapeDtypeStruct((k, d), BF16))
    return ins, outs, shapes


def _do_side_casts(refs):
    n = len(refs) // 2
    for src, dst in zip(refs[:n], refs[n:]):
        dst[...] = src[...].astype(BF16)


def _ffn_up_kernel(xw_ref, ssq_ref, wg_ref, wu_ref, *rest, parts, n_cast):
    o_ref = rest[n_cast]
    _do_side_casts(rest[:n_cast] + rest[n_cast + 1:])
    wg = wg_ref[...].astype(BF16)
    wu = wu_ref[...].astype(BF16)
    d = xw_ref.shape[1]
    for start, rows in parts:
        sl = pl.ds(start, rows)
        r = lax.rsqrt(ssq_ref[sl, 0:1] * (1.0 / d) + EPS)
        h = xw_ref[sl, :]
        g = r * _dot(h, wg)
        u = r * _dot(h, wu)
        o_ref[sl, :] = (_silu(g) * u).astype(BF16)


def _ffn_up(xw, ssq, w_gu, layer, cast_ws=(), tn=512):
    t, d = xw.shape
    dff = w_gu.shape[2] // 2
    tm = _pick_tm(t, WIDE_TM_CAP)
    nj = dff // tn
    side = _side_casts(cast_ws, layer, t // tm, nj) if cast_ws else None
    c_in, c_out, c_shape = side if side else ([], [], [])
    outs = pl.pallas_call(
        functools.partial(_ffn_up_kernel, parts=_row_parts(tm), n_cast=len(c_in)),
        grid=(t // tm, nj),
        in_specs=[pl.BlockSpec((tm, d), lambda i, j: (i, 0)),
                  pl.BlockSpec((tm, LANES), lambda i, j: (i, 0)),
                  pl.BlockSpec((None, d, tn), lambda i, j: (layer, 0, j)),
                  pl.BlockSpec((None, d, tn), lambda i, j: (layer, 0, j + nj))] + c_in,
        out_specs=[pl.BlockSpec((tm, tn), lambda i, j: (i, j))] + c_out,
        out_shape=[jax.ShapeDtypeStruct((t, dff), BF16)] + c_shape,
        compiler_params=_cparams(2),
        name="ffn_up",
    )(xw, ssq, w_gu, w_gu, *(cast_ws if side else ()))
    return outs[0], (tuple(outs[1:]) if side else None)


def _resid_matmul_kernel(a_ref, w_ref, x_ref, *rest, scale, emit, parts):
    wb = w_ref[...].astype(BF16)
    for start, rows in parts:
        sl = pl.ds(start, rows)
        xn = x_ref[sl, :] + scale * _dot(a_ref[sl, :], wb)
        if emit:
            nw_ref, o_ref, xw_ref, ssq_ref = rest
            _emit_normed(xn, nw_ref, xw_ref.at[sl, :], ssq_ref.at[sl, :], pl.program_id(1))
        else:
            (o_ref,) = rest
        o_ref[sl, :] = xn


def _wspec(w, layer, tn):
    if w.ndim == 3:
        return pl.BlockSpec((None, w.shape[1], tn), lambda i, j: (layer, 0, j))
    return pl.BlockSpec((w.shape[0], tn), lambda i, j: (0, j))


def _resid_matmul(a, w, layer, x, scale, tn, name, next_nw=None, tm_cap=1100):
    t, k = a.shape
    d = w.shape[-1]
    tm = _pick_tm(t, tm_cap)
    emit = next_nw is not None
    tile = pl.BlockSpec((tm, tn), lambda i, j: (i, j))
    in_specs = [pl.BlockSpec((tm, k), lambda i, j: (i, 0)), _wspec(w, layer, tn), tile]
    args = [a, w, x]
    out_specs = [tile]
    out_shape = [jax.ShapeDtypeStruct((t, d), F32)]
    if emit:
        in_specs.append(pl.BlockSpec((1, tn), lambda i, j: (0, j)))
        args.append(next_nw.reshape(1, d))
        out_specs += [tile, pl.BlockSpec((tm, LANES), lambda i, j: (i, 0))]
        out_shape += [jax.ShapeDtypeStruct((t, d), BF16), jax.ShapeDtypeStruct((t, LANES), F32)]
    outs = pl.pallas_call(
        functools.partial(_resid_matmul_kernel, scale=scale, emit=emit,
                          parts=_row_parts(tm) if k <= d else ((0, tm),)),
        grid=(t // tm, d // tn),
        in_specs=in_specs,
        out_specs=out_specs,
        out_shape=out_shape,
        compiler_params=_cparams(2),
        name=name,
    )(*args)
    return outs if emit else outs[0]


def _pack_w_in_kernel(w_ref, o_ref, *, copies, n_used):
    for src, dst, width in copies:
        if width % LANES == 0:
            o_ref[:, dst:dst + width] = w_ref[src:src + width, :].T.astype(BF16)
        else:
            base = src // LANES * LANES
            lo = src - base
            assert dst % LANES == lo and lo + width <= LANES
            t = w_ref[base:base + LANES, :].T
            o_ref[:, dst:dst + width] = t[:, lo:lo + width].astype(BF16)
    pad = o_ref.shape[1] - n_used
    if pad:
        o_ref[:, n_used:] = jnp.zeros((o_ref.shape[0], pad), BF16)


def _pack_w_in(w_t, layer, copies, n_used, n_cols, tk=128):
    _, n_src, d = w_t.shape
    return pl.pallas_call(
        functools.partial(_pack_w_in_kernel, copies=copies, n_used=n_used),
        grid=(d // tk,),
        in_specs=[pl.BlockSpec((None, n_src, tk), lambda i: (layer, 0, i))],
        out_specs=pl.BlockSpec((tk, n_cols), lambda i: (i, 0)),
        out_shape=jax.ShapeDtypeStruct((d, n_cols), BF16),
        compiler_params=_cparams(1),
        name="pack_w_in",
    )(w_t)


def _in_proj_kernel(xw_ref, ssq_ref, w_ref, *rest, n_cast):
    o_ref = rest[n_cast]
    _do_side_casts(rest[:n_cast] + rest[n_cast + 1:])
    o_ref[...] = _row_scale(ssq_ref, xw_ref.shape[1]) * _dot(xw_ref[...], w_ref[...])


def _in_proj(xw, ssq, w, layer, cast_ws=(), tn=1024):
    t, d = xw.shape
    n = w.shape[1]
    tm = _pick_tm(t, WIDE_TM_CAP)
    side = _side_casts(cast_ws, layer, t // tm, n // tn) if cast_ws else None
    c_in, c_out, c_shape = side if side else ([], [], [])
    outs = pl.pallas_call(
        functools.partial(_in_proj_kernel, n_cast=len(c_in)),
        grid=(t // tm, n // tn),
        in_specs=[pl.BlockSpec((tm, d), lambda i, j: (i, 0)),
                  pl.BlockSpec((tm, LANES), lambda i, j: (i, 0)),
                  pl.BlockSpec((d, tn), lambda i, j: (0, j))] + c_in,
        out_specs=[pl.BlockSpec((tm, tn), lambda i, j: (i, j))] + c_out,
        out_shape=[jax.ShapeDtypeStruct((t, n), F32)] + c_shape,
        compiler_params=_cparams(2),
        name="in_proj",
    )(xw, ssq, w, *(cast_ws if side else ()))
    return outs[0], (tuple(outs[1:]) if side else None)


def _merge_kernel(ym_ref, yh_ref, yg_ref, wm_ref, wh_ref, wg_ref, g0_ref, g1_ref, g2_ref, o_ref, *, parts):
    wm = wm_ref[...].astype(BF16)
    wh = wh_ref[...].astype(BF16)
    wg = wg_ref[...].astype(BF16)
    for start, rows in parts:
        sl = pl.ds(start, rows)
        acc = _sigmoid(g0_ref[sl, :]) * _dot(ym_ref[sl, :], wm)
        acc = acc + _sigmoid(g1_ref[sl, :]) * _dot(yh_ref[sl, :], wh)
        acc = acc + _sigmoid(g2_ref[sl, :]) * _dot(yg_ref[sl, :], wg)
        o_ref[sl, :] = acc.astype(BF16)


def _merge(ym, yh, yg, wm, wh, wg, layer, proj, gate_off, tn=512):
    t, k = ym.shape
    d = wm.shape[-1]
    tm = _pick_tm(t)
    gb = gate_off // tn
    nb = d // tn
    yspec = pl.BlockSpec((tm, k), lambda i, j: (i, 0))
    wspec = _wspec(wm, layer, tn)

    def gspec(b):
        return pl.BlockSpec((tm, tn), lambda i, j: (i, gb + b * nb + j))

    return pl.pallas_call(
        functools.partial(_merge_kernel, parts=_row_parts(tm)),
        grid=(t // tm, nb),
        in_specs=[yspec, yspec, yspec, wspec, wspec, wspec, gspec(0), gspec(1), gspec(2)],
        out_specs=pl.BlockSpec((tm, tn), lambda i, j: (i, j)),
        out_shape=jax.ShapeDtypeStruct((t, d), BF16),
        compiler_params=_cparams(2),
        name="merge",
    )(ym, yh, yg, wm, wh, wg, proj, proj, proj)


def _final_norm_kernel(x_ref, nw_ref, o_ref):
    o_ref[...] = _rms(x_ref[...], nw_ref[...])


def _final_norm(x, nw, row0, rows, name):
    d = x.shape[1]
    tm = _pick_tm(rows)
    assert row0 % tm == 0
    return pl.pallas_call(
        _final_norm_kernel,
        grid=(rows // tm,),
        in_specs=[pl.BlockSpec((tm, d), lambda i: (row0 // tm + i, 0)),
                  pl.BlockSpec((1, d), lambda i: (0, 0))],
        out_specs=pl.BlockSpec((tm, d), lambda i: (i, 0)),
        out_shape=jax.ShapeDtypeStruct((rows, d), F32),
        compiler_params=_cparams(1),
        name=name,
    )(x, nw.reshape(1, d))


def _scan_constants():
    c = CHUNK
    pm = np.zeros((2, c, c), np.float32)
    sg = np.zeros((N_LEVELS - 1, c, LANES), np.float32)
    mk = np.zeros((N_LEVELS + 1, c, c), np.float32)
    r = np.arange(c)
    for lvl in range(N_LEVELS):
        h = c >> (lvl + 1)
        for t in range(c):
            blk, pos = divmod(t, 2 * h)
            ridx = blk * 2 * h + h - 1
            upper = pos >= h
            if upper:
                mk[lvl, t] = ((r // (2 * h)) == blk) & ((r % (2 * h)) < h)
            if h >= 4:
                sg[lvl, t] = 1.0 if upper else -1.0
            elif h == 2:
                pm[0, t] = ((r > ridx) & (r <= t)) if upper else ((r > t) & (r <= ridx))
            else:
                sg[N_LEVELS - 2, t] = 1.0 if upper else 0.0
    pm[1] = r[None, :] <= r[:, None]
    mk[N_LEVELS] = np.eye(c)
    return jnp.asarray(pm.reshape(2 * c, c), BF16), jnp.asarray(sg), jnp.asarray(mk)


def _lower_bound(logits, layer):
    m = jnp.max(logits, axis=0, keepdims=True)
    e = jnp.exp(logits - m)
    sm = e / jnp.sum(e, axis=0, keepdims=True)
    lb = jnp.zeros_like(m)
    for i in range(1, layer + 1):
        lb = lb + sm[i:i + 1, :]
    return lb


def _gla_inputs(refs, hgrn, layer):
    if hgrn:
        q_ref, f_ref, i_ref, lbl_ref = refs
        lb = _lower_bound(lbl_ref[...], layer)
        q = _silu(q_ref[...]) * (LANES ** -0.5)
        zf = f_ref[...]
        f = lb + (1.0 - lb) * _sigmoid(zf)
        gl = jnp.log(jnp.maximum(f, F32_TINY))
        k = (1.0 - lb) * _sigmoid(-zf)
        v = i_ref[...]
    else:
        q_ref, k_ref, v_ref, a_ref, wd_ref, bd_ref = refs
        q = q_ref[...] * (LANES ** -0.5)
        k = k_ref[...]
        v = v_ref[...]
        gl = _log_sigmoid(_dot_f32(a_ref[...], wd_ref[...]) + bd_ref[...]) / G_NORMALIZER
    return q, k, v, gl


def _level_factor(lvl, gl, cum, z_h2, sg_ref):
    h = CHUNK >> (lvl + 1)
    if h >= 4:
        c3 = cum.reshape(CHUNK // (2 * h), 2 * h, LANES)
        d = (c3 - c3[:, h - 1:h, :]).reshape(CHUNK, LANES)
        return jnp.exp(sg_ref[lvl] * d)
    if h == 2:
        return jnp.exp(z_h2)
    return jnp.exp(sg_ref[N_LEVELS - 2] * gl)


def _gla_prompt_kernel(*refs, hgrn, layer, hp, dv):
    n_in = 4 if hgrn else 6
    in_refs = refs[:n_in]
    g_ref, nw_ref, pm_ref, sg_ref, mk_ref, _, y_ref, so_ref, s_scr = refs[n_in:]
    c = pl.program_id(2)
    widths = (LANES, LANES, dv, LANES) if hgrn else (LANES, LANES, dv, None, LANES, LANES)

    @pl.when(c == 0)
    def _():
        s_scr[...] = jnp.zeros_like(s_scr)

    finals = []
    for u in range(hp):
        sub = [r if w is None else r.at[:, u * w:(u + 1) * w] for r, w in zip(in_refs, widths)]
        q, k, v, gl = _gla_inputs(sub, hgrn, layer)
        zz = _dot_exact_lhs(pm_ref[...], gl)
        z_h2, cum = zz[:CHUNK], zz[CHUNK:]
        qb, kb = q.astype(BF16), k.astype(BF16)
        att = mk_ref[N_LEVELS] * _dot_nt(qb, kb)
        for lvl in range(N_LEVELS):
            eb = _level_factor(lvl, gl, cum, z_h2, sg_ref).astype(BF16)
            att = att + mk_ref[lvl] * _dot_nt(qb * eb, kb * eb)
        e_cum = jnp.exp(cum)
        e_tail = jnp.exp(cum[CHUNK - 1:CHUNK, :] - cum)
        vb = v.astype(BF16)
        s = s_scr[u]
        o = _dot(att.astype(BF16), vb) + _dot((q * e_cum).astype(BF16), s.astype(BF16))
        e_last = e_cum.T[:, CHUNK - 1:CHUNK]
        s_new = e_last * s + _dot_tn((k * e_tail).astype(BF16), vb)
        s_scr[u] = s_new
        finals.append(s_new)
        gate = g_ref[:, u * dv:(u + 1) * dv]
        y_ref[:, u * dv:(u + 1) * dv] = (_rms(o, nw_ref[...]) * _silu(gate)).astype(BF16)

    @pl.when(c == pl.num_programs(2) - 1)
    def _():
        for u in range(hp):
            so_ref[u] = finals[u]


def _gla_prompt(proj, offs, extra, nw, consts, y_init, *, hgrn, nseq, nchunks, heads, dv, layer, hp=4):
    pm, sg, mk = consts
    assert heads % hp == 0

    def col(off, w):
        assert off % (hp * w) == 0
        return pl.BlockSpec((CHUNK, hp * w), lambda h, b, c: (b * nchunks + c, off // (hp * w) + h))

    def full(a):
        nd = a.ndim
        return pl.BlockSpec(a.shape, lambda h, b, c: (0,) * nd)

    if hgrn:
        oq, of, oi, og = offs
        (lbl,) = extra
        in_specs = [col(oq, LANES), col(of, LANES), col(oi, dv),
                    pl.BlockSpec((lbl.shape[0], hp * LANES), lambda h, b, c: (0, h))]
        args = [proj, proj, proj, lbl]
    else:
        oq, ok, ov, og, oa = offs
        wd, bd = extra
        in_specs = [col(oq, LANES), col(ok, LANES), col(ov, dv),
                    pl.BlockSpec((CHUNK, LANES), lambda h, b, c: (b * nchunks + c, oa // LANES)),
                    pl.BlockSpec((LANES, hp * LANES), lambda h, b, c: (0, h)),
                    pl.BlockSpec((1, hp * LANES), lambda h, b, c: (0, h))]
        args = [proj, proj, proj, proj, wd, bd]
    in_specs += [col(og, dv), full(nw), full(pm), full(sg), full(mk), pl.BlockSpec(memory_space=pl.ANY)]
    args += [proj, nw, pm, sg, mk, y_init]
    y, s = pl.pallas_call(
        functools.partial(_gla_prompt_kernel, hgrn=hgrn, layer=layer, hp=hp, dv=dv),
        grid=(heads // hp, nseq, nchunks),
        in_specs=in_specs,
        out_specs=[pl.BlockSpec((CHUNK, hp * dv), lambda h, b, c: (b * nchunks + c, h)),
                   pl.BlockSpec((None, hp, LANES, dv), lambda h, b, c: (b, h, 0, 0))],
        out_shape=[jax.ShapeDtypeStruct(y_init.shape, BF16),
                   jax.ShapeDtypeStruct((nseq, heads, LANES, dv), F32)],
        input_output_aliases={len(args) - 1: 0},
        scratch_shapes=[pltpu.VMEM((hp, LANES, dv), F32)],
        compiler_params=_cparams(3),
        name="hgrn_prompt" if hgrn else "gla_prompt",
    )(*args)
    return y, s


def _columns(x):
    nb = x.shape[0]
    if nb < LANES:
        x = jnp.concatenate([x, jnp.zeros((LANES - nb, x.shape[1]), x.dtype)], axis=0)
    return x.T


def _store_state(so_ref, layer, b, s_new, first):
    if first:
        for l in range(so_ref.shape[0]):
            so_ref[l, b] = s_new if l == layer else jnp.zeros_like(s_new)
    else:
        so_ref[b] = s_new


def _gla_sample_kernel(*refs, hgrn, nb, layer, first):
    n_in = 4 if hgrn else 6
    in_refs = refs[:n_in]
    g_ref, nw_ref, s_ref = refs[n_in:n_in + 3]
    y_ref, so_ref = refs[-2:]
    q, k, v, gl = _gla_inputs(in_refs, hgrn, layer)
    e_t = _columns(jnp.exp(gl))
    k_t = _columns(k)
    qb = q.astype(BF16)
    rowid = lax.broadcasted_iota(jnp.int32, v.shape, 0)
    o = jnp.zeros(v.shape, F32)
    for b in range(nb):
        s_new = e_t[:, b:b + 1] * s_ref[b] + k_t[:, b:b + 1] * v[b:b + 1, :]
        _store_state(so_ref, layer, b, s_new, first)
        o = jnp.where(rowid == b, _dot(qb, s_new.astype(BF16)), o)
    y_ref[...] = (_rms(o, nw_ref[...]) * _silu(g_ref[...])).astype(BF16)


def _state_specs(depth, layer, nb, dv, first):
    ispec = pl.BlockSpec((None, nb, None, LANES, dv), lambda h, i: (layer, i, h, 0, 0))
    if first:
        ospec = pl.BlockSpec((depth, nb, None, LANES, dv), lambda h, i: (0, i, h, 0, 0))
    else:
        ospec = ispec
    return ispec, ospec


def _sample_batch_block(ns, dv):
    nb = min(ns, STATE_BLOCK_BYTES // (LANES * dv * 4))
    assert ns % nb == 0
    return nb


def _gla_sample(proj, state, prev, y_all, row0, offs, extra, nw, *, hgrn, heads, dv, layer):
    depth, ns = state.shape[:2]
    nb = _sample_batch_block(ns, dv)
    rb = row0 // nb
    first = prev is None

    def col(off, w):
        return pl.BlockSpec((nb, w), lambda h, i: (rb + i, off // w + h))

    if hgrn:
        oq, of, oi, og = offs
        (lbl,) = extra
        in_specs = [col(oq, LANES), col(of, LANES), col(oi, dv),
                    pl.BlockSpec((lbl.shape[0], LANES), lambda h, i: (0, h))]
        args = [proj, proj, proj, lbl]
    else:
        oq, ok, ov, og, oa = offs
        wd, bd = extra
        in_specs = [col(oq, LANES), col(ok, LANES), col(ov, dv),
                    pl.BlockSpec((nb, LANES), lambda h, i: (rb + i, oa // LANES)),
                    pl.BlockSpec((LANES, LANES), lambda h, i: (0, h)),
                    pl.BlockSpec((1, LANES), lambda h, i: (0, h))]
        args = [proj, proj, proj, proj, wd, bd]
    ispec, ospec = _state_specs(depth, layer, nb, dv, first)
    in_specs += [col(og, dv), pl.BlockSpec(nw.shape, lambda h, i: (0, 0)), ispec,
                 pl.BlockSpec(memory_space=pl.ANY)]
    args += [proj, nw, state, y_all]
    aliases = {len(args) - 1: 0}
    if not first:
        in_specs.append(pl.BlockSpec(memory_space=pl.ANY))
        args.append(prev)
        aliases[len(args) - 1] = 1
    y, s = pl.pallas_call(
        functools.partial(_gla_sample_kernel, hgrn=hgrn, nb=nb, layer=layer, first=first),
        grid=(heads, ns // nb),
        in_specs=in_specs,
        out_specs=[pl.BlockSpec((nb, dv), lambda h, i: (rb + i, h)), ospec],
        out_shape=[jax.ShapeDtypeStruct(y_all.shape, BF16),
                   jax.ShapeDtypeStruct(state.shape, F32)],
        input_output_aliases=aliases,
        compiler_params=_cparams(2),
        name="hgrn_sample" if hgrn else "gla_sample",
    )(*args)
    return y, s


def _mamba_post(y, xs, z, dskip, nw):
    y = (y + dskip * xs) * _silu(z)
    gs = y.shape[1] // M_GROUPS
    outs = [_rms(y[:, g * gs:(g + 1) * gs], nw[:, g * gs:(g + 1) * gs]) for g in range(M_GROUPS)]
    return jnp.concatenate(outs, axis=1)


def _mamba_prompt_kernel(z_ref, x_ref, bc_ref, sm_ref, cwx_ref, cwb_ref, cbx_ref, cbb_ref,
                         dtb_ref, alog_ref, dsk_ref, nw_ref, tril_ref, mask_ref, _,
                         y_ref, cox_ref, cob_ref, so_ref, ex_scr, eb_scr, s_scr):
    c = pl.program_id(1)
    nc = pl.num_programs(1)
    tail = SUBLANES

    @pl.when(c == 0)
    def _():
        ex_scr[0:tail, :] = jnp.zeros((tail, ex_scr.shape[1]), F32)
        eb_scr[0:tail, :] = jnp.zeros((tail, eb_scr.shape[1]), F32)
        s_scr[...] = jnp.zeros_like(s_scr)

    ex_scr[tail:tail + CHUNK, :] = x_ref[...]
    eb_scr[tail:tail + CHUNK, :] = bc_ref[...]

    def conv(scr, cw_ref, cb_ref):
        acc = cb_ref[...]
        for w in range(M_CONV):
            sh = M_CONV - 1 - w
            acc = acc + cw_ref[w:w + 1, :] * scr[tail - sh:tail - sh + CHUNK, :]
        return _silu(acc)

    xs = conv(ex_scr, cwx_ref, cbx_ref)
    bcm = conv(eb_scr, cwb_ref, cbb_ref)

    @pl.when(c == nc - 1)
    def _():
        cox_ref[...] = ex_scr[tail + CHUNK - (M_CONV - 1):tail + CHUNK, :]
        cob_ref[...] = eb_scr[tail + CHUNK - (M_CONV - 1):tail + CHUNK, :]

    ex_scr[0:tail, :] = ex_scr[CHUNK:CHUNK + tail, :]
    eb_scr[0:tail, :] = eb_scr[CHUNK:CHUNK + tail, :]

    dt = _softplus(sm_ref[...] + dtb_ref[...])
    a = -jnp.exp(alog_ref[...])
    cum = _dot_exact_lhs(tril_ref[...], dt * a)
    cum_t = cum.T
    cl = cum[CHUNK - 1:CHUNK, :]
    mask = mask_ref[...] > 0.5
    gw = M_GROUPS * M_STATE
    lo = lax.broadcasted_iota(jnp.int32, (CHUNK, LANES), 1) < M_HEADDIM
    lo_r = lax.broadcasted_iota(jnp.int32, (LANES, LANES), 0) < M_HEADDIM
    n_pairs = xs.shape[1] // LANES
    per_group = n_pairs // M_GROUPS
    ys = []
    for j in range(n_pairs):
        g = j // per_group
        h0, h1 = 2 * j, 2 * j + 1
        if j % per_group == 0:
            bg = bcm[:, g * M_STATE:(g + 1) * M_STATE].astype(BF16)
            cg = bcm[:, gw + g * M_STATE:gw + (g + 1) * M_STATE].astype(BF16)
            cb = jnp.where(mask, _dot_nt(cg, bg), 0.0)

        def dec(h):
            return jnp.exp(jnp.minimum(cum[:, h:h + 1] - cum_t[h:h + 1, :], 0.0))

        x2 = xs[:, j * LANES:(j + 1) * LANES]
        dt2 = jnp.where(lo, dt[:, h0:h0 + 1], dt[:, h1:h1 + 1])
        xdt = (x2 * dt2).astype(BF16)
        y_in = jnp.where(lo, _dot((cb * dec(h0)).astype(BF16), xdt),
                         _dot((cb * dec(h1)).astype(BF16), xdt))
        ec2 = jnp.where(lo, jnp.exp(cum[:, h0:h0 + 1]), jnp.exp(cum[:, h1:h1 + 1]))
        s = s_scr[j]
        ys.append(y_in + _dot_nt(cg, s.astype(BF16)) * ec2)
        w2 = dt2 * jnp.where(lo, jnp.exp(cl[:, h0:h0 + 1] - cum[:, h0:h0 + 1]),
                             jnp.exp(cl[:, h1:h1 + 1] - cum[:, h1:h1 + 1]))
        el2 = jnp.where(lo_r, jnp.exp(cl[:, h0:h0 + 1]), jnp.exp(cl[:, h1:h1 + 1]))
        s_scr[j] = el2 * s + _dot_tn((x2 * w2).astype(BF16), bg)
    y = jnp.concatenate(ys, axis=1)
    y_ref[...] = _mamba_post(y, xs, z_ref[...], dsk_ref[...], nw_ref[...]).astype(BF16)

    @pl.when(c == nc - 1)
    def _():
        so_ref[...] = s_scr[...]


def _mamba_prompt(proj, offs, p, consts, y_init, *, nseq, nchunks):
    oz, ox, obc, osm = offs
    mw = p["dskip"].shape[1]
    bcw = p["cwb"].shape[1]
    n_pairs = mw // LANES

    def col(off, w):
        return pl.BlockSpec((CHUNK, w), lambda b, c: (b * nchunks + c, off // w))

    def full(a):
        nd = a.ndim
        return pl.BlockSpec(a.shape, lambda b, c: (0,) * nd)

    small = [p["cwx"], p["cwb"], p["cbx"], p["cbb"], p["dtb"], p["alog"], p["dskip"], p["nw"],
             consts[0], consts[1]]
    y, cox, cob, s = pl.pallas_call(
        _mamba_prompt_kernel,
        grid=(nseq, nchunks),
        in_specs=[col(oz, mw), col(ox, mw), col(obc, bcw), col(osm, LANES)] + [full(a) for a in small]
        + [pl.BlockSpec(memory_space=pl.ANY)],
        out_specs=[pl.BlockSpec((CHUNK, mw), lambda b, c: (b * nchunks + c, 0)),
                   pl.BlockSpec((None, M_CONV - 1, mw), lambda b, c: (b, 0, 0)),
                   pl.BlockSpec((None, M_CONV - 1, bcw), lambda b, c: (b, 0, 0)),
                   pl.BlockSpec((None, n_pairs, LANES, M_STATE), lambda b, c: (b, 0, 0, 0))],
        input_output_aliases={4 + len(small): 0},
        out_shape=[jax.ShapeDtypeStruct(y_init.shape, BF16),
                   jax.ShapeDtypeStruct((nseq, M_CONV - 1, mw), F32),
                   jax.ShapeDtypeStruct((nseq, M_CONV - 1, bcw), F32),
                   jax.ShapeDtypeStruct((nseq, n_pairs, LANES, M_STATE), F32)],
        scratch_shapes=[pltpu.VMEM((CHUNK + SUBLANES, mw), F32), pltpu.VMEM((CHUNK + SUBLANES, bcw), F32),
                        pltpu.VMEM((n_pairs, LANES, M_STATE), F32)],
        compiler_params=_cparams(2),
        name="mamba_prompt",
    )(proj, proj, proj, proj, *small, y_init)
    return y, jnp.concatenate([cox, cob], axis=-1), s


def _mamba_sample_prep_kernel(cs_ref, x_ref, bc_ref, sm_ref, cw_ref, cb_ref, dtb_ref, alog_ref, exp_ref,
                              co_ref, act_ref, dte_ref, ee_ref):
    cd = cw_ref.shape[1]
    new = jnp.concatenate([x_ref[...], bc_ref[...]], axis=1)
    acc = cb_ref[...] + cw_ref[M_CONV - 1:M_CONV, :] * new
    for w in range(M_CONV - 1):
        acc = acc + cw_ref[w:w + 1, :] * cs_ref[:, w * cd:(w + 1) * cd]
    act_ref[...] = _silu(acc)
    for w in range(1, M_CONV - 1):
        co_ref[:, (w - 1) * cd:w * cd] = cs_ref[:, w * cd:(w + 1) * cd]
    co_ref[:, (M_CONV - 2) * cd:(M_CONV - 1) * cd] = new
    dt = _softplus(sm_ref[...] + dtb_ref[...])
    a = -jnp.exp(alog_ref[...])
    dte_ref[...] = _dot_exact_rhs(dt, exp_ref[...])
    ee_ref[...] = jnp.exp(_dot_exact_rhs(dt * a, exp_ref[...]))


def _mamba_sample_state_kernel(x_ref, b_ref, c_ref, dte_ref, ee_ref, s_ref, *out_refs, nb, layer, first):
    y_ref, so_ref = out_refs[-2:]
    xdt_t = _columns(x_ref[...] * dte_ref[...])
    e_t = _columns(ee_ref[...])
    bv = b_ref[...]
    cb = c_ref[...].astype(BF16)
    rowid = lax.broadcasted_iota(jnp.int32, (nb, LANES), 0)
    y = jnp.zeros((nb, LANES), F32)
    for b in range(nb):
        s_new = e_t[:, b:b + 1] * s_ref[b] + xdt_t[:, b:b + 1] * bv[b:b + 1, :]
        _store_state(so_ref, layer, b, s_new, first)
        y = jnp.where(rowid == b, _dot_nt(cb, s_new.astype(BF16)), y)
    y_ref[...] = y


def _mamba_sample_post_kernel(y_ref, x_ref, z_ref, dsk_ref, nw_ref, _, o_ref):
    o_ref[...] = _mamba_post(y_ref[...], x_ref[...], z_ref[...], dsk_ref[...], nw_ref[...]).astype(BF16)


def _mamba_sample(proj, conv_state, ssm_state, prev, y_all, row0, offs, p, expand, layer):
    oz, ox, obc, osm = offs
    depth, ns = ssm_state.shape[:2]
    nb = _sample_batch_block(ns, M_STATE)
    mw = p["dskip"].shape[1]
    bcw = p["cwb"].shape[1]
    cd = mw + bcw
    n_pairs = mw // LANES
    per_group = n_pairs // M_GROUPS
    rb = row0 // ns
    first = prev is None
    cw = jnp.concatenate([p["cwx"], p["cwb"]], axis=1)
    cb = jnp.concatenate([p["cbx"], p["cbb"]], axis=1)

    def full1(a):
        nd = a.ndim
        return pl.BlockSpec(a.shape, lambda i: (0,) * nd)

    cs2 = conv_state.reshape(ns, (M_CONV - 1) * cd)
    small = [cw, cb, p["dtb"], p["alog"], expand]
    co, act, dte, ee = pl.pallas_call(
        _mamba_sample_prep_kernel,
        grid=(1,),
        in_specs=[full1(cs2),
                  pl.BlockSpec((ns, mw), lambda i: (rb, ox // mw)),
                  pl.BlockSpec((ns, bcw), lambda i: (rb, obc // bcw)),
                  pl.BlockSpec((ns, LANES), lambda i: (rb, osm // LANES))] + [full1(a) for a in small],
        out_specs=[pl.BlockSpec((ns, (M_CONV - 1) * cd), lambda i: (0, 0)),
                   pl.BlockSpec((ns, cd), lambda i: (0, 0)),
                   pl.BlockSpec((ns, mw), lambda i: (0, 0)),
                   pl.BlockSpec((ns, mw), lambda i: (0, 0))],
        out_shape=[jax.ShapeDtypeStruct((ns, (M_CONV - 1) * cd), F32),
                   jax.ShapeDtypeStruct((ns, cd), F32),
                   jax.ShapeDtypeStruct((ns, mw), F32),
                   jax.ShapeDtypeStruct((ns, mw), F32)],
        compiler_params=_cparams(1),
        name="mamba_sample_prep",
    )(cs2, proj, proj, proj, *small)

    bblk = mw // LANES
    cblk = bblk + M_GROUPS * M_STATE // LANES
    ispec, ospec = _state_specs(depth, layer, nb, M_STATE, first)
    in_specs = [pl.BlockSpec((nb, LANES), lambda j, i: (i, j)),
                pl.BlockSpec((nb, LANES), lambda j, i: (i, bblk + j // per_group)),
                pl.BlockSpec((nb, LANES), lambda j, i: (i, cblk + j // per_group)),
                pl.BlockSpec((nb, LANES), lambda j, i: (i, j)),
                pl.BlockSpec((nb, LANES), lambda j, i: (i, j)),
                ispec]
    args = [act, act, act, dte, ee, ssm_state]
    aliases = {}
    if not first:
        in_specs.append(pl.BlockSpec(memory_space=pl.ANY))
        args.append(prev)
        aliases = {len(args) - 1: 1}
    y, s_new = pl.pallas_call(
        functools.partial(_mamba_sample_state_kernel, nb=nb, layer=layer, first=first),
        grid=(n_pairs, ns // nb),
        in_specs=in_specs,
        out_specs=[pl.BlockSpec((nb, LANES), lambda j, i: (i, j)), ospec],
        out_shape=[jax.ShapeDtypeStruct((ns, mw), F32),
                   jax.ShapeDtypeStruct(ssm_state.shape, F32)],
        input_output_aliases=aliases,
        compiler_params=_cparams(2),
        name="mamba_sample_state",
    )(*args)

    ym = pl.pallas_call(
        _mamba_sample_post_kernel,
        grid=(1,),
        in_specs=[full1(y),
                  pl.BlockSpec((ns, mw), lambda i: (0, 0)),
                  pl.BlockSpec((ns, mw), lambda i: (rb, oz // mw)),
                  full1(p["dskip"]), full1(p["nw"]), pl.BlockSpec(memory_space=pl.ANY)],
        out_specs=pl.BlockSpec((ns, mw), lambda i: (rb, 0)),
        out_shape=jax.ShapeDtypeStruct(y_all.shape, BF16),
        input_output_aliases={5: 0},
        compiler_params=_cparams(1),
        name="mamba_sample_post",
    )(y, act, proj, p["dskip"], p["nw"], y_all)
    return ym, co.reshape(ns, M_CONV - 1, cd), s_new


def _pad_lanes(v, n=LANES):
    v = v.reshape(1, -1)
    return jnp.pad(v, ((0, 0), (0, n - v.shape[1])))


def kernel(x_prompt, x_sample, state_conv, state_ssm, state_hgrn, state_gla, ffn1_norm, ffn1_w_gate_up, ffn1_w_down, mix_norm, w_in, conv_w, conv_b, dt_bias, a_log, d_skip, mamba_norm, hgrn_lb_logits, hgrn_norm, gla_w_decay, gla_b_decay, gla_norm, w_branch_mamba, w_branch_hgrn, w_branch_gla, w_out, ffn2_norm, ffn2_w_gate_up, ffn2_w_down, final_norm):
    nseq, seq, d = x_prompt.shape
    ns = x_sample.shape[0]
    depth = w_in.shape[0]
    nchunks = seq // CHUNK
    n_prompt = nseq * seq
    mw = w_branch_mamba.shape[1]
    hw = w_branch_hgrn.shape[1]
    gw = w_branch_gla.shape[1]
    gk = gla_w_decay.shape[2]
    m_heads = dt_bias.shape[1]
    h_heads = state_hgrn.shape[2]
    g_heads = state_gla.shape[2]
    g_dv = state_gla.shape[4]
    bcw = 2 * M_GROUPS * M_STATE
    n_pairs = mw // LANES
    assert seq % CHUNK == 0 and n_prompt % ns == 0 and ns % 16 == 0
    assert m_heads <= G_RANK + m_heads <= LANES and gk // g_heads == LANES and hw // h_heads == LANES

    seg_w = {"z": mw, "xs": mw, "bc": bcw, "dt": m_heads, "hq": hw, "hf": hw, "hi": hw, "hg": hw,
             "gq": gk, "gk": gk, "gv": gw, "gg": gw, "ga": G_RANK, "gate": N_BRANCH * d}
    src_order = ("z", "xs", "bc", "dt", "hq", "hf", "hi", "hg", "gq", "gk", "gv", "gg", "ga", "gate")
    dst_order = ("z", "xs", "hq", "hf", "hi", "hg", "gv", "gg", "gq", "gk", "bc", "gate", "dt", "ga")
    src, off = {}, {}
    pos = 0
    for name in src_order:
        src[name] = pos
        pos += seg_w[name]
    pos = 0
    for name in dst_order:
        off[name] = pos
        pos += seg_w[name]
    n_used = pos
    n_cols = -(-n_used // 1024) * 1024
    copies = tuple((src[name], off[name], seg_w[name]) for name in dst_order)
    oz, ox, obc, osm = off["z"], off["xs"], off["bc"], off["dt"]
    assert off["ga"] == osm + m_heads and osm % LANES == 0

    consts = _scan_constants()
    tril = consts[0][CHUNK:]
    mconsts = (tril, jnp.asarray(np.tril(np.ones((CHUNK, CHUNK), np.float32))))
    expand_np = np.zeros((LANES, mw), np.float32)
    for h in range(m_heads):
        expand_np[h, h * M_HEADDIM:(h + 1) * M_HEADDIM] = 1.0
    expand = jnp.asarray(expand_np, BF16)

    x, xw, ssq = _prep(x_prompt.reshape(n_prompt, d), x_sample.reshape(ns, d), ffn1_norm[0])
    t_all = n_prompt + ns
    w_in_t = jnp.swapaxes(w_in, 1, 2)
    ssm5 = state_ssm.reshape(depth, ns, n_pairs, LANES, M_STATE)

    pc, ps, ph, pg, sc = [], [], [], [], []
    ss = sh = sg = None
    for l in range(depth):
        w_perm = _pack_w_in(w_in_t, l, copies, n_used, n_cols)
        mp = {
            "cwx": conv_w[l][:, :mw], "cwb": conv_w[l][:, mw:],
            "cbx": conv_b[l][:mw].reshape(1, mw), "cbb": conv_b[l][mw:].reshape(1, bcw),
            "dtb": _pad_lanes(dt_bias[l]), "alog": _pad_lanes(a_log[l]),
            "dskip": jnp.repeat(d_skip[l], M_HEADDIM).reshape(1, mw),
            "nw": mamba_norm[l].reshape(1, mw),
        }
        wd = jnp.zeros((LANES, gk), F32).at[m_heads:m_heads + G_RANK].set(gla_w_decay[l])
        bd = gla_b_decay[l].reshape(1, gk)
        hnw = hgrn_norm[l].reshape(1, LANES)
        gnw = gla_norm[l].reshape(1, g_dv)

        act, cast = _ffn_up(xw, ssq, ffn1_w_gate_up, l, (ffn1_w_down,))
        w_dn, tn_dn = (cast[0], 512) if cast else (ffn1_w_down, 256)
        x, xw, ssq = _resid_matmul(act, w_dn, l, x, 0.5, tn_dn, "ffn_down", mix_norm[l])
        proj, cast = _in_proj(xw, ssq, w_perm, l, (w_branch_mamba, w_branch_hgrn, w_branch_gla, w_out))
        w_bm, w_bh, w_bg, w_o = cast if cast else (w_branch_mamba, w_branch_hgrn, w_branch_gla, w_out)

        moffs = (oz, ox, obc, osm)
        hoffs = (off["hq"], off["hf"], off["hi"], off["hg"])
        goffs = (off["gq"], off["gk"], off["gv"], off["gg"], osm)
        ym, c1, s1 = _mamba_prompt(proj, moffs, mp, mconsts, jnp.zeros((t_all, mw), BF16),
                                   nseq=nseq, nchunks=nchunks)
        yh, h1 = _gla_prompt(proj, hoffs, (hgrn_lb_logits,), hnw, consts, jnp.zeros((t_all, hw), BF16),
                             hgrn=True, nseq=nseq, nchunks=nchunks, heads=h_heads, dv=LANES, layer=l)
        yg, g1 = _gla_prompt(proj, goffs, (wd, bd), gnw, consts, jnp.zeros((t_all, gw), BF16),
                             hgrn=False, nseq=nseq, nchunks=nchunks, heads=g_heads, dv=g_dv, layer=l)
        ym, c2, ss = _mamba_sample(proj, state_conv[l], ssm5, ss, ym, n_prompt, moffs, mp, expand, l)
        yh, sh = _gla_sample(proj, state_hgrn, sh, yh, n_prompt, hoffs, (hgrn_lb_logits,), hnw,
                             hgrn=True, heads=h_heads, dv=LANES, layer=l)
        yg, sg = _gla_sample(proj, state_gla, sg, yg, n_prompt, goffs, (wd, bd), gnw,
                             hgrn=False, heads=g_heads, dv=g_dv, layer=l)
        merged = _merge(ym, yh, yg, w_bm, w_bh, w_bg, l, proj, off["gate"])
        x, xw, ssq = _resid_matmul(merged, w_o, l, x, 1.0, 512, "out_proj", ffn2_norm[l], WIDE_TM_CAP)

        act, cast = _ffn_up(xw, ssq, ffn2_w_gate_up, l, (ffn2_w_down,))
        w_dn, tn_dn = (cast[0], 512) if cast else (ffn2_w_down, 256)
        if l + 1 < depth:
            x, xw, ssq = _resid_matmul(act, w_dn, l, x, 0.5, tn_dn, "ffn_down", ffn1_norm[l + 1])
        else:
            x = _resid_matmul(act, w_dn, l, x, 0.5, tn_dn, "ffn_down")

        pc.append(c1)
        ps.append(s1.reshape(nseq, m_heads, M_HEADDIM, M_STATE))
        ph.append(h1)
        pg.append(g1)
        sc.append(c2)

    y_prompt = _final_norm(x, final_norm, 0, n_prompt, "final_norm_prompt").reshape(nseq, seq, d)
    y_sample = _final_norm(x, final_norm, n_prompt, ns, "final_norm_sample").reshape(ns, 1, d)
    return (y_prompt, y_sample, jnp.stack(pc), jnp.stack(ps), jnp.stack(ph), jnp.stack(pg),
            jnp.stack(sc), ss.reshape(state_ssm.shape), sh, sg)
```

```python
import functools

import jax
import jax.numpy as jnp
import numpy as np
from jax import lax
from jax.experimental import pallas as pl
from jax.experimental.pallas import tpu as pltpu

F32 = jnp.float32
BF16 = jnp.bfloat16
EPS = 1e-6
F32_TINY = float(np.finfo(np.float32).tiny)

CHUNK = 256
LANES = 128
SUBLANES = 8
M_HEADDIM = 64
M_STATE = 128
M_GROUPS = 2
M_CONV = 4
G_RANK = 16
G_NORMALIZER = 16.0
N_BRANCH = 3
GLA_SUB = 128
VMEM_LIMIT = 56 * 1024 * 1024
STATE_BLOCK_BYTES = 4 * 1024 * 1024
WIDE_TM_CAP = 2100


def _cparams(n_axes):
    return pltpu.CompilerParams(dimension_semantics=("arbitrary",) * n_axes,
                                vmem_limit_bytes=VMEM_LIMIT)


def _dot(a, b):
    return jnp.dot(a, b, preferred_element_type=F32)


def _dot_nt(a, b):
    return lax.dot_general(a, b, (((1,), (1,)), ((), ())), preferred_element_type=F32)


def _dot_tn(a, b):
    return lax.dot_general(a, b, (((0,), (0,)), ((), ())), preferred_element_type=F32)


def _split3(x):
    hi = x.astype(BF16)
    r = x - hi.astype(F32)
    mid = r.astype(BF16)
    lo = (r - mid.astype(F32)).astype(BF16)
    return hi, mid, lo


def _dot_exact_lhs(p_bf16, x):
    hi, mid, lo = _split3(x)
    return _dot(p_bf16, hi) + _dot(p_bf16, mid) + _dot(p_bf16, lo)


def _dot_exact_rhs(x, p_bf16):
    hi, mid, lo = _split3(x)
    return _dot(hi, p_bf16) + _dot(mid, p_bf16) + _dot(lo, p_bf16)


def _dot_f32(a, b):
    ah, am, _ = _split3(a)
    bh, bm, _ = _split3(b)
    return _dot(ah, bh) + _dot(ah, bm) + _dot(am, bh)


def _sigmoid(x):
    return jax.nn.sigmoid(x)


def _silu(x):
    return x * _sigmoid(x)


def _softplus(x):
    return jnp.maximum(x, 0.0) + jnp.log1p(jnp.exp(-jnp.abs(x)))


def _log_sigmoid(x):
    return jnp.minimum(x, 0.0) - jnp.log(1.0 + jnp.exp(-jnp.abs(x)))


def _rms(x, w):
    ms = jnp.mean(x * x, axis=-1, keepdims=True)
    return x * lax.rsqrt(ms + EPS) * w


def _pick_tm(t, cap=1100):
    best = 16
    for tm in range(16, min(t, cap) + 1, 16):
        if t % tm == 0:
            best = tm
    return best


def _row_scale(ssq_ref, d):
    return lax.rsqrt(ssq_ref[:, 0:1] * (1.0 / d) + EPS)


def _emit_normed(xn, nw_ref, xw_ref, ssq_ref, j=None):
    xw_ref[...] = (xn * nw_ref[...]).astype(BF16)
    part = jnp.broadcast_to(jnp.sum(xn * xn, axis=1, keepdims=True), ssq_ref.shape)
    if j is None:
        ssq_ref[...] = part
        return

    @pl.when(j == 0)
    def _():
        ssq_ref[...] = part

    @pl.when(j != 0)
    def _():
        ssq_ref[...] = ssq_ref[...] + part


def _prep_kernel(xp_ref, xs_ref, nw_ref, x_ref, xw_ref, ssq_ref, *, n_prompt_blocks):
    x = jnp.where(pl.program_id(0) < n_prompt_blocks, xp_ref[...], xs_ref[...])
    x_ref[...] = x
    _emit_normed(x, nw_ref, xw_ref, ssq_ref)


def _prep(xp, xs, nw):
    n_prompt, d = xp.shape
    rb = xs.shape[0]
    assert n_prompt % rb == 0
    npb = n_prompt // rb
    t = n_prompt + rb
    row = pl.BlockSpec((rb, d), lambda i: (i, 0))
    return pl.pallas_call(
        functools.partial(_prep_kernel, n_prompt_blocks=npb),
        grid=(npb + 1,),
        in_specs=[pl.BlockSpec((rb, d), lambda i: (jnp.minimum(i, npb - 1), 0)),
                  pl.BlockSpec((rb, d), lambda i: (0, 0)),
                  pl.BlockSpec((1, d), lambda i: (0, 0))],
        out_specs=[row, row, pl.BlockSpec((rb, LANES), lambda i: (i, 0))],
        out_shape=[jax.ShapeDtypeStruct((t, d), F32), jax.ShapeDtypeStruct((t, d), BF16),
                   jax.ShapeDtypeStruct((t, LANES), F32)],
        compiler_params=_cparams(1),
        name="prep",
    )(xp, xs, nw.reshape(1, d))


def _row_parts(tm):
    half = -(-(tm // 2) // 16) * 16
    return ((0, half), (half, tm - half)) if 0 < half < tm else ((0, tm),)


def _side_casts(ws, layer, n_i, nj):
    steps = n_i * nj
    ins, outs, shapes = [], [], []
    for w in ws:
        _, k, d = w.shape
        rows = k // steps
        if rows * steps != k or rows % 16:
            return None
        ins.append(pl.BlockSpec((None, rows, d), lambda i, j: (layer, i * nj + j, 0)))
        outs.append(pl.BlockSpec((rows, d), lambda i, j: (i * nj + j, 0)))
        shapes.append(jax.ShapeDtypeStruct((k, d), BF16))
    return ins, outs, shapes


def _do_side_casts(refs):
    n = len(refs) // 2
    for src, dst in zip(refs[:n], refs[n:]):
        dst[...] = src[...].astype(BF16)


def _ffn_up_kernel(xw_ref, ssq_ref, wg_ref, wu_ref, *rest, parts, n_cast):
    o_ref = rest[n_cast]
    _do_side_casts(rest[:n_cast] + rest[n_cast + 1:])
    wg = wg_ref[...].astype(BF16)
    wu = wu_ref[...].astype(BF16)
    d = xw_ref.shape[1]
    for start, rows in parts:
        sl = pl.ds(start, rows)
        r = lax.rsqrt(ssq_ref[sl, 0:1] * (1.0 / d) + EPS)
        h = xw_ref[sl, :]
        g = r * _dot(h, wg)
        u = r * _dot(h, wu)
        o_ref[sl, :] = (_silu(g) * u).astype(BF16)


def _ffn_up(xw, ssq, w_gu, layer, cast_ws=(), tn=512):
    t, d = xw.shape
    dff = w_gu.shape[2] // 2
    tm = _pick_tm(t, WIDE_TM_CAP)
    nj = dff // tn
    side = _side_casts(cast_ws, layer, t // tm, nj) if cast_ws else None
    c_in, c_out, c_shape = side if side else ([], [], [])
    outs = pl.pallas_call(
        functools.partial(_ffn_up_kernel, parts=_row_parts(tm), n_cast=len(c_in)),
        grid=(t // tm, nj),
        in_specs=[pl.BlockSpec((tm, d), lambda i, j: (i, 0)),
                  pl.BlockSpec((tm, LANES), lambda i, j: (i, 0)),
                  pl.BlockSpec((None, d, tn), lambda i, j: (layer, 0, j)),
                  pl.BlockSpec((None, d, tn), lambda i, j: (layer, 0, j + nj))] + c_in,
        out_specs=[pl.BlockSpec((tm, tn), lambda i, j: (i, j))] + c_out,
        out_shape=[jax.ShapeDtypeStruct((t, dff), BF16)] + c_shape,
        compiler_params=_cparams(2),
        name="ffn_up",
    )(xw, ssq, w_gu, w_gu, *(cast_ws if side else ()))
    return outs[0], (tuple(outs[1:]) if side else None)


def _resid_matmul_kernel(a_ref, w_ref, x_ref, *rest, scale, emit, parts):
    wb = w_ref[...].astype(BF16)
    for start, rows in parts:
        sl = pl.ds(start, rows)
        xn = x_ref[sl, :] + scale * _dot(a_ref[sl, :], wb)
        if emit:
            nw_ref, o_ref, xw_ref, ssq_ref = rest
            _emit_normed(xn, nw_ref, xw_ref.at[sl, :], ssq_ref.at[sl, :], pl.program_id(1))
        else:
            (o_ref,) = rest
        o_ref[sl, :] = xn


def _wspec(w, layer, tn):
    if w.ndim == 3:
        return pl.BlockSpec((None, w.shape[1], tn), lambda i, j: (layer, 0, j))
    return pl.BlockSpec((w.shape[0], tn), lambda i, j: (0, j))


def _resid_matmul(a, w, layer, x, scale, tn, name, next_nw=None, tm_cap=1100):
    t, k = a.shape
    d = w.shape[-1]
    tm = _pick_tm(t, tm_cap)
    emit = next_nw is not None
    tile = pl.BlockSpec((tm, tn), lambda i, j: (i, j))
    in_specs = [pl.BlockSpec((tm, k), lambda i, j: (i, 0)), _wspec(w, layer, tn), tile]
    args = [a, w, x]
    out_specs = [tile]
    out_shape = [jax.ShapeDtypeStruct((t, d), F32)]
    if emit:
        in_specs.append(pl.BlockSpec((1, tn), lambda i, j: (0, j)))
        args.append(next_nw.reshape(1, d))
        out_specs += [tile, pl.BlockSpec((tm, LANES), lambda i, j: (i, 0))]
        out_shape += [jax.ShapeDtypeStruct((t, d), BF16), jax.ShapeDtypeStruct((t, LANES), F32)]
    outs = pl.pallas_call(
        functools.partial(_resid_matmul_kernel, scale=scale, emit=emit,
                          parts=_row_parts(tm) if k <= d else ((0, tm),)),
        grid=(t // tm, d // tn),
        in_specs=in_specs,
        out_specs=out_specs,
        out_shape=out_shape,
        compiler_params=_cparams(2),
        name=name,
    )(*args)
    return outs if emit else outs[0]


def _pack_w_in_kernel(w_ref, o_ref, *, copies, n_used):
    for src, dst, width in copies:
        if width % LANES == 0:
            o_ref[:, dst:dst + width] = w_ref[src:src + width, :].T.astype(BF16)
        else:
            base = src // LANES * LANES
            lo = src - base
            assert dst % LANES == lo and lo + width <= LANES
            t = w_ref[base:base + LANES, :].T
            o_ref[:, dst:dst + width] = t[:, lo:lo + width].astype(BF16)
    pad = o_ref.shape[1] - n_used
    if pad:
        o_ref[:, n_used:] = jnp.zeros((o_ref.shape[0], pad), BF16)


def _pack_w_in(w_t, layer, copies, n_used, n_cols, tk=128):
    _, n_src, d = w_t.shape
    return pl.pallas_call(
        functools.partial(_pack_w_in_kernel, copies=copies, n_used=n_used),
        grid=(d // tk,),
        in_specs=[pl.BlockSpec((None, n_src, tk), lambda i: (layer, 0, i))],
        out_specs=pl.BlockSpec((tk, n_cols), lambda i: (i, 0)),
        out_shape=jax.ShapeDtypeStruct((d, n_cols), BF16),
        compiler_params=_cparams(1),
        name="pack_w_in",
    )(w_t)


def _in_proj_kernel(xw_ref, ssq_ref, w_ref, *rest, n_cast):
    o_ref = rest[n_cast]
    _do_side_casts(rest[:n_cast] + rest[n_cast + 1:])
    o_ref[...] = _row_scale(ssq_ref, xw_ref.shape[1]) * _dot(xw_ref[...], w_ref[...])


def _in_proj(xw, ssq, w, layer, cast_ws=(), tn=1024):
    t, d = xw.shape
    n = w.shape[1]
    tm = _pick_tm(t, WIDE_TM_CAP)
    side = _side_casts(cast_ws, layer, t // tm, n // tn) if cast_ws else None
    c_in, c_out, c_shape = side if side else ([], [], [])
    outs = pl.pallas_call(
        functools.partial(_in_proj_kernel, n_cast=len(c_in)),
        grid=(t // tm, n // tn),
        in_specs=[pl.BlockSpec((tm, d), lambda i, j: (i, 0)),
                  pl.BlockSpec((tm, LANES), lambda i, j: (i, 0)),
                  pl.BlockSpec((d, tn), lambda i, j: (0, j))] + c_in,
        out_specs=[pl.BlockSpec((tm, tn), lambda i, j: (i, j))] + c_out,
        out_shape=[jax.ShapeDtypeStruct((t, n), F32)] + c_shape,
        compiler_params=_cparams(2),
        name="in_proj",
    )(xw, ssq, w, *(cast_ws if side else ()))
    return outs[0], (tuple(outs[1:]) if side else None)


def _merge_kernel(ym_ref, yh_ref, yg_ref, wm_ref, wh_ref, wg_ref, g0_ref, g1_ref, g2_ref, o_ref, *, parts):
    wm = wm_ref[...].astype(BF16)
    wh = wh_ref[...].astype(BF16)
    wg = wg_ref[...].astype(BF16)
    for start, rows in parts:
        sl = pl.ds(start, rows)
        acc = _sigmoid(g0_ref[sl, :]) * _dot(ym_ref[sl, :], wm)
        acc = acc + _sigmoid(g1_ref[sl, :]) * _dot(yh_ref[sl, :], wh)
        acc = acc + _sigmoid(g2_ref[sl, :]) * _dot(yg_ref[sl, :], wg)
        o_ref[sl, :] = acc.astype(BF16)


def _merge(ym, yh, yg, wm, wh, wg, layer, proj, gate_off, tn=512):
    t, k = ym.shape
    d = wm.shape[-1]
    tm = _pick_tm(t)
    gb = gate_off // tn
    nb = d // tn
    yspec = pl.BlockSpec((tm, k), lambda i, j: (i, 0))
    wspec = _wspec(wm, layer, tn)

    def gspec(b):
        return pl.BlockSpec((tm, tn), lambda i, j: (i, gb + b * nb + j))

    return pl.pallas_call(
        functools.partial(_merge_kernel, parts=_row_parts(tm)),
        grid=(t // tm, nb),
        in_specs=[yspec, yspec, yspec, wspec, wspec, wspec, gspec(0), gspec(1), gspec(2)],
        out_specs=pl.BlockSpec((tm, tn), lambda i, j: (i, j)),
        out_shape=jax.ShapeDtypeStruct((t, d), BF16),
        compiler_params=_cparams(2),
        name="merge",
    )(ym, yh, yg, wm, wh, wg, proj, proj, proj)


def _final_norm_kernel(x_ref, nw_ref, o_ref):
    o_ref[...] = _rms(x_ref[...], nw_ref[...])


def _final_norm(x, nw, row0, rows, name):
    d = x.shape[1]
    tm = _pick_tm(rows)
    assert row0 % tm == 0
    return pl.pallas_call(
        _final_norm_kernel,
        grid=(rows // tm,),
        in_specs=[pl.BlockSpec((tm, d), lambda i: (row0 // tm + i, 0)),
                  pl.BlockSpec((1, d), lambda i: (0, 0))],
        out_specs=pl.BlockSpec((tm, d), lambda i: (i, 0)),
        out_shape=jax.ShapeDtypeStruct((rows, d), F32),
        compiler_params=_cparams(1),
        name=name,
    )(x, nw.reshape(1, d))


def _scan_constants(c):
    N_LEVELS = int(np.log2(c))
    assert 1 << N_LEVELS == c and N_LEVELS >= 3
    pm = np.zeros((2, c, c), np.float32)
    sg = np.zeros((N_LEVELS - 1, c, LANES), np.float32)
    mk = np.zeros((N_LEVELS + 1, c, c), np.float32)
    r = np.arange(c)
    for lvl in range(N_LEVELS):
        h = c >> (lvl + 1)
        for t in range(c):
            blk, pos = divmod(t, 2 * h)
            ridx = blk * 2 * h + h - 1
            upper = pos >= h
            if upper:
                mk[lvl, t] = ((r // (2 * h)) == blk) & ((r % (2 * h)) < h)
            if h >= 4:
                sg[lvl, t] = 1.0 if upper else -1.0
            elif h == 2:
                pm[0, t] = ((r > ridx) & (r <= t)) if upper else ((r > t) & (r <= ridx))
            else:
                sg[N_LEVELS - 2, t] = 1.0 if upper else 0.0
    pm[1] = r[None, :] <= r[:, None]
    mk[N_LEVELS] = np.eye(c)
    return jnp.asarray(pm.reshape(2 * c, c), BF16), jnp.asarray(sg), jnp.asarray(mk)


def _lower_bound(logits, layer):
    m = jnp.max(logits, axis=0, keepdims=True)
    e = jnp.exp(logits - m)
    sm = e / jnp.sum(e, axis=0, keepdims=True)
    lb = jnp.zeros_like(m)
    for i in range(1, layer + 1):
        lb = lb + sm[i:i + 1, :]
    return lb


def _gla_inputs(refs, hgrn, layer):
    if hgrn:
        q_ref, f_ref, i_ref, lbl_ref = refs
        lb = _lower_bound(lbl_ref[...], layer)
        q = _silu(q_ref[...]) * (LANES ** -0.5)
        zf = f_ref[...]
        f = lb + (1.0 - lb) * _sigmoid(zf)
        gl = jnp.log(jnp.maximum(f, F32_TINY))
        k = (1.0 - lb) * _sigmoid(-zf)
        v = i_ref[...]
    else:
        q_ref, k_ref, v_ref, a_ref, wd_ref, bd_ref = refs
        q = q_ref[...] * (LANES ** -0.5)
        k = k_ref[...]
        v = v_ref[...]
        gl = _log_sigmoid(_dot_f32(a_ref[...], wd_ref[...]) + bd_ref[...]) / G_NORMALIZER
    return q, k, v, gl


def _level_factor(lvl, gl, cum, z_h2, sg_ref):
    c = gl.shape[0]
    n_levels = sg_ref.shape[0] + 1
    h = c >> (lvl + 1)
    if h >= 4:
        c3 = cum.reshape(c // (2 * h), 2 * h, LANES)
        d = (c3 - c3[:, h - 1:h, :]).reshape(c, LANES)
        return jnp.exp(sg_ref[lvl] * d)
    if h == 2:
        return jnp.exp(z_h2)
    return jnp.exp(sg_ref[n_levels - 2] * gl)


def _gla_prompt_kernel(*refs, hgrn, layer, hp, dv):
    n_in = 4 if hgrn else 6
    in_refs = refs[:n_in]
    g_ref, nw_ref, pm_ref, sg_ref, mk_ref, _, y_ref, so_ref, s_scr = refs[n_in:]
    c = pl.program_id(2)
    sub = mk_ref.shape[1]
    n_levels = mk_ref.shape[0] - 1

    @pl.when(c == 0)
    def _():
        s_scr[...] = jnp.zeros_like(s_scr)

    q_all, k_all, v_all, gl_all = _gla_inputs(in_refs, hgrn, layer)
    finals = []
    for u in range(hp):
        s = s_scr[u]
        for r0 in range(0, CHUNK, sub):
            rows = pl.ds(r0, sub)
            q, k, gl = (a[r0:r0 + sub, u * LANES:(u + 1) * LANES] for a in (q_all, k_all, gl_all))
            v = v_all[r0:r0 + sub, u * dv:(u + 1) * dv]
            zz = _dot_exact_lhs(pm_ref[...], gl)
            z_h2, cum = zz[:sub], zz[sub:]
            qb, kb = q.astype(BF16), k.astype(BF16)
            att = mk_ref[n_levels] * _dot_nt(qb, kb)
            for lvl in range(n_levels):
                eb = _level_factor(lvl, gl, cum, z_h2, sg_ref).astype(BF16)
                att = att + mk_ref[lvl] * _dot_nt(qb * eb, kb * eb)
            e_cum = jnp.exp(cum)
            e_tail = jnp.exp(cum[sub - 1:sub, :] - cum)
            vb = v.astype(BF16)
            o = _dot(att.astype(BF16), vb) + _dot((q * e_cum).astype(BF16), s.astype(BF16))
            e_last = e_cum.T[:, sub - 1:sub]
            s = e_last * s + _dot_tn((k * e_tail).astype(BF16), vb)
            gate = g_ref[rows, u * dv:(u + 1) * dv]
            y_ref[rows, u * dv:(u + 1) * dv] = (_rms(o, nw_ref[...]) * _silu(gate)).astype(BF16)
        s_scr[u] = s
        finals.append(s)

    @pl.when(c == pl.num_programs(2) - 1)
    def _():
        for u in range(hp):
            so_ref[u] = finals[u]


def _gla_prompt(proj, offs, extra, nw, consts, y_init, *, hgrn, nseq, nchunks, heads, dv, layer, hp=4):
    pm, sg, mk = consts
    assert heads % hp == 0

    def col(off, w):
        assert off % (hp * w) == 0
        return pl.BlockSpec((CHUNK, hp * w), lambda h, b, c: (b * nchunks + c, off // (hp * w) + h))

    def full(a):
        nd = a.ndim
        return pl.BlockSpec(a.shape, lambda h, b, c: (0,) * nd)

    if hgrn:
        oq, of, oi, og = offs
        (lbl,) = extra
        in_specs = [col(oq, LANES), col(of, LANES), col(oi, dv),
                    pl.BlockSpec((lbl.shape[0], hp * LANES), lambda h, b, c: (0, h))]
        args = [proj, proj, proj, lbl]
    else:
        oq, ok, ov, og, oa = offs
        wd, bd = extra
        in_specs = [col(oq, LANES), col(ok, LANES), col(ov, dv),
                    pl.BlockSpec((CHUNK, LANES), lambda h, b, c: (b * nchunks + c, oa // LANES)),
                    pl.BlockSpec((LANES, hp * LANES), lambda h, b, c: (0, h)),
                    pl.BlockSpec((1, hp * LANES), lambda h, b, c: (0, h))]
        args = [proj, proj, proj, proj, wd, bd]
    in_specs += [col(og, dv), full(nw), full(pm), full(sg), full(mk), pl.BlockSpec(memory_space=pl.ANY)]
    args += [proj, nw, pm, sg, mk, y_init]
    y, s = pl.pallas_call(
        functools.partial(_gla_prompt_kernel, hgrn=hgrn, layer=layer, hp=hp, dv=dv),
        grid=(heads // hp, nseq, nchunks),
        in_specs=in_specs,
        out_specs=[pl.BlockSpec((CHUNK, hp * dv), lambda h, b, c: (b * nchunks + c, h)),
                   pl.BlockSpec((None, hp, LANES, dv), lambda h, b, c: (b, h, 0, 0))],
        out_shape=[jax.ShapeDtypeStruct(y_init.shape, BF16),
                   jax.ShapeDtypeStruct((nseq, heads, LANES, dv), F32)],
        input_output_aliases={len(args) - 1: 0},
        scratch_shapes=[pltpu.VMEM((hp, LANES, dv), F32)],
        compiler_params=_cparams(3),
        name="hgrn_prompt" if hgrn else "gla_prompt",
    )(*args)
    return y, s


def _columns(x):
    nb = x.shape[0]
    if nb < LANES:
        x = jnp.concatenate([x, jnp.zeros((LANES - nb, x.shape[1]), x.dtype)], axis=0)
    return x.T


def _store_state(so_ref, layer, b, s_new, first):
    if first:
        for l in range(so_ref.shape[0]):
            so_ref[l, b] = s_new if l == layer else jnp.zeros_like(s_new)
    else:
        so_ref[b] = s_new


def _gla_sample_kernel(*refs, hgrn, nb, layer, first):
    n_in = 4 if hgrn else 6
    in_refs = refs[:n_in]
    g_ref, nw_ref, s_ref = refs[n_in:n_in + 3]
    y_ref, so_ref = refs[-2:]
    q, k, v, gl = _gla_inputs(in_refs, hgrn, layer)
    e_t = _columns(jnp.exp(gl))
    k_t = _columns(k)
    qb = q.astype(BF16)
    rowid = lax.broadcasted_iota(jnp.int32, v.shape, 0)
    o = jnp.zeros(v.shape, F32)
    for b in range(nb):
        s_new = e_t[:, b:b + 1] * s_ref[b] + k_t[:, b:b + 1] * v[b:b + 1, :]
        _store_state(so_ref, layer, b, s_new, first)
        o = jnp.where(rowid == b, _dot(qb, s_new.astype(BF16)), o)
    y_ref[...] = (_rms(o, nw_ref[...]) * _silu(g_ref[...])).astype(BF16)


def _state_specs(depth, layer, nb, dv, first):
    ispec = pl.BlockSpec((None, nb, None, LANES, dv), lambda h, i: (layer, i, h, 0, 0))
    if first:
        ospec = pl.BlockSpec((depth, nb, None, LANES, dv), lambda h, i: (0, i, h, 0, 0))
    else:
        ospec = ispec
    return ispec, ospec


def _sample_batch_block(ns, dv):
    nb = min(ns, STATE_BLOCK_BYTES // (LANES * dv * 4))
    assert ns % nb == 0
    return nb


def _gla_sample(proj, state, prev, y_all, row0, offs, extra, nw, *, hgrn, heads, dv, layer):
    depth, ns = state.shape[:2]
    nb = _sample_batch_block(ns, dv)
    rb = row0 // nb
    first = prev is None

    def col(off, w):
        return pl.BlockSpec((nb, w), lambda h, i: (rb + i, off // w + h))

    if hgrn:
        oq, of, oi, og = offs
        (lbl,) = extra
        in_specs = [col(oq, LANES), col(of, LANES), col(oi, dv),
                    pl.BlockSpec((lbl.shape[0], LANES), lambda h, i: (0, h))]
        args = [proj, proj, proj, lbl]
    else:
        oq, ok, ov, og, oa = offs
        wd, bd = extra
        in_specs = [col(oq, LANES), col(ok, LANES), col(ov, dv),
                    pl.BlockSpec((nb, LANES), lambda h, i: (rb + i, oa // LANES)),
                    pl.BlockSpec((LANES, LANES), lambda h, i: (0, h)),
                    pl.BlockSpec((1, LANES), lambda h, i: (0, h))]
        args = [proj, proj, proj, proj, wd, bd]
    ispec, ospec = _state_specs(depth, layer, nb, dv, first)
    in_specs += [col(og, dv), pl.BlockSpec(nw.shape, lambda h, i: (0, 0)), ispec,
                 pl.BlockSpec(memory_space=pl.ANY)]
    args += [proj, nw, state, y_all]
    aliases = {len(args) - 1: 0}
    if not first:
        in_specs.append(pl.BlockSpec(memory_space=pl.ANY))
        args.append(prev)
        aliases[len(args) - 1] = 1
    y, s = pl.pallas_call(
        functools.partial(_gla_sample_kernel, hgrn=hgrn, nb=nb, layer=layer, first=first),
        grid=(heads, ns // nb),
        in_specs=in_specs,
        out_specs=[pl.BlockSpec((nb, dv), lambda h, i: (rb + i, h)), ospec],
        out_shape=[jax.ShapeDtypeStruct(y_all.shape, BF16),
                   jax.ShapeDtypeStruct(state.shape, F32)],
        input_output_aliases=aliases,
        compiler_params=_cparams(2),
        name="hgrn_sample" if hgrn else "gla_sample",
    )(*args)
    return y, s


def _mamba_post(y, xs, z, dskip, nw):
    y = (y + dskip * xs) * _silu(z)
    gs = y.shape[1] // M_GROUPS
    outs = [_rms(y[:, g * gs:(g + 1) * gs], nw[:, g * gs:(g + 1) * gs]) for g in range(M_GROUPS)]
    return jnp.concatenate(outs, axis=1)


def _mamba_prompt_kernel(z_ref, x_ref, bc_ref, sm_ref, cwx_ref, cwb_ref, cbx_ref, cbb_ref,
                         dtb_ref, alog_ref, dsk_ref, nw_ref, tril_ref, mask_ref, _,
                         y_ref, cox_ref, cob_ref, so_ref, ex_scr, eb_scr, s_scr):
    c = pl.program_id(1)
    nc = pl.num_programs(1)
    tail = SUBLANES

    @pl.when(c == 0)
    def _():
        ex_scr[0:tail, :] = jnp.zeros((tail, ex_scr.shape[1]), F32)
        eb_scr[0:tail, :] = jnp.zeros((tail, eb_scr.shape[1]), F32)
        s_scr[...] = jnp.zeros_like(s_scr)

    ex_scr[tail:tail + CHUNK, :] = x_ref[...]
    eb_scr[tail:tail + CHUNK, :] = bc_ref[...]

    def conv(scr, cw_ref, cb_ref):
        acc = cb_ref[...]
        for w in range(M_CONV):
            sh = M_CONV - 1 - w
            acc = acc + cw_ref[w:w + 1, :] * scr[tail - sh:tail - sh + CHUNK, :]
        return _silu(acc)

    xs = conv(ex_scr, cwx_ref, cbx_ref)
    bcm = conv(eb_scr, cwb_ref, cbb_ref)

    @pl.when(c == nc - 1)
    def _():
        cox_ref[...] = ex_scr[tail + CHUNK - (M_CONV - 1):tail + CHUNK, :]
        cob_ref[...] = eb_scr[tail + CHUNK - (M_CONV - 1):tail + CHUNK, :]

    ex_scr[0:tail, :] = ex_scr[CHUNK:CHUNK + tail, :]
    eb_scr[0:tail, :] = eb_scr[CHUNK:CHUNK + tail, :]

    dt = _softplus(sm_ref[...] + dtb_ref[...])
    a = -jnp.exp(alog_ref[...])
    cum = _dot_exact_lhs(tril_ref[...], dt * a)
    cum_t = cum.T
    cl = cum[CHUNK - 1:CHUNK, :]
    mask = mask_ref[...] > 0.5
    gw = M_GROUPS * M_STATE
    lo = lax.broadcasted_iota(jnp.int32, (CHUNK, LANES), 1) < M_HEADDIM
    lo_r = lax.broadcasted_iota(jnp.int32, (LANES, LANES), 0) < M_HEADDIM
    n_pairs = xs.shape[1] // LANES
    per_group = n_pairs // M_GROUPS
    ys = []
    for j in range(n_pairs):
        g = j // per_group
        h0, h1 = 2 * j, 2 * j + 1
        if j % per_group == 0:
            bg = bcm[:, g * M_STATE:(g + 1) * M_STATE].astype(BF16)
            cg = bcm[:, gw + g * M_STATE:gw + (g + 1) * M_STATE].astype(BF16)
            cb = jnp.where(mask, _dot_nt(cg, bg), 0.0)

        def dec(h):
            return jnp.exp(jnp.minimum(cum[:, h:h + 1] - cum_t[h:h + 1, :], 0.0))

        x2 = xs[:, j * LANES:(j + 1) * LANES]
        dt2 = jnp.where(lo, dt[:, h0:h0 + 1], dt[:, h1:h1 + 1])
        xdt = (x2 * dt2).astype(BF16)
        y_in = jnp.where(lo, _dot((cb * dec(h0)).astype(BF16), xdt),
                         _dot((cb * dec(h1)).astype(BF16), xdt))
        ec2 = jnp.where(lo, jnp.exp(cum[:, h0:h0 + 1]), jnp.exp(cum[:, h1:h1 + 1]))
        s = s_scr[j]
        ys.append(y_in + _dot_nt(cg, s.astype(BF16)) * ec2)
        w2 = dt2 * jnp.where(lo, jnp.exp(cl[:, h0:h0 + 1] - cum[:, h0:h0 + 1]),
                             jnp.exp(cl[:, h1:h1 + 1] - cum[:, h1:h1 + 1]))
        el2 = jnp.where(lo_r, jnp.exp(cl[:, h0:h0 + 1]), jnp.exp(cl[:, h1:h1 + 1]))
        s_scr[j] = el2 * s + _dot_tn((x2 * w2).astype(BF16), bg)
    y = jnp.concatenate(ys, axis=1)
    y_ref[...] = _mamba_post(y, xs, z_ref[...], dsk_ref[...], nw_ref[...]).astype(BF16)

    @pl.when(c == nc - 1)
    def _():
        so_ref[...] = s_scr[...]


def _mamba_prompt(proj, offs, p, consts, y_init, *, nseq, nchunks):
    oz, ox, obc, osm = offs
    mw = p["dskip"].shape[1]
    bcw = p["cwb"].shape[1]
    n_pairs = mw // LANES

    def col(off, w):
        return pl.BlockSpec((CHUNK, w), lambda b, c: (b * nchunks + c, off // w))

    def full(a):
        nd = a.ndim
        return pl.BlockSpec(a.shape, lambda b, c: (0,) * nd)

    small = [p["cwx"], p["cwb"], p["cbx"], p["cbb"], p["dtb"], p["alog"], p["dskip"], p["nw"],
             consts[0], consts[1]]
    y, cox, cob, s = pl.pallas_call(
        _mamba_prompt_kernel,
        grid=(nseq, nchunks),
        in_specs=[col(oz, mw), col(ox, mw), col(obc, bcw), col(osm, LANES)] + [full(a) for a in small]
        + [pl.BlockSpec(memory_space=pl.ANY)],
        out_specs=[pl.BlockSpec((CHUNK, mw), lambda b, c: (b * nchunks + c, 0)),
                   pl.BlockSpec((None, M_CONV - 1, mw), lambda b, c: (b, 0, 0)),
                   pl.BlockSpec((None, M_CONV - 1, bcw), lambda b, c: (b, 0, 0)),
                   pl.BlockSpec((None, n_pairs, LANES, M_STATE), lambda b, c: (b, 0, 0, 0))],
        input_output_aliases={4 + len(small): 0},
        out_shape=[jax.ShapeDtypeStruct(y_init.shape, BF16),
                   jax.ShapeDtypeStruct((nseq, M_CONV - 1, mw), F32),
                   jax.ShapeDtypeStruct((nseq, M_CONV - 1, bcw), F32),
                   jax.ShapeDtypeStruct((nseq, n_pairs, LANES, M_STATE), F32)],
        scratch_shapes=[pltpu.VMEM((CHUNK + SUBLANES, mw), F32), pltpu.VMEM((CHUNK + SUBLANES, bcw), F32),
                        pltpu.VMEM((n_pairs, LANES, M_STATE), F32)],
        compiler_params=_cparams(2),
        name="mamba_prompt",
    )(proj, proj, proj, proj, *small, y_init)
    return y, jnp.concatenate([cox, cob], axis=-1), s


def _mamba_sample_prep_kernel(cs_ref, x_ref, bc_ref, sm_ref, cw_ref, cb_ref, dtb_ref, alog_ref, exp_ref,
                              co_ref, act_ref, dte_ref, ee_ref):
    cd = cw_ref.shape[1]
    new = jnp.concatenate([x_ref[...], bc_ref[...]], axis=1)
    acc = cb_ref[...] + cw_ref[M_CONV - 1:M_CONV, :] * new
    for w in range(M_CONV - 1):
        acc = acc + cw_ref[w:w + 1, :] * cs_ref[:, w * cd:(w + 1) * cd]
    act_ref[...] = _silu(acc)
    for w in range(1, M_CONV - 1):
        co_ref[:, (w - 1) * cd:w * cd] = cs_ref[:, w * cd:(w + 1) * cd]
    co_ref[:, (M_CONV - 2) * cd:(M_CONV - 1) * cd] = new
    dt = _softplus(sm_ref[...] + dtb_ref[...])
    a = -jnp.exp(alog_ref[...])
    dte_ref[...] = _dot_exact_rhs(dt, exp_ref[...])
    ee_ref[...] = jnp.exp(_dot_exact_rhs(dt * a, exp_ref[...]))


def _mamba_sample_state_kernel(x_ref, b_ref, c_ref, dte_ref, ee_ref, s_ref, *out_refs, nb, layer, first):
    y_ref, so_ref = out_refs[-2:]
    xdt_t = _columns(x_ref[...] * dte_ref[...])
    e_t = _columns(ee_ref[...])
    bv = b_ref[...]
    cb = c_ref[...].astype(BF16)
    rowid = lax.broadcasted_iota(jnp.int32, (nb, LANES), 0)
    y = jnp.zeros((nb, LANES), F32)
    for b in range(nb):
        s_new = e_t[:, b:b + 1] * s_ref[b] + xdt_t[:, b:b + 1] * bv[b:b + 1, :]
        _store_state(so_ref, layer, b, s_new, first)
        y = jnp.where(rowid == b, _dot_nt(cb, s_new.astype(BF16)), y)
    y_ref[...] = y


def _mamba_sample_post_kernel(y_ref, x_ref, z_ref, dsk_ref, nw_ref, _, o_ref):
    o_ref[...] = _mamba_post(y_ref[...], x_ref[...], z_ref[...], dsk_ref[...], nw_ref[...]).astype(BF16)


def _mamba_sample(proj, conv_state, ssm_state, prev, y_all, row0, offs, p, expand, layer):
    oz, ox, obc, osm = offs
    depth, ns = ssm_state.shape[:2]
    nb = _sample_batch_block(ns, M_STATE)
    mw = p["dskip"].shape[1]
    bcw = p["cwb"].shape[1]
    cd = mw + bcw
    n_pairs = mw // LANES
    per_group = n_pairs // M_GROUPS
    rb = row0 // ns
    first = prev is None
    cw = jnp.concatenate([p["cwx"], p["cwb"]], axis=1)
    cb = jnp.concatenate([p["cbx"], p["cbb"]], axis=1)

    def full1(a):
        nd = a.ndim
        return pl.BlockSpec(a.shape, lambda i: (0,) * nd)

    cs2 = conv_state.reshape(ns, (M_CONV - 1) * cd)
    small = [cw, cb, p["dtb"], p["alog"], expand]
    co, act, dte, ee = pl.pallas_call(
        _mamba_sample_prep_kernel,
        grid=(1,),
        in_specs=[full1(cs2),
                  pl.BlockSpec((ns, mw), lambda i: (rb, ox // mw)),
                  pl.BlockSpec((ns, bcw), lambda i: (rb, obc // bcw)),
                  pl.BlockSpec((ns, LANES), lambda i: (rb, osm // LANES))] + [full1(a) for a in small],
        out_specs=[pl.BlockSpec((ns, (M_CONV - 1) * cd), lambda i: (0, 0)),
                   pl.BlockSpec((ns, cd), lambda i: (0, 0)),
                   pl.BlockSpec((ns, mw), lambda i: (0, 0)),
                   pl.BlockSpec((ns, mw), lambda i: (0, 0))],
        out_shape=[jax.ShapeDtypeStruct((ns, (M_CONV - 1) * cd), F32),
                   jax.ShapeDtypeStruct((ns, cd), F32),
                   jax.ShapeDtypeStruct((ns, mw), F32),
                   jax.ShapeDtypeStruct((ns, mw), F32)],
        compiler_params=_cparams(1),
        name="mamba_sample_prep",
    )(cs2, proj, proj, proj, *small)

    bblk = mw // LANES
    cblk = bblk + M_GROUPS * M_STATE // LANES
    ispec, ospec = _state_specs(depth, layer, nb, M_STATE, first)
    in_specs = [pl.BlockSpec((nb, LANES), lambda j, i: (i, j)),
                pl.BlockSpec((nb, LANES), lambda j, i: (i, bblk + j // per_group)),
                pl.BlockSpec((nb, LANES), lambda j, i: (i, cblk + j // per_group)),
                pl.BlockSpec((nb, LANES), lambda j, i: (i, j)),
                pl.BlockSpec((nb, LANES), lambda j, i: (i, j)),
                ispec]
    args = [act, act, act, dte, ee, ssm_state]
    aliases = {}
    if not first:
        in_specs.append(pl.BlockSpec(memory_space=pl.ANY))
        args.append(prev)
        aliases = {len(args) - 1: 1}
    y, s_new = pl.pallas_call(
        functools.partial(_mamba_sample_state_kernel, nb=nb, layer=layer, first=first),
        grid=(n_pairs, ns // nb),
        in_specs=in_specs,
        out_specs=[pl.BlockSpec((nb, LANES), lambda j, i: (i, j)), ospec],
        out_shape=[jax.ShapeDtypeStruct((ns, mw), F32),
                   jax.ShapeDtypeStruct(ssm_state.shape, F32)],
        input_output_aliases=aliases,
        compiler_params=_cparams(2),
        name="mamba_sample_state",
    )(*args)

    ym = pl.pallas_call(
        _mamba_sample_post_kernel,
        grid=(1,),
        in_specs=[full1(y),
                  pl.BlockSpec((ns, mw), lambda i: (0, 0)),
                  pl.BlockSpec((ns, mw), lambda i: (rb, oz // mw)),
                  full1(p["dskip"]), full1(p["nw"]), pl.BlockSpec(memory_space=pl.ANY)],
        out_specs=pl.BlockSpec((ns, mw), lambda i: (rb, 0)),
        out_shape=jax.ShapeDtypeStruct(y_all.shape, BF16),
        input_output_aliases={5: 0},
        compiler_params=_cparams(1),
        name="mamba_sample_post",
    )(y, act, proj, p["dskip"], p["nw"], y_all)
    return ym, co.reshape(ns, M_CONV - 1, cd), s_new


def _pad_lanes(v, n=LANES):
    v = v.reshape(1, -1)
    return jnp.pad(v, ((0, 0), (0, n - v.shape[1])))


def kernel(x_prompt, x_sample, state_conv, state_ssm, state_hgrn, state_gla, ffn1_norm, ffn1_w_gate_up, ffn1_w_down, mix_norm, w_in, conv_w, conv_b, dt_bias, a_log, d_skip, mamba_norm, hgrn_lb_logits, hgrn_norm, gla_w_decay, gla_b_decay, gla_norm, w_branch_mamba, w_branch_hgrn, w_branch_gla, w_out, ffn2_norm, ffn2_w_gate_up, ffn2_w_down, final_norm):
    nseq, seq, d = x_prompt.shape
    ns = x_sample.shape[0]
    depth = w_in.shape[0]
    nchunks = seq // CHUNK
    n_prompt = nseq * seq
    mw = w_branch_mamba.shape[1]
    hw = w_branch_hgrn.shape[1]
    gw = w_branch_gla.shape[1]
    gk = gla_w_decay.shape[2]
    m_heads = dt_bias.shape[1]
    h_heads = state_hgrn.shape[2]
    g_heads = state_gla.shape[2]
    g_dv = state_gla.shape[4]
    bcw = 2 * M_GROUPS * M_STATE
    n_pairs = mw // LANES
    assert seq % CHUNK == 0 and n_prompt % ns == 0 and ns % 16 == 0
    assert m_heads <= G_RANK + m_heads <= LANES and gk // g_heads == LANES and hw // h_heads == LANES

    seg_w = {"z": mw, "xs": mw, "bc": bcw, "dt": m_heads, "hq": hw, "hf": hw, "hi": hw, "hg": hw,
             "gq": gk, "gk": gk, "gv": gw, "gg": gw, "ga": G_RANK, "gate": N_BRANCH * d}
    src_order = ("z", "xs", "bc", "dt", "hq", "hf", "hi", "hg", "gq", "gk", "gv", "gg", "ga", "gate")
    dst_order = ("z", "xs", "hq", "hf", "hi", "hg", "gv", "gg", "gq", "gk", "bc", "gate", "dt", "ga")
    src, off = {}, {}
    pos = 0
    for name in src_order:
        src[name] = pos
        pos += seg_w[name]
    pos = 0
    for name in dst_order:
        off[name] = pos
        pos += seg_w[name]
    n_used = pos
    n_cols = -(-n_used // 1024) * 1024
    copies = tuple((src[name], off[name], seg_w[name]) for name in dst_order)
    oz, ox, obc, osm = off["z"], off["xs"], off["bc"], off["dt"]
    assert off["ga"] == osm + m_heads and osm % LANES == 0

    consts = _scan_constants(GLA_SUB)
    tril_np = np.tril(np.ones((CHUNK, CHUNK), np.float32))
    mconsts = (jnp.asarray(tril_np, BF16), jnp.asarray(tril_np))
    expand_np = np.zeros((LANES, mw), np.float32)
    for h in range(m_heads):
        expand_np[h, h * M_HEADDIM:(h + 1) * M_HEADDIM] = 1.0
    expand = jnp.asarray(expand_np, BF16)

    x, xw, ssq = _prep(x_prompt.reshape(n_prompt, d), x_sample.reshape(ns, d), ffn1_norm[0])
    t_all = n_prompt + ns
    w_in_t = jnp.swapaxes(w_in, 1, 2)
    ssm5 = state_ssm.reshape(depth, ns, n_pairs, LANES, M_STATE)

    pc, ps, ph, pg, sc = [], [], [], [], []
    ss = sh = sg = None
    for l in range(depth):
        w_perm = _pack_w_in(w_in_t, l, copies, n_used, n_cols)
        mp = {
            "cwx": conv_w[l][:, :mw], "cwb": conv_w[l][:, mw:],
            "cbx": conv_b[l][:mw].reshape(1, mw), "cbb": conv_b[l][mw:].reshape(1, bcw),
            "dtb": _pad_lanes(dt_bias[l]), "alog": _pad_lanes(a_log[l]),
            "dskip": jnp.repeat(d_skip[l], M_HEADDIM).reshape(1, mw),
            "nw": mamba_norm[l].reshape(1, mw),
        }
        wd = jnp.zeros((LANES, gk), F32).at[m_heads:m_heads + G_RANK].set(gla_w_decay[l])
        bd = gla_b_decay[l].reshape(1, gk)
        hnw = hgrn_norm[l].reshape(1, LANES)
        gnw = gla_norm[l].reshape(1, g_dv)

        act, cast = _ffn_up(xw, ssq, ffn1_w_gate_up, l, (ffn1_w_down,))
        w_dn, tn_dn = (cast[0], 512) if cast else (ffn1_w_down, 256)
        x, xw, ssq = _resid_matmul(act, w_dn, l, x, 0.5, tn_dn, "ffn_down", mix_norm[l])
        proj, cast = _in_proj(xw, ssq, w_perm, l, (w_branch_mamba, w_branch_hgrn, w_branch_gla, w_out))
        w_bm, w_bh, w_bg, w_o = cast if cast else (w_branch_mamba, w_branch_hgrn, w_branch_gla, w_out)

        moffs = (oz, ox, obc, osm)
        hoffs = (off["hq"], off["hf"], off["hi"], off["hg"])
        goffs = (off["gq"], off["gk"], off["gv"], off["gg"], osm)
        ym, c1, s1 = _mamba_prompt(proj, moffs, mp, mconsts, jnp.zeros((t_all, mw), BF16),
                                   nseq=nseq, nchunks=nchunks)
        yh, h1 = _gla_prompt(proj, hoffs, (hgrn_lb_logits,), hnw, consts, jnp.zeros((t_all, hw), BF16),
                             hgrn=True, nseq=nseq, nchunks=nchunks, heads=h_heads, dv=LANES, layer=l)
        yg, g1 = _gla_prompt(proj, goffs, (wd, bd), gnw, consts, jnp.zeros((t_all, gw), BF16),
                             hgrn=False, nseq=nseq, nchunks=nchunks, heads=g_heads, dv=g_dv, layer=l)
        ym, c2, ss = _mamba_sample(proj, state_conv[l], ssm5, ss, ym, n_prompt, moffs, mp, expand, l)
        yh, sh = _gla_sample(proj, state_hgrn, sh, yh, n_prompt, hoffs, (hgrn_lb_logits,), hnw,
                             hgrn=True, heads=h_heads, dv=LANES, layer=l)
        yg, sg = _gla_sample(proj, state_gla, sg, yg, n_prompt, goffs, (wd, bd), gnw,
                             hgrn=False, heads=g_heads, dv=g_dv, layer=l)
        merged = _merge(ym, yh, yg, w_bm, w_bh, w_bg, l, proj, off["gate"])
        x, xw, ssq = _resid_matmul(merged, w_o, l, x, 1.0, 512, "out_proj", ffn2_norm[l], WIDE_TM_CAP)

        act, cast = _ffn_up(xw, ssq, ffn2_w_gate_up, l, (ffn2_w_down,))
        w_dn, tn_dn = (cast[0], 512) if cast else (ffn2_w_down, 256)
        if l + 1 < depth:
            x, xw, ssq = _resid_matmul(act, w_dn, l, x, 0.5, tn_dn, "ffn_down", ffn1_norm[l + 1])
        else:
            x = _resid_matmul(act, w_dn, l, x, 0.5, tn_dn, "ffn_down")

        pc.append(c1)
        ps.append(s1.reshape(nseq, m_heads, M_HEADDIM, M_STATE))
        ph.append(h1)
        pg.append(g1)
        sc.append(c2)

    y_prompt = _final_norm(x, final_norm, 0, n_prompt, "final_norm_prompt").reshape(nseq, seq, d)
    y_sample = _final_norm(x, final_norm, n_prompt, ns, "final_norm_sample").reshape(ns, 1, d)
    return (y_prompt, y_sample, jnp.stack(pc), jnp.stack(ps), jnp.stack(ph), jnp.stack(pg),
            jnp.stack(sc), ss.reshape(state_ssm.shape), sh, sg)
```

```python
import functools

import jax
import jax.numpy as jnp
import numpy as np
from jax import lax
from jax.experimental import pallas as pl
from jax.experimental.pallas import tpu as pltpu

F32 = jnp.float32
BF16 = jnp.bfloat16
EPS = 1e-6
F32_TINY = float(np.finfo(np.float32).tiny)

CHUNK = 256
LANES = 128
SUBLANES = 8
M_HEADDIM = 64
M_STATE = 128
M_GROUPS = 2
M_CONV = 4
G_RANK = 16
G_NORMALIZER = 16.0
N_BRANCH = 3
GLA_SUB = 128
VMEM_LIMIT = 56 * 1024 * 1024
STATE_BLOCK_BYTES = 4 * 1024 * 1024
WIDE_TM_CAP = 2100


def _cparams(n_axes):
    return pltpu.CompilerParams(dimension_semantics=("arbitrary",) * n_axes,
                                vmem_limit_bytes=VMEM_LIMIT)


def _dot(a, b):
    return jnp.dot(a, b, preferred_element_type=F32)


def _dot_nt(a, b):
    return lax.dot_general(a, b, (((1,), (1,)), ((), ())), preferred_element_type=F32)


def _dot_tn(a, b):
    return lax.dot_general(a, b, (((0,), (0,)), ((), ())), preferred_element_type=F32)


def _split3(x):
    hi = x.astype(BF16)
    r = x - hi.astype(F32)
    mid = r.astype(BF16)
    lo = (r - mid.astype(F32)).astype(BF16)
    return hi, mid, lo


def _dot_exact_lhs(p_bf16, x):
    hi, mid, lo = _split3(x)
    return _dot(p_bf16, hi) + _dot(p_bf16, mid) + _dot(p_bf16, lo)


def _dot_exact_rhs(x, p_bf16):
    hi, mid, lo = _split3(x)
    return _dot(hi, p_bf16) + _dot(mid, p_bf16) + _dot(lo, p_bf16)


def _dot_f32(a, b):
    ah, am, _ = _split3(a)
    bh, bm, _ = _split3(b)
    return _dot(ah, bh) + _dot(ah, bm) + _dot(am, bh)


def _sigmoid(x):
    return jax.nn.sigmoid(x)


def _silu(x):
    return x * _sigmoid(x)


def _softplus(x):
    return jnp.maximum(x, 0.0) + jnp.log1p(jnp.exp(-jnp.abs(x)))


def _log_sigmoid(x):
    return jnp.minimum(x, 0.0) - jnp.log(1.0 + jnp.exp(-jnp.abs(x)))


def _rms(x, w):
    ms = jnp.mean(x * x, axis=-1, keepdims=True)
    return x * lax.rsqrt(ms + EPS) * w


def _pick_tm(t, cap=1100):
    best = 16
    for tm in range(16, min(t, cap) + 1, 16):
        if t % tm == 0:
            best = tm
    return best


def _row_scale(ssq_ref, d):
    return lax.rsqrt(ssq_ref[:, 0:1] * (1.0 / d) + EPS)


def _emit_normed(xn, nw_ref, xw_ref, ssq_ref, accumulate):
    xw_ref[...] = (xn * nw_ref[...]).astype(BF16)
    part = jnp.broadcast_to(jnp.sum(xn * xn, axis=1, keepdims=True), ssq_ref.shape)
    ssq_ref[...] = ssq_ref[...] + part if accumulate else part


def _prep_kernel(xp_ref, xs_ref, nw_ref, x_ref, xw_ref, ssq_ref, *, n_prompt_blocks):
    x = jnp.where(pl.program_id(0) < n_prompt_blocks, xp_ref[...], xs_ref[...])
    x_ref[...] = x
    _emit_normed(x, nw_ref, xw_ref, ssq_ref, accumulate=False)


def _prep(xp, xs, nw):
    n_prompt, d = xp.shape
    rb = xs.shape[0]
    assert n_prompt % rb == 0
    npb = n_prompt // rb
    t = n_prompt + rb
    row = pl.BlockSpec((rb, d), lambda i: (i, 0))
    return pl.pallas_call(
        functools.partial(_prep_kernel, n_prompt_blocks=npb),
        grid=(npb + 1,),
        in_specs=[pl.BlockSpec((rb, d), lambda i: (jnp.minimum(i, npb - 1), 0)),
                  pl.BlockSpec((rb, d), lambda i: (0, 0)),
                  pl.BlockSpec((1, d), lambda i: (0, 0))],
        out_specs=[row, row, pl.BlockSpec((rb, LANES), lambda i: (i, 0))],
        out_shape=[jax.ShapeDtypeStruct((t, d), F32), jax.ShapeDtypeStruct((t, d), BF16),
                   jax.ShapeDtypeStruct((t, LANES), F32)],
        compiler_params=_cparams(1),
        name="prep",
    )(xp, xs, nw.reshape(1, d))


def _row_parts(tm):
    half = -(-(tm // 2) // 16) * 16
    return ((0, half), (half, tm - half)) if 0 < half < tm else ((0, tm),)


def _side_casts(ws, layer, n_i, nj):
    steps = n_i * nj
    ins, outs, shapes = [], [], []
    for w in ws:
        _, k, d = w.shape
        rows = k // steps
        if rows * steps != k or rows % 16:
            return None
        ins.append(pl.BlockSpec((None, rows, d), lambda i, j: (layer, i * nj + j, 0)))
        outs.append(pl.BlockSpec((rows, d), lambda i, j: (i * nj + j, 0)))
        shapes.append(jax.ShapeDtypeStruct((k, d), BF16))
    return ins, outs, shapes


def _do_side_casts(refs):
    n = len(refs) // 2
    for src, dst in zip(refs[:n], refs[n:]):
        dst[...] = src[...].astype(BF16)


def _ffn_up_kernel(xw_ref, ssq_ref, wg_ref, wu_ref, *rest, parts, n_cast):
    o_ref = rest[n_cast]
    _do_side_casts(rest[:n_cast] + rest[n_cast + 1:])
    wg = wg_ref[...].astype(BF16)
    wu = wu_ref[...].astype(BF16)
    d = xw_ref.shape[1]
    for start, rows in parts:
        sl = pl.ds(start, rows)
        r = lax.rsqrt(ssq_ref[sl, 0:1] * (1.0 / d) + EPS)
        h = xw_ref[sl, :]
        g = r * _dot(h, wg)
        u = r * _dot(h, wu)
        o_ref[sl, :] = (_silu(g) * u).astype(BF16)


def _ffn_up(xw, ssq, w_gu, layer, cast_ws=(), tn=512):
    t, d = xw.shape
    dff = w_gu.shape[2] // 2
    tm = _pick_tm(t, WIDE_TM_CAP)
    nj = dff // tn
    side = _side_casts(cast_ws, layer, t // tm, nj) if cast_ws else None
    c_in, c_out, c_shape = side if side else ([], [], [])
    outs = pl.pallas_call(
        functools.partial(_ffn_up_kernel, parts=_row_parts(tm), n_cast=len(c_in)),
        grid=(t // tm, nj),
        in_specs=[pl.BlockSpec((tm, d), lambda i, j: (i, 0)),
                  pl.BlockSpec((tm, LANES), lambda i, j: (i, 0)),
                  pl.BlockSpec((None, d, tn), lambda i, j: (layer, 0, j)),
                  pl.BlockSpec((None, d, tn), lambda i, j: (layer, 0, j + nj))] + c_in,
        out_specs=[pl.BlockSpec((tm, tn), lambda i, j: (i, j))] + c_out,
        out_shape=[jax.ShapeDtypeStruct((t, dff), BF16)] + c_shape,
        compiler_params=_cparams(2),
        name="ffn_up",
    )(xw, ssq, w_gu, w_gu, *(cast_ws if side else ()))
    return outs[0], (tuple(outs[1:]) if side else None)


def _resid_matmul_kernel(a_ref, w_ref, x_ref, *rest, scale, emit, parts):
    if emit:
        nw_ref, o_ref, xw_ref, ssq_ref = rest

        @pl.when(pl.program_id(1) == 0)
        def _():
            ssq_ref[...] = jnp.zeros_like(ssq_ref)
    else:
        (o_ref,) = rest
    wb = w_ref[...].astype(BF16)
    for start, rows in parts:
        sl = pl.ds(start, rows)
        xn = x_ref[sl, :] + scale * _dot(a_ref[sl, :], wb)
        if emit:
            _emit_normed(xn, nw_ref, xw_ref.at[sl, :], ssq_ref.at[sl, :], accumulate=True)
        o_ref[sl, :] = xn


def _wspec(w, layer, tn):
    if w.ndim == 3:
        return pl.BlockSpec((None, w.shape[1], tn), lambda i, j: (layer, 0, j))
    return pl.BlockSpec((w.shape[0], tn), lambda i, j: (0, j))


def _resid_matmul(a, w, layer, x, scale, tn, name, next_nw=None, tm_cap=1100):
    t, k = a.shape
    d = w.shape[-1]
    tm = _pick_tm(t, tm_cap)
    emit = next_nw is not None
    tile = pl.BlockSpec((tm, tn), lambda i, j: (i, j))
    in_specs = [pl.BlockSpec((tm, k), lambda i, j: (i, 0)), _wspec(w, layer, tn), tile]
    args = [a, w, x]
    out_specs = [tile]
    out_shape = [jax.ShapeDtypeStruct((t, d), F32)]
    if emit:
        in_specs.append(pl.BlockSpec((1, tn), lambda i, j: (0, j)))
        args.append(next_nw.reshape(1, d))
        out_specs += [tile, pl.BlockSpec((tm, LANES), lambda i, j: (i, 0))]
        out_shape += [jax.ShapeDtypeStruct((t, d), BF16), jax.ShapeDtypeStruct((t, LANES), F32)]
    outs = pl.pallas_call(
        functools.partial(_resid_matmul_kernel, scale=scale, emit=emit,
                          parts=_row_parts(tm) if k <= d else ((0, tm),)),
        grid=(t // tm, d // tn),
        in_specs=in_specs,
        out_specs=out_specs,
        out_shape=out_shape,
        compiler_params=_cparams(2),
        name=name,
    )(*args)
    return outs if emit else outs[0]


def _pack_w_in_kernel(w_ref, o_ref, *, copies, n_used):
    for src, dst, width in copies:
        if width % LANES == 0:
            o_ref[:, dst:dst + width] = w_ref[src:src + width, :].T.astype(BF16)
        else:
            base = src // LANES * LANES
            lo = src - base
            assert dst % LANES == lo and lo + width <= LANES
            t = w_ref[base:base + LANES, :].T
            o_ref[:, dst:dst + width] = t[:, lo:lo + width].astype(BF16)
    pad = o_ref.shape[1] - n_used
    if pad:
        o_ref[:, n_used:] = jnp.zeros((o_ref.shape[0], pad), BF16)


def _pack_w_in(w_t, layer, copies, n_used, n_cols, tk=128):
    _, n_src, d = w_t.shape
    return pl.pallas_call(
        functools.partial(_pack_w_in_kernel, copies=copies, n_used=n_used),
        grid=(d // tk,),
        in_specs=[pl.BlockSpec((None, n_src, tk), lambda i: (layer, 0, i))],
        out_specs=pl.BlockSpec((tk, n_cols), lambda i: (i, 0)),
        out_shape=jax.ShapeDtypeStruct((d, n_cols), BF16),
        compiler_params=_cparams(1),
        name="pack_w_in",
    )(w_t)


def _in_proj_kernel(xw_ref, ssq_ref, w_ref, *rest, n_cast):
    o_ref = rest[n_cast]
    _do_side_casts(rest[:n_cast] + rest[n_cast + 1:])
    o_ref[...] = _row_scale(ssq_ref, xw_ref.shape[1]) * _dot(xw_ref[...], w_ref[...])


def _in_proj(xw, ssq, w, layer, cast_ws=(), tn=1024):
    t, d = xw.shape
    n = w.shape[1]
    tm = _pick_tm(t, WIDE_TM_CAP)
    side = _side_casts(cast_ws, layer, t // tm, n // tn) if cast_ws else None
    c_in, c_out, c_shape = side if side else ([], [], [])
    outs = pl.pallas_call(
        functools.partial(_in_proj_kernel, n_cast=len(c_in)),
        grid=(t // tm, n // tn),
        in_specs=[pl.BlockSpec((tm, d), lambda i, j: (i, 0)),
                  pl.BlockSpec((tm, LANES), lambda i, j: (i, 0)),
                  pl.BlockSpec((d, tn), lambda i, j: (0, j))] + c_in,
        out_specs=[pl.BlockSpec((tm, tn), lambda i, j: (i, j))] + c_out,
        out_shape=[jax.ShapeDtypeStruct((t, n), F32)] + c_shape,
        compiler_params=_cparams(2),
        name="in_proj",
    )(xw, ssq, w, *(cast_ws if side else ()))
    return outs[0], (tuple(outs[1:]) if side else None)


def _merge_kernel(ym_ref, yh_ref, yg_ref, wm_ref, wh_ref, wg_ref, g0_ref, g1_ref, g2_ref, o_ref, *, parts):
    wm = wm_ref[...].astype(BF16)
    wh = wh_ref[...].astype(BF16)
    wg = wg_ref[...].astype(BF16)
    for start, rows in parts:
        sl = pl.ds(start, rows)
        acc = _sigmoid(g0_ref[sl, :]) * _dot(ym_ref[sl, :], wm)
        acc = acc + _sigmoid(g1_ref[sl, :]) * _dot(yh_ref[sl, :], wh)
        acc = acc + _sigmoid(g2_ref[sl, :]) * _dot(yg_ref[sl, :], wg)
        o_ref[sl, :] = acc.astype(BF16)


def _merge(ym, yh, yg, wm, wh, wg, layer, proj, gate_off, tn=512):
    t, k = ym.shape
    d = wm.shape[-1]
    tm = _pick_tm(t)
    gb = gate_off // tn
    nb = d // tn
    yspec = pl.BlockSpec((tm, k), lambda i, j: (i, 0))
    wspec = _wspec(wm, layer, tn)

    def gspec(b):
        return pl.BlockSpec((tm, tn), lambda i, j: (i, gb + b * nb + j))

    return pl.pallas_call(
        functools.partial(_merge_kernel, parts=_row_parts(tm)),
        grid=(t // tm, nb),
        in_specs=[yspec, yspec, yspec, wspec, wspec, wspec, gspec(0), gspec(1), gspec(2)],
        out_specs=pl.BlockSpec((tm, tn), lambda i, j: (i, j)),
        out_shape=jax.ShapeDtypeStruct((t, d), BF16),
        compiler_params=_cparams(2),
        name="merge",
    )(ym, yh, yg, wm, wh, wg, proj, proj, proj)


def _final_norm_kernel(x_ref, nw_ref, o_ref):
    o_ref[...] = _rms(x_ref[...], nw_ref[...])


def _final_norm(x, nw, row0, rows, name):
    d = x.shape[1]
    tm = _pick_tm(rows)
    assert row0 % tm == 0
    return pl.pallas_call(
        _final_norm_kernel,
        grid=(rows // tm,),
        in_specs=[pl.BlockSpec((tm, d), lambda i: (row0 // tm + i, 0)),
                  pl.BlockSpec((1, d), lambda i: (0, 0))],
        out_specs=pl.BlockSpec((tm, d), lambda i: (i, 0)),
        out_shape=jax.ShapeDtypeStruct((rows, d), F32),
        compiler_params=_cparams(1),
        name=name,
    )(x, nw.reshape(1, d))


def _scan_constants(c):
    N_LEVELS = int(np.log2(c))
    assert 1 << N_LEVELS == c and N_LEVELS >= 3
    pm = np.zeros((2, c, c), np.float32)
    sg = np.zeros((N_LEVELS - 1, c, LANES), np.float32)
    mk = np.zeros((N_LEVELS + 1, c, c), np.float32)
    r = np.arange(c)
    for lvl in range(N_LEVELS):
        h = c >> (lvl + 1)
        for t in range(c):
            blk, pos = divmod(t, 2 * h)
            ridx = blk * 2 * h + h - 1
            upper = pos >= h
            if upper:
                mk[lvl, t] = ((r // (2 * h)) == blk) & ((r % (2 * h)) < h)
            if h >= 4:
                sg[lvl, t] = 1.0 if upper else -1.0
            elif h == 2:
                pm[0, t] = ((r > ridx) & (r <= t)) if upper else ((r > t) & (r <= ridx))
            else:
                sg[N_LEVELS - 2, t] = 1.0 if upper else 0.0
    pm[1] = r[None, :] <= r[:, None]
    mk[N_LEVELS] = np.eye(c)
    return jnp.asarray(pm.reshape(2 * c, c), BF16), jnp.asarray(sg), jnp.asarray(mk)


def _lower_bound(logits, layer):
    m = jnp.max(logits, axis=0, keepdims=True)
    e = jnp.exp(logits - m)
    sm = e / jnp.sum(e, axis=0, keepdims=True)
    lb = jnp.zeros_like(m)
    for i in range(1, layer + 1):
        lb = lb + sm[i:i + 1, :]
    return lb


def _gla_inputs(refs, hgrn, layer):
    if hgrn:
        q_ref, f_ref, i_ref, lbl_ref = refs
        lb = _lower_bound(lbl_ref[...], layer)
        q = _silu(q_ref[...]) * (LANES ** -0.5)
        zf = f_ref[...]
        f = lb + (1.0 - lb) * _sigmoid(zf)
        gl = jnp.log(jnp.maximum(f, F32_TINY))
        k = (1.0 - lb) * _sigmoid(-zf)
        v = i_ref[...]
    else:
        q_ref, k_ref, v_ref, a_ref, wd_ref, bd_ref = refs
        q = q_ref[...] * (LANES ** -0.5)
        k = k_ref[...]
        v = v_ref[...]
        gl = _log_sigmoid(_dot_f32(a_ref[...], wd_ref[...]) + bd_ref[...]) / G_NORMALIZER
    return q, k, v, gl


def _level_factor(lvl, gl, cum, z_h2, sg_ref):
    c = gl.shape[0]
    n_levels = sg_ref.shape[0] + 1
    h = c >> (lvl + 1)
    if h >= 4:
        c3 = cum.reshape(c // (2 * h), 2 * h, LANES)
        d = (c3 - c3[:, h - 1:h, :]).reshape(c, LANES)
        return jnp.exp(sg_ref[lvl] * d)
    if h == 2:
        return jnp.exp(z_h2)
    return jnp.exp(sg_ref[n_levels - 2] * gl)


def _gla_prompt_kernel(*refs, hgrn, layer, hp, dv):
    n_in = 4 if hgrn else 6
    in_refs = refs[:n_in]
    g_ref, nw_ref, pm_ref, sg_ref, mk_ref, _, y_ref, so_ref, s_scr = refs[n_in:]
    c = pl.program_id(2)
    sub = mk_ref.shape[1]
    n_levels = mk_ref.shape[0] - 1

    @pl.when(c == 0)
    def _():
        s_scr[...] = jnp.zeros_like(s_scr)

    q_all, k_all, v_all, gl_all = _gla_inputs(in_refs, hgrn, layer)
    finals = []
    for u in range(hp):
        s = s_scr[u]
        for r0 in range(0, CHUNK, sub):
            rows = pl.ds(r0, sub)
            q, k, gl = (a[r0:r0 + sub, u * LANES:(u + 1) * LANES] for a in (q_all, k_all, gl_all))
            v = v_all[r0:r0 + sub, u * dv:(u + 1) * dv]
            zz = _dot_exact_lhs(pm_ref[...], gl)
            z_h2, cum = zz[:sub], zz[sub:]
            qb, kb = q.astype(BF16), k.astype(BF16)
            att = mk_ref[n_levels] * _dot_nt(qb, kb)
            for lvl in range(n_levels):
                eb = _level_factor(lvl, gl, cum, z_h2, sg_ref).astype(BF16)
                att = att + mk_ref[lvl] * _dot_nt(qb * eb, kb * eb)
            e_cum = jnp.exp(cum)
            e_tail = jnp.exp(cum[sub - 1:sub, :] - cum)
            vb = v.astype(BF16)
            o = _dot(att.astype(BF16), vb) + _dot((q * e_cum).astype(BF16), s.astype(BF16))
            e_last = e_cum.T[:, sub - 1:sub]
            s = e_last * s + _dot_tn((k * e_tail).astype(BF16), vb)
            gate = g_ref[rows, u * dv:(u + 1) * dv]
            y_ref[rows, u * dv:(u + 1) * dv] = (_rms(o, nw_ref[...]) * _silu(gate)).astype(BF16)
        s_scr[u] = s
        finals.append(s)

    @pl.when(c == pl.num_programs(2) - 1)
    def _():
        for u in range(hp):
            so_ref[u] = finals[u]


def _gla_prompt(proj, offs, extra, nw, consts, y_init, *, hgrn, nseq, nchunks, heads, dv, layer, hp=4):
    pm, sg, mk = consts
    assert heads % hp == 0

    def col(off, w):
        assert off % (hp * w) == 0
        return pl.BlockSpec((CHUNK, hp * w), lambda h, b, c: (b * nchunks + c, off // (hp * w) + h))

    def full(a):
        nd = a.ndim
        return pl.BlockSpec(a.shape, lambda h, b, c: (0,) * nd)

    if hgrn:
        oq, of, oi, og = offs
        (lbl,) = extra
        in_specs = [col(oq, LANES), col(of, LANES), col(oi, dv),
                    pl.BlockSpec((lbl.shape[0], hp * LANES), lambda h, b, c: (0, h))]
        args = [proj, proj, proj, lbl]
    else:
        oq, ok, ov, og, oa = offs
        wd, bd = extra
        in_specs = [col(oq, LANES), col(ok, LANES), col(ov, dv),
                    pl.BlockSpec((CHUNK, LANES), lambda h, b, c: (b * nchunks + c, oa // LANES)),
                    pl.BlockSpec((LANES, hp * LANES), lambda h, b, c: (0, h)),
                    pl.BlockSpec((1, hp * LANES), lambda h, b, c: (0, h))]
        args = [proj, proj, proj, proj, wd, bd]
    in_specs += [col(og, dv), full(nw), full(pm), full(sg), full(mk), pl.BlockSpec(memory_space=pl.ANY)]
    args += [proj, nw, pm, sg, mk, y_init]
    y, s = pl.pallas_call(
        functools.partial(_gla_prompt_kernel, hgrn=hgrn, layer=layer, hp=hp, dv=dv),
        grid=(heads // hp, nseq, nchunks),
        in_specs=in_specs,
        out_specs=[pl.BlockSpec((CHUNK, hp * dv), lambda h, b, c: (b * nchunks + c, h)),
                   pl.BlockSpec((None, hp, LANES, dv), lambda h, b, c: (b, h, 0, 0))],
        out_shape=[jax.ShapeDtypeStruct(y_init.shape, BF16),
                   jax.ShapeDtypeStruct((nseq, heads, LANES, dv), F32)],
        input_output_aliases={len(args) - 1: 0},
        scratch_shapes=[pltpu.VMEM((hp, LANES, dv), F32)],
        compiler_params=_cparams(3),
        name="hgrn_prompt" if hgrn else "gla_prompt",
    )(*args)
    return y, s


def _columns(x):
    nb = x.shape[0]
    if nb < LANES:
        x = jnp.concatenate([x, jnp.zeros((LANES - nb, x.shape[1]), x.dtype)], axis=0)
    return x.T


def _store_state(so_ref, layer, b, s_new, first):
    if first:
        for l in range(so_ref.shape[0]):
            so_ref[l, b] = s_new if l == layer else jnp.zeros_like(s_new)
    else:
        so_ref[b] = s_new


def _gla_sample_kernel(*refs, hgrn, nb, layer, first):
    n_in = 4 if hgrn else 6
    in_refs = refs[:n_in]
    g_ref, nw_ref, s_ref = refs[n_in:n_in + 3]
    y_ref, so_ref = refs[-2:]
    q, k, v, gl = _gla_inputs(in_refs, hgrn, layer)
    e_t = _columns(jnp.exp(gl))
    k_t = _columns(k)
    qb = q.astype(BF16)
    rowid = lax.broadcasted_iota(jnp.int32, v.shape, 0)
    o = jnp.zeros(v.shape, F32)
    for b in range(nb):
        s_new = e_t[:, b:b + 1] * s_ref[b] + k_t[:, b:b + 1] * v[b:b + 1, :]
        _store_state(so_ref, layer, b, s_new, first)
        o = jnp.where(rowid == b, _dot(qb, s_new.astype(BF16)), o)
    y_ref[...] = (_rms(o, nw_ref[...]) * _silu(g_ref[...])).astype(BF16)


def _state_specs(depth, layer, nb, dv, first):
    ispec = pl.BlockSpec((None, nb, None, LANES, dv), lambda h, i: (layer, i, h, 0, 0))
    if first:
        ospec = pl.BlockSpec((depth, nb, None, LANES, dv), lambda h, i: (0, i, h, 0, 0))
    else:
        ospec = ispec
    return ispec, ospec


def _sample_batch_block(ns, dv):
    nb = min(ns, STATE_BLOCK_BYTES // (LANES * dv * 4))
    assert ns % nb == 0
    return nb


def _gla_sample(proj, state, prev, y_all, row0, offs, extra, nw, *, hgrn, heads, dv, layer):
    depth, ns = state.shape[:2]
    nb = _sample_batch_block(ns, dv)
    rb = row0 // nb
    first = prev is None

    def col(off, w):
        return pl.BlockSpec((nb, w), lambda h, i: (rb + i, off // w + h))

    if hgrn:
        oq, of, oi, og = offs
        (lbl,) = extra
        in_specs = [col(oq, LANES), col(of, LANES), col(oi, dv),
                    pl.BlockSpec((lbl.shape[0], LANES), lambda h, i: (0, h))]
        args = [proj, proj, proj, lbl]
    else:
        oq, ok, ov, og, oa = offs
        wd, bd = extra
        in_specs = [col(oq, LANES), col(ok, LANES), col(ov, dv),
                    pl.BlockSpec((nb, LANES), lambda h, i: (rb + i, oa // LANES)),
                    pl.BlockSpec((LANES, LANES), lambda h, i: (0, h)),
                    pl.BlockSpec((1, LANES), lambda h, i: (0, h))]
        args = [proj, proj, proj, proj, wd, bd]
    ispec, ospec = _state_specs(depth, layer, nb, dv, first)
    in_specs += [col(og, dv), pl.BlockSpec(nw.shape, lambda h, i: (0, 0)), ispec,
                 pl.BlockSpec(memory_space=pl.ANY)]
    args += [proj, nw, state, y_all]
    aliases = {len(args) - 1: 0}
    if not first:
        in_specs.append(pl.BlockSpec(memory_space=pl.ANY))
        args.append(prev)
        aliases[len(args) - 1] = 1
    y, s = pl.pallas_call(
        functools.partial(_gla_sample_kernel, hgrn=hgrn, nb=nb, layer=layer, first=first),
        grid=(heads, ns // nb),
        in_specs=in_specs,
        out_specs=[pl.BlockSpec((nb, dv), lambda h, i: (rb + i, h)), ospec],
        out_shape=[jax.ShapeDtypeStruct(y_all.shape, BF16),
                   jax.ShapeDtypeStruct(state.shape, F32)],
        input_output_aliases=aliases,
        compiler_params=_cparams(2),
        name="hgrn_sample" if hgrn else "gla_sample",
    )(*args)
    return y, s


def _mamba_post(y, xs, z, dskip, nw):
    y = (y + dskip * xs) * _silu(z)
    gs = y.shape[1] // M_GROUPS
    outs = [_rms(y[:, g * gs:(g + 1) * gs], nw[:, g * gs:(g + 1) * gs]) for g in range(M_GROUPS)]
    return jnp.concatenate(outs, axis=1)


def _mamba_prompt_kernel(z_ref, x_ref, bc_ref, sm_ref, cwx_ref, cwb_ref, cbx_ref, cbb_ref,
                         dtb_ref, alog_ref, dsk_ref, nw_ref, tril_ref, mask_ref, _,
                         y_ref, cox_ref, cob_ref, so_ref, ex_scr, eb_scr, s_scr):
    c = pl.program_id(1)
    nc = pl.num_programs(1)
    tail = SUBLANES

    @pl.when(c == 0)
    def _():
        ex_scr[0:tail, :] = jnp.zeros((tail, ex_scr.shape[1]), F32)
        eb_scr[0:tail, :] = jnp.zeros((tail, eb_scr.shape[1]), F32)
        s_scr[...] = jnp.zeros_like(s_scr)

    ex_scr[tail:tail + CHUNK, :] = x_ref[...]
    eb_scr[tail:tail + CHUNK, :] = bc_ref[...]

    def conv(scr, cw_ref, cb_ref):
        xe = scr[...]
        acc = cw_ref[0:1, :] * xe
        for w in range(1, M_CONV):
            acc = pltpu.roll(acc, 1, axis=0) + cw_ref[w:w + 1, :] * xe
        return _silu(acc[tail:, :] + cb_ref[...])

    xs = conv(ex_scr, cwx_ref, cbx_ref)
    bcm = conv(eb_scr, cwb_ref, cbb_ref)

    ex_scr[0:tail, :] = ex_scr[CHUNK:CHUNK + tail, :]
    eb_scr[0:tail, :] = eb_scr[CHUNK:CHUNK + tail, :]

    dt = _softplus(sm_ref[...] + dtb_ref[...])
    a = -jnp.exp(alog_ref[...])
    cum = _dot_exact_lhs(tril_ref[...], dt * a)
    cum_t = cum.T
    cl = cum[CHUNK - 1:CHUNK, :]
    mask = mask_ref[...] > 0.5
    gw = M_GROUPS * M_STATE
    lo = lax.broadcasted_iota(jnp.int32, (CHUNK, LANES), 1) < M_HEADDIM
    lo_r = lax.broadcasted_iota(jnp.int32, (LANES, LANES), 0) < M_HEADDIM
    n_pairs = xs.shape[1] // LANES
    per_group = n_pairs // M_GROUPS
    ys = []
    for j in range(n_pairs):
        g = j // per_group
        h0, h1 = 2 * j, 2 * j + 1
        if j % per_group == 0:
            bg = bcm[:, g * M_STATE:(g + 1) * M_STATE].astype(BF16)
            cg = bcm[:, gw + g * M_STATE:gw + (g + 1) * M_STATE].astype(BF16)
            cb = jnp.where(mask, _dot_nt(cg, bg), 0.0)

        def dec(h):
            return jnp.exp(jnp.minimum(cum[:, h:h + 1] - cum_t[h:h + 1, :], 0.0))

        x2 = xs[:, j * LANES:(j + 1) * LANES]
        dt2 = jnp.where(lo, dt[:, h0:h0 + 1], dt[:, h1:h1 + 1])
        xdt = (x2 * dt2).astype(BF16)
        y_in = jnp.where(lo, _dot((cb * dec(h0)).astype(BF16), xdt),
                         _dot((cb * dec(h1)).astype(BF16), xdt))
        ec2 = jnp.where(lo, jnp.exp(cum[:, h0:h0 + 1]), jnp.exp(cum[:, h1:h1 + 1]))
        s = s_scr[j]
        ys.append(y_in + _dot_nt(cg, s.astype(BF16)) * ec2)
        w2 = dt2 * jnp.where(lo, jnp.exp(cl[:, h0:h0 + 1] - cum[:, h0:h0 + 1]),
                             jnp.exp(cl[:, h1:h1 + 1] - cum[:, h1:h1 + 1]))
        el2 = jnp.where(lo_r, jnp.exp(cl[:, h0:h0 + 1]), jnp.exp(cl[:, h1:h1 + 1]))
        s_scr[j] = el2 * s + _dot_tn((x2 * w2).astype(BF16), bg)
    y = jnp.concatenate(ys, axis=1)
    y_ref[...] = _mamba_post(y, xs, z_ref[...], dsk_ref[...], nw_ref[...]).astype(BF16)

    @pl.when(c == nc - 1)
    def _():
        so_ref[...] = s_scr[...]
        cox_ref[...] = ex_scr[tail + CHUNK - (M_CONV - 1):tail + CHUNK, :]
        cob_ref[...] = eb_scr[tail + CHUNK - (M_CONV - 1):tail + CHUNK, :]


def _mamba_prompt(proj, offs, p, consts, y_init, *, nseq, nchunks):
    oz, ox, obc, osm = offs
    mw = p["dskip"].shape[1]
    bcw = p["cwb"].shape[1]
    n_pairs = mw // LANES

    def col(off, w):
        return pl.BlockSpec((CHUNK, w), lambda b, c: (b * nchunks + c, off // w))

    def full(a):
        nd = a.ndim
        return pl.BlockSpec(a.shape, lambda b, c: (0,) * nd)

    small = [p["cwx"], p["cwb"], p["cbx"], p["cbb"], p["dtb"], p["alog"], p["dskip"], p["nw"],
             consts[0], consts[1]]
    y, cox, cob, s = pl.pallas_call(
        _mamba_prompt_kernel,
        grid=(nseq, nchunks),
        in_specs=[col(oz, mw), col(ox, mw), col(obc, bcw), col(osm, LANES)] + [full(a) for a in small]
        + [pl.BlockSpec(memory_space=pl.ANY)],
        out_specs=[pl.BlockSpec((CHUNK, mw), lambda b, c: (b * nchunks + c, 0)),
                   pl.BlockSpec((None, M_CONV - 1, mw), lambda b, c: (b, 0, 0)),
                   pl.BlockSpec((None, M_CONV - 1, bcw), lambda b, c: (b, 0, 0)),
                   pl.BlockSpec((None, n_pairs, LANES, M_STATE), lambda b, c: (b, 0, 0, 0))],
        input_output_aliases={4 + len(small): 0},
        out_shape=[jax.ShapeDtypeStruct(y_init.shape, BF16),
                   jax.ShapeDtypeStruct((nseq, M_CONV - 1, mw), F32),
                   jax.ShapeDtypeStruct((nseq, M_CONV - 1, bcw), F32),
                   jax.ShapeDtypeStruct((nseq, n_pairs, LANES, M_STATE), F32)],
        scratch_shapes=[pltpu.VMEM((CHUNK + SUBLANES, mw), F32), pltpu.VMEM((CHUNK + SUBLANES, bcw), F32),
                        pltpu.VMEM((n_pairs, LANES, M_STATE), F32)],
        compiler_params=_cparams(2),
        name="mamba_prompt",
    )(proj, proj, proj, proj, *small, y_init)
    return y, jnp.concatenate([cox, cob], axis=-1), s


def _mamba_sample_prep_kernel(cs_ref, x_ref, bc_ref, sm_ref, cw_ref, cb_ref, dtb_ref, alog_ref, exp_ref,
                              co_ref, act_ref, dte_ref, ee_ref):
    cd = cw_ref.shape[1]
    new = jnp.concatenate([x_ref[...], bc_ref[...]], axis=1)
    acc = cb_ref[...] + cw_ref[M_CONV - 1:M_CONV, :] * new
    for w in range(M_CONV - 1):
        acc = acc + cw_ref[w:w + 1, :] * cs_ref[:, w * cd:(w + 1) * cd]
    act_ref[...] = _silu(acc)
    for w in range(1, M_CONV - 1):
        co_ref[:, (w - 1) * cd:w * cd] = cs_ref[:, w * cd:(w + 1) * cd]
    co_ref[:, (M_CONV - 2) * cd:(M_CONV - 1) * cd] = new
    dt = _softplus(sm_ref[...] + dtb_ref[...])
    a = -jnp.exp(alog_ref[...])
    dte_ref[...] = _dot_exact_rhs(dt, exp_ref[...])
    ee_ref[...] = jnp.exp(_dot_exact_rhs(dt * a, exp_ref[...]))


def _mamba_sample_state_kernel(x_ref, b_ref, c_ref, dte_ref, ee_ref, s_ref, *out_refs, nb, layer, first):
    y_ref, so_ref = out_refs[-2:]
    xdt_t = _columns(x_ref[...] * dte_ref[...])
    e_t = _columns(ee_ref[...])
    bv = b_ref[...]
    cb = c_ref[...].astype(BF16)
    rowid = lax.broadcasted_iota(jnp.int32, (nb, LANES), 0)
    y = jnp.zeros((nb, LANES), F32)
    for b in range(nb):
        s_new = e_t[:, b:b + 1] * s_ref[b] + xdt_t[:, b:b + 1] * bv[b:b + 1, :]
        _store_state(so_ref, layer, b, s_new, first)
        y = jnp.where(rowid == b, _dot_nt(cb, s_new.astype(BF16)), y)
    y_ref[...] = y


def _mamba_sample_post_kernel(y_ref, x_ref, z_ref, dsk_ref, nw_ref, _, o_ref):
    o_ref[...] = _mamba_post(y_ref[...], x_ref[...], z_ref[...], dsk_ref[...], nw_ref[...]).astype(BF16)


def _mamba_sample(proj, conv_state, ssm_state, prev, y_all, row0, offs, p, expand, layer):
    oz, ox, obc, osm = offs
    depth, ns = ssm_state.shape[:2]
    nb = _sample_batch_block(ns, M_STATE)
    mw = p["dskip"].shape[1]
    bcw = p["cwb"].shape[1]
    cd = mw + bcw
    n_pairs = mw // LANES
    per_group = n_pairs // M_GROUPS
    rb = row0 // ns
    first = prev is None
    cw = jnp.concatenate([p["cwx"], p["cwb"]], axis=1)
    cb = jnp.concatenate([p["cbx"], p["cbb"]], axis=1)

    def full1(a):
        nd = a.ndim
        return pl.BlockSpec(a.shape, lambda i: (0,) * nd)

    cs2 = conv_state.reshape(ns, (M_CONV - 1) * cd)
    small = [cw, cb, p["dtb"], p["alog"], expand]
    co, act, dte, ee = pl.pallas_call(
        _mamba_sample_prep_kernel,
        grid=(1,),
        in_specs=[full1(cs2),
                  pl.BlockSpec((ns, mw), lambda i: (rb, ox // mw)),
                  pl.BlockSpec((ns, bcw), lambda i: (rb, obc // bcw)),
                  pl.BlockSpec((ns, LANES), lambda i: (rb, osm // LANES))] + [full1(a) for a in small],
        out_specs=[pl.BlockSpec((ns, (M_CONV - 1) * cd), lambda i: (0, 0)),
                   pl.BlockSpec((ns, cd), lambda i: (0, 0)),
                   pl.BlockSpec((ns, mw), lambda i: (0, 0)),
                   pl.BlockSpec((ns, mw), lambda i: (0, 0))],
        out_shape=[jax.ShapeDtypeStruct((ns, (M_CONV - 1) * cd), F32),
                   jax.ShapeDtypeStruct((ns, cd), F32),
                   jax.ShapeDtypeStruct((ns, mw), F32),
                   jax.ShapeDtypeStruct((ns, mw), F32)],
        compiler_params=_cparams(1),
        name="mamba_sample_prep",
    )(cs2, proj, proj, proj, *small)

    bblk = mw // LANES
    cblk = bblk + M_GROUPS * M_STATE // LANES
    ispec, ospec = _state_specs(depth, layer, nb, M_STATE, first)
    in_specs = [pl.BlockSpec((nb, LANES), lambda j, i: (i, j)),
                pl.BlockSpec((nb, LANES), lambda j, i: (i, bblk + j // per_group)),
                pl.BlockSpec((nb, LANES), lambda j, i: (i, cblk + j // per_group)),
                pl.BlockSpec((nb, LANES), lambda j, i: (i, j)),
                pl.BlockSpec((nb, LANES), lambda j, i: (i, j)),
                ispec]
    args = [act, act, act, dte, ee, ssm_state]
    aliases = {}
    if not first:
        in_specs.append(pl.BlockSpec(memory_space=pl.ANY))
        args.append(prev)
        aliases = {len(args) - 1: 1}
    y, s_new = pl.pallas_call(
        functools.partial(_mamba_sample_state_kernel, nb=nb, layer=layer, first=first),
        grid=(n_pairs, ns // nb),
        in_specs=in_specs,
        out_specs=[pl.BlockSpec((nb, LANES), lambda j, i: (i, j)), ospec],
        out_shape=[jax.ShapeDtypeStruct((ns, mw), F32),
                   jax.ShapeDtypeStruct(ssm_state.shape, F32)],
        input_output_aliases=aliases,
        compiler_params=_cparams(2),
        name="mamba_sample_state",
    )(*args)

    ym = pl.pallas_call(
        _mamba_sample_post_kernel,
        grid=(1,),
        in_specs=[full1(y),
                  pl.BlockSpec((ns, mw), lambda i: (0, 0)),
                  pl.BlockSpec((ns, mw), lambda i: (rb, oz // mw)),
                  full1(p["dskip"]), full1(p["nw"]), pl.BlockSpec(memory_space=pl.ANY)],
        out_specs=pl.BlockSpec((ns, mw), lambda i: (rb, 0)),
        out_shape=jax.ShapeDtypeStruct(y_all.shape, BF16),
        input_output_aliases={5: 0},
        compiler_params=_cparams(1),
        name="mamba_sample_post",
    )(y, act, proj, p["dskip"], p["nw"], y_all)
    return ym, co.reshape(ns, M_CONV - 1, cd), s_new


def _pad_lanes(v, n=LANES):
    v = v.reshape(1, -1)
    return jnp.pad(v, ((0, 0), (0, n - v.shape[1])))


def kernel(x_prompt, x_sample, state_conv, state_ssm, state_hgrn, state_gla, ffn1_norm, ffn1_w_gate_up, ffn1_w_down, mix_norm, w_in, conv_w, conv_b, dt_bias, a_log, d_skip, mamba_norm, hgrn_lb_logits, hgrn_norm, gla_w_decay, gla_b_decay, gla_norm, w_branch_mamba, w_branch_hgrn, w_branch_gla, w_out, ffn2_norm, ffn2_w_gate_up, ffn2_w_down, final_norm):
    nseq, seq, d = x_prompt.shape
    ns = x_sample.shape[0]
    depth = w_in.shape[0]
    nchunks = seq // CHUNK
    n_prompt = nseq * seq
    mw = w_branch_mamba.shape[1]
    hw = w_branch_hgrn.shape[1]
    gw = w_branch_gla.shape[1]
    gk = gla_w_decay.shape[2]
    m_heads = dt_bias.shape[1]
    h_heads = state_hgrn.shape[2]
    g_heads = state_gla.shape[2]
    g_dv = state_gla.shape[4]
    bcw = 2 * M_GROUPS * M_STATE
    n_pairs = mw // LANES
    assert seq % CHUNK == 0 and n_prompt % ns == 0 and ns % 16 == 0
    assert m_heads <= G_RANK + m_heads <= LANES and gk // g_heads == LANES and hw // h_heads == LANES

    seg_w = {"z": mw, "xs": mw, "bc": bcw, "dt": m_heads, "hq": hw, "hf": hw, "hi": hw, "hg": hw,
             "gq": gk, "gk": gk, "gv": gw, "gg": gw, "ga": G_RANK, "gate": N_BRANCH * d}
    src_order = ("z", "xs", "bc", "dt", "hq", "hf", "hi", "hg", "gq", "gk", "gv", "gg", "ga", "gate")
    dst_order = ("z", "xs", "hq", "hf", "hi", "hg", "gv", "gg", "gq", "gk", "bc", "gate", "dt", "ga")
    src, off = {}, {}
    pos = 0
    for name in src_order:
        src[name] = pos
        pos += seg_w[name]
    pos = 0
    for name in dst_order:
        off[name] = pos
        pos += seg_w[name]
    n_used = pos
    n_cols = -(-n_used // 1024) * 1024
    copies = tuple((src[name], off[name], seg_w[name]) for name in dst_order)
    oz, ox, obc, osm = off["z"], off["xs"], off["bc"], off["dt"]
    assert off["ga"] == osm + m_heads and osm % LANES == 0

    consts = _scan_constants(GLA_SUB)
    tril_np = np.tril(np.ones((CHUNK, CHUNK), np.float32))
    mconsts = (jnp.asarray(tril_np, BF16), jnp.asarray(tril_np))
    expand_np = np.zeros((LANES, mw), np.float32)
    for h in range(m_heads):
        expand_np[h, h * M_HEADDIM:(h + 1) * M_HEADDIM] = 1.0
    expand = jnp.asarray(expand_np, BF16)

    x, xw, ssq = _prep(x_prompt.reshape(n_prompt, d), x_sample.reshape(ns, d), ffn1_norm[0])
    t_all = n_prompt + ns
    w_in_t = jnp.swapaxes(w_in, 1, 2)
    ssm5 = state_ssm.reshape(depth, ns, n_pairs, LANES, M_STATE)

    pc, ps, ph, pg, sc = [], [], [], [], []
    ss = sh = sg = None
    for l in range(depth):
        w_perm = _pack_w_in(w_in_t, l, copies, n_used, n_cols)
        mp = {
            "cwx": conv_w[l][:, :mw], "cwb": conv_w[l][:, mw:],
            "cbx": conv_b[l][:mw].reshape(1, mw), "cbb": conv_b[l][mw:].reshape(1, bcw),
            "dtb": _pad_lanes(dt_bias[l]), "alog": _pad_lanes(a_log[l]),
            "dskip": jnp.repeat(d_skip[l], M_HEADDIM).reshape(1, mw),
            "nw": mamba_norm[l].reshape(1, mw),
        }
        wd = jnp.zeros((LANES, gk), F32).at[m_heads:m_heads + G_RANK].set(gla_w_decay[l])
        bd = gla_b_decay[l].reshape(1, gk)
        hnw = hgrn_norm[l].reshape(1, LANES)
        gnw = gla_norm[l].reshape(1, g_dv)

        act, cast = _ffn_up(xw, ssq, ffn1_w_gate_up, l, (ffn1_w_down,))
        w_dn, tn_dn = (cast[0], 512) if cast else (ffn1_w_down, 256)
        x, xw, ssq = _resid_matmul(act, w_dn, l, x, 0.5, tn_dn, "ffn_down", mix_norm[l])
        proj, cast = _in_proj(xw, ssq, w_perm, l, (w_branch_mamba, w_branch_hgrn, w_branch_gla, w_out))
        w_bm, w_bh, w_bg, w_o = cast if cast else (w_branch_mamba, w_branch_hgrn, w_branch_gla, w_out)

        moffs = (oz, ox, obc, osm)
        hoffs = (off["hq"], off["hf"], off["hi"], off["hg"])
        goffs = (off["gq"], off["gk"], off["gv"], off["gg"], osm)
        ym, c1, s1 = _mamba_prompt(proj, moffs, mp, mconsts, jnp.zeros((t_all, mw), BF16),
                                   nseq=nseq, nchunks=nchunks)
        yh, h1 = _gla_prompt(proj, hoffs, (hgrn_lb_logits,), hnw, consts, jnp.zeros((t_all, hw), BF16),
                             hgrn=True, nseq=nseq, nchunks=nchunks, heads=h_heads, dv=LANES, layer=l)
        yg, g1 = _gla_prompt(proj, goffs, (wd, bd), gnw, consts, jnp.zeros((t_all, gw), BF16),
                             hgrn=False, nseq=nseq, nchunks=nchunks, heads=g_heads, dv=g_dv, layer=l)
        ym, c2, ss = _mamba_sample(proj, state_conv[l], ssm5, ss, ym, n_prompt, moffs, mp, expand, l)
        yh, sh = _gla_sample(proj, state_hgrn, sh, yh, n_prompt, hoffs, (hgrn_lb_logits,), hnw,
                             hgrn=True, heads=h_heads, dv=LANES, layer=l)
        yg, sg = _gla_sample(proj, state_gla, sg, yg, n_prompt, goffs, (wd, bd), gnw,
                             hgrn=False, heads=g_heads, dv=g_dv, layer=l)
        merged = _merge(ym, yh, yg, w_bm, w_bh, w_bg, l, proj, off["gate"])
        x, xw, ssq = _resid_matmul(merged, w_o, l, x, 1.0, 512, "out_proj", ffn2_norm[l], WIDE_TM_CAP)

        act, cast = _ffn_up(xw, ssq, ffn2_w_gate_up, l, (ffn2_w_down,))
        w_dn, tn_dn = (cast[0], 512) if cast else (ffn2_w_down, 256)
        if l + 1 < depth:
            x, xw, ssq = _resid_matmul(act, w_dn, l, x, 0.5, tn_dn, "ffn_down", ffn1_norm[l + 1])
        else:
            x = _resid_matmul(act, w_dn, l, x, 0.5, tn_dn, "ffn_down")

        pc.append(c1)
        ps.append(s1.reshape(nseq, m_heads, M_HEADDIM, M_STATE))
        ph.append(h1)
        pg.append(g1)
        sc.append(c2)

    y_prompt = _final_norm(x, final_norm, 0, n_prompt, "final_norm_prompt").reshape(nseq, seq, d)
    y_sample = _final_norm(x, final_norm, n_prompt, ns, "final_norm_sample").reshape(ns, 1, d)
    return (y_prompt, y_sample, jnp.stack(pc), jnp.stack(ps), jnp.stack(ph), jnp.stack(pg),
            jnp.stack(sc), ss.reshape(state_ssm.shape), sh, sg)
```

```python
import functools

import jax
import jax.numpy as jnp
import numpy as np
from jax import lax
from jax.experimental import pallas as pl
from jax.experimental.pallas import tpu as pltpu

F32 = jnp.float32
BF16 = jnp.bfloat16
EPS = 1e-6
F32_TINY = float(np.finfo(np.float32).tiny)

CHUNK = 256
LANES = 128
SUBLANES = 8
M_HEADDIM = 64
M_STATE = 128
M_GROUPS = 2
M_CONV = 4
G_RANK = 16
G_NORMALIZER = 16.0
N_BRANCH = 3
GLA_SUB = 128
VMEM_LIMIT = 56 * 1024 * 1024
STATE_BLOCK_BYTES = 4 * 1024 * 1024
WIDE_TM_CAP = 2100
MAIN_TN = 1024
GATE_TN = 1536


def _cparams(n_axes):
    return pltpu.CompilerParams(dimension_semantics=("arbitrary",) * n_axes,
                                vmem_limit_bytes=VMEM_LIMIT)


def _dot(a, b):
    return jnp.dot(a, b, preferred_element_type=F32)


def _dot_nt(a, b):
    return lax.dot_general(a, b, (((1,), (1,)), ((), ())), preferred_element_type=F32)


def _dot_tn(a, b):
    return lax.dot_general(a, b, (((0,), (0,)), ((), ())), preferred_element_type=F32)


def _split3(x):
    hi = x.astype(BF16)
    r = x - hi.astype(F32)
    mid = r.astype(BF16)
    lo = (r - mid.astype(F32)).astype(BF16)
    return hi, mid, lo


def _dot_exact_lhs(p_bf16, x):
    hi, mid, lo = _split3(x)
    return _dot(p_bf16, hi) + _dot(p_bf16, mid) + _dot(p_bf16, lo)


def _dot_exact_rhs(x, p_bf16):
    hi, mid, lo = _split3(x)
    return _dot(hi, p_bf16) + _dot(mid, p_bf16) + _dot(lo, p_bf16)


def _dot_f32(a, b):
    ah, am, _ = _split3(a)
    bh, bm, _ = _split3(b)
    return _dot(ah, bh) + _dot(ah, bm) + _dot(am, bh)


def _sigmoid(x):
    return jax.nn.sigmoid(x)


def _silu(x):
    return x * _sigmoid(x)


def _softplus(x):
    return jnp.maximum(x, 0.0) + jnp.log1p(jnp.exp(-jnp.abs(x)))


def _log_sigmoid(x):
    return jnp.minimum(x, 0.0) - jnp.log(1.0 + jnp.exp(-jnp.abs(x)))


def _rms(x, w):
    ms = jnp.mean(x * x, axis=-1, keepdims=True)
    return x * lax.rsqrt(ms + EPS) * w


def _pick_tm(t, cap=1100):
    best = 16
    for tm in range(16, min(t, cap) + 1, 16):
        if t % tm == 0:
            best = tm
    return best


def _row_scale(ssq_ref, d):
    return lax.rsqrt(ssq_ref[:, 0:1] * (1.0 / d) + EPS)


def _emit_normed(xn, nw_ref, xw_ref, ssq_ref, accumulate):
    xw_ref[...] = (xn * nw_ref[...]).astype(BF16)
    part = jnp.broadcast_to(jnp.sum(xn * xn, axis=1, keepdims=True), ssq_ref.shape)
    ssq_ref[...] = ssq_ref[...] + part if accumulate else part


def _prep_kernel(xp_ref, xs_ref, nw_ref, x_ref, xw_ref, ssq_ref, *, n_prompt_blocks):
    x = jnp.where(pl.program_id(0) < n_prompt_blocks, xp_ref[...], xs_ref[...])
    x_ref[...] = x
    _emit_normed(x, nw_ref, xw_ref, ssq_ref, accumulate=False)


def _prep(xp, xs, nw):
    n_prompt, d = xp.shape
    rb = xs.shape[0]
    assert n_prompt % rb == 0
    npb = n_prompt // rb
    t = n_prompt + rb
    row = pl.BlockSpec((rb, d), lambda i: (i, 0))
    return pl.pallas_call(
        functools.partial(_prep_kernel, n_prompt_blocks=npb),
        grid=(npb + 1,),
        in_specs=[pl.BlockSpec((rb, d), lambda i: (jnp.minimum(i, npb - 1), 0)),
                  pl.BlockSpec((rb, d), lambda i: (0, 0)),
                  pl.BlockSpec((1, d), lambda i: (0, 0))],
        out_specs=[row, row, pl.BlockSpec((rb, LANES), lambda i: (i, 0))],
        out_shape=[jax.ShapeDtypeStruct((t, d), F32), jax.ShapeDtypeStruct((t, d), BF16),
                   jax.ShapeDtypeStruct((t, LANES), F32)],
        compiler_params=_cparams(1),
        name="prep",
    )(xp, xs, nw.reshape(1, d))


def _row_parts(tm):
    half = -(-(tm // 2) // 16) * 16
    return ((0, half), (half, tm - half)) if 0 < half < tm else ((0, tm),)


def _side_casts(ws, layer, n_i, nj):
    steps = n_i * nj
    ins, outs, shapes = [], [], []
    for w in ws:
        _, k, d = w.shape
        rows = k // steps
        if rows * steps != k or rows % 16:
            return None
        ins.append(pl.BlockSpec((None, rows, d), lambda i, j: (layer, i * nj + j, 0)))
        outs.append(pl.BlockSpec((rows, d), lambda i, j: (i * nj + j, 0)))
        shapes.append(jax.ShapeDtypeStruct((k, d), BF16))
    return ins, outs, shapes


def _do_side_casts(refs):
    n = len(refs) // 2
    for src, dst in zip(refs[:n], refs[n:]):
        dst[...] = src[...].astype(BF16)


def _ffn_up_kernel(xw_ref, ssq_ref, wg_ref, wu_ref, *rest, parts, n_cast):
    o_ref = rest[n_cast]
    _do_side_casts(rest[:n_cast] + rest[n_cast + 1:])
    wg = wg_ref[...].astype(BF16)
    wu = wu_ref[...].astype(BF16)
    d = xw_ref.shape[1]
    for start, rows in parts:
        sl = pl.ds(start, rows)
        r = lax.rsqrt(ssq_ref[sl, 0:1] * (1.0 / d) + EPS)
        h = xw_ref[sl, :]
        g = r * _dot(h, wg)
        u = r * _dot(h, wu)
        o_ref[sl, :] = (_silu(g) * u).astype(BF16)


def _ffn_up(xw, ssq, w_gu, layer, cast_ws=(), tn=512):
    t, d = xw.shape
    dff = w_gu.shape[2] // 2
    tm = _pick_tm(t, WIDE_TM_CAP)
    nj = dff // tn
    side = _side_casts(cast_ws, layer, t // tm, nj) if cast_ws else None
    c_in, c_out, c_shape = side if side else ([], [], [])
    outs = pl.pallas_call(
        functools.partial(_ffn_up_kernel, parts=_row_parts(tm), n_cast=len(c_in)),
        grid=(t // tm, nj),
        in_specs=[pl.BlockSpec((tm, d), lambda i, j: (i, 0)),
                  pl.BlockSpec((tm, LANES), lambda i, j: (i, 0)),
                  pl.BlockSpec((None, d, tn), lambda i, j: (layer, 0, j)),
                  pl.BlockSpec((None, d, tn), lambda i, j: (layer, 0, j + nj))] + c_in,
        out_specs=[pl.BlockSpec((tm, tn), lambda i, j: (i, j))] + c_out,
        out_shape=[jax.ShapeDtypeStruct((t, dff), BF16)] + c_shape,
        compiler_params=_cparams(2),
        name="ffn_up",
    )(xw, ssq, w_gu, w_gu, *(cast_ws if side else ()))
    return outs[0], (tuple(outs[1:]) if side else None)


def _resid_matmul_kernel(a_ref, w_ref, x_ref, *rest, scale, emit, parts):
    if emit:
        nw_ref, o_ref, xw_ref, ssq_ref = rest

        @pl.when(pl.program_id(1) == 0)
        def _():
            ssq_ref[...] = jnp.zeros_like(ssq_ref)
    else:
        (o_ref,) = rest
    wb = w_ref[...].astype(BF16)
    for start, rows in parts:
        sl = pl.ds(start, rows)
        xn = x_ref[sl, :] + scale * _dot(a_ref[sl, :], wb)
        if emit:
            _emit_normed(xn, nw_ref, xw_ref.at[sl, :], ssq_ref.at[sl, :], accumulate=True)
        o_ref[sl, :] = xn


def _wspec(w, layer, tn):
    if w.ndim == 3:
        return pl.BlockSpec((None, w.shape[1], tn), lambda i, j: (layer, 0, j))
    return pl.BlockSpec((w.shape[0], tn), lambda i, j: (0, j))


def _resid_matmul(a, w, layer, x, scale, tn, name, next_nw=None, tm_cap=1100):
    t, k = a.shape
    d = w.shape[-1]
    tm = _pick_tm(t, tm_cap)
    emit = next_nw is not None
    tile = pl.BlockSpec((tm, tn), lambda i, j: (i, j))
    in_specs = [pl.BlockSpec((tm, k), lambda i, j: (i, 0)), _wspec(w, layer, tn), tile]
    args = [a, w, x]
    out_specs = [tile]
    out_shape = [jax.ShapeDtypeStruct((t, d), F32)]
    if emit:
        in_specs.append(pl.BlockSpec((1, tn), lambda i, j: (0, j)))
        args.append(next_nw.reshape(1, d))
        out_specs += [tile, pl.BlockSpec((tm, LANES), lambda i, j: (i, 0))]
        out_shape += [jax.ShapeDtypeStruct((t, d), BF16), jax.ShapeDtypeStruct((t, LANES), F32)]
    outs = pl.pallas_call(
        functools.partial(_resid_matmul_kernel, scale=scale, emit=emit,
                          parts=_row_parts(tm) if k <= d else ((0, tm),)),
        grid=(t // tm, d // tn),
        in_specs=in_specs,
        out_specs=out_specs,
        out_shape=out_shape,
        compiler_params=_cparams(2),
        name=name,
    )(*args)
    return outs if emit else outs[0]


def _pack_w_in_kernel(w_ref, o_ref, *, copies, zero):
    for src, dst, width in copies:
        if width % LANES == 0:
            o_ref[:, dst:dst + width] = w_ref[src:src + width, :].T.astype(BF16)
        else:
            base = src // LANES * LANES
            lo = src - base
            assert dst % LANES == lo and lo + width <= LANES
            t = w_ref[base:base + LANES, :].T
            o_ref[:, dst:dst + width] = t[:, lo:lo + width].astype(BF16)
    start, stop = zero
    if stop > start:
        o_ref[:, start:stop] = jnp.zeros((o_ref.shape[0], stop - start), BF16)


def _pack_w_in(w_t, layer, copies, zero, n_cols, tk=128):
    _, n_src, d = w_t.shape
    return pl.pallas_call(
        functools.partial(_pack_w_in_kernel, copies=copies, zero=zero),
        grid=(d // tk,),
        in_specs=[pl.BlockSpec((None, n_src, tk), lambda i: (layer, 0, i))],
        out_specs=pl.BlockSpec((tk, n_cols), lambda i: (i, 0)),
        out_shape=jax.ShapeDtypeStruct((d, n_cols), BF16),
        compiler_params=_cparams(1),
        name="pack_w_in",
    )(w_t)


def _in_proj_kernel(xw_ref, ssq_ref, w_ref, *rest, n_cast, gates, parts):
    o_ref = rest[n_cast]
    _do_side_casts(rest[:n_cast] + rest[n_cast + 1:])
    d = xw_ref.shape[1]
    w = w_ref[...]
    for start, rows in parts:
        sl = pl.ds(start, rows)
        p = lax.rsqrt(ssq_ref[sl, 0:1] * (1.0 / d) + EPS) * _dot(xw_ref[sl, :], w)
        o_ref[sl, :] = _sigmoid(p).astype(BF16) if gates else p


def _in_proj(xw, ssq, w, layer, col0, n, *, gates, cast_ws=(), tn=1024):
    t, d = xw.shape
    tm = _pick_tm(t, WIDE_TM_CAP)
    assert col0 % tn == 0 and n % tn == 0
    jb = col0 // tn
    side = _side_casts(cast_ws, layer, t // tm, n // tn) if cast_ws else None
    c_in, c_out, c_shape = side if side else ([], [], [])
    outs = pl.pallas_call(
        functools.partial(_in_proj_kernel, n_cast=len(c_in), gates=gates,
                          parts=_row_parts(tm) if gates else ((0, tm),)),
        grid=(t // tm, n // tn),
        in_specs=[pl.BlockSpec((tm, d), lambda i, j: (i, 0)),
                  pl.BlockSpec((tm, LANES), lambda i, j: (i, 0)),
                  pl.BlockSpec((d, tn), lambda i, j: (0, jb + j))] + c_in,
        out_specs=[pl.BlockSpec((tm, tn), lambda i, j: (i, j))] + c_out,
        out_shape=[jax.ShapeDtypeStruct((t, n), BF16 if gates else F32)] + c_shape,
        compiler_params=_cparams(2),
        name="in_proj_gates" if gates else "in_proj",
    )(xw, ssq, w, *(cast_ws if side else ()))
    return outs[0], (tuple(outs[1:]) if side else None)


def _merge_kernel(ym_ref, yh_ref, yg_ref, wm_ref, wh_ref, wg_ref, g0_ref, g1_ref, g2_ref, o_ref, *, parts):
    wm = wm_ref[...].astype(BF16)
    wh = wh_ref[...].astype(BF16)
    wg = wg_ref[...].astype(BF16)
    for start, rows in parts:
        sl = pl.ds(start, rows)
        acc = g0_ref[sl, :].astype(F32) * _dot(ym_ref[sl, :], wm)
        acc = acc + g1_ref[sl, :].astype(F32) * _dot(yh_ref[sl, :], wh)
        acc = acc + g2_ref[sl, :].astype(F32) * _dot(yg_ref[sl, :], wg)
        o_ref[sl, :] = acc.astype(BF16)


def _merge(ym, yh, yg, wm, wh, wg, layer, gates, tn=512):
    t, k = ym.shape
    d = wm.shape[-1]
    tm = _pick_tm(t)
    nb = d // tn
    yspec = pl.BlockSpec((tm, k), lambda i, j: (i, 0))
    wspec = _wspec(wm, layer, tn)

    def gspec(b):
        return pl.BlockSpec((tm, tn), lambda i, j: (i, b * nb + j))

    return pl.pallas_call(
        functools.partial(_merge_kernel, parts=_row_parts(tm)),
        grid=(t // tm, nb),
        in_specs=[yspec, yspec, yspec, wspec, wspec, wspec, gspec(0), gspec(1), gspec(2)],
        out_specs=pl.BlockSpec((tm, tn), lambda i, j: (i, j)),
        out_shape=jax.ShapeDtypeStruct((t, d), BF16),
        compiler_params=_cparams(2),
        name="merge",
    )(ym, yh, yg, wm, wh, wg, gates, gates, gates)


def _final_norm_kernel(x_ref, nw_ref, o_ref):
    o_ref[...] = _rms(x_ref[...], nw_ref[...])


def _final_norm(x, nw, row0, rows, name):
    d = x.shape[1]
    tm = _pick_tm(rows)
    assert row0 % tm == 0
    return pl.pallas_call(
        _final_norm_kernel,
        grid=(rows // tm,),
        in_specs=[pl.BlockSpec((tm, d), lambda i: (row0 // tm + i, 0)),
                  pl.BlockSpec((1, d), lambda i: (0, 0))],
        out_specs=pl.BlockSpec((tm, d), lambda i: (i, 0)),
        out_shape=jax.ShapeDtypeStruct((rows, d), F32),
        compiler_params=_cparams(1),
        name=name,
    )(x, nw.reshape(1, d))


def _scan_constants(c):
    N_LEVELS = int(np.log2(c))
    assert 1 << N_LEVELS == c and N_LEVELS >= 3
    pm = np.zeros((2, c, c), np.float32)
    sg = np.zeros((N_LEVELS - 1, c, LANES), np.float32)
    mk = np.zeros((N_LEVELS + 1, c, c), np.float32)
    r = np.arange(c)
    for lvl in range(N_LEVELS):
        h = c >> (lvl + 1)
        for t in range(c):
            blk, pos = divmod(t, 2 * h)
            ridx = blk * 2 * h + h - 1
            upper = pos >= h
            if upper:
                mk[lvl, t] = ((r // (2 * h)) == blk) & ((r % (2 * h)) < h)
            if h >= 4:
                sg[lvl, t] = 1.0 if upper else -1.0
            elif h == 2:
                pm[0, t] = ((r > ridx) & (r <= t)) if upper else ((r > t) & (r <= ridx))
            else:
                sg[N_LEVELS - 2, t] = 1.0 if upper else 0.0
    pm[1] = r[None, :] <= r[:, None]
    mk[N_LEVELS] = np.eye(c)
    return jnp.asarray(pm.reshape(2 * c, c), BF16), jnp.asarray(sg), jnp.asarray(mk)


def _lower_bound(logits, layer):
    m = jnp.max(logits, axis=0, keepdims=True)
    e = jnp.exp(logits - m)
    sm = e / jnp.sum(e, axis=0, keepdims=True)
    lb = jnp.zeros_like(m)
    for i in range(1, layer + 1):
        lb = lb + sm[i:i + 1, :]
    return lb


def _gla_inputs(refs, hgrn, layer):
    if hgrn:
        q_ref, f_ref, i_ref, lbl_ref = refs
        lb = _lower_bound(lbl_ref[...], layer)
        q = _silu(q_ref[...]) * (LANES ** -0.5)
        zf = f_ref[...]
        f = lb + (1.0 - lb) * _sigmoid(zf)
        gl = jnp.log(jnp.maximum(f, F32_TINY))
        k = (1.0 - lb) * _sigmoid(-zf)
        v = i_ref[...]
    else:
        q_ref, k_ref, v_ref, a_ref, wd_ref, bd_ref = refs
        q = q_ref[...] * (LANES ** -0.5)
        k = k_ref[...]
        v = v_ref[...]
        gl = _log_sigmoid(_dot_f32(a_ref[...], wd_ref[...]) + bd_ref[...]) / G_NORMALIZER
    return q, k, v, gl


def _level_factor(lvl, gl, cum, z_h2, sg_ref):
    c = gl.shape[0]
    n_levels = sg_ref.shape[0] + 1
    h = c >> (lvl + 1)
    if h >= 4:
        c3 = cum.reshape(c // (2 * h), 2 * h, LANES)
        d = (c3 - c3[:, h - 1:h, :]).reshape(c, LANES)
        return jnp.exp(sg_ref[lvl] * d)
    if h == 2:
        return jnp.exp(z_h2)
    return jnp.exp(sg_ref[n_levels - 2] * gl)


def _gla_prompt_kernel(*refs, hgrn, layer, hp, dv):
    n_in = 4 if hgrn else 6
    in_refs = refs[:n_in]
    g_ref, nw_ref, pm_ref, sg_ref, mk_ref, _, y_ref, so_ref, s_scr = refs[n_in:]
    c = pl.program_id(2)
    sub = mk_ref.shape[1]
    n_levels = mk_ref.shape[0] - 1

    @pl.when(c == 0)
    def _():
        s_scr[...] = jnp.zeros_like(s_scr)

    q_all, k_all, v_all, gl_all = _gla_inputs(in_refs, hgrn, layer)
    finals = []
    for u in range(hp):
        s = s_scr[u]
        for r0 in range(0, CHUNK, sub):
            rows = pl.ds(r0, sub)
            q, k, gl = (a[r0:r0 + sub, u * LANES:(u + 1) * LANES] for a in (q_all, k_all, gl_all))
            v = v_all[r0:r0 + sub, u * dv:(u + 1) * dv]
            zz = _dot_exact_lhs(pm_ref[...], gl)
            z_h2, cum = zz[:sub], zz[sub:]
            qb, kb = q.astype(BF16), k.astype(BF16)
            att = mk_ref[n_levels] * _dot_nt(qb, kb)
            for lvl in range(n_levels):
                eb = _level_factor(lvl, gl, cum, z_h2, sg_ref).astype(BF16)
                att = att + mk_ref[lvl] * _dot_nt(qb * eb, kb * eb)
            e_cum = jnp.exp(cum)
            e_tail = jnp.exp(cum[sub - 1:sub, :] - cum)
            vb = v.astype(BF16)
            o = _dot(att.astype(BF16), vb) + _dot((q * e_cum).astype(BF16), s.astype(BF16))
            e_last = e_cum.T[:, sub - 1:sub]
            s = e_last * s + _dot_tn((k * e_tail).astype(BF16), vb)
            gate = g_ref[rows, u * dv:(u + 1) * dv]
            y_ref[rows, u * dv:(u + 1) * dv] = (_rms(o, nw_ref[...]) * _silu(gate)).astype(BF16)
        s_scr[u] = s
        finals.append(s)

    @pl.when(c == pl.num_programs(2) - 1)
    def _():
        for u in range(hp):
            so_ref[u] = finals[u]


def _gla_prompt(proj, offs, extra, nw, consts, y_init, *, hgrn, nseq, nchunks, heads, dv, layer, hp=4):
    pm, sg, mk = consts
    assert heads % hp == 0

    def col(off, w):
        assert off % (hp * w) == 0
        return pl.BlockSpec((CHUNK, hp * w), lambda h, b, c: (b * nchunks + c, off // (hp * w) + h))

    def full(a):
        nd = a.ndim
        return pl.BlockSpec(a.shape, lambda h, b, c: (0,) * nd)

    if hgrn:
        oq, of, oi, og = offs
        (lbl,) = extra
        in_specs = [col(oq, LANES), col(of, LANES), col(oi, dv),
                    pl.BlockSpec((lbl.shape[0], hp * LANES), lambda h, b, c: (0, h))]
        args = [proj, proj, proj, lbl]
    else:
        oq, ok, ov, og, oa = offs
        wd, bd = extra
        in_specs = [col(oq, LANES), col(ok, LANES), col(ov, dv),
                    pl.BlockSpec((CHUNK, LANES), lambda h, b, c: (b * nchunks + c, oa // LANES)),
                    pl.BlockSpec((LANES, hp * LANES), lambda h, b, c: (0, h)),
                    pl.BlockSpec((1, hp * LANES), lambda h, b, c: (0, h))]
        args = [proj, proj, proj, proj, wd, bd]
    in_specs += [col(og, dv), full(nw), full(pm), full(sg), full(mk), pl.BlockSpec(memory_space=pl.ANY)]
    args += [proj, nw, pm, sg, mk, y_init]
    y, s = pl.pallas_call(
        functools.partial(_gla_prompt_kernel, hgrn=hgrn, layer=layer, hp=hp, dv=dv),
        grid=(heads // hp, nseq, nchunks),
        in_specs=in_specs,
        out_specs=[pl.BlockSpec((CHUNK, hp * dv), lambda h, b, c: (b * nchunks + c, h)),
                   pl.BlockSpec((None, hp, LANES, dv), lambda h, b, c: (b, h, 0, 0))],
        out_shape=[jax.ShapeDtypeStruct(y_init.shape, BF16),
                   jax.ShapeDtypeStruct((nseq, heads, LANES, dv), F32)],
        input_output_aliases={len(args) - 1: 0},
        scratch_shapes=[pltpu.VMEM((hp, LANES, dv), F32)],
        compiler_params=_cparams(3),
        name="hgrn_prompt" if hgrn else "gla_prompt",
    )(*args)
    return y, s


def _columns(x):
    nb = x.shape[0]
    if nb < LANES:
        x = jnp.concatenate([x, jnp.zeros((LANES - nb, x.shape[1]), x.dtype)], axis=0)
    return x.T


def _store_state(so_ref, layer, b, s_new, first):
    if first:
        for l in range(so_ref.shape[0]):
            so_ref[l, b] = s_new if l == layer else jnp.zeros_like(s_new)
    else:
        so_ref[b] = s_new


def _gla_sample_kernel(*refs, hgrn, nb, layer, first):
    n_in = 4 if hgrn else 6
    in_refs = refs[:n_in]
    g_ref, nw_ref, s_ref = refs[n_in:n_in + 3]
    y_ref, so_ref = refs[-2:]
    q, k, v, gl = _gla_inputs(in_refs, hgrn, layer)
    e_t = _columns(jnp.exp(gl))
    k_t = _columns(k)
    qb = q.astype(BF16)
    rowid = lax.broadcasted_iota(jnp.int32, v.shape, 0)
    o = jnp.zeros(v.shape, F32)
    for b in range(nb):
        s_new = e_t[:, b:b + 1] * s_ref[b] + k_t[:, b:b + 1] * v[b:b + 1, :]
        _store_state(so_ref, layer, b, s_new, first)
        o = jnp.where(rowid == b, _dot(qb, s_new.astype(BF16)), o)
    y_ref[...] = (_rms(o, nw_ref[...]) * _silu(g_ref[...])).astype(BF16)


def _state_specs(depth, layer, nb, dv, first):
    ispec = pl.BlockSpec((None, nb, None, LANES, dv), lambda h, i: (layer, i, h, 0, 0))
    if first:
        ospec = pl.BlockSpec((depth, nb, None, LANES, dv), lambda h, i: (0, i, h, 0, 0))
    else:
        ospec = ispec
    return ispec, ospec


def _sample_batch_block(ns, dv):
    nb = min(ns, STATE_BLOCK_BYTES // (LANES * dv * 4))
    assert ns % nb == 0
    return nb


def _gla_sample(proj, state, prev, y_all, row0, offs, extra, nw, *, hgrn, heads, dv, layer):
    depth, ns = state.shape[:2]
    nb = _sample_batch_block(ns, dv)
    rb = row0 // nb
    first = prev is None

    def col(off, w):
        return pl.BlockSpec((nb, w), lambda h, i: (rb + i, off // w + h))

    if hgrn:
        oq, of, oi, og = offs
        (lbl,) = extra
        in_specs = [col(oq, LANES), col(of, LANES), col(oi, dv),
                    pl.BlockSpec((lbl.shape[0], LANES), lambda h, i: (0, h))]
        args = [proj, proj, proj, lbl]
    else:
        oq, ok, ov, og, oa = offs
        wd, bd = extra
        in_specs = [col(oq, LANES), col(ok, LANES), col(ov, dv),
                    pl.BlockSpec((nb, LANES), lambda h, i: (rb + i, oa // LANES)),
                    pl.BlockSpec((LANES, LANES), lambda h, i: (0, h)),
                    pl.BlockSpec((1, LANES), lambda h, i: (0, h))]
        args = [proj, proj, proj, proj, wd, bd]
    ispec, ospec = _state_specs(depth, layer, nb, dv, first)
    in_specs += [col(og, dv), pl.BlockSpec(nw.shape, lambda h, i: (0, 0)), ispec,
                 pl.BlockSpec(memory_space=pl.ANY)]
    args += [proj, nw, state, y_all]
    aliases = {len(args) - 1: 0}
    if not first:
        in_specs.append(pl.BlockSpec(memory_space=pl.ANY))
        args.append(prev)
        aliases[len(args) - 1] = 1
    y, s = pl.pallas_call(
        functools.partial(_gla_sample_kernel, hgrn=hgrn, nb=nb, layer=layer, first=first),
        grid=(heads, ns // nb),
        in_specs=in_specs,
        out_specs=[pl.BlockSpec((nb, dv), lambda h, i: (rb + i, h)), ospec],
        out_shape=[jax.ShapeDtypeStruct(y_all.shape, BF16),
                   jax.ShapeDtypeStruct(state.shape, F32)],
        input_output_aliases=aliases,
        compiler_params=_cparams(2),
        name="hgrn_sample" if hgrn else "gla_sample",
    )(*args)
    return y, s


def _mamba_post(y, xs, z, dskip, nw):
    y = (y + dskip * xs) * _silu(z)
    gs = y.shape[1] // M_GROUPS
    outs = [_rms(y[:, g * gs:(g + 1) * gs], nw[:, g * gs:(g + 1) * gs]) for g in range(M_GROUPS)]
    return jnp.concatenate(outs, axis=1)


def _mamba_prompt_kernel(z_ref, x_ref, bc_ref, sm_ref, cwx_ref, cwb_ref, cbx_ref, cbb_ref,
                         dtb_ref, alog_ref, dsk_ref, nw_ref, tril_ref, mask_ref, _,
                         y_ref, cox_ref, cob_ref, so_ref, ex_scr, eb_scr, s_scr):
    c = pl.program_id(1)
    nc = pl.num_programs(1)
    tail = SUBLANES

    @pl.when(c == 0)
    def _():
        ex_scr[0:tail, :] = jnp.zeros((tail, ex_scr.shape[1]), F32)
        eb_scr[0:tail, :] = jnp.zeros((tail, eb_scr.shape[1]), F32)
        s_scr[...] = jnp.zeros_like(s_scr)

    ex_scr[tail:tail + CHUNK, :] = x_ref[...]
    eb_scr[tail:tail + CHUNK, :] = bc_ref[...]

    def conv(scr, cw_ref, cb_ref):
        xe = scr[...]
        acc = cw_ref[0:1, :] * xe
        for w in range(1, M_CONV):
            acc = pltpu.roll(acc, 1, axis=0) + cw_ref[w:w + 1, :] * xe
        return _silu(acc[tail:, :] + cb_ref[...])

    xs = conv(ex_scr, cwx_ref, cbx_ref)
    bcm = conv(eb_scr, cwb_ref, cbb_ref)

    ex_scr[0:tail, :] = ex_scr[CHUNK:CHUNK + tail, :]
    eb_scr[0:tail, :] = eb_scr[CHUNK:CHUNK + tail, :]

    dt = _softplus(sm_ref[...] + dtb_ref[...])
    a = -jnp.exp(alog_ref[...])
    cum = _dot_exact_lhs(tril_ref[...], dt * a)
    cum_t = cum.T
    cl = cum[CHUNK - 1:CHUNK, :]
    mask = mask_ref[...] > 0.5
    gw = M_GROUPS * M_STATE
    lo = lax.broadcasted_iota(jnp.int32, (CHUNK, LANES), 1) < M_HEADDIM
    lo_r = lax.broadcasted_iota(jnp.int32, (LANES, LANES), 0) < M_HEADDIM
    n_pairs = xs.shape[1] // LANES
    per_group = n_pairs // M_GROUPS
    ys = []
    for j in range(n_pairs):
        g = j // per_group
        h0, h1 = 2 * j, 2 * j + 1
        if j % per_group == 0:
            bg = bcm[:, g * M_STATE:(g + 1) * M_STATE].astype(BF16)
            cg = bcm[:, gw + g * M_STATE:gw + (g + 1) * M_STATE].astype(BF16)
            cb = jnp.where(mask, _dot_nt(cg, bg), 0.0)

        def dec(h):
            return jnp.exp(jnp.minimum(cum[:, h:h + 1] - cum_t[h:h + 1, :], 0.0))

        x2 = xs[:, j * LANES:(j + 1) * LANES]
        dt2 = jnp.where(lo, dt[:, h0:h0 + 1], dt[:, h1:h1 + 1])
        xdt = (x2 * dt2).astype(BF16)
        y_in = jnp.where(lo, _dot((cb * dec(h0)).astype(BF16), xdt),
                         _dot((cb * dec(h1)).astype(BF16), xdt))
        ec2 = jnp.where(lo, jnp.exp(cum[:, h0:h0 + 1]), jnp.exp(cum[:, h1:h1 + 1]))
        s = s_scr[j]
        ys.append(y_in + _dot_nt(cg, s.astype(BF16)) * ec2)
        w2 = dt2 * jnp.where(lo, jnp.exp(cl[:, h0:h0 + 1] - cum[:, h0:h0 + 1]),
                             jnp.exp(cl[:, h1:h1 + 1] - cum[:, h1:h1 + 1]))
        el2 = jnp.where(lo_r, jnp.exp(cl[:, h0:h0 + 1]), jnp.exp(cl[:, h1:h1 + 1]))
        s_scr[j] = el2 * s + _dot_tn((x2 * w2).astype(BF16), bg)
    y = jnp.concatenate(ys, axis=1)
    y_ref[...] = _mamba_post(y, xs, z_ref[...], dsk_ref[...], nw_ref[...]).astype(BF16)

    @pl.when(c == nc - 1)
    def _():
        so_ref[...] = s_scr[...]
        cox_ref[...] = ex_scr[tail + CHUNK - (M_CONV - 1):tail + CHUNK, :]
        cob_ref[...] = eb_scr[tail + CHUNK - (M_CONV - 1):tail + CHUNK, :]


def _mamba_prompt(proj, offs, p, consts, y_init, *, nseq, nchunks):
    oz, ox, obc, osm = offs
    mw = p["dskip"].shape[1]
    bcw = p["cwb"].shape[1]
    n_pairs = mw // LANES

    def col(off, w):
        return pl.BlockSpec((CHUNK, w), lambda b, c: (b * nchunks + c, off // w))

    def full(a):
        nd = a.ndim
        return pl.BlockSpec(a.shape, lambda b, c: (0,) * nd)

    small = [p["cwx"], p["cwb"], p["cbx"], p["cbb"], p["dtb"], p["alog"], p["dskip"], p["nw"],
             consts[0], consts[1]]
    y, cox, cob, s = pl.pallas_call(
        _mamba_prompt_kernel,
        grid=(nseq, nchunks),
        in_specs=[col(oz, mw), col(ox, mw), col(obc, bcw), col(osm, LANES)] + [full(a) for a in small]
        + [pl.BlockSpec(memory_space=pl.ANY)],
        out_specs=[pl.BlockSpec((CHUNK, mw), lambda b, c: (b * nchunks + c, 0)),
                   pl.BlockSpec((None, M_CONV - 1, mw), lambda b, c: (b, 0, 0)),
                   pl.BlockSpec((None, M_CONV - 1, bcw), lambda b, c: (b, 0, 0)),
                   pl.BlockSpec((None, n_pairs, LANES, M_STATE), lambda b, c: (b, 0, 0, 0))],
        input_output_aliases={4 + len(small): 0},
        out_shape=[jax.ShapeDtypeStruct(y_init.shape, BF16),
                   jax.ShapeDtypeStruct((nseq, M_CONV - 1, mw), F32),
                   jax.ShapeDtypeStruct((nseq, M_CONV - 1, bcw), F32),
                   jax.ShapeDtypeStruct((nseq, n_pairs, LANES, M_STATE), F32)],
        scratch_shapes=[pltpu.VMEM((CHUNK + SUBLANES, mw), F32), pltpu.VMEM((CHUNK + SUBLANES, bcw), F32),
                        pltpu.VMEM((n_pairs, LANES, M_STATE), F32)],
        compiler_params=_cparams(2),
        name="mamba_prompt",
    )(proj, proj, proj, proj, *small, y_init)
    return y, jnp.concatenate([cox, cob], axis=-1), s


def _mamba_sample_prep_kernel(cs_ref, x_ref, bc_ref, sm_ref, cw_ref, cb_ref, dtb_ref, alog_ref, exp_ref,
                              co_ref, act_ref, dte_ref, ee_ref):
    cd = cw_ref.shape[1]
    new = jnp.concatenate([x_ref[...], bc_ref[...]], axis=1)
    acc = cb_ref[...] + cw_ref[M_CONV - 1:M_CONV, :] * new
    for w in range(M_CONV - 1):
        acc = acc + cw_ref[w:w + 1, :] * cs_ref[:, w * cd:(w + 1) * cd]
    act_ref[...] = _silu(acc)
    for w in range(1, M_CONV - 1):
        co_ref[:, (w - 1) * cd:w * cd] = cs_ref[:, w * cd:(w + 1) * cd]
    co_ref[:, (M_CONV - 2) * cd:(M_CONV - 1) * cd] = new
    dt = _softplus(sm_ref[...] + dtb_ref[...])
    a = -jnp.exp(alog_ref[...])
    dte_ref[...] = _dot_exact_rhs(dt, exp_ref[...])
    ee_ref[...] = jnp.exp(_dot_exact_rhs(dt * a, exp_ref[...]))


def _mamba_sample_state_kernel(x_ref, b_ref, c_ref, dte_ref, ee_ref, s_ref, *out_refs, nb, layer, first):
    y_ref, so_ref = out_refs[-2:]
    xdt_t = _columns(x_ref[...] * dte_ref[...])
    e_t = _columns(ee_ref[...])
    bv = b_ref[...]
    cb = c_ref[...].astype(BF16)
    rowid = lax.broadcasted_iota(jnp.int32, (nb, LANES), 0)
    y = jnp.zeros((nb, LANES), F32)
    for b in range(nb):
        s_new = e_t[:, b:b + 1] * s_ref[b] + xdt_t[:, b:b + 1] * bv[b:b + 1, :]
        _store_state(so_ref, layer, b, s_new, first)
        y = jnp.where(rowid == b, _dot_nt(cb, s_new.astype(BF16)), y)
    y_ref[...] = y


def _mamba_sample_post_kernel(y_ref, x_ref, z_ref, dsk_ref, nw_ref, _, o_ref):
    o_ref[...] = _mamba_post(y_ref[...], x_ref[...], z_ref[...], dsk_ref[...], nw_ref[...]).astype(BF16)


def _mamba_sample(proj, conv_state, ssm_state, prev, y_all, row0, offs, p, expand, layer):
    oz, ox, obc, osm = offs
    depth, ns = ssm_state.shape[:2]
    nb = _sample_batch_block(ns, M_STATE)
    mw = p["dskip"].shape[1]
    bcw = p["cwb"].shape[1]
    cd = mw + bcw
    n_pairs = mw // LANES
    per_group = n_pairs // M_GROUPS
    rb = row0 // ns
    first = prev is None
    cw = jnp.concatenate([p["cwx"], p["cwb"]], axis=1)
    cb = jnp.concatenate([p["cbx"], p["cbb"]], axis=1)

    def full1(a):
        nd = a.ndim
        return pl.BlockSpec(a.shape, lambda i: (0,) * nd)

    cs2 = conv_state.reshape(ns, (M_CONV - 1) * cd)
    small = [cw, cb, p["dtb"], p["alog"], expand]
    co, act, dte, ee = pl.pallas_call(
        _mamba_sample_prep_kernel,
        grid=(1,),
        in_specs=[full1(cs2),
                  pl.BlockSpec((ns, mw), lambda i: (rb, ox // mw)),
                  pl.BlockSpec((ns, bcw), lambda i: (rb, obc // bcw)),
                  pl.BlockSpec((ns, LANES), lambda i: (rb, osm // LANES))] + [full1(a) for a in small],
        out_specs=[pl.BlockSpec((ns, (M_CONV - 1) * cd), lambda i: (0, 0)),
                   pl.BlockSpec((ns, cd), lambda i: (0, 0)),
                   pl.BlockSpec((ns, mw), lambda i: (0, 0)),
                   pl.BlockSpec((ns, mw), lambda i: (0, 0))],
        out_shape=[jax.ShapeDtypeStruct((ns, (M_CONV - 1) * cd), F32),
                   jax.ShapeDtypeStruct((ns, cd), F32),
                   jax.ShapeDtypeStruct((ns, mw), F32),
                   jax.ShapeDtypeStruct((ns, mw), F32)],
        compiler_params=_cparams(1),
        name="mamba_sample_prep",
    )(cs2, proj, proj, proj, *small)

    bblk = mw // LANES
    cblk = bblk + M_GROUPS * M_STATE // LANES
    ispec, ospec = _state_specs(depth, layer, nb, M_STATE, first)
    in_specs = [pl.BlockSpec((nb, LANES), lambda j, i: (i, j)),
                pl.BlockSpec((nb, LANES), lambda j, i: (i, bblk + j // per_group)),
                pl.BlockSpec((nb, LANES), lambda j, i: (i, cblk + j // per_group)),
                pl.BlockSpec((nb, LANES), lambda j, i: (i, j)),
                pl.BlockSpec((nb, LANES), lambda j, i: (i, j)),
                ispec]
    args = [act, act, act, dte, ee, ssm_state]
    aliases = {}
    if not first:
        in_specs.append(pl.BlockSpec(memory_space=pl.ANY))
        args.append(prev)
        aliases = {len(args) - 1: 1}
    y, s_new = pl.pallas_call(
        functools.partial(_mamba_sample_state_kernel, nb=nb, layer=layer, first=first),
        grid=(n_pairs, ns // nb),
        in_specs=in_specs,
        out_specs=[pl.BlockSpec((nb, LANES), lambda j, i: (i, j)), ospec],
        out_shape=[jax.ShapeDtypeStruct((ns, mw), F32),
                   jax.ShapeDtypeStruct(ssm_state.shape, F32)],
        input_output_aliases=aliases,
        compiler_params=_cparams(2),
        name="mamba_sample_state",
    )(*args)

    ym = pl.pallas_call(
        _mamba_sample_post_kernel,
        grid=(1,),
        in_specs=[full1(y),
                  pl.BlockSpec((ns, mw), lambda i: (0, 0)),
                  pl.BlockSpec((ns, mw), lambda i: (rb, oz // mw)),
                  full1(p["dskip"]), full1(p["nw"]), pl.BlockSpec(memory_space=pl.ANY)],
        out_specs=pl.BlockSpec((ns, mw), lambda i: (rb, 0)),
        out_shape=jax.ShapeDtypeStruct(y_all.shape, BF16),
        input_output_aliases={5: 0},
        compiler_params=_cparams(1),
        name="mamba_sample_post",
    )(y, act, proj, p["dskip"], p["nw"], y_all)
    return ym, co.reshape(ns, M_CONV - 1, cd), s_new


def _pad_lanes(v, n=LANES):
    v = v.reshape(1, -1)
    return jnp.pad(v, ((0, 0), (0, n - v.shape[1])))


def kernel(x_prompt, x_sample, state_conv, state_ssm, state_hgrn, state_gla, ffn1_norm, ffn1_w_gate_up, ffn1_w_down, mix_norm, w_in, conv_w, conv_b, dt_bias, a_log, d_skip, mamba_norm, hgrn_lb_logits, hgrn_norm, gla_w_decay, gla_b_decay, gla_norm, w_branch_mamba, w_branch_hgrn, w_branch_gla, w_out, ffn2_norm, ffn2_w_gate_up, ffn2_w_down, final_norm):
    nseq, seq, d = x_prompt.shape
    ns = x_sample.shape[0]
    depth = w_in.shape[0]
    nchunks = seq // CHUNK
    n_prompt = nseq * seq
    mw = w_branch_mamba.shape[1]
    hw = w_branch_hgrn.shape[1]
    gw = w_branch_gla.shape[1]
    gk = gla_w_decay.shape[2]
    m_heads = dt_bias.shape[1]
    h_heads = state_hgrn.shape[2]
    g_heads = state_gla.shape[2]
    g_dv = state_gla.shape[4]
    bcw = 2 * M_GROUPS * M_STATE
    n_pairs = mw // LANES
    assert seq % CHUNK == 0 and n_prompt % ns == 0 and ns % 16 == 0
    assert m_heads <= G_RANK + m_heads <= LANES and gk // g_heads == LANES and hw // h_heads == LANES

    seg_w = {"z": mw, "xs": mw, "bc": bcw, "dt": m_heads, "hq": hw, "hf": hw, "hi": hw, "hg": hw,
             "gq": gk, "gk": gk, "gv": gw, "gg": gw, "ga": G_RANK, "gate": N_BRANCH * d}
    src_order = ("z", "xs", "bc", "dt", "hq", "hf", "hi", "hg", "gq", "gk", "gv", "gg", "ga", "gate")
    dst_order = ("z", "xs", "hq", "hf", "hi", "hg", "gv", "gg", "gq", "gk", "bc", "dt", "ga")
    src, off = {}, {}
    pos = 0
    for name in src_order:
        src[name] = pos
        pos += seg_w[name]
    pos = 0
    for name in dst_order:
        off[name] = pos
        pos += seg_w[name]
    n_used = pos
    n_main = -(-n_used // MAIN_TN) * MAIN_TN
    off["gate"] = -(-n_main // GATE_TN) * GATE_TN
    n_cols = off["gate"] + seg_w["gate"]
    assert seg_w["gate"] % GATE_TN == 0
    copies = tuple((src[name], off[name], seg_w[name]) for name in dst_order + ("gate",))
    oz, ox, obc, osm = off["z"], off["xs"], off["bc"], off["dt"]
    assert off["ga"] == osm + m_heads and osm % LANES == 0

    consts = _scan_constants(GLA_SUB)
    tril_np = np.tril(np.ones((CHUNK, CHUNK), np.float32))
    mconsts = (jnp.asarray(tril_np, BF16), jnp.asarray(tril_np))
    expand_np = np.zeros((LANES, mw), np.float32)
    for h in range(m_heads):
        expand_np[h, h * M_HEADDIM:(h + 1) * M_HEADDIM] = 1.0
    expand = jnp.asarray(expand_np, BF16)

    x, xw, ssq = _prep(x_prompt.reshape(n_prompt, d), x_sample.reshape(ns, d), ffn1_norm[0])
    t_all = n_prompt + ns
    w_in_t = jnp.swapaxes(w_in, 1, 2)
    ssm5 = state_ssm.reshape(depth, ns, n_pairs, LANES, M_STATE)

    pc, ps, ph, pg, sc = [], [], [], [], []
    ss = sh = sg = None
    for l in range(depth):
        w_perm = _pack_w_in(w_in_t, l, copies, (n_used, off["gate"]), n_cols)
        mp = {
            "cwx": conv_w[l][:, :mw], "cwb": conv_w[l][:, mw:],
            "cbx": conv_b[l][:mw].reshape(1, mw), "cbb": conv_b[l][mw:].reshape(1, bcw),
            "dtb": _pad_lanes(dt_bias[l]), "alog": _pad_lanes(a_log[l]),
            "dskip": jnp.repeat(d_skip[l], M_HEADDIM).reshape(1, mw),
            "nw": mamba_norm[l].reshape(1, mw),
        }
        wd = jnp.zeros((LANES, gk), F32).at[m_heads:m_heads + G_RANK].set(gla_w_decay[l])
        bd = gla_b_decay[l].reshape(1, gk)
        hnw = hgrn_norm[l].reshape(1, LANES)
        gnw = gla_norm[l].reshape(1, g_dv)

        act, cast = _ffn_up(xw, ssq, ffn1_w_gate_up, l, (ffn1_w_down,))
        w_dn, tn_dn = (cast[0], 512) if cast else (ffn1_w_down, 256)
        x, xw, ssq = _resid_matmul(act, w_dn, l, x, 0.5, tn_dn, "ffn_down", mix_norm[l])
        proj, _ = _in_proj(xw, ssq, w_perm, l, 0, n_main, gates=False, tn=MAIN_TN)
        gates, cast = _in_proj(xw, ssq, w_perm, l, off["gate"], seg_w["gate"], gates=True, tn=GATE_TN,
                               cast_ws=(w_branch_mamba, w_branch_hgrn, w_branch_gla, w_out))
        w_bm, w_bh, w_bg, w_o = cast if cast else (w_branch_mamba, w_branch_hgrn, w_branch_gla, w_out)

        moffs = (oz, ox, obc, osm)
        hoffs = (off["hq"], off["hf"], off["hi"], off["hg"])
        goffs = (off["gq"], off["gk"], off["gv"], off["gg"], osm)
        ym, c1, s1 = _mamba_prompt(proj, moffs, mp, mconsts, jnp.zeros((t_all, mw), BF16),
                                   nseq=nseq, nchunks=nchunks)
        yh, h1 = _gla_prompt(proj, hoffs, (hgrn_lb_logits,), hnw, consts, jnp.zeros((t_all, hw), BF16),
                             hgrn=True, nseq=nseq, nchunks=nchunks, heads=h_heads, dv=LANES, layer=l)
        yg, g1 = _gla_prompt(proj, goffs, (wd, bd), gnw, consts, jnp.zeros((t_all, gw), BF16),
                             hgrn=False, nseq=nseq, nchunks=nchunks, heads=g_heads, dv=g_dv, layer=l)
        ym, c2, ss = _mamba_sample(proj, state_conv[l], ssm5, ss, ym, n_prompt, moffs, mp, expand, l)
        yh, sh = _gla_sample(proj, state_hgrn, sh, yh, n_prompt, hoffs, (hgrn_lb_logits,), hnw,
                             hgrn=True, heads=h_heads, dv=LANES, layer=l)
        yg, sg = _gla_sample(proj, state_gla, sg, yg, n_prompt, goffs, (wd, bd), gnw,
                             hgrn=False, heads=g_heads, dv=g_dv, layer=l)
        merged = _merge(ym, yh, yg, w_bm, w_bh, w_bg, l, gates)
        x, xw, ssq = _resid_matmul(merged, w_o, l, x, 1.0, 512, "out_proj", ffn2_norm[l], WIDE_TM_CAP)

        act, cast = _ffn_up(xw, ssq, ffn2_w_gate_up, l, (ffn2_w_down,))
        w_dn, tn_dn = (cast[0], 512) if cast else (ffn2_w_down, 256)
        if l + 1 < depth:
            x, xw, ssq = _resid_matmul(act, w_dn, l, x, 0.5, tn_dn, "ffn_down", ffn1_norm[l + 1])
        else:
            x = _resid_matmul(act, w_dn, l, x, 0.5, tn_dn, "ffn_down")

        pc.append(c1)
        ps.append(s1.reshape(nseq, m_heads, M_HEADDIM, M_STATE))
        ph.append(h1)
        pg.append(g1)
        sc.append(c2)

    y_prompt = _final_norm(x, final_norm, 0, n_prompt, "final_norm_prompt").reshape(nseq, seq, d)
    y_sample = _final_norm(x, final_norm, n_prompt, ns, "final_norm_sample").reshape(ns, 1, d)
    return (y_prompt, y_sample, jnp.stack(pc), jnp.stack(ps), jnp.stack(ph), jnp.stack(pg),
            jnp.stack(sc), ss.reshape(state_ssm.shape), sh, sg)
```

```python
import functools

import jax
import jax.numpy as jnp
import numpy as np
from jax import lax
from jax.experimental import pallas as pl
from jax.experimental.pallas import tpu as pltpu

F32 = jnp.float32
BF16 = jnp.bfloat16
EPS = 1e-6
F32_TINY = float(np.finfo(np.float32).tiny)

CHUNK = 256
LANES = 128
SUBLANES = 8
M_HEADDIM = 64
M_STATE = 128
M_GROUPS = 2
M_CONV = 4
G_RANK = 16
G_NORMALIZER = 16.0
N_BRANCH = 3
GLA_SUB = 128
VMEM_LIMIT = 56 * 1024 * 1024
STATE_BLOCK_BYTES = 4 * 1024 * 1024
WIDE_TM_CAP = 2100
MAIN_TN = 1024


def _cparams(n_axes):
    return pltpu.CompilerParams(dimension_semantics=("arbitrary",) * n_axes,
                                vmem_limit_bytes=VMEM_LIMIT)


def _dot(a, b):
    return jnp.dot(a, b, preferred_element_type=F32)


def _dot_nt(a, b):
    return lax.dot_general(a, b, (((1,), (1,)), ((), ())), preferred_element_type=F32)


def _dot_tn(a, b):
    return lax.dot_general(a, b, (((0,), (0,)), ((), ())), preferred_element_type=F32)


def _split3(x):
    hi = x.astype(BF16)
    r = x - hi.astype(F32)
    mid = r.astype(BF16)
    lo = (r - mid.astype(F32)).astype(BF16)
    return hi, mid, lo


def _dot_exact_lhs(p_bf16, x):
    hi, mid, lo = _split3(x)
    return _dot(p_bf16, hi) + _dot(p_bf16, mid) + _dot(p_bf16, lo)


def _dot_exact_rhs(x, p_bf16):
    hi, mid, lo = _split3(x)
    return _dot(hi, p_bf16) + _dot(mid, p_bf16) + _dot(lo, p_bf16)


def _dot_f32(a, b):
    ah, am, _ = _split3(a)
    bh, bm, _ = _split3(b)
    return _dot(ah, bh) + _dot(ah, bm) + _dot(am, bh)


def _sigmoid(x):
    return jax.nn.sigmoid(x)


def _silu(x):
    return x * _sigmoid(x)


def _softplus(x):
    return jnp.maximum(x, 0.0) + jnp.log1p(jnp.exp(-jnp.abs(x)))


def _log_sigmoid(x):
    return jnp.minimum(x, 0.0) - jnp.log(1.0 + jnp.exp(-jnp.abs(x)))


def _rms(x, w):
    ms = jnp.mean(x * x, axis=-1, keepdims=True)
    return x * lax.rsqrt(ms + EPS) * w


def _pick_tm(t, cap=1100):
    best = 16
    for tm in range(16, min(t, cap) + 1, 16):
        if t % tm == 0:
            best = tm
    return best


def _row_scale(ssq_ref, d):
    return lax.rsqrt(ssq_ref[:, 0:1] * (1.0 / d) + EPS)


def _emit_normed(xn, nw_ref, xw_ref, ssq_ref, accumulate):
    xw_ref[...] = (xn * nw_ref[...]).astype(BF16)
    part = jnp.broadcast_to(jnp.sum(xn * xn, axis=1, keepdims=True), ssq_ref.shape)
    ssq_ref[...] = ssq_ref[...] + part if accumulate else part


def _prep_kernel(xp_ref, xs_ref, nw_ref, x_ref, xw_ref, ssq_ref, *, n_prompt_blocks):
    x = jnp.where(pl.program_id(0) < n_prompt_blocks, xp_ref[...], xs_ref[...])
    x_ref[...] = x
    _emit_normed(x, nw_ref, xw_ref, ssq_ref, accumulate=False)


def _prep(xp, xs, nw):
    n_prompt, d = xp.shape
    rb = xs.shape[0]
    assert n_prompt % rb == 0
    npb = n_prompt // rb
    t = n_prompt + rb
    row = pl.BlockSpec((rb, d), lambda i: (i, 0))
    return pl.pallas_call(
        functools.partial(_prep_kernel, n_prompt_blocks=npb),
        grid=(npb + 1,),
        in_specs=[pl.BlockSpec((rb, d), lambda i: (jnp.minimum(i, npb - 1), 0)),
                  pl.BlockSpec((rb, d), lambda i: (0, 0)),
                  pl.BlockSpec((1, d), lambda i: (0, 0))],
        out_specs=[row, row, pl.BlockSpec((rb, LANES), lambda i: (i, 0))],
        out_shape=[jax.ShapeDtypeStruct((t, d), F32), jax.ShapeDtypeStruct((t, d), BF16),
                   jax.ShapeDtypeStruct((t, LANES), F32)],
        compiler_params=_cparams(1),
        name="prep",
    )(xp, xs, nw.reshape(1, d))


def _row_parts(tm):
    half = -(-(tm // 2) // 16) * 16
    return ((0, half), (half, tm - half)) if 0 < half < tm else ((0, tm),)


def _side_casts(ws, layer, n_i, nj):
    steps = n_i * nj
    ins, outs, shapes = [], [], []
    for w in ws:
        _, k, d = w.shape
        rows = k // steps
        if rows * steps != k or rows % 16:
            return None
        ins.append(pl.BlockSpec((None, rows, d), lambda i, j: (layer, i * nj + j, 0)))
        outs.append(pl.BlockSpec((rows, d), lambda i, j: (i * nj + j, 0)))
        shapes.append(jax.ShapeDtypeStruct((k, d), BF16))
    return ins, outs, shapes


def _do_side_casts(refs):
    n = len(refs) // 2
    for src, dst in zip(refs[:n], refs[n:]):
        dst[...] = src[...].astype(BF16)


def _ffn_up_kernel(xw_ref, ssq_ref, wg_ref, wu_ref, *rest, parts, n_cast):
    o_ref = rest[n_cast]
    _do_side_casts(rest[:n_cast] + rest[n_cast + 1:])
    wg = wg_ref[...].astype(BF16)
    wu = wu_ref[...].astype(BF16)
    d = xw_ref.shape[1]
    for start, rows in parts:
        sl = pl.ds(start, rows)
        r = lax.rsqrt(ssq_ref[sl, 0:1] * (1.0 / d) + EPS)
        h = xw_ref[sl, :]
        g = r * _dot(h, wg)
        u = r * _dot(h, wu)
        o_ref[sl, :] = (_silu(g) * u).astype(BF16)


def _ffn_up(xw, ssq, w_gu, layer, cast_ws=(), tn=512):
    t, d = xw.shape
    dff = w_gu.shape[2] // 2
    tm = _pick_tm(t, WIDE_TM_CAP)
    nj = dff // tn
    side = _side_casts(cast_ws, layer, t // tm, nj) if cast_ws else None
    c_in, c_out, c_shape = side if side else ([], [], [])
    outs = pl.pallas_call(
        functools.partial(_ffn_up_kernel, parts=_row_parts(tm), n_cast=len(c_in)),
        grid=(t // tm, nj),
        in_specs=[pl.BlockSpec((tm, d), lambda i, j: (i, 0)),
                  pl.BlockSpec((tm, LANES), lambda i, j: (i, 0)),
                  pl.BlockSpec((None, d, tn), lambda i, j: (layer, 0, j)),
                  pl.BlockSpec((None, d, tn), lambda i, j: (layer, 0, j + nj))] + c_in,
        out_specs=[pl.BlockSpec((tm, tn), lambda i, j: (i, j))] + c_out,
        out_shape=[jax.ShapeDtypeStruct((t, dff), BF16)] + c_shape,
        compiler_params=_cparams(2),
        name="ffn_up",
    )(xw, ssq, w_gu, w_gu, *(cast_ws if side else ()))
    return outs[0], (tuple(outs[1:]) if side else None)


def _resid_matmul_kernel(a_ref, w_ref, x_ref, *rest, scale, emit, parts):
    if emit:
        nw_ref, o_ref, xw_ref, ssq_ref = rest

        @pl.when(pl.program_id(1) == 0)
        def _():
            ssq_ref[...] = jnp.zeros_like(ssq_ref)
    else:
        (o_ref,) = rest
    wb = w_ref[...].astype(BF16)
    for start, rows in parts:
        sl = pl.ds(start, rows)
        xn = x_ref[sl, :] + scale * _dot(a_ref[sl, :], wb)
        if emit:
            _emit_normed(xn, nw_ref, xw_ref.at[sl, :], ssq_ref.at[sl, :], accumulate=True)
        o_ref[sl, :] = xn


def _wspec(w, layer, tn):
    if w.ndim == 3:
        return pl.BlockSpec((None, w.shape[1], tn), lambda i, j: (layer, 0, j))
    return pl.BlockSpec((w.shape[0], tn), lambda i, j: (0, j))


def _resid_matmul(a, w, layer, x, scale, tn, name, next_nw=None, tm_cap=1100):
    t, k = a.shape
    d = w.shape[-1]
    tm = _pick_tm(t, tm_cap)
    emit = next_nw is not None
    tile = pl.BlockSpec((tm, tn), lambda i, j: (i, j))
    in_specs = [pl.BlockSpec((tm, k), lambda i, j: (i, 0)), _wspec(w, layer, tn), tile]
    args = [a, w, x]
    out_specs = [tile]
    out_shape = [jax.ShapeDtypeStruct((t, d), F32)]
    if emit:
        in_specs.append(pl.BlockSpec((1, tn), lambda i, j: (0, j)))
        args.append(next_nw.reshape(1, d))
        out_specs += [tile, pl.BlockSpec((tm, LANES), lambda i, j: (i, 0))]
        out_shape += [jax.ShapeDtypeStruct((t, d), BF16), jax.ShapeDtypeStruct((t, LANES), F32)]
    outs = pl.pallas_call(
        functools.partial(_resid_matmul_kernel, scale=scale, emit=emit,
                          parts=_row_parts(tm) if k <= d else ((0, tm),)),
        grid=(t // tm, d // tn),
        in_specs=in_specs,
        out_specs=out_specs,
        out_shape=out_shape,
        compiler_params=_cparams(2),
        name=name,
    )(*args)
    return outs if emit else outs[0]


def _pack_w_in_kernel(w_ref, o_ref, *, copies, zero):
    for src, dst, width in copies:
        if width % LANES == 0:
            o_ref[:, dst:dst + width] = w_ref[src:src + width, :].T.astype(BF16)
        else:
            base = src // LANES * LANES
            lo = src - base
            assert dst % LANES == lo and lo + width <= LANES
            t = w_ref[base:base + LANES, :].T
            o_ref[:, dst:dst + width] = t[:, lo:lo + width].astype(BF16)
    start, stop = zero
    if stop > start:
        o_ref[:, start:stop] = jnp.zeros((o_ref.shape[0], stop - start), BF16)


def _pack_w_in(w_t, layer, copies, zero, n_cols, tk=128):
    _, n_src, d = w_t.shape
    return pl.pallas_call(
        functools.partial(_pack_w_in_kernel, copies=copies, zero=zero),
        grid=(d // tk,),
        in_specs=[pl.BlockSpec((None, n_src, tk), lambda i: (layer, 0, i))],
        out_specs=pl.BlockSpec((tk, n_cols), lambda i: (i, 0)),
        out_shape=jax.ShapeDtypeStruct((d, n_cols), BF16),
        compiler_params=_cparams(1),
        name="pack_w_in",
    )(w_t)


def _in_proj_kernel(xw_ref, ssq_ref, w_ref, *rest, n_cast):
    o_ref = rest[n_cast]
    _do_side_casts(rest[:n_cast] + rest[n_cast + 1:])
    o_ref[...] = _row_scale(ssq_ref, xw_ref.shape[1]) * _dot(xw_ref[...], w_ref[...])


def _in_proj(xw, ssq, w, layer, cast_ws=(), tn=MAIN_TN):
    t, d = xw.shape
    n = w.shape[1]
    tm = _pick_tm(t, WIDE_TM_CAP)
    side = _side_casts(cast_ws, layer, t // tm, n // tn) if cast_ws else None
    c_in, c_out, c_shape = side if side else ([], [], [])
    outs = pl.pallas_call(
        functools.partial(_in_proj_kernel, n_cast=len(c_in)),
        grid=(t // tm, n // tn),
        in_specs=[pl.BlockSpec((tm, d), lambda i, j: (i, 0)),
                  pl.BlockSpec((tm, LANES), lambda i, j: (i, 0)),
                  pl.BlockSpec((d, tn), lambda i, j: (0, j))] + c_in,
        out_specs=[pl.BlockSpec((tm, tn), lambda i, j: (i, j))] + c_out,
        out_shape=[jax.ShapeDtypeStruct((t, n), F32)] + c_shape,
        compiler_params=_cparams(2),
        name="in_proj",
    )(xw, ssq, w, *(cast_ws if side else ()))
    return outs[0], (tuple(outs[1:]) if side else None)


def _merge_kernel(ym_ref, yh_ref, yg_ref, wm_ref, wh_ref, wg_ref, g0_ref, g1_ref, g2_ref, o_ref, *, parts):
    wm = wm_ref[...].astype(BF16)
    wh = wh_ref[...].astype(BF16)
    wg = wg_ref[...].astype(BF16)
    for start, rows in parts:
        sl = pl.ds(start, rows)
        acc = _sigmoid(g0_ref[sl, :]) * _dot(ym_ref[sl, :], wm)
        acc = acc + _sigmoid(g1_ref[sl, :]) * _dot(yh_ref[sl, :], wh)
        acc = acc + _sigmoid(g2_ref[sl, :]) * _dot(yg_ref[sl, :], wg)
        o_ref[sl, :] = acc.astype(BF16)


def _merge(ym, yh, yg, wm, wh, wg, layer, proj, gate_off, tn=512):
    t, k = ym.shape
    d = wm.shape[-1]
    tm = _pick_tm(t)
    assert gate_off % tn == 0
    gb = gate_off // tn
    nb = d // tn
    yspec = pl.BlockSpec((tm, k), lambda i, j: (i, 0))
    wspec = _wspec(wm, layer, tn)

    def gspec(b):
        return pl.BlockSpec((tm, tn), lambda i, j: (i, gb + b * nb + j))

    return pl.pallas_call(
        functools.partial(_merge_kernel, parts=_row_parts(tm)),
        grid=(t // tm, nb),
        in_specs=[yspec, yspec, yspec, wspec, wspec, wspec, gspec(0), gspec(1), gspec(2)],
        out_specs=pl.BlockSpec((tm, tn), lambda i, j: (i, j)),
        out_shape=jax.ShapeDtypeStruct((t, d), BF16),
        compiler_params=_cparams(2),
        name="merge",
    )(ym, yh, yg, wm, wh, wg, proj, proj, proj)


def _final_norm_kernel(x_ref, nw_ref, o_ref):
    o_ref[...] = _rms(x_ref[...], nw_ref[...])


def _final_norm(x, nw, row0, rows, name):
    d = x.shape[1]
    tm = _pick_tm(rows)
    assert row0 % tm == 0
    return pl.pallas_call(
        _final_norm_kernel,
        grid=(rows // tm,),
        in_specs=[pl.BlockSpec((tm, d), lambda i: (row0 // tm + i, 0)),
                  pl.BlockSpec((1, d), lambda i: (0, 0))],
        out_specs=pl.BlockSpec((tm, d), lambda i: (i, 0)),
        out_shape=jax.ShapeDtypeStruct((rows, d), F32),
        compiler_params=_cparams(1),
        name=name,
    )(x, nw.reshape(1, d))


def _scan_constants(c):
    N_LEVELS = int(np.log2(c))
    assert 1 << N_LEVELS == c and N_LEVELS >= 3
    pm = np.zeros((2, c, c), np.float32)
    sg = np.zeros((N_LEVELS - 1, c, LANES), np.float32)
    mk = np.zeros((N_LEVELS + 1, c, c), np.float32)
    r = np.arange(c)
    for lvl in range(N_LEVELS):
        h = c >> (lvl + 1)
        for t in range(c):
            blk, pos = divmod(t, 2 * h)
            ridx = blk * 2 * h + h - 1
            upper = pos >= h
            if upper:
                mk[lvl, t] = ((r // (2 * h)) == blk) & ((r % (2 * h)) < h)
            if h >= 4:
                sg[lvl, t] = 1.0 if upper else -1.0
            elif h == 2:
                pm[0, t] = ((r > ridx) & (r <= t)) if upper else ((r > t) & (r <= ridx))
            else:
                sg[N_LEVELS - 2, t] = 1.0 if upper else 0.0
    pm[1] = r[None, :] <= r[:, None]
    mk[N_LEVELS] = np.eye(c)
    return jnp.asarray(pm.reshape(2 * c, c), BF16), jnp.asarray(sg), jnp.asarray(mk)


def _lower_bound(logits, layer):
    m = jnp.max(logits, axis=0, keepdims=True)
    e = jnp.exp(logits - m)
    sm = e / jnp.sum(e, axis=0, keepdims=True)
    lb = jnp.zeros_like(m)
    for i in range(1, layer + 1):
        lb = lb + sm[i:i + 1, :]
    return lb


def _gla_inputs(refs, hgrn, layer):
    if hgrn:
        q_ref, f_ref, i_ref, lbl_ref = refs
        lb = _lower_bound(lbl_ref[...], layer)
        q = _silu(q_ref[...]) * (LANES ** -0.5)
        zf = f_ref[...]
        a = (1.0 - lb) * _sigmoid(zf)
        f = lb + a
        gl = jnp.log(jnp.maximum(f, F32_TINY))
        k = (1.0 - lb) - a
        v = i_ref[...]
    else:
        q_ref, k_ref, v_ref, a_ref, wd_ref, bd_ref = refs
        q = q_ref[...] * (LANES ** -0.5)
        k = k_ref[...]
        v = v_ref[...]
        gl = _log_sigmoid(_dot_f32(a_ref[...], wd_ref[...]) + bd_ref[...]) / G_NORMALIZER
    return q, k, v, gl


def _level_factor(lvl, gl, cum, z_h2, sg_ref):
    c = gl.shape[0]
    n_levels = sg_ref.shape[0] + 1
    h = c >> (lvl + 1)
    if h >= 4:
        c3 = cum.reshape(c // (2 * h), 2 * h, LANES)
        d = (c3 - c3[:, h - 1:h, :]).reshape(c, LANES)
        return jnp.exp(sg_ref[lvl] * d)
    if h == 2:
        return jnp.exp(z_h2)
    return jnp.exp(sg_ref[n_levels - 2] * gl)


def _gla_prompt_kernel(*refs, hgrn, layer, hp, dv):
    n_in = 4 if hgrn else 6
    in_refs = refs[:n_in]
    g_ref, nw_ref, pm_ref, sg_ref, mk_ref, _, y_ref, so_ref, s_scr = refs[n_in:]
    c = pl.program_id(2)
    sub = mk_ref.shape[1]
    n_levels = mk_ref.shape[0] - 1

    @pl.when(c == 0)
    def _():
        s_scr[...] = jnp.zeros_like(s_scr)

    q_all, k_all, v_all, gl_all = _gla_inputs(in_refs, hgrn, layer)
    finals = []
    for u in range(hp):
        s = s_scr[u]
        for r0 in range(0, CHUNK, sub):
            rows = pl.ds(r0, sub)
            q, k, gl = (a[r0:r0 + sub, u * LANES:(u + 1) * LANES] for a in (q_all, k_all, gl_all))
            v = v_all[r0:r0 + sub, u * dv:(u + 1) * dv]
            zz = _dot_exact_lhs(pm_ref[...], gl)
            z_h2, cum = zz[:sub], zz[sub:]
            qb, kb = q.astype(BF16), k.astype(BF16)
            att = mk_ref[n_levels] * _dot_nt(qb, kb)
            for lvl in range(n_levels):
                eb = _level_factor(lvl, gl, cum, z_h2, sg_ref).astype(BF16)
                att = att + mk_ref[lvl] * _dot_nt(qb * eb, kb * eb)
            e_cum = jnp.exp(cum)
            e_tail = jnp.exp(cum[sub - 1:sub, :] - cum)
            vb = v.astype(BF16)
            o = _dot(att.astype(BF16), vb) + _dot((q * e_cum).astype(BF16), s.astype(BF16))
            e_last = e_cum.T[:, sub - 1:sub]
            s = e_last * s + _dot_tn((k * e_tail).astype(BF16), vb)
            gate = g_ref[rows, u * dv:(u + 1) * dv]
            y_ref[rows, u * dv:(u + 1) * dv] = (_rms(o, nw_ref[...]) * _silu(gate)).astype(BF16)
        s_scr[u] = s
        finals.append(s)

    @pl.when(c == pl.num_programs(2) - 1)
    def _():
        for u in range(hp):
            so_ref[u] = finals[u]


def _gla_prompt(proj, offs, extra, nw, consts, y_init, *, hgrn, nseq, nchunks, heads, dv, layer, hp=4):
    pm, sg, mk = consts
    assert heads % hp == 0

    def col(off, w):
        assert off % (hp * w) == 0
        return pl.BlockSpec((CHUNK, hp * w), lambda h, b, c: (b * nchunks + c, off // (hp * w) + h))

    def full(a):
        nd = a.ndim
        return pl.BlockSpec(a.shape, lambda h, b, c: (0,) * nd)

    if hgrn:
        oq, of, oi, og = offs
        (lbl,) = extra
        in_specs = [col(oq, LANES), col(of, LANES), col(oi, dv),
                    pl.BlockSpec((lbl.shape[0], hp * LANES), lambda h, b, c: (0, h))]
        args = [proj, proj, proj, lbl]
    else:
        oq, ok, ov, og, oa = offs
        wd, bd = extra
        in_specs = [col(oq, LANES), col(ok, LANES), col(ov, dv),
                    pl.BlockSpec((CHUNK, LANES), lambda h, b, c: (b * nchunks + c, oa // LANES)),
                    pl.BlockSpec((LANES, hp * LANES), lambda h, b, c: (0, h)),
                    pl.BlockSpec((1, hp * LANES), lambda h, b, c: (0, h))]
        args = [proj, proj, proj, proj, wd, bd]
    in_specs += [col(og, dv), full(nw), full(pm), full(sg), full(mk), pl.BlockSpec(memory_space=pl.ANY)]
    args += [proj, nw, pm, sg, mk, y_init]
    y, s = pl.pallas_call(
        functools.partial(_gla_prompt_kernel, hgrn=hgrn, layer=layer, hp=hp, dv=dv),
        grid=(heads // hp, nseq, nchunks),
        in_specs=in_specs,
        out_specs=[pl.BlockSpec((CHUNK, hp * dv), lambda h, b, c: (b * nchunks + c, h)),
                   pl.BlockSpec((None, hp, LANES, dv), lambda h, b, c: (b, h, 0, 0))],
        out_shape=[jax.ShapeDtypeStruct(y_init.shape, BF16),
                   jax.ShapeDtypeStruct((nseq, heads, LANES, dv), F32)],
        input_output_aliases={len(args) - 1: 0},
        scratch_shapes=[pltpu.VMEM((hp, LANES, dv), F32)],
        compiler_params=_cparams(3),
        name="hgrn_prompt" if hgrn else "gla_prompt",
    )(*args)
    return y, s


def _columns(x):
    nb = x.shape[0]
    if nb < LANES:
        x = jnp.concatenate([x, jnp.zeros((LANES - nb, x.shape[1]), x.dtype)], axis=0)
    return x.T


def _store_state(so_ref, layer, b, s_new, first):
    if first:
        for l in range(so_ref.shape[0]):
            so_ref[l, b] = s_new if l == layer else jnp.zeros_like(s_new)
    else:
        so_ref[b] = s_new


def _gla_sample_kernel(*refs, hgrn, nb, layer, first):
    n_in = 4 if hgrn else 6
    in_refs = refs[:n_in]
    g_ref, nw_ref, s_ref = refs[n_in:n_in + 3]
    y_ref, so_ref = refs[-2:]
    q, k, v, gl = _gla_inputs(in_refs, hgrn, layer)
    e_t = _columns(jnp.exp(gl))
    k_t = _columns(k)
    qb = q.astype(BF16)
    rowid = lax.broadcasted_iota(jnp.int32, v.shape, 0)
    o = jnp.zeros(v.shape, F32)
    for b in range(nb):
        s_new = e_t[:, b:b + 1] * s_ref[b] + k_t[:, b:b + 1] * v[b:b + 1, :]
        _store_state(so_ref, layer, b, s_new, first)
        o = jnp.where(rowid == b, _dot(qb, s_new.astype(BF16)), o)
    y_ref[...] = (_rms(o, nw_ref[...]) * _silu(g_ref[...])).astype(BF16)


def _state_specs(depth, layer, nb, dv, first):
    ispec = pl.BlockSpec((None, nb, None, LANES, dv), lambda h, i: (layer, i, h, 0, 0))
    if first:
        ospec = pl.BlockSpec((depth, nb, None, LANES, dv), lambda h, i: (0, i, h, 0, 0))
    else:
        ospec = ispec
    return ispec, ospec


def _sample_batch_block(ns, dv):
    nb = min(ns, STATE_BLOCK_BYTES // (LANES * dv * 4))
    assert ns % nb == 0
    return nb


def _gla_sample(proj, state, prev, y_all, row0, offs, extra, nw, *, hgrn, heads, dv, layer):
    depth, ns = state.shape[:2]
    nb = _sample_batch_block(ns, dv)
    rb = row0 // nb
    first = prev is None

    def col(off, w):
        return pl.BlockSpec((nb, w), lambda h, i: (rb + i, off // w + h))

    if hgrn:
        oq, of, oi, og = offs
        (lbl,) = extra
        in_specs = [col(oq, LANES), col(of, LANES), col(oi, dv),
                    pl.BlockSpec((lbl.shape[0], LANES), lambda h, i: (0, h))]
        args = [proj, proj, proj, lbl]
    else:
        oq, ok, ov, og, oa = offs
        wd, bd = extra
        in_specs = [col(oq, LANES), col(ok, LANES), col(ov, dv),
                    pl.BlockSpec((nb, LANES), lambda h, i: (rb + i, oa // LANES)),
                    pl.BlockSpec((LANES, LANES), lambda h, i: (0, h)),
                    pl.BlockSpec((1, LANES), lambda h, i: (0, h))]
        args = [proj, proj, proj, proj, wd, bd]
    ispec, ospec = _state_specs(depth, layer, nb, dv, first)
    in_specs += [col(og, dv), pl.BlockSpec(nw.shape, lambda h, i: (0, 0)), ispec,
                 pl.BlockSpec(memory_space=pl.ANY)]
    args += [proj, nw, state, y_all]
    aliases = {len(args) - 1: 0}
    if not first:
        in_specs.append(pl.BlockSpec(memory_space=pl.ANY))
        args.append(prev)
        aliases[len(args) - 1] = 1
    y, s = pl.pallas_call(
        functools.partial(_gla_sample_kernel, hgrn=hgrn, nb=nb, layer=layer, first=first),
        grid=(heads, ns // nb),
        in_specs=in_specs,
        out_specs=[pl.BlockSpec((nb, dv), lambda h, i: (rb + i, h)), ospec],
        out_shape=[jax.ShapeDtypeStruct(y_all.shape, BF16),
                   jax.ShapeDtypeStruct(state.shape, F32)],
        input_output_aliases=aliases,
        compiler_params=_cparams(2),
        name="hgrn_sample" if hgrn else "gla_sample",
    )(*args)
    return y, s


def _mamba_post(y, xs, z, dskip, nw):
    y = (y + dskip * xs) * _silu(z)
    gs = y.shape[1] // M_GROUPS
    outs = [_rms(y[:, g * gs:(g + 1) * gs], nw[:, g * gs:(g + 1) * gs]) for g in range(M_GROUPS)]
    return jnp.concatenate(outs, axis=1)


def _mamba_prompt_kernel(z_ref, x_ref, bc_ref, sm_ref, cwx_ref, cwb_ref, cbx_ref, cbb_ref,
                         dtb_ref, alog_ref, dsk_ref, nw_ref, tril_ref, mask_ref, _,
                         y_ref, cox_ref, cob_ref, so_ref, ex_scr, eb_scr, s_scr):
    c = pl.program_id(1)
    nc = pl.num_programs(1)
    tail = SUBLANES

    @pl.when(c == 0)
    def _():
        ex_scr[0:tail, :] = jnp.zeros((tail, ex_scr.shape[1]), F32)
        eb_scr[0:tail, :] = jnp.zeros((tail, eb_scr.shape[1]), F32)
        s_scr[...] = jnp.zeros_like(s_scr)

    ex_scr[tail:tail + CHUNK, :] = x_ref[...]
    eb_scr[tail:tail + CHUNK, :] = bc_ref[...]

    def conv(scr, cw_ref, cb_ref):
        xe = scr[...]
        acc = cw_ref[0:1, :] * xe
        for w in range(1, M_CONV):
            acc = pltpu.roll(acc, 1, axis=0) + cw_ref[w:w + 1, :] * xe
        return _silu(acc[tail:, :] + cb_ref[...])

    xs = conv(ex_scr, cwx_ref, cbx_ref)
    bcm = conv(eb_scr, cwb_ref, cbb_ref)

    ex_scr[0:tail, :] = ex_scr[CHUNK:CHUNK + tail, :]
    eb_scr[0:tail, :] = eb_scr[CHUNK:CHUNK + tail, :]

    dt = _softplus(sm_ref[...] + dtb_ref[...])
    a = -jnp.exp(alog_ref[...])
    cum = _dot_exact_lhs(tril_ref[...], dt * a)
    cum_t = cum.T
    cl = cum[CHUNK - 1:CHUNK, :]
    mask = mask_ref[...] > 0.5
    gw = M_GROUPS * M_STATE
    lo = lax.broadcasted_iota(jnp.int32, (CHUNK, LANES), 1) < M_HEADDIM
    lo_r = lax.broadcasted_iota(jnp.int32, (LANES, LANES), 0) < M_HEADDIM
    n_pairs = xs.shape[1] // LANES
    per_group = n_pairs // M_GROUPS
    ys = []
    for j in range(n_pairs):
        g = j // per_group
        h0, h1 = 2 * j, 2 * j + 1
        if j % per_group == 0:
            bg = bcm[:, g * M_STATE:(g + 1) * M_STATE].astype(BF16)
            cg = bcm[:, gw + g * M_STATE:gw + (g + 1) * M_STATE].astype(BF16)
            cb = jnp.where(mask, _dot_nt(cg, bg), 0.0)

        def dec(h):
            return jnp.exp(jnp.minimum(cum[:, h:h + 1] - cum_t[h:h + 1, :], 0.0))

        x2 = xs[:, j * LANES:(j + 1) * LANES]
        dt2 = jnp.where(lo, dt[:, h0:h0 + 1], dt[:, h1:h1 + 1])
        xdt = (x2 * dt2).astype(BF16)
        y_in = jnp.where(lo, _dot((cb * dec(h0)).astype(BF16), xdt),
                         _dot((cb * dec(h1)).astype(BF16), xdt))
        ec2 = jnp.where(lo, jnp.exp(cum[:, h0:h0 + 1]), jnp.exp(cum[:, h1:h1 + 1]))
        s = s_scr[j]
        ys.append(y_in + _dot_nt(cg, s.astype(BF16)) * ec2)
        w2 = dt2 * jnp.where(lo, jnp.exp(cl[:, h0:h0 + 1] - cum[:, h0:h0 + 1]),
                             jnp.exp(cl[:, h1:h1 + 1] - cum[:, h1:h1 + 1]))
        el2 = jnp.where(lo_r, jnp.exp(cl[:, h0:h0 + 1]), jnp.exp(cl[:, h1:h1 + 1]))
        s_scr[j] = el2 * s + _dot_tn((x2 * w2).astype(BF16), bg)
    y = jnp.concatenate(ys, axis=1)
    y_ref[...] = _mamba_post(y, xs, z_ref[...], dsk_ref[...], nw_ref[...]).astype(BF16)

    @pl.when(c == nc - 1)
    def _():
        so_ref[...] = s_scr[...]
        cox_ref[...] = ex_scr[tail + CHUNK - (M_CONV - 1):tail + CHUNK, :]
        cob_ref[...] = eb_scr[tail + CHUNK - (M_CONV - 1):tail + CHUNK, :]


def _mamba_prompt(proj, offs, p, consts, y_init, *, nseq, nchunks):
    oz, ox, obc, osm = offs
    mw = p["dskip"].shape[1]
    bcw = p["cwb"].shape[1]
    n_pairs = mw // LANES

    def col(off, w):
        return pl.BlockSpec((CHUNK, w), lambda b, c: (b * nchunks + c, off // w))

    def full(a):
        nd = a.ndim
        return pl.BlockSpec(a.shape, lambda b, c: (0,) * nd)

    small = [p["cwx"], p["cwb"], p["cbx"], p["cbb"], p["dtb"], p["alog"], p["dskip"], p["nw"],
             consts[0], consts[1]]
    y, cox, cob, s = pl.pallas_call(
        _mamba_prompt_kernel,
        grid=(nseq, nchunks),
        in_specs=[col(oz, mw), col(ox, mw), col(obc, bcw), col(osm, LANES)] + [full(a) for a in small]
        + [pl.BlockSpec(memory_space=pl.ANY)],
        out_specs=[pl.BlockSpec((CHUNK, mw), lambda b, c: (b * nchunks + c, 0)),
                   pl.BlockSpec((None, M_CONV - 1, mw), lambda b, c: (b, 0, 0)),
                   pl.BlockSpec((None, M_CONV - 1, bcw), lambda b, c: (b, 0, 0)),
                   pl.BlockSpec((None, n_pairs, LANES, M_STATE), lambda b, c: (b, 0, 0, 0))],
        input_output_aliases={4 + len(small): 0},
        out_shape=[jax.ShapeDtypeStruct(y_init.shape, BF16),
                   jax.ShapeDtypeStruct((nseq, M_CONV - 1, mw), F32),
                   jax.ShapeDtypeStruct((nseq, M_CONV - 1, bcw), F32),
                   jax.ShapeDtypeStruct((nseq, n_pairs, LANES, M_STATE), F32)],
        scratch_shapes=[pltpu.VMEM((CHUNK + SUBLANES, mw), F32), pltpu.VMEM((CHUNK + SUBLANES, bcw), F32),
                        pltpu.VMEM((n_pairs, LANES, M_STATE), F32)],
        compiler_params=_cparams(2),
        name="mamba_prompt",
    )(proj, proj, proj, proj, *small, y_init)
    return y, jnp.concatenate([cox, cob], axis=-1), s


def _mamba_sample_prep_kernel(cs_ref, x_ref, bc_ref, sm_ref, cw_ref, cb_ref, dtb_ref, alog_ref, exp_ref,
                              co_ref, act_ref, dte_ref, ee_ref):
    cd = cw_ref.shape[1]
    new = jnp.concatenate([x_ref[...], bc_ref[...]], axis=1)
    acc = cb_ref[...] + cw_ref[M_CONV - 1:M_CONV, :] * new
    for w in range(M_CONV - 1):
        acc = acc + cw_ref[w:w + 1, :] * cs_ref[:, w * cd:(w + 1) * cd]
    act_ref[...] = _silu(acc)
    for w in range(1, M_CONV - 1):
        co_ref[:, (w - 1) * cd:w * cd] = cs_ref[:, w * cd:(w + 1) * cd]
    co_ref[:, (M_CONV - 2) * cd:(M_CONV - 1) * cd] = new
    dt = _softplus(sm_ref[...] + dtb_ref[...])
    a = -jnp.exp(alog_ref[...])
    dte_ref[...] = _dot_exact_rhs(dt, exp_ref[...])
    ee_ref[...] = jnp.exp(_dot_exact_rhs(dt * a, exp_ref[...]))


def _mamba_sample_state_kernel(x_ref, b_ref, c_ref, dte_ref, ee_ref, s_ref, *out_refs, nb, layer, first):
    y_ref, so_ref = out_refs[-2:]
    xdt_t = _columns(x_ref[...] * dte_ref[...])
    e_t = _columns(ee_ref[...])
    bv = b_ref[...]
    cb = c_ref[...].astype(BF16)
    rowid = lax.broadcasted_iota(jnp.int32, (nb, LANES), 0)
    y = jnp.zeros((nb, LANES), F32)
    for b in range(nb):
        s_new = e_t[:, b:b + 1] * s_ref[b] + xdt_t[:, b:b + 1] * bv[b:b + 1, :]
        _store_state(so_ref, layer, b, s_new, first)
        y = jnp.where(rowid == b, _dot_nt(cb, s_new.astype(BF16)), y)
    y_ref[...] = y


def _mamba_sample_post_kernel(y_ref, x_ref, z_ref, dsk_ref, nw_ref, _, o_ref):
    o_ref[...] = _mamba_post(y_ref[...], x_ref[...], z_ref[...], dsk_ref[...], nw_ref[...]).astype(BF16)


def _mamba_sample(proj, conv_state, ssm_state, prev, y_all, row0, offs, p, expand, layer):
    oz, ox, obc, osm = offs
    depth, ns = ssm_state.shape[:2]
    nb = _sample_batch_block(ns, M_STATE)
    mw = p["dskip"].shape[1]
    bcw = p["cwb"].shape[1]
    cd = mw + bcw
    n_pairs = mw // LANES
    per_group = n_pairs // M_GROUPS
    rb = row0 // ns
    first = prev is None
    cw = jnp.concatenate([p["cwx"], p["cwb"]], axis=1)
    cb = jnp.concatenate([p["cbx"], p["cbb"]], axis=1)

    def full1(a):
        nd = a.ndim
        return pl.BlockSpec(a.shape, lambda i: (0,) * nd)

    cs2 = conv_state.reshape(ns, (M_CONV - 1) * cd)
    small = [cw, cb, p["dtb"], p["alog"], expand]
    co, act, dte, ee = pl.pallas_call(
        _mamba_sample_prep_kernel,
        grid=(1,),
        in_specs=[full1(cs2),
                  pl.BlockSpec((ns, mw), lambda i: (rb, ox // mw)),
                  pl.BlockSpec((ns, bcw), lambda i: (rb, obc // bcw)),
                  pl.BlockSpec((ns, LANES), lambda i: (rb, osm // LANES))] + [full1(a) for a in small],
        out_specs=[pl.BlockSpec((ns, (M_CONV - 1) * cd), lambda i: (0, 0)),
                   pl.BlockSpec((ns, cd), lambda i: (0, 0)),
                   pl.BlockSpec((ns, mw), lambda i: (0, 0)),
                   pl.BlockSpec((ns, mw), lambda i: (0, 0))],
        out_shape=[jax.ShapeDtypeStruct((ns, (M_CONV - 1) * cd), F32),
                   jax.ShapeDtypeStruct((ns, cd), F32),
                   jax.ShapeDtypeStruct((ns, mw), F32),
                   jax.ShapeDtypeStruct((ns, mw), F32)],
        compiler_params=_cparams(1),
        name="mamba_sample_prep",
    )(cs2, proj, proj, proj, *small)

    bblk = mw // LANES
    cblk = bblk + M_GROUPS * M_STATE // LANES
    ispec, ospec = _state_specs(depth, layer, nb, M_STATE, first)
    in_specs = [pl.BlockSpec((nb, LANES), lambda j, i: (i, j)),
                pl.BlockSpec((nb, LANES), lambda j, i: (i, bblk + j // per_group)),
                pl.BlockSpec((nb, LANES), lambda j, i: (i, cblk + j // per_group)),
                pl.BlockSpec((nb, LANES), lambda j, i: (i, j)),
                pl.BlockSpec((nb, LANES), lambda j, i: (i, j)),
                ispec]
    args = [act, act, act, dte, ee, ssm_state]
    aliases = {}
    if not first:
        in_specs.append(pl.BlockSpec(memory_space=pl.ANY))
        args.append(prev)
        aliases = {len(args) - 1: 1}
    y, s_new = pl.pallas_call(
        functools.partial(_mamba_sample_state_kernel, nb=nb, layer=layer, first=first),
        grid=(n_pairs, ns // nb),
        in_specs=in_specs,
        out_specs=[pl.BlockSpec((nb, LANES), lambda j, i: (i, j)), ospec],
        out_shape=[jax.ShapeDtypeStruct((ns, mw), F32),
                   jax.ShapeDtypeStruct(ssm_state.shape, F32)],
        input_output_aliases=aliases,
        compiler_params=_cparams(2),
        name="mamba_sample_state",
    )(*args)

    ym = pl.pallas_call(
        _mamba_sample_post_kernel,
        grid=(1,),
        in_specs=[full1(y),
                  pl.BlockSpec((ns, mw), lambda i: (0, 0)),
                  pl.BlockSpec((ns, mw), lambda i: (rb, oz // mw)),
                  full1(p["dskip"]), full1(p["nw"]), pl.BlockSpec(memory_space=pl.ANY)],
        out_specs=pl.BlockSpec((ns, mw), lambda i: (rb, 0)),
        out_shape=jax.ShapeDtypeStruct(y_all.shape, BF16),
        input_output_aliases={5: 0},
        compiler_params=_cparams(1),
        name="mamba_sample_post",
    )(y, act, proj, p["dskip"], p["nw"], y_all)
    return ym, co.reshape(ns, M_CONV - 1, cd), s_new


def _pad_lanes(v, n=LANES):
    v = v.reshape(1, -1)
    return jnp.pad(v, ((0, 0), (0, n - v.shape[1])))


def kernel(x_prompt, x_sample, state_conv, state_ssm, state_hgrn, state_gla, ffn1_norm, ffn1_w_gate_up, ffn1_w_down, mix_norm, w_in, conv_w, conv_b, dt_bias, a_log, d_skip, mamba_norm, hgrn_lb_logits, hgrn_norm, gla_w_decay, gla_b_decay, gla_norm, w_branch_mamba, w_branch_hgrn, w_branch_gla, w_out, ffn2_norm, ffn2_w_gate_up, ffn2_w_down, final_norm):
    nseq, seq, d = x_prompt.shape
    ns = x_sample.shape[0]
    depth = w_in.shape[0]
    nchunks = seq // CHUNK
    n_prompt = nseq * seq
    mw = w_branch_mamba.shape[1]
    hw = w_branch_hgrn.shape[1]
    gw = w_branch_gla.shape[1]
    gk = gla_w_decay.shape[2]
    m_heads = dt_bias.shape[1]
    h_heads = state_hgrn.shape[2]
    g_heads = state_gla.shape[2]
    g_dv = state_gla.shape[4]
    bcw = 2 * M_GROUPS * M_STATE
    n_pairs = mw // LANES
    assert seq % CHUNK == 0 and n_prompt % ns == 0 and ns % 16 == 0
    assert m_heads <= G_RANK + m_heads <= LANES and gk // g_heads == LANES and hw // h_heads == LANES

    seg_w = {"z": mw, "xs": mw, "bc": bcw, "dt": m_heads, "hq": hw, "hf": hw, "hi": hw, "hg": hw,
             "gq": gk, "gk": gk, "gv": gw, "gg": gw, "ga": G_RANK, "gate": N_BRANCH * d}
    src_order = ("z", "xs", "bc", "dt", "hq", "hf", "hi", "hg", "gq", "gk", "gv", "gg", "ga", "gate")
    dst_order = ("z", "xs", "hq", "hf", "hi", "hg", "gv", "gg", "gq", "gk", "bc", "dt", "ga")
    src, off = {}, {}
    pos = 0
    for name in src_order:
        src[name] = pos
        pos += seg_w[name]
    pos = 0
    for name in dst_order:
        off[name] = pos
        pos += seg_w[name]
    n_used = pos
    off["gate"] = -(-n_used // MAIN_TN) * MAIN_TN
    n_cols = off["gate"] + seg_w["gate"]
    assert n_cols % MAIN_TN == 0
    copies = tuple((src[name], off[name], seg_w[name]) for name in dst_order + ("gate",))
    oz, ox, obc, osm = off["z"], off["xs"], off["bc"], off["dt"]
    assert off["ga"] == osm + m_heads and osm % LANES == 0

    consts = _scan_constants(GLA_SUB)
    tril_np = np.tril(np.ones((CHUNK, CHUNK), np.float32))
    mconsts = (jnp.asarray(tril_np, BF16), jnp.asarray(tril_np))
    expand_np = np.zeros((LANES, mw), np.float32)
    for h in range(m_heads):
        expand_np[h, h * M_HEADDIM:(h + 1) * M_HEADDIM] = 1.0
    expand = jnp.asarray(expand_np, BF16)

    x, xw, ssq = _prep(x_prompt.reshape(n_prompt, d), x_sample.reshape(ns, d), ffn1_norm[0])
    t_all = n_prompt + ns
    w_in_t = jnp.swapaxes(w_in, 1, 2)
    ssm5 = state_ssm.reshape(depth, ns, n_pairs, LANES, M_STATE)

    pc, ps, ph, pg, sc = [], [], [], [], []
    ss = sh = sg = None
    for l in range(depth):
        w_perm = _pack_w_in(w_in_t, l, copies, (n_used, off["gate"]), n_cols)
        mp = {
            "cwx": conv_w[l][:, :mw], "cwb": conv_w[l][:, mw:],
            "cbx": conv_b[l][:mw].reshape(1, mw), "cbb": conv_b[l][mw:].reshape(1, bcw),
            "dtb": _pad_lanes(dt_bias[l]), "alog": _pad_lanes(a_log[l]),
            "dskip": jnp.repeat(d_skip[l], M_HEADDIM).reshape(1, mw),
            "nw": mamba_norm[l].reshape(1, mw),
        }
        wd = jnp.zeros((LANES, gk), F32).at[m_heads:m_heads + G_RANK].set(gla_w_decay[l])
        bd = gla_b_decay[l].reshape(1, gk)
        hnw = hgrn_norm[l].reshape(1, LANES)
        gnw = gla_norm[l].reshape(1, g_dv)

        act, cast = _ffn_up(xw, ssq, ffn1_w_gate_up, l, (ffn1_w_down,))
        w_dn, tn_dn = (cast[0], 512) if cast else (ffn1_w_down, 256)
        x, xw, ssq = _resid_matmul(act, w_dn, l, x, 0.5, tn_dn, "ffn_down", mix_norm[l])
        proj, cast = _in_proj(xw, ssq, w_perm, l, (w_branch_mamba, w_branch_hgrn, w_branch_gla, w_out))
        w_bm, w_bh, w_bg, w_o = cast if cast else (w_branch_mamba, w_branch_hgrn, w_branch_gla, w_out)

        moffs = (oz, ox, obc, osm)
        hoffs = (off["hq"], off["hf"], off["hi"], off["hg"])
        goffs = (off["gq"], off["gk"], off["gv"], off["gg"], osm)
        ym, c1, s1 = _mamba_prompt(proj, moffs, mp, mconsts, jnp.zeros((t_all, mw), BF16),
                                   nseq=nseq, nchunks=nchunks)
        yh, h1 = _gla_prompt(proj, hoffs, (hgrn_lb_logits,), hnw, consts, jnp.zeros((t_all, hw), BF16),
                             hgrn=True, nseq=nseq, nchunks=nchunks, heads=h_heads, dv=LANES, layer=l,
                             hp=h_heads)
        yg, g1 = _gla_prompt(proj, goffs, (wd, bd), gnw, consts, jnp.zeros((t_all, gw), BF16),
                             hgrn=False, nseq=nseq, nchunks=nchunks, heads=g_heads, dv=g_dv, layer=l)
        ym, c2, ss = _mamba_sample(proj, state_conv[l], ssm5, ss, ym, n_prompt, moffs, mp, expand, l)
        yh, sh = _gla_sample(proj, state_hgrn, sh, yh, n_prompt, hoffs, (hgrn_lb_logits,), hnw,
                             hgrn=True, heads=h_heads, dv=LANES, layer=l)
        yg, sg = _gla_sample(proj, state_gla, sg, yg, n_prompt, goffs, (wd, bd), gnw,
                             hgrn=False, heads=g_heads, dv=g_dv, layer=l)
        merged = _merge(ym, yh, yg, w_bm, w_bh, w_bg, l, proj, off["gate"])
        x, xw, ssq = _resid_matmul(merged, w_o, l, x, 1.0, 512, "out_proj", ffn2_norm[l], WIDE_TM_CAP)

        act, cast = _ffn_up(xw, ssq, ffn2_w_gate_up, l, (ffn2_w_down,))
        w_dn, tn_dn = (cast[0], 512) if cast else (ffn2_w_down, 256)
        if l + 1 < depth:
            x, xw, ssq = _resid_matmul(act, w_dn, l, x, 0.5, tn_dn, "ffn_down", ffn1_norm[l + 1])
        else:
            x = _resid_matmul(act, w_dn, l, x, 0.5, tn_dn, "ffn_down")

        pc.append(c1)
        ps.append(s1.reshape(nseq, m_heads, M_HEADDIM, M_STATE))
        ph.append(h1)
        pg.append(g1)
        sc.append(c2)

    y_prompt = _final_norm(x, final_norm, 0, n_prompt, "final_norm_prompt").reshape(nseq, seq, d)
    y_sample = _final_norm(x, final_norm, n_prompt, ns, "final_norm_sample").reshape(ns, 1, d)
    return (y_prompt, y_sample, jnp.stack(pc), jnp.stack(ps), jnp.stack(ph), jnp.stack(pg),
            jnp.stack(sc), ss.reshape(state_ssm.shape), sh, sg)
```

```python
import functools

import jax
import jax.numpy as jnp
import numpy as np
from jax import lax
from jax.experimental import pallas as pl
from jax.experimental.pallas import tpu as pltpu

F32 = jnp.float32
BF16 = jnp.bfloat16
EPS = 1e-6
F32_TINY = float(np.finfo(np.float32).tiny)

CHUNK = 256
LANES = 128
SUBLANES = 8
M_HEADDIM = 64
M_STATE = 128
M_GROUPS = 2
M_CONV = 4
G_RANK = 16
G_NORMALIZER = 16.0
N_BRANCH = 3
GLA_SUB = 128
VMEM_LIMIT = 56 * 1024 * 1024
STATE_BLOCK_BYTES = 4 * 1024 * 1024
WIDE_TM_CAP = 2100
MAIN_TN = 1024


def _cparams(n_axes):
    return pltpu.CompilerParams(dimension_semantics=("arbitrary",) * n_axes,
                                vmem_limit_bytes=VMEM_LIMIT)


def _dot(a, b):
    return jnp.dot(a, b, preferred_element_type=F32)


def _dot_nt(a, b):
    return lax.dot_general(a, b, (((1,), (1,)), ((), ())), preferred_element_type=F32)


def _dot_tn(a, b):
    return lax.dot_general(a, b, (((0,), (0,)), ((), ())), preferred_element_type=F32)


def _split3(x):
    hi = x.astype(BF16)
    r = x - hi.astype(F32)
    mid = r.astype(BF16)
    lo = (r - mid.astype(F32)).astype(BF16)
    return hi, mid, lo


def _dot_exact_lhs(p_bf16, x):
    hi, mid, lo = _split3(x)
    return _dot(p_bf16, hi) + _dot(p_bf16, mid) + _dot(p_bf16, lo)


def _dot_exact_rhs(x, p_bf16):
    hi, mid, lo = _split3(x)
    return _dot(hi, p_bf16) + _dot(mid, p_bf16) + _dot(lo, p_bf16)


def _dot_f32(a, b):
    ah, am, _ = _split3(a)
    bh, bm, _ = _split3(b)
    return _dot(ah, bh) + _dot(ah, bm) + _dot(am, bh)


def _sigmoid(x):
    return jax.nn.sigmoid(x)


def _silu(x):
    return x * _sigmoid(x)


def _softplus(x):
    return jnp.maximum(x, 0.0) + jnp.log1p(jnp.exp(-jnp.abs(x)))


def _log_sigmoid(x):
    return jnp.minimum(x, 0.0) - jnp.log(1.0 + jnp.exp(-jnp.abs(x)))


def _rms(x, w):
    ms = jnp.mean(x * x, axis=-1, keepdims=True)
    return x * lax.rsqrt(ms + EPS) * w


def _pick_tm(t, cap=1100):
    best = 16
    for tm in range(16, min(t, cap) + 1, 16):
        if t % tm == 0:
            best = tm
    return best


def _row_scale(ssq_ref, d):
    return lax.rsqrt(ssq_ref[:, 0:1] * (1.0 / d) + EPS)


def _emit_normed(xn, nw_ref, xw_ref, ssq_ref, accumulate):
    xw_ref[...] = (xn * nw_ref[...]).astype(BF16)
    part = jnp.broadcast_to(jnp.sum(xn * xn, axis=1, keepdims=True), ssq_ref.shape)
    ssq_ref[...] = ssq_ref[...] + part if accumulate else part


def _prep_kernel(xp_ref, xs_ref, nw_ref, x_ref, xw_ref, ssq_ref, *, n_prompt_blocks):
    x = jnp.where(pl.program_id(0) < n_prompt_blocks, xp_ref[...], xs_ref[...])
    x_ref[...] = x
    _emit_normed(x, nw_ref, xw_ref, ssq_ref, accumulate=False)


def _prep(xp, xs, nw):
    n_prompt, d = xp.shape
    rb = xs.shape[0]
    assert n_prompt % rb == 0
    npb = n_prompt // rb
    t = n_prompt + rb
    row = pl.BlockSpec((rb, d), lambda i: (i, 0))
    return pl.pallas_call(
        functools.partial(_prep_kernel, n_prompt_blocks=npb),
        grid=(npb + 1,),
        in_specs=[pl.BlockSpec((rb, d), lambda i: (jnp.minimum(i, npb - 1), 0)),
                  pl.BlockSpec((rb, d), lambda i: (0, 0)),
                  pl.BlockSpec((1, d), lambda i: (0, 0))],
        out_specs=[row, row, pl.BlockSpec((rb, LANES), lambda i: (i, 0))],
        out_shape=[jax.ShapeDtypeStruct((t, d), F32), jax.ShapeDtypeStruct((t, d), BF16),
                   jax.ShapeDtypeStruct((t, LANES), F32)],
        compiler_params=_cparams(1),
        name="prep",
    )(xp, xs, nw.reshape(1, d))


def _row_parts(tm):
    half = -(-(tm // 2) // 16) * 16
    return ((0, half), (half, tm - half)) if 0 < half < tm else ((0, tm),)


def _side_casts(ws, layer, n_i, nj):
    steps = n_i * nj
    ins, outs, shapes = [], [], []
    for w in ws:
        _, k, d = w.shape
        rows = k // steps
        if rows * steps != k or rows % 16:
            return None
        ins.append(pl.BlockSpec((None, rows, d), lambda i, j: (layer, i * nj + j, 0)))
        outs.append(pl.BlockSpec((rows, d), lambda i, j: (i * nj + j, 0)))
        shapes.append(jax.ShapeDtypeStruct((k, d), BF16))
    return ins, outs, shapes


def _do_side_casts(refs):
    n = len(refs) // 2
    for src, dst in zip(refs[:n], refs[n:]):
        dst[...] = src[...].astype(BF16)


def _ffn_up_kernel(xw_ref, ssq_ref, wg_ref, wu_ref, *rest, parts, n_cast):
    o_ref = rest[n_cast]
    _do_side_casts(rest[:n_cast] + rest[n_cast + 1:])
    wg = wg_ref[...].astype(BF16)
    wu = wu_ref[...].astype(BF16)
    d = xw_ref.shape[1]
    for start, rows in parts:
        sl = pl.ds(start, rows)
        r = lax.rsqrt(ssq_ref[sl, 0:1] * (1.0 / d) + EPS)
        h = xw_ref[sl, :]
        g = r * _dot(h, wg)
        u = r * _dot(h, wu)
        o_ref[sl, :] = (_silu(g) * u).astype(BF16)


def _ffn_up(xw, ssq, w_gu, layer, cast_ws=(), tn=512):
    t, d = xw.shape
    dff = w_gu.shape[2] // 2
    tm = _pick_tm(t, WIDE_TM_CAP)
    nj = dff // tn
    side = _side_casts(cast_ws, layer, t // tm, nj) if cast_ws else None
    c_in, c_out, c_shape = side if side else ([], [], [])
    outs = pl.pallas_call(
        functools.partial(_ffn_up_kernel, parts=_row_parts(tm), n_cast=len(c_in)),
        grid=(t // tm, nj),
        in_specs=[pl.BlockSpec((tm, d), lambda i, j: (i, 0)),
                  pl.BlockSpec((tm, LANES), lambda i, j: (i, 0)),
                  pl.BlockSpec((None, d, tn), lambda i, j: (layer, 0, j)),
                  pl.BlockSpec((None, d, tn), lambda i, j: (layer, 0, j + nj))] + c_in,
        out_specs=[pl.BlockSpec((tm, tn), lambda i, j: (i, j))] + c_out,
        out_shape=[jax.ShapeDtypeStruct((t, dff), BF16)] + c_shape,
        compiler_params=_cparams(2),
        name="ffn_up",
    )(xw, ssq, w_gu, w_gu, *(cast_ws if side else ()))
    return outs[0], (tuple(outs[1:]) if side else None)


def _resid_matmul_kernel(a_ref, w_ref, x_ref, *rest, scale, emit, parts):
    if emit:
        nw_ref, o_ref, xw_ref, ssq_ref = rest

        @pl.when(pl.program_id(1) == 0)
        def _():
            ssq_ref[...] = jnp.zeros_like(ssq_ref)
    else:
        (o_ref,) = rest
    wb = w_ref[...].astype(BF16)
    for start, rows in parts:
        sl = pl.ds(start, rows)
        xn = x_ref[sl, :] + scale * _dot(a_ref[sl, :], wb)
        if emit:
            _emit_normed(xn, nw_ref, xw_ref.at[sl, :], ssq_ref.at[sl, :], accumulate=True)
        o_ref[sl, :] = xn


def _wspec(w, layer, tn):
    if w.ndim == 3:
        return pl.BlockSpec((None, w.shape[1], tn), lambda i, j: (layer, 0, j))
    return pl.BlockSpec((w.shape[0], tn), lambda i, j: (0, j))


def _resid_matmul(a, w, layer, x, scale, tn, name, next_nw=None, tm_cap=1100):
    t, k = a.shape
    d = w.shape[-1]
    tm = _pick_tm(t, tm_cap)
    emit = next_nw is not None
    tile = pl.BlockSpec((tm, tn), lambda i, j: (i, j))
    in_specs = [pl.BlockSpec((tm, k), lambda i, j: (i, 0)), _wspec(w, layer, tn), tile]
    args = [a, w, x]
    out_specs = [tile]
    out_shape = [jax.ShapeDtypeStruct((t, d), F32)]
    if emit:
        in_specs.append(pl.BlockSpec((1, tn), lambda i, j: (0, j)))
        args.append(next_nw.reshape(1, d))
        out_specs += [tile, pl.BlockSpec((tm, LANES), lambda i, j: (i, 0))]
        out_shape += [jax.ShapeDtypeStruct((t, d), BF16), jax.ShapeDtypeStruct((t, LANES), F32)]
    outs = pl.pallas_call(
        functools.partial(_resid_matmul_kernel, scale=scale, emit=emit,
                          parts=_row_parts(tm) if k <= d else ((0, tm),)),
        grid=(t // tm, d // tn),
        in_specs=in_specs,
        out_specs=out_specs,
        out_shape=out_shape,
        compiler_params=_cparams(2),
        name=name,
    )(*args)
    return outs if emit else outs[0]


def _pack_w_in_kernel(w_ref, o_ref, *, copies, zero):
    for src, dst, width in copies:
        if width % LANES == 0:
            o_ref[:, dst:dst + width] = w_ref[src:src + width, :].T.astype(BF16)
        else:
            base = src // LANES * LANES
            lo = src - base
            assert dst % LANES == lo and lo + width <= LANES
            t = w_ref[base:base + LANES, :].T
            o_ref[:, dst:dst + width] = t[:, lo:lo + width].astype(BF16)
    start, stop = zero
    if stop > start:
        o_ref[:, start:stop] = jnp.zeros((o_ref.shape[0], stop - start), BF16)


def _pack_w_in(w_t, layer, copies, zero, n_cols, tk=128):
    _, n_src, d = w_t.shape
    return pl.pallas_call(
        functools.partial(_pack_w_in_kernel, copies=copies, zero=zero),
        grid=(d // tk,),
        in_specs=[pl.BlockSpec((None, n_src, tk), lambda i: (layer, 0, i))],
        out_specs=pl.BlockSpec((tk, n_cols), lambda i: (i, 0)),
        out_shape=jax.ShapeDtypeStruct((d, n_cols), BF16),
        compiler_params=_cparams(1),
        name="pack_w_in",
    )(w_t)


def _in_proj_kernel(xw_ref, ssq_ref, w_ref, *rest, n_cast):
    o_ref = rest[n_cast]
    _do_side_casts(rest[:n_cast] + rest[n_cast + 1:])
    o_ref[...] = _row_scale(ssq_ref, xw_ref.shape[1]) * _dot(xw_ref[...], w_ref[...])


def _in_proj(xw, ssq, w, layer, cast_ws=(), tn=MAIN_TN):
    t, d = xw.shape
    n = w.shape[1]
    tm = _pick_tm(t, WIDE_TM_CAP)
    side = _side_casts(cast_ws, layer, t // tm, n // tn) if cast_ws else None
    c_in, c_out, c_shape = side if side else ([], [], [])
    outs = pl.pallas_call(
        functools.partial(_in_proj_kernel, n_cast=len(c_in)),
        grid=(t // tm, n // tn),
        in_specs=[pl.BlockSpec((tm, d), lambda i, j: (i, 0)),
                  pl.BlockSpec((tm, LANES), lambda i, j: (i, 0)),
                  pl.BlockSpec((d, tn), lambda i, j: (0, j))] + c_in,
        out_specs=[pl.BlockSpec((tm, tn), lambda i, j: (i, j))] + c_out,
        out_shape=[jax.ShapeDtypeStruct((t, n), F32)] + c_shape,
        compiler_params=_cparams(2),
        name="in_proj",
    )(xw, ssq, w, *(cast_ws if side else ()))
    return outs[0], (tuple(outs[1:]) if side else None)


def _merge_kernel(ym_ref, yh_ref, yg_ref, wm_ref, wh_ref, wg_ref, g0_ref, g1_ref, g2_ref, o_ref, *, parts):
    wm = wm_ref[...].astype(BF16)
    wh = wh_ref[...].astype(BF16)
    wg = wg_ref[...].astype(BF16)
    for start, rows in parts:
        sl = pl.ds(start, rows)
        acc = _sigmoid(g0_ref[sl, :]) * _dot(ym_ref[sl, :], wm)
        acc = acc + _sigmoid(g1_ref[sl, :]) * _dot(yh_ref[sl, :], wh)
        acc = acc + _sigmoid(g2_ref[sl, :]) * _dot(yg_ref[sl, :], wg)
        o_ref[sl, :] = acc.astype(BF16)


def _merge(ym, yh, yg, wm, wh, wg, layer, proj, gate_off, tn=512):
    t, k = ym.shape
    d = wm.shape[-1]
    tm = _pick_tm(t)
    assert gate_off % tn == 0
    gb = gate_off // tn
    nb = d // tn
    yspec = pl.BlockSpec((tm, k), lambda i, j: (i, 0))
    wspec = _wspec(wm, layer, tn)

    def gspec(b):
        return pl.BlockSpec((tm, tn), lambda i, j: (i, gb + b * nb + j))

    return pl.pallas_call(
        functools.partial(_merge_kernel, parts=_row_parts(tm)),
        grid=(t // tm, nb),
        in_specs=[yspec, yspec, yspec, wspec, wspec, wspec, gspec(0), gspec(1), gspec(2)],
        out_specs=pl.BlockSpec((tm, tn), lambda i, j: (i, j)),
        out_shape=jax.ShapeDtypeStruct((t, d), BF16),
        compiler_params=_cparams(2),
        name="merge",
    )(ym, yh, yg, wm, wh, wg, proj, proj, proj)


def _final_norm_kernel(x_ref, nw_ref, o_ref):
    o_ref[...] = _rms(x_ref[...], nw_ref[...])


def _final_norm(x, nw, row0, rows, name):
    d = x.shape[1]
    tm = _pick_tm(rows)
    assert row0 % tm == 0
    return pl.pallas_call(
        _final_norm_kernel,
        grid=(rows // tm,),
        in_specs=[pl.BlockSpec((tm, d), lambda i: (row0 // tm + i, 0)),
                  pl.BlockSpec((1, d), lambda i: (0, 0))],
        out_specs=pl.BlockSpec((tm, d), lambda i: (i, 0)),
        out_shape=jax.ShapeDtypeStruct((rows, d), F32),
        compiler_params=_cparams(1),
        name=name,
    )(x, nw.reshape(1, d))


def _scan_constants(c):
    N_LEVELS = int(np.log2(c))
    assert 1 << N_LEVELS == c and N_LEVELS >= 3
    pm = np.zeros((2, c, c), np.float32)
    sg = np.zeros((N_LEVELS - 1, c, LANES), np.float32)
    mk = np.zeros((N_LEVELS + 1, c, c), np.float32)
    r = np.arange(c)
    for lvl in range(N_LEVELS):
        h = c >> (lvl + 1)
        for t in range(c):
            blk, pos = divmod(t, 2 * h)
            ridx = blk * 2 * h + h - 1
            upper = pos >= h
            if upper:
                mk[lvl, t] = ((r // (2 * h)) == blk) & ((r % (2 * h)) < h)
            if h >= 4:
                sg[lvl, t] = 1.0 if upper else -1.0
            elif h == 2:
                pm[0, t] = ((r > ridx) & (r <= t)) if upper else ((r > t) & (r <= ridx))
            else:
                sg[N_LEVELS - 2, t] = 1.0 if upper else 0.0
    pm[1] = r[None, :] <= r[:, None]
    mk[N_LEVELS] = np.eye(c)
    return jnp.asarray(pm.reshape(2 * c, c), BF16), jnp.asarray(sg), jnp.asarray(mk)


def _lower_bound(logits, layer):
    m = jnp.max(logits, axis=0, keepdims=True)
    e = jnp.exp(logits - m)
    sm = e / jnp.sum(e, axis=0, keepdims=True)
    lb = jnp.zeros_like(m)
    for i in range(1, layer + 1):
        lb = lb + sm[i:i + 1, :]
    return lb


def _gla_inputs(refs, hgrn, layer):
    if hgrn:
        q_ref, f_ref, i_ref, lbl_ref = refs
        lb = _lower_bound(lbl_ref[...], layer)
        q = _silu(q_ref[...]) * (LANES ** -0.5)
        zf = f_ref[...]
        a = (1.0 - lb) * _sigmoid(zf)
        f = lb + a
        gl = jnp.log(jnp.maximum(f, F32_TINY))
        k = (1.0 - lb) - a
        v = i_ref[...]
    else:
        q_ref, k_ref, v_ref, a_ref, wd_ref, bd_ref = refs
        q = q_ref[...] * (LANES ** -0.5)
        k = k_ref[...]
        v = v_ref[...]
        gl = _log_sigmoid(_dot_f32(a_ref[...], wd_ref[...]) + bd_ref[...]) / G_NORMALIZER
    return q, k, v, gl


def _level_factor(lvl, gl, cum, z_h2, sg_ref):
    c = gl.shape[0]
    n_levels = sg_ref.shape[0] + 1
    h = c >> (lvl + 1)
    if h >= 4:
        c3 = cum.reshape(c // (2 * h), 2 * h, LANES)
        d = (c3 - c3[:, h - 1:h, :]).reshape(c, LANES)
        return jnp.exp(sg_ref[lvl] * d)
    if h == 2:
        return jnp.exp(z_h2)
    return jnp.exp(sg_ref[n_levels - 2] * gl)


def _gla_prompt_kernel(*refs, hgrn, layer, hp, dv):
    n_in = 4 if hgrn else 6
    in_refs = refs[:n_in]
    g_ref, nw_ref, pm_ref, sg_ref, mk_ref, _, y_ref, so_ref, s_scr = refs[n_in:]
    c = pl.program_id(2)
    sub = mk_ref.shape[1]
    n_levels = mk_ref.shape[0] - 1

    @pl.when(c == 0)
    def _():
        s_scr[...] = jnp.zeros_like(s_scr)

    q_all, k_all, v_all, gl_all = _gla_inputs(in_refs, hgrn, layer)
    finals = []
    for u in range(hp):
        s = s_scr[u]
        for r0 in range(0, CHUNK, sub):
            rows = pl.ds(r0, sub)
            q, k, gl = (a[r0:r0 + sub, u * LANES:(u + 1) * LANES] for a in (q_all, k_all, gl_all))
            v = v_all[r0:r0 + sub, u * dv:(u + 1) * dv]
            zz = _dot_exact_lhs(pm_ref[...], gl)
            z_h2, cum = zz[:sub], zz[sub:]
            qb, kb = q.astype(BF16), k.astype(BF16)
            att = mk_ref[n_levels] * _dot_nt(qb, kb)
            for lvl in range(n_levels):
                eb = _level_factor(lvl, gl, cum, z_h2, sg_ref).astype(BF16)
                att = att + mk_ref[lvl] * _dot_nt(qb * eb, kb * eb)
            e_cum = jnp.exp(cum)
            e_tail = jnp.exp(cum[sub - 1:sub, :] - cum)
            vb = v.astype(BF16)
            o = _dot(att.astype(BF16), vb) + _dot((q * e_cum).astype(BF16), s.astype(BF16))
            e_last = e_cum.T[:, sub - 1:sub]
            s = e_last * s + _dot_tn((k * e_tail).astype(BF16), vb)
            gate = g_ref[rows, u * dv:(u + 1) * dv]
            y_ref[rows, u * dv:(u + 1) * dv] = (_rms(o, nw_ref[...]) * _silu(gate)).astype(BF16)
        s_scr[u] = s
        finals.append(s)

    @pl.when(c == pl.num_programs(2) - 1)
    def _():
        for u in range(hp):
            so_ref[u] = finals[u]


def _gla_prompt(proj, offs, extra, nw, consts, y_init, *, hgrn, nseq, nchunks, heads, dv, layer, hp=4):
    pm, sg, mk = consts
    assert heads % hp == 0

    def col(off, w):
        assert off % (hp * w) == 0
        return pl.BlockSpec((CHUNK, hp * w), lambda h, b, c: (b * nchunks + c, off // (hp * w) + h))

    def full(a):
        nd = a.ndim
        return pl.BlockSpec(a.shape, lambda h, b, c: (0,) * nd)

    if hgrn:
        oq, of, oi, og = offs
        (lbl,) = extra
        in_specs = [col(oq, LANES), col(of, LANES), col(oi, dv),
                    pl.BlockSpec((lbl.shape[0], hp * LANES), lambda h, b, c: (0, h))]
        args = [proj, proj, proj, lbl]
    else:
        oq, ok, ov, og, oa = offs
        wd, bd = extra
        in_specs = [col(oq, LANES), col(ok, LANES), col(ov, dv),
                    pl.BlockSpec((CHUNK, LANES), lambda h, b, c: (b * nchunks + c, oa // LANES)),
                    pl.BlockSpec((LANES, hp * LANES), lambda h, b, c: (0, h)),
                    pl.BlockSpec((1, hp * LANES), lambda h, b, c: (0, h))]
        args = [proj, proj, proj, proj, wd, bd]
    in_specs += [col(og, dv), full(nw), full(pm), full(sg), full(mk), pl.BlockSpec(memory_space=pl.ANY)]
    args += [proj, nw, pm, sg, mk, y_init]
    y, s = pl.pallas_call(
        functools.partial(_gla_prompt_kernel, hgrn=hgrn, layer=layer, hp=hp, dv=dv),
        grid=(heads // hp, nseq, nchunks),
        in_specs=in_specs,
        out_specs=[pl.BlockSpec((CHUNK, hp * dv), lambda h, b, c: (b * nchunks + c, h)),
                   pl.BlockSpec((None, hp, LANES, dv), lambda h, b, c: (b, h, 0, 0))],
        out_shape=[jax.ShapeDtypeStruct(y_init.shape, BF16),
                   jax.ShapeDtypeStruct((nseq, heads, LANES, dv), F32)],
        input_output_aliases={len(args) - 1: 0},
        scratch_shapes=[pltpu.VMEM((hp, LANES, dv), F32)],
        compiler_params=_cparams(3),
        name="hgrn_prompt" if hgrn else "gla_prompt",
    )(*args)
    return y, s


def _columns(x):
    nb = x.shape[0]
    if nb < LANES:
        x = jnp.concatenate([x, jnp.zeros((LANES - nb, x.shape[1]), x.dtype)], axis=0)
    return x.T


def _store_state(so_ref, layer, b, s_new, first):
    if first:
        for l in range(so_ref.shape[0]):
            so_ref[l, b] = s_new if l == layer else jnp.zeros_like(s_new)
    else:
        so_ref[b] = s_new


def _gla_sample_kernel(*refs, hgrn, nb, layer, first):
    n_in = 4 if hgrn else 6
    in_refs = refs[:n_in]
    g_ref, nw_ref, s_ref = refs[n_in:n_in + 3]
    y_ref, so_ref = refs[-2:]
    q, k, v, gl = _gla_inputs(in_refs, hgrn, layer)
    e_t = _columns(jnp.exp(gl))
    k_t = _columns(k)
    qb = q.astype(BF16)
    rowid = lax.broadcasted_iota(jnp.int32, v.shape, 0)
    o = jnp.zeros(v.shape, F32)
    for b in range(nb):
        s_new = e_t[:, b:b + 1] * s_ref[b] + k_t[:, b:b + 1] * v[b:b + 1, :]
        _store_state(so_ref, layer, b, s_new, first)
        o = jnp.where(rowid == b, _dot(qb, s_new.astype(BF16)), o)
    y_ref[...] = (_rms(o, nw_ref[...]) * _silu(g_ref[...])).astype(BF16)


def _state_specs(depth, layer, nb, dv, first):
    ispec = pl.BlockSpec((None, nb, None, LANES, dv), lambda h, i: (layer, i, h, 0, 0))
    if first:
        ospec = pl.BlockSpec((depth, nb, None, LANES, dv), lambda h, i: (0, i, h, 0, 0))
    else:
        ospec = ispec
    return ispec, ospec


def _sample_batch_block(ns, dv):
    nb = min(ns, STATE_BLOCK_BYTES // (LANES * dv * 4))
    assert ns % nb == 0
    return nb


def _gla_sample(proj, state, prev, y_all, row0, offs, extra, nw, *, hgrn, heads, dv, layer):
    depth, ns = state.shape[:2]
    nb = _sample_batch_block(ns, dv)
    rb = row0 // nb
    first = prev is None

    def col(off, w):
        return pl.BlockSpec((nb, w), lambda h, i: (rb + i, off // w + h))

    if hgrn:
        oq, of, oi, og = offs
        (lbl,) = extra
        in_specs = [col(oq, LANES), col(of, LANES), col(oi, dv),
                    pl.BlockSpec((lbl.shape[0], LANES), lambda h, i: (0, h))]
        args = [proj, proj, proj, lbl]
    else:
        oq, ok, ov, og, oa = offs
        wd, bd = extra
        in_specs = [col(oq, LANES), col(ok, LANES), col(ov, dv),
                    pl.BlockSpec((nb, LANES), lambda h, i: (rb + i, oa // LANES)),
                    pl.BlockSpec((LANES, LANES), lambda h, i: (0, h)),
                    pl.BlockSpec((1, LANES), lambda h, i: (0, h))]
        args = [proj, proj, proj, proj, wd, bd]
    ispec, ospec = _state_specs(depth, layer, nb, dv, first)
    in_specs += [col(og, dv), pl.BlockSpec(nw.shape, lambda h, i: (0, 0)), ispec,
                 pl.BlockSpec(memory_space=pl.ANY)]
    args += [proj, nw, state, y_all]
    aliases = {len(args) - 1: 0}
    if not first:
        in_specs.append(pl.BlockSpec(memory_space=pl.ANY))
        args.append(prev)
        aliases[len(args) - 1] = 1
    y, s = pl.pallas_call(
        functools.partial(_gla_sample_kernel, hgrn=hgrn, nb=nb, layer=layer, first=first),
        grid=(heads, ns // nb),
        in_specs=in_specs,
        out_specs=[pl.BlockSpec((nb, dv), lambda h, i: (rb + i, h)), ospec],
        out_shape=[jax.ShapeDtypeStruct(y_all.shape, BF16),
                   jax.ShapeDtypeStruct(state.shape, F32)],
        input_output_aliases=aliases,
        compiler_params=_cparams(2),
        name="hgrn_sample" if hgrn else "gla_sample",
    )(*args)
    return y, s


def _mamba_post(y, xs, z, dskip, nw):
    y = (y + dskip * xs) * _silu(z)
    gs = y.shape[1] // M_GROUPS
    outs = [_rms(y[:, g * gs:(g + 1) * gs], nw[:, g * gs:(g + 1) * gs]) for g in range(M_GROUPS)]
    return jnp.concatenate(outs, axis=1)


def _mamba_prompt_kernel(z_ref, x_ref, bc_ref, sm_ref, cwx_ref, cwb_ref, cbx_ref, cbb_ref,
                         dtb_ref, alog_ref, dsk_ref, nw_ref, tril_ref, mask_ref, _,
                         y_ref, cox_ref, cob_ref, so_ref, ex_scr, eb_scr, s_scr):
    c = pl.program_id(1)
    nc = pl.num_programs(1)
    tail = SUBLANES

    @pl.when(c == 0)
    def _():
        ex_scr[0:tail, :] = jnp.zeros((tail, ex_scr.shape[1]), F32)
        eb_scr[0:tail, :] = jnp.zeros((tail, eb_scr.shape[1]), F32)
        s_scr[...] = jnp.zeros_like(s_scr)

    ex_scr[tail:tail + CHUNK, :] = x_ref[...]
    eb_scr[tail:tail + CHUNK, :] = bc_ref[...]

    def conv(scr, cw_ref, cb_ref):
        xe = scr[...]
        acc = cw_ref[0:1, :] * xe
        for w in range(1, M_CONV):
            acc = pltpu.roll(acc, 1, axis=0) + cw_ref[w:w + 1, :] * xe
        return _silu(acc[tail:, :] + cb_ref[...])

    xs = conv(ex_scr, cwx_ref, cbx_ref)
    bcm = conv(eb_scr, cwb_ref, cbb_ref)

    ex_scr[0:tail, :] = ex_scr[CHUNK:CHUNK + tail, :]
    eb_scr[0:tail, :] = eb_scr[CHUNK:CHUNK + tail, :]

    dt = _softplus(sm_ref[...] + dtb_ref[...])
    a = -jnp.exp(alog_ref[...])
    cum = _dot_exact_lhs(tril_ref[...], dt * a)
    cum_t = cum.T
    cl = cum[CHUNK - 1:CHUNK, :]
    mask = mask_ref[...] > 0.5
    gw = M_GROUPS * M_STATE
    lo = lax.broadcasted_iota(jnp.int32, (CHUNK, LANES), 1) < M_HEADDIM
    lo_r = lax.broadcasted_iota(jnp.int32, (LANES, LANES), 0) < M_HEADDIM
    n_pairs = xs.shape[1] // LANES
    per_group = n_pairs // M_GROUPS
    ys = []
    for j in range(n_pairs):
        g = j // per_group
        h0, h1 = 2 * j, 2 * j + 1
        if j % per_group == 0:
            bg = bcm[:, g * M_STATE:(g + 1) * M_STATE].astype(BF16)
            cg = bcm[:, gw + g * M_STATE:gw + (g + 1) * M_STATE].astype(BF16)
            cb = jnp.where(mask, _dot_nt(cg, bg), 0.0)

        def dec(h):
            return jnp.exp(jnp.minimum(cum[:, h:h + 1] - cum_t[h:h + 1, :], 0.0))

        x2 = xs[:, j * LANES:(j + 1) * LANES]
        dt2 = jnp.where(lo, dt[:, h0:h0 + 1], dt[:, h1:h1 + 1])
        xdt = (x2 * dt2).astype(BF16)
        y_in = jnp.where(lo, _dot((cb * dec(h0)).astype(BF16), xdt),
                         _dot((cb * dec(h1)).astype(BF16), xdt))
        ec2 = jnp.where(lo, jnp.exp(cum[:, h0:h0 + 1]), jnp.exp(cum[:, h1:h1 + 1]))
        s = s_scr[j]
        ys.append(y_in + _dot_nt(cg, s.astype(BF16)) * ec2)
        w2 = dt2 * jnp.where(lo, jnp.exp(cl[:, h0:h0 + 1] - cum[:, h0:h0 + 1]),
                             jnp.exp(cl[:, h1:h1 + 1] - cum[:, h1:h1 + 1]))
        el2 = jnp.where(lo_r, jnp.exp(cl[:, h0:h0 + 1]), jnp.exp(cl[:, h1:h1 + 1]))
        s_scr[j] = el2 * s + _dot_tn((x2 * w2).astype(BF16), bg)
    y = jnp.concatenate(ys, axis=1)
    y_ref[...] = _mamba_post(y, xs, z_ref[...], dsk_ref[...], nw_ref[...]).astype(BF16)

    @pl.when(c == nc - 1)
    def _():
        so_ref[...] = s_scr[...]
        cox_ref[...] = ex_scr[tail + CHUNK - (M_CONV - 1):tail + CHUNK, :]
        cob_ref[...] = eb_scr[tail + CHUNK - (M_CONV - 1):tail + CHUNK, :]


def _mamba_prompt(proj, offs, p, consts, y_init, *, nseq, nchunks):
    oz, ox, obc, osm = offs
    mw = p["dskip"].shape[1]
    bcw = p["cwb"].shape[1]
    n_pairs = mw // LANES

    def col(off, w):
        return pl.BlockSpec((CHUNK, w), lambda b, c: (b * nchunks + c, off // w))

    def full(a):
        nd = a.ndim
        return pl.BlockSpec(a.shape, lambda b, c: (0,) * nd)

    small = [p["cwx"], p["cwb"], p["cbx"], p["cbb"], p["dtb"], p["alog"], p["dskip"], p["nw"],
             consts[0], consts[1]]
    y, cox, cob, s = pl.pallas_call(
        _mamba_prompt_kernel,
        grid=(nseq, nchunks),
        in_specs=[col(oz, mw), col(ox, mw), col(obc, bcw), col(osm, LANES)] + [full(a) for a in small]
        + [pl.BlockSpec(memory_space=pl.ANY)],
        out_specs=[pl.BlockSpec((CHUNK, mw), lambda b, c: (b * nchunks + c, 0)),
                   pl.BlockSpec((None, M_CONV - 1, mw), lambda b, c: (b, 0, 0)),
                   pl.BlockSpec((None, M_CONV - 1, bcw), lambda b, c: (b, 0, 0)),
                   pl.BlockSpec((None, n_pairs, LANES, M_STATE), lambda b, c: (b, 0, 0, 0))],
        input_output_aliases={4 + len(small): 0},
        out_shape=[jax.ShapeDtypeStruct(y_init.shape, BF16),
                   jax.ShapeDtypeStruct((nseq, M_CONV - 1, mw), F32),
                   jax.ShapeDtypeStruct((nseq, M_CONV - 1, bcw), F32),
                   jax.ShapeDtypeStruct((nseq, n_pairs, LANES, M_STATE), F32)],
        scratch_shapes=[pltpu.VMEM((CHUNK + SUBLANES, mw), F32), pltpu.VMEM((CHUNK + SUBLANES, bcw), F32),
                        pltpu.VMEM((n_pairs, LANES, M_STATE), F32)],
        compiler_params=_cparams(2),
        name="mamba_prompt",
    )(proj, proj, proj, proj, *small, y_init)
    return y, jnp.concatenate([cox, cob], axis=-1), s


def _mamba_sample_prep_kernel(cs_ref, x_ref, bc_ref, sm_ref, cw_ref, cb_ref, dtb_ref, alog_ref, exp_ref,
                              co_ref, act_ref, dte_ref, ee_ref):
    cd = cw_ref.shape[1]
    new = jnp.concatenate([x_ref[...], bc_ref[...]], axis=1)
    acc = cb_ref[...] + cw_ref[M_CONV - 1:M_CONV, :] * new
    for w in range(M_CONV - 1):
        acc = acc + cw_ref[w:w + 1, :] * cs_ref[:, w * cd:(w + 1) * cd]
    act_ref[...] = _silu(acc)
    for w in range(1, M_CONV - 1):
        co_ref[:, (w - 1) * cd:w * cd] = cs_ref[:, w * cd:(w + 1) * cd]
    co_ref[:, (M_CONV - 2) * cd:(M_CONV - 1) * cd] = new
    dt = _softplus(sm_ref[...] + dtb_ref[...])
    a = -jnp.exp(alog_ref[...])
    dte_ref[...] = _dot_exact_rhs(dt, exp_ref[...])
    ee_ref[...] = jnp.exp(_dot_exact_rhs(dt * a, exp_ref[...]))


def _mamba_sample_state_kernel(x_ref, b_ref, c_ref, dte_ref, ee_ref, s_ref, *out_refs, nb, layer, first):
    y_ref, so_ref = out_refs[-2:]
    xdt_t = _columns(x_ref[...] * dte_ref[...])
    e_t = _columns(ee_ref[...])
    bv = b_ref[...]
    cb = c_ref[...].astype(BF16)
    rowid = lax.broadcasted_iota(jnp.int32, (nb, LANES), 0)
    y = jnp.zeros((nb, LANES), F32)
    for b in range(nb):
        s_new = e_t[:, b:b + 1] * s_ref[b] + xdt_t[:, b:b + 1] * bv[b:b + 1, :]
        _store_state(so_ref, layer, b, s_new, first)
        y = jnp.where(rowid == b, _dot_nt(cb, s_new.astype(BF16)), y)
    y_ref[...] = y


def _mamba_sample_post_kernel(y_ref, x_ref, z_ref, dsk_ref, nw_ref, _, o_ref):
    o_ref[...] = _mamba_post(y_ref[...], x_ref[...], z_ref[...], dsk_ref[...], nw_ref[...]).astype(BF16)


def _mamba_sample(proj, conv_state, ssm_state, prev, y_all, row0, offs, p, expand, layer):
    oz, ox, obc, osm = offs
    depth, ns = ssm_state.shape[:2]
    nb = _sample_batch_block(ns, M_STATE)
    mw = p["dskip"].shape[1]
    bcw = p["cwb"].shape[1]
    cd = mw + bcw
    n_pairs = mw // LANES
    per_group = n_pairs // M_GROUPS
    rb = row0 // ns
    first = prev is None
    cw = jnp.concatenate([p["cwx"], p["cwb"]], axis=1)
    cb = jnp.concatenate([p["cbx"], p["cbb"]], axis=1)

    def full1(a):
        nd = a.ndim
        return pl.BlockSpec(a.shape, lambda i: (0,) * nd)

    cs2 = conv_state.reshape(ns, (M_CONV - 1) * cd)
    small = [cw, cb, p["dtb"], p["alog"], expand]
    co, act, dte, ee = pl.pallas_call(
        _mamba_sample_prep_kernel,
        grid=(1,),
        in_specs=[full1(cs2),
                  pl.BlockSpec((ns, mw), lambda i: (rb, ox // mw)),
                  pl.BlockSpec((ns, bcw), lambda i: (rb, obc // bcw)),
                  pl.BlockSpec((ns, LANES), lambda i: (rb, osm // LANES))] + [full1(a) for a in small],
        out_specs=[pl.BlockSpec((ns, (M_CONV - 1) * cd), lambda i: (0, 0)),
                   pl.BlockSpec((ns, cd), lambda i: (0, 0)),
                   pl.BlockSpec((ns, mw), lambda i: (0, 0)),
                   pl.BlockSpec((ns, mw), lambda i: (0, 0))],
        out_shape=[jax.ShapeDtypeStruct((ns, (M_CONV - 1) * cd), F32),
                   jax.ShapeDtypeStruct((ns, cd), F32),
                   jax.ShapeDtypeStruct((ns, mw), F32),
                   jax.ShapeDtypeStruct((ns, mw), F32)],
        compiler_params=_cparams(1),
        name="mamba_sample_prep",
    )(cs2, proj, proj, proj, *small)

    bblk = mw // LANES
    cblk = bblk + M_GROUPS * M_STATE // LANES
    ispec, ospec = _state_specs(depth, layer, nb, M_STATE, first)
    in_specs = [pl.BlockSpec((nb, LANES), lambda j, i: (i, j)),
                pl.BlockSpec((nb, LANES), lambda j, i: (i, bblk + j // per_group)),
                pl.BlockSpec((nb, LANES), lambda j, i: (i, cblk + j // per_group)),
                pl.BlockSpec((nb, LANES), lambda j, i: (i, j)),
                pl.BlockSpec((nb, LANES), lambda j, i: (i, j)),
                ispec]
    args = [act, act, act, dte, ee, ssm_state]
    aliases = {}
    if not first:
        in_specs.append(pl.BlockSpec(memory_space=pl.ANY))
        args.append(prev)
        aliases = {len(args) - 1: 1}
    y, s_new = pl.pallas_call(
        functools.partial(_mamba_sample_state_kernel, nb=nb, layer=layer, first=first),
        grid=(n_pairs, ns // nb),
        in_specs=in_specs,
        out_specs=[pl.BlockSpec((nb, LANES), lambda j, i: (i, j)), ospec],
        out_shape=[jax.ShapeDtypeStruct((ns, mw), F32),
                   jax.ShapeDtypeStruct(ssm_state.shape, F32)],
        input_output_aliases=aliases,
        compiler_params=_cparams(2),
        name="mamba_sample_state",
    )(*args)

    ym = pl.pallas_call(
        _mamba_sample_post_kernel,
        grid=(1,),
        in_specs=[full1(y),
                  pl.BlockSpec((ns, mw), lambda i: (0, 0)),
                  pl.BlockSpec((ns, mw), lambda i: (rb, oz // mw)),
                  full1(p["dskip"]), full1(p["nw"]), pl.BlockSpec(memory_space=pl.ANY)],
        out_specs=pl.BlockSpec((ns, mw), lambda i: (rb, 0)),
        out_shape=jax.ShapeDtypeStruct(y_all.shape, BF16),
        input_output_aliases={5: 0},
        compiler_params=_cparams(1),
        name="mamba_sample_post",
    )(y, act, proj, p["dskip"], p["nw"], y_all)
    return ym, co.reshape(ns, M_CONV - 1, cd), s_new


def _pad_lanes(v, n=LANES):
    v = v.reshape(1, -1)
    return jnp.pad(v, ((0, 0), (0, n - v.shape[1])))


def kernel(x_prompt, x_sample, state_conv, state_ssm, state_hgrn, state_gla, ffn1_norm, ffn1_w_gate_up, ffn1_w_down, mix_norm, w_in, conv_w, conv_b, dt_bias, a_log, d_skip, mamba_norm, hgrn_lb_logits, hgrn_norm, gla_w_decay, gla_b_decay, gla_norm, w_branch_mamba, w_branch_hgrn, w_branch_gla, w_out, ffn2_norm, ffn2_w_gate_up, ffn2_w_down, final_norm):
    nseq, seq, d = x_prompt.shape
    ns = x_sample.shape[0]
    depth = w_in.shape[0]
    nchunks = seq // CHUNK
    n_prompt = nseq * seq
    mw = w_branch_mamba.shape[1]
    hw = w_branch_hgrn.shape[1]
    gw = w_branch_gla.shape[1]
    gk = gla_w_decay.shape[2]
    m_heads = dt_bias.shape[1]
    h_heads = state_hgrn.shape[2]
    g_heads = state_gla.shape[2]
    g_dv = state_gla.shape[4]
    bcw = 2 * M_GROUPS * M_STATE
    n_pairs = mw // LANES
    assert seq % CHUNK == 0 and n_prompt % ns == 0 and ns % 16 == 0
    assert m_heads <= G_RANK + m_heads <= LANES and gk // g_heads == LANES and hw // h_heads == LANES

    seg_w = {"z": mw, "xs": mw, "bc": bcw, "dt": m_heads, "hq": hw, "hf": hw, "hi": hw, "hg": hw,
             "gq": gk, "gk": gk, "gv": gw, "gg": gw, "ga": G_RANK, "gate": N_BRANCH * d}
    src_order = ("z", "xs", "bc", "dt", "hq", "hf", "hi", "hg", "gq", "gk", "gv", "gg", "ga", "gate")
    dst_order = ("z", "xs", "hq", "hf", "hi", "hg", "gv", "gg", "gq", "gk", "bc", "dt", "ga")
    src, off = {}, {}
    pos = 0
    for name in src_order:
        src[name] = pos
        pos += seg_w[name]
    pos = 0
    for name in dst_order:
        off[name] = pos
        pos += seg_w[name]
    n_used = pos
    off["gate"] = -(-n_used // MAIN_TN) * MAIN_TN
    n_cols = off["gate"] + seg_w["gate"]
    assert n_cols % MAIN_TN == 0
    copies = tuple((src[name], off[name], seg_w[name]) for name in dst_order + ("gate",))
    oz, ox, obc, osm = off["z"], off["xs"], off["bc"], off["dt"]
    assert off["ga"] == osm + m_heads and osm % LANES == 0

    consts = _scan_constants(GLA_SUB)
    tril_np = np.tril(np.ones((CHUNK, CHUNK), np.float32))
    mconsts = (jnp.asarray(tril_np, BF16), jnp.asarray(tril_np))
    expand_np = np.zeros((LANES, mw), np.float32)
    for h in range(m_heads):
        expand_np[h, h * M_HEADDIM:(h + 1) * M_HEADDIM] = 1.0
    expand = jnp.asarray(expand_np, BF16)

    x, xw, ssq = _prep(x_prompt.reshape(n_prompt, d), x_sample.reshape(ns, d), ffn1_norm[0])
    t_all = n_prompt + ns
    w_in_t = jnp.swapaxes(w_in, 1, 2)
    ssm5 = state_ssm.reshape(depth, ns, n_pairs, LANES, M_STATE)

    pc, ps, ph, pg, sc = [], [], [], [], []
    ss = sh = sg = None
    for l in range(depth):
        w_perm = _pack_w_in(w_in_t, l, copies, (n_used, off["gate"]), n_cols)
        mp = {
            "cwx": conv_w[l][:, :mw], "cwb": conv_w[l][:, mw:],
            "cbx": conv_b[l][:mw].reshape(1, mw), "cbb": conv_b[l][mw:].reshape(1, bcw),
            "dtb": _pad_lanes(dt_bias[l]), "alog": _pad_lanes(a_log[l]),
            "dskip": jnp.repeat(d_skip[l], M_HEADDIM).reshape(1, mw),
            "nw": mamba_norm[l].reshape(1, mw),
        }
        wd = jnp.zeros((LANES, gk), F32).at[m_heads:m_heads + G_RANK].set(gla_w_decay[l])
        bd = gla_b_decay[l].reshape(1, gk)
        hnw = hgrn_norm[l].reshape(1, LANES)
        gnw = gla_norm[l].reshape(1, g_dv)

        act, cast = _ffn_up(xw, ssq, ffn1_w_gate_up, l, (ffn1_w_down,))
        w_dn, tn_dn = (cast[0], 512) if cast else (ffn1_w_down, 256)
        x, xw, ssq = _resid_matmul(act, w_dn, l, x, 0.5, tn_dn, "ffn_down", mix_norm[l])
        proj, cast = _in_proj(xw, ssq, w_perm, l, (w_branch_mamba, w_branch_hgrn, w_branch_gla, w_out))
        w_bm, w_bh, w_bg, w_o = cast if cast else (w_branch_mamba, w_branch_hgrn, w_branch_gla, w_out)

        moffs = (oz, ox, obc, osm)
        hoffs = (off["hq"], off["hf"], off["hi"], off["hg"])
        goffs = (off["gq"], off["gk"], off["gv"], off["gg"], osm)
        if l == 0:
            ym, yh, yg = (jnp.zeros((t_all, w), BF16) for w in (mw, hw, gw))
        ym, c1, s1 = _mamba_prompt(proj, moffs, mp, mconsts, ym, nseq=nseq, nchunks=nchunks)
        yh, h1 = _gla_prompt(proj, hoffs, (hgrn_lb_logits,), hnw, consts, yh,
                             hgrn=True, nseq=nseq, nchunks=nchunks, heads=h_heads, dv=LANES, layer=l,
                             hp=h_heads)
        yg, g1 = _gla_prompt(proj, goffs, (wd, bd), gnw, consts, yg,
                             hgrn=False, nseq=nseq, nchunks=nchunks, heads=g_heads, dv=g_dv, layer=l)
        ym, c2, ss = _mamba_sample(proj, state_conv[l], ssm5, ss, ym, n_prompt, moffs, mp, expand, l)
        yh, sh = _gla_sample(proj, state_hgrn, sh, yh, n_prompt, hoffs, (hgrn_lb_logits,), hnw,
                             hgrn=True, heads=h_heads, dv=LANES, layer=l)
        yg, sg = _gla_sample(proj, state_gla, sg, yg, n_prompt, goffs, (wd, bd), gnw,
                             hgrn=False, heads=g_heads, dv=g_dv, layer=l)
        merged = _merge(ym, yh, yg, w_bm, w_bh, w_bg, l, proj, off["gate"])
        x, xw, ssq = _resid_matmul(merged, w_o, l, x, 1.0, 512, "out_proj", ffn2_norm[l], WIDE_TM_CAP)

        act, cast = _ffn_up(xw, ssq, ffn2_w_gate_up, l, (ffn2_w_down,))
        w_dn, tn_dn = (cast[0], 512) if cast else (ffn2_w_down, 256)
        if l + 1 < depth:
            x, xw, ssq = _resid_matmul(act, w_dn, l, x, 0.5, tn_dn, "ffn_down", ffn1_norm[l + 1])
        else:
            x = _resid_matmul(act, w_dn, l, x, 0.5, tn_dn, "ffn_down")

        pc.append(c1)
        ps.append(s1.reshape(nseq, m_heads, M_HEADDIM, M_STATE))
        ph.append(h1)
        pg.append(g1)
        sc.append(c2)

    y_prompt = _final_norm(x, final_norm, 0, n_prompt, "final_norm_prompt").reshape(nseq, seq, d)
    y_sample = _final_norm(x, final_norm, n_prompt, ns, "final_norm_sample").reshape(ns, 1, d)
    return (y_prompt, y_sample, jnp.stack(pc), jnp.stack(ps), jnp.stack(ph), jnp.stack(pg),
            jnp.stack(sc), ss.reshape(state_ssm.shape), sh, sg)
```

```python
import functools

import jax
import jax.numpy as jnp
import numpy as np
from jax import lax
from jax.experimental import pallas as pl
from jax.experimental.pallas import tpu as pltpu

F32 = jnp.float32
BF16 = jnp.bfloat16
EPS = 1e-6
F32_TINY = float(np.finfo(np.float32).tiny)

CHUNK = 256
LANES = 128
SUBLANES = 8
M_HEADDIM = 64
M_STATE = 128
M_GROUPS = 2
M_CONV = 4
G_RANK = 16
G_NORMALIZER = 16.0
N_BRANCH = 3
GLA_SUB = 128
VMEM_LIMIT = 56 * 1024 * 1024
STATE_BLOCK_BYTES = 4 * 1024 * 1024
BF16_ROWS = 16
TM_CAP = 1100
WIDE_TM_CAP = 2100
MAIN_TN = 1024
PROJ_TN = 512
F32_WEIGHT_TN = 256


def _cparams(n_axes):
    return pltpu.CompilerParams(dimension_semantics=("arbitrary",) * n_axes,
                                vmem_limit_bytes=VMEM_LIMIT)


def _dot(a, b):
    return jnp.dot(a, b, preferred_element_type=F32)


def _dot_nt(a, b):
    return lax.dot_general(a, b, (((1,), (1,)), ((), ())), preferred_element_type=F32)


def _dot_tn(a, b):
    return lax.dot_general(a, b, (((0,), (0,)), ((), ())), preferred_element_type=F32)


def _split3(x):
    hi = x.astype(BF16)
    r = x - hi.astype(F32)
    mid = r.astype(BF16)
    lo = (r - mid.astype(F32)).astype(BF16)
    return hi, mid, lo


def _dot_exact_lhs(p_bf16, x):
    hi, mid, lo = _split3(x)
    return _dot(p_bf16, hi) + _dot(p_bf16, mid) + _dot(p_bf16, lo)


def _dot_exact_rhs(x, p_bf16):
    hi, mid, lo = _split3(x)
    return _dot(hi, p_bf16) + _dot(mid, p_bf16) + _dot(lo, p_bf16)


def _dot_f32(a, b):
    ah, am, _ = _split3(a)
    bh, bm, _ = _split3(b)
    return _dot(ah, bh) + _dot(ah, bm) + _dot(am, bh)


def _sigmoid(x):
    return jax.nn.sigmoid(x)


def _silu(x):
    return x * _sigmoid(x)


def _softplus(x):
    return jnp.maximum(x, 0.0) + jnp.log1p(jnp.exp(-jnp.abs(x)))


def _log_sigmoid(x):
    return jnp.minimum(x, 0.0) - jnp.log(1.0 + jnp.exp(-jnp.abs(x)))


def _rms(x, w):
    ms = jnp.mean(x * x, axis=-1, keepdims=True)
    return x * lax.rsqrt(ms + EPS) * w


def _pick_tm(t, cap=TM_CAP):
    best = BF16_ROWS
    for tm in range(BF16_ROWS, min(t, cap) + 1, BF16_ROWS):
        if t % tm == 0:
            best = tm
    return best


def _row_scale(ssq_ref, d):
    return lax.rsqrt(ssq_ref[:, 0:1] * (1.0 / d) + EPS)


def _emit_normed(xn, nw_ref, xw_ref, ssq_ref, accumulate):
    xw_ref[...] = (xn * nw_ref[...]).astype(BF16)
    part = jnp.broadcast_to(jnp.sum(xn * xn, axis=1, keepdims=True), ssq_ref.shape)
    ssq_ref[...] = ssq_ref[...] + part if accumulate else part


def _prep_kernel(xp_ref, xs_ref, nw_ref, x_ref, xw_ref, ssq_ref, *, n_prompt_blocks):
    x = jnp.where(pl.program_id(0) < n_prompt_blocks, xp_ref[...], xs_ref[...])
    x_ref[...] = x
    _emit_normed(x, nw_ref, xw_ref, ssq_ref, accumulate=False)


def _prep(xp, xs, nw):
    n_prompt, d = xp.shape
    rb = xs.shape[0]
    assert n_prompt % rb == 0
    npb = n_prompt // rb
    t = n_prompt + rb
    row = pl.BlockSpec((rb, d), lambda i: (i, 0))
    return pl.pallas_call(
        functools.partial(_prep_kernel, n_prompt_blocks=npb),
        grid=(npb + 1,),
        in_specs=[pl.BlockSpec((rb, d), lambda i: (jnp.minimum(i, npb - 1), 0)),
                  pl.BlockSpec((rb, d), lambda i: (0, 0)),
                  pl.BlockSpec((1, d), lambda i: (0, 0))],
        out_specs=[row, row, pl.BlockSpec((rb, LANES), lambda i: (i, 0))],
        out_shape=[jax.ShapeDtypeStruct((t, d), F32), jax.ShapeDtypeStruct((t, d), BF16),
                   jax.ShapeDtypeStruct((t, LANES), F32)],
        compiler_params=_cparams(1),
        name="prep",
    )(xp, xs, nw.reshape(1, d))


def _row_parts(tm):
    half = -(-(tm // 2) // BF16_ROWS) * BF16_ROWS
    return ((0, half), (half, tm - half)) if 0 < half < tm else ((0, tm),)


def _side_casts(ws, layer, n_i, nj):
    steps = n_i * nj
    ins, outs, shapes = [], [], []
    for w in ws:
        _, k, d = w.shape
        rows = k // steps
        if rows * steps != k or rows % BF16_ROWS:
            return None
        ins.append(pl.BlockSpec((None, rows, d), lambda i, j: (layer, i * nj + j, 0)))
        outs.append(pl.BlockSpec((rows, d), lambda i, j: (i * nj + j, 0)))
        shapes.append(jax.ShapeDtypeStruct((k, d), BF16))
    return ins, outs, shapes


def _do_side_casts(refs):
    n = len(refs) // 2
    for src, dst in zip(refs[:n], refs[n:]):
        dst[...] = src[...].astype(BF16)


def _ffn_up_kernel(xw_ref, ssq_ref, wg_ref, wu_ref, *rest, parts, n_cast):
    o_ref = rest[n_cast]
    _do_side_casts(rest[:n_cast] + rest[n_cast + 1:])
    wg = wg_ref[...].astype(BF16)
    wu = wu_ref[...].astype(BF16)
    d = xw_ref.shape[1]
    for start, rows in parts:
        sl = pl.ds(start, rows)
        r = lax.rsqrt(ssq_ref[sl, 0:1] * (1.0 / d) + EPS)
        h = xw_ref[sl, :]
        g = r * _dot(h, wg)
        u = r * _dot(h, wu)
        o_ref[sl, :] = (_silu(g) * u).astype(BF16)


def _ffn_up(xw, ssq, w_gu, layer, cast_ws=(), tn=PROJ_TN):
    t, d = xw.shape
    dff = w_gu.shape[2] // 2
    tm = _pick_tm(t, WIDE_TM_CAP)
    nj = dff // tn
    side = _side_casts(cast_ws, layer, t // tm, nj) if cast_ws else None
    c_in, c_out, c_shape = side if side else ([], [], [])
    outs = pl.pallas_call(
        functools.partial(_ffn_up_kernel, parts=_row_parts(tm), n_cast=len(c_in)),
        grid=(t // tm, nj),
        in_specs=[pl.BlockSpec((tm, d), lambda i, j: (i, 0)),
                  pl.BlockSpec((tm, LANES), lambda i, j: (i, 0)),
                  pl.BlockSpec((None, d, tn), lambda i, j: (layer, 0, j)),
                  pl.BlockSpec((None, d, tn), lambda i, j: (layer, 0, j + nj))] + c_in,
        out_specs=[pl.BlockSpec((tm, tn), lambda i, j: (i, j))] + c_out,
        out_shape=[jax.ShapeDtypeStruct((t, dff), BF16)] + c_shape,
        compiler_params=_cparams(2),
        name="ffn_up",
    )(xw, ssq, w_gu, w_gu, *(cast_ws if side else ()))
    return outs[0], (tuple(outs[1:]) if side else None)


def _resid_matmul_kernel(a_ref, w_ref, x_ref, *rest, scale, emit, parts):
    if emit:
        nw_ref, o_ref, xw_ref, ssq_ref = rest

        @pl.when(pl.program_id(1) == 0)
        def _():
            ssq_ref[...] = jnp.zeros_like(ssq_ref)
    else:
        (o_ref,) = rest
    wb = w_ref[...].astype(BF16)
    for start, rows in parts:
        sl = pl.ds(start, rows)
        xn = x_ref[sl, :] + scale * _dot(a_ref[sl, :], wb)
        if emit:
            _emit_normed(xn, nw_ref, xw_ref.at[sl, :], ssq_ref.at[sl, :], accumulate=True)
        o_ref[sl, :] = xn


def _wspec(w, layer, tn):
    if w.ndim == 3:
        return pl.BlockSpec((None, w.shape[1], tn), lambda i, j: (layer, 0, j))
    return pl.BlockSpec((w.shape[0], tn), lambda i, j: (0, j))


def _resid_matmul(a, w, layer, x, scale, tn, name, next_nw=None, tm_cap=TM_CAP):
    t, k = a.shape
    d = w.shape[-1]
    tm = _pick_tm(t, tm_cap)
    emit = next_nw is not None
    tile = pl.BlockSpec((tm, tn), lambda i, j: (i, j))
    in_specs = [pl.BlockSpec((tm, k), lambda i, j: (i, 0)), _wspec(w, layer, tn), tile]
    args = [a, w, x]
    out_specs = [tile]
    out_shape = [jax.ShapeDtypeStruct((t, d), F32)]
    if emit:
        in_specs.append(pl.BlockSpec((1, tn), lambda i, j: (0, j)))
        args.append(next_nw.reshape(1, d))
        out_specs += [tile, pl.BlockSpec((tm, LANES), lambda i, j: (i, 0))]
        out_shape += [jax.ShapeDtypeStruct((t, d), BF16), jax.ShapeDtypeStruct((t, LANES), F32)]
    outs = pl.pallas_call(
        functools.partial(_resid_matmul_kernel, scale=scale, emit=emit,
                          parts=_row_parts(tm) if k <= d else ((0, tm),)),
        grid=(t // tm, d // tn),
        in_specs=in_specs,
        out_specs=out_specs,
        out_shape=out_shape,
        compiler_params=_cparams(2),
        name=name,
    )(*args)
    return outs if emit else outs[0]


def _pack_w_in_kernel(w_ref, o_ref, *, copies, zero):
    for src, dst, width in copies:
        if width % LANES == 0:
            o_ref[:, dst:dst + width] = w_ref[src:src + width, :].T.astype(BF16)
        else:
            base = src // LANES * LANES
            lo = src - base
            assert dst % LANES == lo and lo + width <= LANES
            t = w_ref[base:base + LANES, :].T
            o_ref[:, dst:dst + width] = t[:, lo:lo + width].astype(BF16)
    start, stop = zero
    if stop > start:
        o_ref[:, start:stop] = jnp.zeros((o_ref.shape[0], stop - start), BF16)


def _pack_w_in(w_t, layer, copies, zero, n_cols, tk=LANES):
    _, n_src, d = w_t.shape
    return pl.pallas_call(
        functools.partial(_pack_w_in_kernel, copies=copies, zero=zero),
        grid=(d // tk,),
        in_specs=[pl.BlockSpec((None, n_src, tk), lambda i: (layer, 0, i))],
        out_specs=pl.BlockSpec((tk, n_cols), lambda i: (i, 0)),
        out_shape=jax.ShapeDtypeStruct((d, n_cols), BF16),
        compiler_params=_cparams(1),
        name="pack_w_in",
    )(w_t)


def _in_proj_kernel(xw_ref, ssq_ref, w_ref, *rest, n_cast):
    o_ref = rest[n_cast]
    _do_side_casts(rest[:n_cast] + rest[n_cast + 1:])
    o_ref[...] = _row_scale(ssq_ref, xw_ref.shape[1]) * _dot(xw_ref[...], w_ref[...])


def _in_proj(xw, ssq, w, layer, cast_ws=(), tn=MAIN_TN):
    t, d = xw.shape
    n = w.shape[1]
    tm = _pick_tm(t, WIDE_TM_CAP)
    side = _side_casts(cast_ws, layer, t // tm, n // tn) if cast_ws else None
    c_in, c_out, c_shape = side if side else ([], [], [])
    outs = pl.pallas_call(
        functools.partial(_in_proj_kernel, n_cast=len(c_in)),
        grid=(t // tm, n // tn),
        in_specs=[pl.BlockSpec((tm, d), lambda i, j: (i, 0)),
                  pl.BlockSpec((tm, LANES), lambda i, j: (i, 0)),
                  pl.BlockSpec((d, tn), lambda i, j: (0, j))] + c_in,
        out_specs=[pl.BlockSpec((tm, tn), lambda i, j: (i, j))] + c_out,
        out_shape=[jax.ShapeDtypeStruct((t, n), F32)] + c_shape,
        compiler_params=_cparams(2),
        name="in_proj",
    )(xw, ssq, w, *(cast_ws if side else ()))
    return outs[0], (tuple(outs[1:]) if side else None)


def _merge_kernel(ym_ref, yh_ref, yg_ref, wm_ref, wh_ref, wg_ref, g0_ref, g1_ref, g2_ref, o_ref, *, parts):
    wm = wm_ref[...].astype(BF16)
    wh = wh_ref[...].astype(BF16)
    wg = wg_ref[...].astype(BF16)
    for start, rows in parts:
        sl = pl.ds(start, rows)
        acc = _sigmoid(g0_ref[sl, :]) * _dot(ym_ref[sl, :], wm)
        acc = acc + _sigmoid(g1_ref[sl, :]) * _dot(yh_ref[sl, :], wh)
        acc = acc + _sigmoid(g2_ref[sl, :]) * _dot(yg_ref[sl, :], wg)
        o_ref[sl, :] = acc.astype(BF16)


def _merge(ym, yh, yg, wm, wh, wg, layer, proj, gate_off, tn=PROJ_TN):
    t, k = ym.shape
    d = wm.shape[-1]
    tm = _pick_tm(t)
    assert gate_off % tn == 0
    gb = gate_off // tn
    nb = d // tn
    yspec = pl.BlockSpec((tm, k), lambda i, j: (i, 0))
    wspec = _wspec(wm, layer, tn)

    def gspec(b):
        return pl.BlockSpec((tm, tn), lambda i, j: (i, gb + b * nb + j))

    return pl.pallas_call(
        functools.partial(_merge_kernel, parts=_row_parts(tm)),
        grid=(t // tm, nb),
        in_specs=[yspec, yspec, yspec, wspec, wspec, wspec, gspec(0), gspec(1), gspec(2)],
        out_specs=pl.BlockSpec((tm, tn), lambda i, j: (i, j)),
        out_shape=jax.ShapeDtypeStruct((t, d), BF16),
        compiler_params=_cparams(2),
        name="merge",
    )(ym, yh, yg, wm, wh, wg, proj, proj, proj)


def _final_norm_kernel(x_ref, nw_ref, o_ref):
    o_ref[...] = _rms(x_ref[...], nw_ref[...])


def _final_norm(x, nw, row0, rows, name):
    d = x.shape[1]
    tm = _pick_tm(rows)
    assert row0 % tm == 0
    return pl.pallas_call(
        _final_norm_kernel,
        grid=(rows // tm,),
        in_specs=[pl.BlockSpec((tm, d), lambda i: (row0 // tm + i, 0)),
                  pl.BlockSpec((1, d), lambda i: (0, 0))],
        out_specs=pl.BlockSpec((tm, d), lambda i: (i, 0)),
        out_shape=jax.ShapeDtypeStruct((rows, d), F32),
        compiler_params=_cparams(1),
        name=name,
    )(x, nw.reshape(1, d))


def _scan_constants(c):
    N_LEVELS = int(np.log2(c))
    assert 1 << N_LEVELS == c and N_LEVELS >= 3
    pm = np.zeros((2, c, c), np.float32)
    sg = np.zeros((N_LEVELS - 1, c, LANES), np.float32)
    mk = np.zeros((N_LEVELS + 1, c, c), np.float32)
    r = np.arange(c)
    for lvl in range(N_LEVELS):
        h = c >> (lvl + 1)
        for t in range(c):
            blk, pos = divmod(t, 2 * h)
            ridx = blk * 2 * h + h - 1
            upper = pos >= h
            if upper:
                mk[lvl, t] = ((r // (2 * h)) == blk) & ((r % (2 * h)) < h)
            if h >= 4:
                sg[lvl, t] = 1.0 if upper else -1.0
            elif h == 2:
                pm[0, t] = ((r > ridx) & (r <= t)) if upper else ((r > t) & (r <= ridx))
            else:
                sg[N_LEVELS - 2, t] = 1.0 if upper else 0.0
    pm[1] = r[None, :] <= r[:, None]
    mk[N_LEVELS] = np.eye(c)
    return jnp.asarray(pm.reshape(2 * c, c), BF16), jnp.asarray(sg), jnp.asarray(mk)


def _lower_bound(logits, layer):
    m = jnp.max(logits, axis=0, keepdims=True)
    e = jnp.exp(logits - m)
    sm = e / jnp.sum(e, axis=0, keepdims=True)
    lb = jnp.zeros_like(m)
    for i in range(1, layer + 1):
        lb = lb + sm[i:i + 1, :]
    return lb


def _gla_inputs(refs, hgrn, layer):
    if hgrn:
        q_ref, f_ref, i_ref, lbl_ref = refs
        lb = _lower_bound(lbl_ref[...], layer)
        q = _silu(q_ref[...]) * (LANES ** -0.5)
        zf = f_ref[...]
        a = (1.0 - lb) * _sigmoid(zf)
        f = lb + a
        gl = jnp.log(jnp.maximum(f, F32_TINY))
        k = (1.0 - lb) - a
        v = i_ref[...]
    else:
        q_ref, k_ref, v_ref, a_ref, wd_ref, bd_ref = refs
        q = q_ref[...] * (LANES ** -0.5)
        k = k_ref[...]
        v = v_ref[...]
        gl = _log_sigmoid(_dot_f32(a_ref[...], wd_ref[...]) + bd_ref[...]) / G_NORMALIZER
    return q, k, v, gl


def _level_factor(lvl, gl, cum, z_h2, sg_ref):
    c = gl.shape[0]
    n_levels = sg_ref.shape[0] + 1
    h = c >> (lvl + 1)
    if h >= 4:
        c3 = cum.reshape(c // (2 * h), 2 * h, LANES)
        d = (c3 - c3[:, h - 1:h, :]).reshape(c, LANES)
        return jnp.exp(sg_ref[lvl] * d)
    if h == 2:
        return jnp.exp(z_h2)
    return jnp.exp(sg_ref[n_levels - 2] * gl)


def _gla_prompt_kernel(*refs, hgrn, layer, hp, dv):
    n_in = 4 if hgrn else 6
    in_refs = refs[:n_in]
    g_ref, nw_ref, pm_ref, sg_ref, mk_ref, _, y_ref, so_ref, s_scr = refs[n_in:]
    c = pl.program_id(2)
    sub = mk_ref.shape[1]
    n_levels = mk_ref.shape[0] - 1

    @pl.when(c == 0)
    def _():
        s_scr[...] = jnp.zeros_like(s_scr)

    q_all, k_all, v_all, gl_all = _gla_inputs(in_refs, hgrn, layer)
    finals = []
    for u in range(hp):
        s = s_scr[u]
        for r0 in range(0, CHUNK, sub):
            rows = pl.ds(r0, sub)
            q, k, gl = (a[r0:r0 + sub, u * LANES:(u + 1) * LANES] for a in (q_all, k_all, gl_all))
            v = v_all[r0:r0 + sub, u * dv:(u + 1) * dv]
            zz = _dot_exact_lhs(pm_ref[...], gl)
            z_h2, cum = zz[:sub], zz[sub:]
            qb, kb = q.astype(BF16), k.astype(BF16)
            att = mk_ref[n_levels] * _dot_nt(qb, kb)
            for lvl in range(n_levels):
                eb = _level_factor(lvl, gl, cum, z_h2, sg_ref).astype(BF16)
                att = att + mk_ref[lvl] * _dot_nt(qb * eb, kb * eb)
            e_cum = jnp.exp(cum)
            e_tail = jnp.exp(cum[sub - 1:sub, :] - cum)
            vb = v.astype(BF16)
            o = _dot(att.astype(BF16), vb) + _dot((q * e_cum).astype(BF16), s.astype(BF16))
            e_last = e_cum.T[:, sub - 1:sub]
            s = e_last * s + _dot_tn((k * e_tail).astype(BF16), vb)
            gate = g_ref[rows, u * dv:(u + 1) * dv]
            y_ref[rows, u * dv:(u + 1) * dv] = (_rms(o, nw_ref[...]) * _silu(gate)).astype(BF16)
        s_scr[u] = s
        finals.append(s)

    @pl.when(c == pl.num_programs(2) - 1)
    def _():
        for u in range(hp):
            so_ref[u] = finals[u]


def _gla_prompt(proj, offs, extra, nw, consts, y_init, *, hgrn, nseq, nchunks, heads, dv, layer, hp):
    pm, sg, mk = consts
    assert heads % hp == 0

    def col(off, w):
        assert off % (hp * w) == 0
        return pl.BlockSpec((CHUNK, hp * w), lambda h, b, c: (b * nchunks + c, off // (hp * w) + h))

    def full(a):
        nd = a.ndim
        return pl.BlockSpec(a.shape, lambda h, b, c: (0,) * nd)

    if hgrn:
        oq, of, oi, og = offs
        (lbl,) = extra
        in_specs = [col(oq, LANES), col(of, LANES), col(oi, dv),
                    pl.BlockSpec((lbl.shape[0], hp * LANES), lambda h, b, c: (0, h))]
        args = [proj, proj, proj, lbl]
    else:
        oq, ok, ov, og, oa = offs
        wd, bd = extra
        in_specs = [col(oq, LANES), col(ok, LANES), col(ov, dv),
                    pl.BlockSpec((CHUNK, LANES), lambda h, b, c: (b * nchunks + c, oa // LANES)),
                    pl.BlockSpec((LANES, hp * LANES), lambda h, b, c: (0, h)),
                    pl.BlockSpec((1, hp * LANES), lambda h, b, c: (0, h))]
        args = [proj, proj, proj, proj, wd, bd]
    in_specs += [col(og, dv), full(nw), full(pm), full(sg), full(mk), pl.BlockSpec(memory_space=pl.ANY)]
    args += [proj, nw, pm, sg, mk, y_init]
    y, s = pl.pallas_call(
        functools.partial(_gla_prompt_kernel, hgrn=hgrn, layer=layer, hp=hp, dv=dv),
        grid=(heads // hp, nseq, nchunks),
        in_specs=in_specs,
        out_specs=[pl.BlockSpec((CHUNK, hp * dv), lambda h, b, c: (b * nchunks + c, h)),
                   pl.BlockSpec((None, hp, LANES, dv), lambda h, b, c: (b, h, 0, 0))],
        out_shape=[jax.ShapeDtypeStruct(y_init.shape, BF16),
                   jax.ShapeDtypeStruct((nseq, heads, LANES, dv), F32)],
        input_output_aliases={len(args) - 1: 0},
        scratch_shapes=[pltpu.VMEM((hp, LANES, dv), F32)],
        compiler_params=_cparams(3),
        name="hgrn_prompt" if hgrn else "gla_prompt",
    )(*args)
    return y, s


def _columns(x):
    nb = x.shape[0]
    if nb < LANES:
        x = jnp.concatenate([x, jnp.zeros((LANES - nb, x.shape[1]), x.dtype)], axis=0)
    return x.T


def _store_state(so_ref, layer, b, s_new, first):
    if first:
        for l in range(so_ref.shape[0]):
            so_ref[l, b] = s_new if l == layer else jnp.zeros_like(s_new)
    else:
        so_ref[b] = s_new


def _gla_sample_kernel(*refs, hgrn, nb, layer, first):
    n_in = 4 if hgrn else 6
    in_refs = refs[:n_in]
    g_ref, nw_ref, s_ref = refs[n_in:n_in + 3]
    y_ref, so_ref = refs[-2:]
    q, k, v, gl = _gla_inputs(in_refs, hgrn, layer)
    e_t = _columns(jnp.exp(gl))
    k_t = _columns(k)
    qb = q.astype(BF16)
    rowid = lax.broadcasted_iota(jnp.int32, v.shape, 0)
    o = jnp.zeros(v.shape, F32)
    for b in range(nb):
        s_new = e_t[:, b:b + 1] * s_ref[b] + k_t[:, b:b + 1] * v[b:b + 1, :]
        _store_state(so_ref, layer, b, s_new, first)
        o = jnp.where(rowid == b, _dot(qb, s_new.astype(BF16)), o)
    y_ref[...] = (_rms(o, nw_ref[...]) * _silu(g_ref[...])).astype(BF16)


def _state_specs(depth, layer, nb, dv, first):
    ispec = pl.BlockSpec((None, nb, None, LANES, dv), lambda h, i: (layer, i, h, 0, 0))
    if first:
        ospec = pl.BlockSpec((depth, nb, None, LANES, dv), lambda h, i: (0, i, h, 0, 0))
    else:
        ospec = ispec
    return ispec, ospec


def _sample_batch_block(ns, dv):
    nb = min(ns, STATE_BLOCK_BYTES // (LANES * dv * 4))
    assert ns % nb == 0
    return nb


def _gla_sample(proj, state, prev, y_all, row0, offs, extra, nw, *, hgrn, heads, dv, layer):
    depth, ns = state.shape[:2]
    nb = _sample_batch_block(ns, dv)
    rb = row0 // nb
    first = prev is None

    def col(off, w):
        return pl.BlockSpec((nb, w), lambda h, i: (rb + i, off // w + h))

    if hgrn:
        oq, of, oi, og = offs
        (lbl,) = extra
        in_specs = [col(oq, LANES), col(of, LANES), col(oi, dv),
                    pl.BlockSpec((lbl.shape[0], LANES), lambda h, i: (0, h))]
        args = [proj, proj, proj, lbl]
    else:
        oq, ok, ov, og, oa = offs
        wd, bd = extra
        in_specs = [col(oq, LANES), col(ok, LANES), col(ov, dv),
                    pl.BlockSpec((nb, LANES), lambda h, i: (rb + i, oa // LANES)),
                    pl.BlockSpec((LANES, LANES), lambda h, i: (0, h)),
                    pl.BlockSpec((1, LANES), lambda h, i: (0, h))]
        args = [proj, proj, proj, proj, wd, bd]
    ispec, ospec = _state_specs(depth, layer, nb, dv, first)
    in_specs += [col(og, dv), pl.BlockSpec(nw.shape, lambda h, i: (0, 0)), ispec,
                 pl.BlockSpec(memory_space=pl.ANY)]
    args += [proj, nw, state, y_all]
    aliases = {len(args) - 1: 0}
    if not first:
        in_specs.append(pl.BlockSpec(memory_space=pl.ANY))
        args.append(prev)
        aliases[len(args) - 1] = 1
    y, s = pl.pallas_call(
        functools.partial(_gla_sample_kernel, hgrn=hgrn, nb=nb, layer=layer, first=first),
        grid=(heads, ns // nb),
        in_specs=in_specs,
        out_specs=[pl.BlockSpec((nb, dv), lambda h, i: (rb + i, h)), ospec],
        out_shape=[jax.ShapeDtypeStruct(y_all.shape, BF16),
                   jax.ShapeDtypeStruct(state.shape, F32)],
        input_output_aliases=aliases,
        compiler_params=_cparams(2),
        name="hgrn_sample" if hgrn else "gla_sample",
    )(*args)
    return y, s


def _mamba_post(y, xs, z, dskip, nw):
    y = (y + dskip * xs) * _silu(z)
    gs = y.shape[1] // M_GROUPS
    outs = [_rms(y[:, g * gs:(g + 1) * gs], nw[:, g * gs:(g + 1) * gs]) for g in range(M_GROUPS)]
    return jnp.concatenate(outs, axis=1)


def _mamba_prompt_kernel(z_ref, x_ref, bc_ref, sm_ref, cwx_ref, cwb_ref, cbx_ref, cbb_ref,
                         dtb_ref, alog_ref, dsk_ref, nw_ref, tril_ref, mask_ref, _,
                         y_ref, cox_ref, cob_ref, so_ref, ex_scr, eb_scr, s_scr):
    c = pl.program_id(1)
    nc = pl.num_programs(1)
    tail = SUBLANES

    @pl.when(c == 0)
    def _():
        ex_scr[0:tail, :] = jnp.zeros((tail, ex_scr.shape[1]), F32)
        eb_scr[0:tail, :] = jnp.zeros((tail, eb_scr.shape[1]), F32)
        s_scr[...] = jnp.zeros_like(s_scr)

    ex_scr[tail:tail + CHUNK, :] = x_ref[...]
    eb_scr[tail:tail + CHUNK, :] = bc_ref[...]

    def conv(scr, cw_ref, cb_ref):
        xe = scr[...]
        acc = cw_ref[0:1, :] * xe
        for w in range(1, M_CONV):
            acc = pltpu.roll(acc, 1, axis=0) + cw_ref[w:w + 1, :] * xe
        return _silu(acc[tail:, :] + cb_ref[...])

    xs = conv(ex_scr, cwx_ref, cbx_ref)
    bcm = conv(eb_scr, cwb_ref, cbb_ref)

    ex_scr[0:tail, :] = ex_scr[CHUNK:CHUNK + tail, :]
    eb_scr[0:tail, :] = eb_scr[CHUNK:CHUNK + tail, :]

    dt = _softplus(sm_ref[...] + dtb_ref[...])
    a = -jnp.exp(alog_ref[...])
    cum = _dot_exact_lhs(tril_ref[...], dt * a)
    cum_t = cum.T
    cl = cum[CHUNK - 1:CHUNK, :]
    mask = mask_ref[...] > 0.5
    gw = M_GROUPS * M_STATE
    lo = lax.broadcasted_iota(jnp.int32, (CHUNK, LANES), 1) < M_HEADDIM
    lo_r = lax.broadcasted_iota(jnp.int32, (LANES, LANES), 0) < M_HEADDIM
    n_pairs = xs.shape[1] // LANES
    per_group = n_pairs // M_GROUPS
    ys = []
    for j in range(n_pairs):
        g = j // per_group
        h0, h1 = 2 * j, 2 * j + 1
        if j % per_group == 0:
            bg = bcm[:, g * M_STATE:(g + 1) * M_STATE].astype(BF16)
            cg = bcm[:, gw + g * M_STATE:gw + (g + 1) * M_STATE].astype(BF16)
            cb = jnp.where(mask, _dot_nt(cg, bg), 0.0)

        def dec(h):
            return jnp.exp(jnp.minimum(cum[:, h:h + 1] - cum_t[h:h + 1, :], 0.0))

        x2 = xs[:, j * LANES:(j + 1) * LANES]
        dt2 = jnp.where(lo, dt[:, h0:h0 + 1], dt[:, h1:h1 + 1])
        xdt = (x2 * dt2).astype(BF16)
        y_in = jnp.where(lo, _dot((cb * dec(h0)).astype(BF16), xdt),
                         _dot((cb * dec(h1)).astype(BF16), xdt))
        ec2 = jnp.where(lo, jnp.exp(cum[:, h0:h0 + 1]), jnp.exp(cum[:, h1:h1 + 1]))
        s = s_scr[j]
        ys.append(y_in + _dot_nt(cg, s.astype(BF16)) * ec2)
        w2 = dt2 * jnp.where(lo, jnp.exp(cl[:, h0:h0 + 1] - cum[:, h0:h0 + 1]),
                             jnp.exp(cl[:, h1:h1 + 1] - cum[:, h1:h1 + 1]))
        el2 = jnp.where(lo_r, jnp.exp(cl[:, h0:h0 + 1]), jnp.exp(cl[:, h1:h1 + 1]))
        s_scr[j] = el2 * s + _dot_tn((x2 * w2).astype(BF16), bg)
    y = jnp.concatenate(ys, axis=1)
    y_ref[...] = _mamba_post(y, xs, z_ref[...], dsk_ref[...], nw_ref[...]).astype(BF16)

    @pl.when(c == nc - 1)
    def _():
        so_ref[...] = s_scr[...]
        cox_ref[...] = ex_scr[tail + CHUNK - (M_CONV - 1):tail + CHUNK, :]
        cob_ref[...] = eb_scr[tail + CHUNK - (M_CONV - 1):tail + CHUNK, :]


def _mamba_prompt(proj, offs, p, consts, y_init, *, nseq, nchunks):
    oz, ox, obc, osm = offs
    mw = p["dskip"].shape[1]
    bcw = p["cwb"].shape[1]
    n_pairs = mw // LANES

    def col(off, w):
        return pl.BlockSpec((CHUNK, w), lambda b, c: (b * nchunks + c, off // w))

    def full(a):
        nd = a.ndim
        return pl.BlockSpec(a.shape, lambda b, c: (0,) * nd)

    small = [p["cwx"], p["cwb"], p["cbx"], p["cbb"], p["dtb"], p["alog"], p["dskip"], p["nw"],
             consts[0], consts[1]]
    y, cox, cob, s = pl.pallas_call(
        _mamba_prompt_kernel,
        grid=(nseq, nchunks),
        in_specs=[col(oz, mw), col(ox, mw), col(obc, bcw), col(osm, LANES)] + [full(a) for a in small]
        + [pl.BlockSpec(memory_space=pl.ANY)],
        out_specs=[pl.BlockSpec((CHUNK, mw), lambda b, c: (b * nchunks + c, 0)),
                   pl.BlockSpec((None, M_CONV - 1, mw), lambda b, c: (b, 0, 0)),
                   pl.BlockSpec((None, M_CONV - 1, bcw), lambda b, c: (b, 0, 0)),
                   pl.BlockSpec((None, n_pairs, LANES, M_STATE), lambda b, c: (b, 0, 0, 0))],
        input_output_aliases={4 + len(small): 0},
        out_shape=[jax.ShapeDtypeStruct(y_init.shape, BF16),
                   jax.ShapeDtypeStruct((nseq, M_CONV - 1, mw), F32),
                   jax.ShapeDtypeStruct((nseq, M_CONV - 1, bcw), F32),
                   jax.ShapeDtypeStruct((nseq, n_pairs, LANES, M_STATE), F32)],
        scratch_shapes=[pltpu.VMEM((CHUNK + SUBLANES, mw), F32), pltpu.VMEM((CHUNK + SUBLANES, bcw), F32),
                        pltpu.VMEM((n_pairs, LANES, M_STATE), F32)],
        compiler_params=_cparams(2),
        name="mamba_prompt",
    )(proj, proj, proj, proj, *small, y_init)
    return y, jnp.concatenate([cox, cob], axis=-1), s


def _mamba_sample_prep_kernel(cs_ref, x_ref, bc_ref, sm_ref, cw_ref, cb_ref, dtb_ref, alog_ref, exp_ref,
                              co_ref, act_ref, dte_ref, ee_ref):
    cd = cw_ref.shape[1]
    new = jnp.concatenate([x_ref[...], bc_ref[...]], axis=1)
    acc = cb_ref[...] + cw_ref[M_CONV - 1:M_CONV, :] * new
    for w in range(M_CONV - 1):
        acc = acc + cw_ref[w:w + 1, :] * cs_ref[:, w * cd:(w + 1) * cd]
    act_ref[...] = _silu(acc)
    for w in range(1, M_CONV - 1):
        co_ref[:, (w - 1) * cd:w * cd] = cs_ref[:, w * cd:(w + 1) * cd]
    co_ref[:, (M_CONV - 2) * cd:(M_CONV - 1) * cd] = new
    dt = _softplus(sm_ref[...] + dtb_ref[...])
    a = -jnp.exp(alog_ref[...])
    dte_ref[...] = _dot_exact_rhs(dt, exp_ref[...])
    ee_ref[...] = jnp.exp(_dot_exact_rhs(dt * a, exp_ref[...]))


def _mamba_sample_state_kernel(x_ref, b_ref, c_ref, dte_ref, ee_ref, s_ref, *out_refs, nb, layer, first):
    y_ref, so_ref = out_refs[-2:]
    xdt_t = _columns(x_ref[...] * dte_ref[...])
    e_t = _columns(ee_ref[...])
    bv = b_ref[...]
    cb = c_ref[...].astype(BF16)
    rowid = lax.broadcasted_iota(jnp.int32, (nb, LANES), 0)
    y = jnp.zeros((nb, LANES), F32)
    for b in range(nb):
        s_new = e_t[:, b:b + 1] * s_ref[b] + xdt_t[:, b:b + 1] * bv[b:b + 1, :]
        _store_state(so_ref, layer, b, s_new, first)
        y = jnp.where(rowid == b, _dot_nt(cb, s_new.astype(BF16)), y)
    y_ref[...] = y


def _mamba_sample_post_kernel(y_ref, x_ref, z_ref, dsk_ref, nw_ref, _, o_ref):
    o_ref[...] = _mamba_post(y_ref[...], x_ref[...], z_ref[...], dsk_ref[...], nw_ref[...]).astype(BF16)


def _mamba_sample(proj, conv_state, ssm_state, prev, y_all, row0, offs, p, expand, layer):
    oz, ox, obc, osm = offs
    depth, ns = ssm_state.shape[:2]
    nb = _sample_batch_block(ns, M_STATE)
    mw = p["dskip"].shape[1]
    bcw = p["cwb"].shape[1]
    cd = mw + bcw
    n_pairs = mw // LANES
    per_group = n_pairs // M_GROUPS
    rb = row0 // ns
    first = prev is None
    cw = jnp.concatenate([p["cwx"], p["cwb"]], axis=1)
    cb = jnp.concatenate([p["cbx"], p["cbb"]], axis=1)

    def full1(a):
        nd = a.ndim
        return pl.BlockSpec(a.shape, lambda i: (0,) * nd)

    cs2 = conv_state.reshape(ns, (M_CONV - 1) * cd)
    small = [cw, cb, p["dtb"], p["alog"], expand]
    co, act, dte, ee = pl.pallas_call(
        _mamba_sample_prep_kernel,
        grid=(1,),
        in_specs=[full1(cs2),
                  pl.BlockSpec((ns, mw), lambda i: (rb, ox // mw)),
                  pl.BlockSpec((ns, bcw), lambda i: (rb, obc // bcw)),
                  pl.BlockSpec((ns, LANES), lambda i: (rb, osm // LANES))] + [full1(a) for a in small],
        out_specs=[pl.BlockSpec((ns, (M_CONV - 1) * cd), lambda i: (0, 0)),
                   pl.BlockSpec((ns, cd), lambda i: (0, 0)),
                   pl.BlockSpec((ns, mw), lambda i: (0, 0)),
                   pl.BlockSpec((ns, mw), lambda i: (0, 0))],
        out_shape=[jax.ShapeDtypeStruct((ns, (M_CONV - 1) * cd), F32),
                   jax.ShapeDtypeStruct((ns, cd), F32),
                   jax.ShapeDtypeStruct((ns, mw), F32),
                   jax.ShapeDtypeStruct((ns, mw), F32)],
        compiler_params=_cparams(1),
        name="mamba_sample_prep",
    )(cs2, proj, proj, proj, *small)

    bblk = mw // LANES
    cblk = bblk + M_GROUPS * M_STATE // LANES
    ispec, ospec = _state_specs(depth, layer, nb, M_STATE, first)
    in_specs = [pl.BlockSpec((nb, LANES), lambda j, i: (i, j)),
                pl.BlockSpec((nb, LANES), lambda j, i: (i, bblk + j // per_group)),
                pl.BlockSpec((nb, LANES), lambda j, i: (i, cblk + j // per_group)),
                pl.BlockSpec((nb, LANES), lambda j, i: (i, j)),
                pl.BlockSpec((nb, LANES), lambda j, i: (i, j)),
                ispec]
    args = [act, act, act, dte, ee, ssm_state]
    aliases = {}
    if not first:
        in_specs.append(pl.BlockSpec(memory_space=pl.ANY))
        args.append(prev)
        aliases = {len(args) - 1: 1}
    y, s_new = pl.pallas_call(
        functools.partial(_mamba_sample_state_kernel, nb=nb, layer=layer, first=first),
        grid=(n_pairs, ns // nb),
        in_specs=in_specs,
        out_specs=[pl.BlockSpec((nb, LANES), lambda j, i: (i, j)), ospec],
        out_shape=[jax.ShapeDtypeStruct((ns, mw), F32),
                   jax.ShapeDtypeStruct(ssm_state.shape, F32)],
        input_output_aliases=aliases,
        compiler_params=_cparams(2),
        name="mamba_sample_state",
    )(*args)

    ym = pl.pallas_call(
        _mamba_sample_post_kernel,
        grid=(1,),
        in_specs=[full1(y),
                  pl.BlockSpec((ns, mw), lambda i: (0, 0)),
                  pl.BlockSpec((ns, mw), lambda i: (rb, oz // mw)),
                  full1(p["dskip"]), full1(p["nw"]), pl.BlockSpec(memory_space=pl.ANY)],
        out_specs=pl.BlockSpec((ns, mw), lambda i: (rb, 0)),
        out_shape=jax.ShapeDtypeStruct(y_all.shape, BF16),
        input_output_aliases={5: 0},
        compiler_params=_cparams(1),
        name="mamba_sample_post",
    )(y, act, proj, p["dskip"], p["nw"], y_all)
    return ym, co.reshape(ns, M_CONV - 1, cd), s_new


def _pad_lanes(v, n=LANES):
    v = v.reshape(1, -1)
    return jnp.pad(v, ((0, 0), (0, n - v.shape[1])))


def kernel(x_prompt, x_sample, state_conv, state_ssm, state_hgrn, state_gla, ffn1_norm, ffn1_w_gate_up, ffn1_w_down, mix_norm, w_in, conv_w, conv_b, dt_bias, a_log, d_skip, mamba_norm, hgrn_lb_logits, hgrn_norm, gla_w_decay, gla_b_decay, gla_norm, w_branch_mamba, w_branch_hgrn, w_branch_gla, w_out, ffn2_norm, ffn2_w_gate_up, ffn2_w_down, final_norm):
    nseq, seq, d = x_prompt.shape
    ns = x_sample.shape[0]
    depth = w_in.shape[0]
    nchunks = seq // CHUNK
    n_prompt = nseq * seq
    mw = w_branch_mamba.shape[1]
    hw = w_branch_hgrn.shape[1]
    gw = w_branch_gla.shape[1]
    gk = gla_w_decay.shape[2]
    m_heads = dt_bias.shape[1]
    h_heads = state_hgrn.shape[2]
    g_heads = state_gla.shape[2]
    g_dv = state_gla.shape[4]
    bcw = 2 * M_GROUPS * M_STATE
    n_pairs = mw // LANES
    assert seq % CHUNK == 0 and n_prompt % ns == 0 and ns % BF16_ROWS == 0
    assert m_heads <= G_RANK + m_heads <= LANES and gk // g_heads == LANES and hw // h_heads == LANES

    seg_w = {"z": mw, "xs": mw, "bc": bcw, "dt": m_heads, "hq": hw, "hf": hw, "hi": hw, "hg": hw,
             "gq": gk, "gk": gk, "gv": gw, "gg": gw, "ga": G_RANK, "gate": N_BRANCH * d}
    src_order = ("z", "xs", "bc", "dt", "hq", "hf", "hi", "hg", "gq", "gk", "gv", "gg", "ga", "gate")
    dst_order = ("z", "xs", "hq", "hf", "hi", "hg", "gv", "gg", "gq", "gk", "bc", "dt", "ga")
    src, off = {}, {}
    pos = 0
    for name in src_order:
        src[name] = pos
        pos += seg_w[name]
    pos = 0
    for name in dst_order:
        off[name] = pos
        pos += seg_w[name]
    n_used = pos
    off["gate"] = -(-n_used // MAIN_TN) * MAIN_TN
    n_cols = off["gate"] + seg_w["gate"]
    assert n_cols % MAIN_TN == 0
    copies = tuple((src[name], off[name], seg_w[name]) for name in dst_order + ("gate",))
    oz, ox, obc, osm = off["z"], off["xs"], off["bc"], off["dt"]
    assert off["ga"] == osm + m_heads and osm % LANES == 0

    consts = _scan_constants(GLA_SUB)
    tril_np = np.tril(np.ones((CHUNK, CHUNK), np.float32))
    mconsts = (jnp.asarray(tril_np, BF16), jnp.asarray(tril_np))
    expand_np = np.zeros((LANES, mw), np.float32)
    for h in range(m_heads):
        expand_np[h, h * M_HEADDIM:(h + 1) * M_HEADDIM] = 1.0
    expand = jnp.asarray(expand_np, BF16)

    x, xw, ssq = _prep(x_prompt.reshape(n_prompt, d), x_sample.reshape(ns, d), ffn1_norm[0])
    t_all = n_prompt + ns
    w_in_t = jnp.swapaxes(w_in, 1, 2)
    ssm5 = state_ssm.reshape(depth, ns, n_pairs, LANES, M_STATE)

    pc, ps, ph, pg, sc = [], [], [], [], []
    ss = sh = sg = None
    for l in range(depth):
        w_perm = _pack_w_in(w_in_t, l, copies, (n_used, off["gate"]), n_cols)
        mp = {
            "cwx": conv_w[l][:, :mw], "cwb": conv_w[l][:, mw:],
            "cbx": conv_b[l][:mw].reshape(1, mw), "cbb": conv_b[l][mw:].reshape(1, bcw),
            "dtb": _pad_lanes(dt_bias[l]), "alog": _pad_lanes(a_log[l]),
            "dskip": jnp.repeat(d_skip[l], M_HEADDIM).reshape(1, mw),
            "nw": mamba_norm[l].reshape(1, mw),
        }
        wd = jnp.zeros((LANES, gk), F32).at[m_heads:m_heads + G_RANK].set(gla_w_decay[l])
        bd = gla_b_decay[l].reshape(1, gk)
        hnw = hgrn_norm[l].reshape(1, LANES)
        gnw = gla_norm[l].reshape(1, g_dv)

        act, cast = _ffn_up(xw, ssq, ffn1_w_gate_up, l, (ffn1_w_down,))
        w_dn, tn_dn = (cast[0], PROJ_TN) if cast else (ffn1_w_down, F32_WEIGHT_TN)
        x, xw, ssq = _resid_matmul(act, w_dn, l, x, 0.5, tn_dn, "ffn_down", mix_norm[l])
        proj, cast = _in_proj(xw, ssq, w_perm, l, (w_branch_mamba, w_branch_hgrn, w_branch_gla, w_out))
        w_bm, w_bh, w_bg, w_o = cast if cast else (w_branch_mamba, w_branch_hgrn, w_branch_gla, w_out)

        moffs = (oz, ox, obc, osm)
        hoffs = (off["hq"], off["hf"], off["hi"], off["hg"])
        goffs = (off["gq"], off["gk"], off["gv"], off["gg"], osm)
        if l == 0:
            ym, yh, yg = (jnp.zeros((t_all, w), BF16) for w in (mw, hw, gw))
        ym, c1, s1 = _mamba_prompt(proj, moffs, mp, mconsts, ym, nseq=nseq, nchunks=nchunks)
        yh, h1 = _gla_prompt(proj, hoffs, (hgrn_lb_logits,), hnw, consts, yh,
                             hgrn=True, nseq=nseq, nchunks=nchunks, heads=h_heads, dv=LANES, layer=l,
                             hp=h_heads)
        yg, g1 = _gla_prompt(proj, goffs, (wd, bd), gnw, consts, yg,
                             hgrn=False, nseq=nseq, nchunks=nchunks, heads=g_heads, dv=g_dv, layer=l,
                             hp=g_heads)
        ym, c2, ss = _mamba_sample(proj, state_conv[l], ssm5, ss, ym, n_prompt, moffs, mp, expand, l)
        yh, sh = _gla_sample(proj, state_hgrn, sh, yh, n_prompt, hoffs, (hgrn_lb_logits,), hnw,
                             hgrn=True, heads=h_heads, dv=LANES, layer=l)
        yg, sg = _gla_sample(proj, state_gla, sg, yg, n_prompt, goffs, (wd, bd), gnw,
                             hgrn=False, heads=g_heads, dv=g_dv, layer=l)
        merged = _merge(ym, yh, yg, w_bm, w_bh, w_bg, l, proj, off["gate"])
        x, xw, ssq = _resid_matmul(merged, w_o, l, x, 1.0, PROJ_TN, "out_proj", ffn2_norm[l], WIDE_TM_CAP)

        act, cast = _ffn_up(xw, ssq, ffn2_w_gate_up, l, (ffn2_w_down,))
        w_dn, tn_dn = (cast[0], PROJ_TN) if cast else (ffn2_w_down, F32_WEIGHT_TN)
        if l + 1 < depth:
            x, xw, ssq = _resid_matmul(act, w_dn, l, x, 0.5, tn_dn, "ffn_down", ffn1_norm[l + 1])
        else:
            x = _resid_matmul(act, w_dn, l, x, 0.5, tn_dn, "ffn_down")

        pc.append(c1)
        ps.append(s1.reshape(nseq, m_heads, M_HEADDIM, M_STATE))
        ph.append(h1)
        pg.append(g1)
        sc.append(c2)

    y_prompt = _final_norm(x, final_norm, 0, n_prompt, "final_norm_prompt").reshape(nseq, seq, d)
    y_sample = _final_norm(x, final_norm, n_prompt, ns, "final_norm_sample").reshape(ns, 1, d)
    return (y_prompt, y_sample, jnp.stack(pc), jnp.stack(ps), jnp.stack(ph), jnp.stack(pg),
            jnp.stack(sc), ss.reshape(state_ssm.shape), sh, sg)
```

```python
import functools

import jax
import jax.numpy as jnp
import numpy as np
from jax import lax
from jax.experimental import pallas as pl
from jax.experimental.pallas import tpu as pltpu

F32 = jnp.float32
BF16 = jnp.bfloat16
EPS = 1e-6
F32_TINY = float(np.finfo(np.float32).tiny)

CHUNK = 256
LANES = 128
SUBLANES = 8
M_HEADDIM = 64
M_STATE = 128
M_GROUPS = 2
M_CONV = 4
G_RANK = 16
G_NORMALIZER = 16.0
N_BRANCH = 3
GLA_SUB = 128
VMEM_LIMIT = 56 * 1024 * 1024
STATE_BLOCK_BYTES = 8 * 1024 * 1024
BF16_ROWS = 16
TM_CAP = 1100
WIDE_TM_CAP = 2100
MAIN_TN = 1024
PROJ_TN = 512
F32_WEIGHT_TN = 256


def _cparams(n_axes):
    return pltpu.CompilerParams(dimension_semantics=("arbitrary",) * n_axes,
                                vmem_limit_bytes=VMEM_LIMIT)


def _dot(a, b):
    return jnp.dot(a, b, preferred_element_type=F32)


def _dot_nt(a, b):
    return lax.dot_general(a, b, (((1,), (1,)), ((), ())), preferred_element_type=F32)


def _dot_tn(a, b):
    return lax.dot_general(a, b, (((0,), (0,)), ((), ())), preferred_element_type=F32)


def _split3(x):
    hi = x.astype(BF16)
    r = x - hi.astype(F32)
    mid = r.astype(BF16)
    lo = (r - mid.astype(F32)).astype(BF16)
    return hi, mid, lo


def _dot_exact_lhs(p_bf16, x):
    hi, mid, lo = _split3(x)
    return _dot(p_bf16, hi) + _dot(p_bf16, mid) + _dot(p_bf16, lo)


def _dot_exact_rhs(x, p_bf16):
    hi, mid, lo = _split3(x)
    return _dot(hi, p_bf16) + _dot(mid, p_bf16) + _dot(lo, p_bf16)


def _dot_f32(a, b):
    ah, am, _ = _split3(a)
    bh, bm, _ = _split3(b)
    return _dot(ah, bh) + _dot(ah, bm) + _dot(am, bh)


def _sigmoid(x):
    return jax.nn.sigmoid(x)


def _silu(x):
    return x * _sigmoid(x)


def _softplus(x):
    return jnp.maximum(x, 0.0) + jnp.log1p(jnp.exp(-jnp.abs(x)))


def _log_sigmoid(x):
    return jnp.minimum(x, 0.0) - jnp.log(1.0 + jnp.exp(-jnp.abs(x)))


def _rms(x, w):
    ms = jnp.mean(x * x, axis=-1, keepdims=True)
    return x * lax.rsqrt(ms + EPS) * w


def _pick_tm(t, cap=TM_CAP):
    best = BF16_ROWS
    for tm in range(BF16_ROWS, min(t, cap) + 1, BF16_ROWS):
        if t % tm == 0:
            best = tm
    return best


def _row_scale(ssq_ref, d):
    return lax.rsqrt(ssq_ref[:, 0:1] * (1.0 / d) + EPS)


def _emit_normed(xn, nw_ref, xw_ref, ssq_ref, accumulate):
    xw_ref[...] = (xn * nw_ref[...]).astype(BF16)
    part = jnp.broadcast_to(jnp.sum(xn * xn, axis=1, keepdims=True), ssq_ref.shape)
    ssq_ref[...] = ssq_ref[...] + part if accumulate else part


def _prep_kernel(xp_ref, xs_ref, nw_ref, x_ref, xw_ref, ssq_ref, *, n_prompt_blocks):
    x = jnp.where(pl.program_id(0) < n_prompt_blocks, xp_ref[...], xs_ref[...])
    x_ref[...] = x
    _emit_normed(x, nw_ref, xw_ref, ssq_ref, accumulate=False)


def _prep(xp, xs, nw):
    n_prompt, d = xp.shape
    rb = xs.shape[0]
    assert n_prompt % rb == 0
    npb = n_prompt // rb
    t = n_prompt + rb
    row = pl.BlockSpec((rb, d), lambda i: (i, 0))
    return pl.pallas_call(
        functools.partial(_prep_kernel, n_prompt_blocks=npb),
        grid=(npb + 1,),
        in_specs=[pl.BlockSpec((rb, d), lambda i: (jnp.minimum(i, npb - 1), 0)),
                  pl.BlockSpec((rb, d), lambda i: (0, 0)),
                  pl.BlockSpec((1, d), lambda i: (0, 0))],
        out_specs=[row, row, pl.BlockSpec((rb, LANES), lambda i: (i, 0))],
        out_shape=[jax.ShapeDtypeStruct((t, d), F32), jax.ShapeDtypeStruct((t, d), BF16),
                   jax.ShapeDtypeStruct((t, LANES), F32)],
        compiler_params=_cparams(1),
        name="prep",
    )(xp, xs, nw.reshape(1, d))


def _row_parts(tm):
    half = -(-(tm // 2) // BF16_ROWS) * BF16_ROWS
    return ((0, half), (half, tm - half)) if 0 < half < tm else ((0, tm),)


def _side_casts(ws, layer, n_i, nj):
    steps = n_i * nj
    ins, outs, shapes = [], [], []
    for w in ws:
        _, k, d = w.shape
        rows = k // steps
        if rows * steps != k or rows % BF16_ROWS:
            return None
        ins.append(pl.BlockSpec((None, rows, d), lambda i, j: (layer, i * nj + j, 0)))
        outs.append(pl.BlockSpec((rows, d), lambda i, j: (i * nj + j, 0)))
        shapes.append(jax.ShapeDtypeStruct((k, d), BF16))
    return ins, outs, shapes


def _do_side_casts(refs):
    n = len(refs) // 2
    for src, dst in zip(refs[:n], refs[n:]):
        dst[...] = src[...].astype(BF16)


def _ffn_up_kernel(xw_ref, ssq_ref, wg_ref, wu_ref, *rest, parts, n_cast):
    o_ref = rest[n_cast]
    _do_side_casts(rest[:n_cast] + rest[n_cast + 1:])
    wg = wg_ref[...].astype(BF16)
    wu = wu_ref[...].astype(BF16)
    d = xw_ref.shape[1]
    for start, rows in parts:
        sl = pl.ds(start, rows)
        r = lax.rsqrt(ssq_ref[sl, 0:1] * (1.0 / d) + EPS)
        h = xw_ref[sl, :]
        g = r * _dot(h, wg)
        u = r * _dot(h, wu)
        o_ref[sl, :] = (_silu(g) * u).astype(BF16)


def _ffn_up(xw, ssq, w_gu, layer, cast_ws=(), tn=PROJ_TN):
    t, d = xw.shape
    dff = w_gu.shape[2] // 2
    tm = _pick_tm(t, WIDE_TM_CAP)
    nj = dff // tn
    side = _side_casts(cast_ws, layer, t // tm, nj) if cast_ws else None
    c_in, c_out, c_shape = side if side else ([], [], [])
    outs = pl.pallas_call(
        functools.partial(_ffn_up_kernel, parts=_row_parts(tm), n_cast=len(c_in)),
        grid=(t // tm, nj),
        in_specs=[pl.BlockSpec((tm, d), lambda i, j: (i, 0)),
                  pl.BlockSpec((tm, LANES), lambda i, j: (i, 0)),
                  pl.BlockSpec((None, d, tn), lambda i, j: (layer, 0, j)),
                  pl.BlockSpec((None, d, tn), lambda i, j: (layer, 0, j + nj))] + c_in,
        out_specs=[pl.BlockSpec((tm, tn), lambda i, j: (i, j))] + c_out,
        out_shape=[jax.ShapeDtypeStruct((t, dff), BF16)] + c_shape,
        compiler_params=_cparams(2),
        name="ffn_up",
    )(xw, ssq, w_gu, w_gu, *(cast_ws if side else ()))
    return outs[0], (tuple(outs[1:]) if side else None)


def _resid_matmul_kernel(a_ref, w_ref, x_ref, *rest, scale, emit, parts):
    if emit:
        nw_ref, o_ref, xw_ref, ssq_ref = rest

        @pl.when(pl.program_id(1) == 0)
        def _():
            ssq_ref[...] = jnp.zeros_like(ssq_ref)
    else:
        (o_ref,) = rest
    wb = w_ref[...].astype(BF16)
    for start, rows in parts:
        sl = pl.ds(start, rows)
        xn = x_ref[sl, :] + scale * _dot(a_ref[sl, :], wb)
        if emit:
            _emit_normed(xn, nw_ref, xw_ref.at[sl, :], ssq_ref.at[sl, :], accumulate=True)
        o_ref[sl, :] = xn


def _wspec(w, layer, tn):
    if w.ndim == 3:
        return pl.BlockSpec((None, w.shape[1], tn), lambda i, j: (layer, 0, j))
    return pl.BlockSpec((w.shape[0], tn), lambda i, j: (0, j))


def _resid_matmul(a, w, layer, x, scale, tn, name, next_nw=None, tm_cap=TM_CAP):
    t, k = a.shape
    d = w.shape[-1]
    tm = _pick_tm(t, tm_cap)
    emit = next_nw is not None
    tile = pl.BlockSpec((tm, tn), lambda i, j: (i, j))
    in_specs = [pl.BlockSpec((tm, k), lambda i, j: (i, 0)), _wspec(w, layer, tn), tile]
    args = [a, w, x]
    out_specs = [tile]
    out_shape = [jax.ShapeDtypeStruct((t, d), F32)]
    if emit:
        in_specs.append(pl.BlockSpec((1, tn), lambda i, j: (0, j)))
        args.append(next_nw.reshape(1, d))
        out_specs += [tile, pl.BlockSpec((tm, LANES), lambda i, j: (i, 0))]
        out_shape += [jax.ShapeDtypeStruct((t, d), BF16), jax.ShapeDtypeStruct((t, LANES), F32)]
    outs = pl.pallas_call(
        functools.partial(_resid_matmul_kernel, scale=scale, emit=emit,
                          parts=_row_parts(tm) if k <= d else ((0, tm),)),
        grid=(t // tm, d // tn),
        in_specs=in_specs,
        out_specs=out_specs,
        out_shape=out_shape,
        compiler_params=_cparams(2),
        name=name,
    )(*args)
    return outs if emit else outs[0]


def _pack_w_in_kernel(w_ref, o_ref, *, copies, zero):
    for src, dst, width in copies:
        if width % LANES == 0:
            o_ref[:, dst:dst + width] = w_ref[src:src + width, :].T.astype(BF16)
        else:
            base = src // LANES * LANES
            lo = src - base
            assert dst % LANES == lo and lo + width <= LANES
            t = w_ref[base:base + LANES, :].T
            o_ref[:, dst:dst + width] = t[:, lo:lo + width].astype(BF16)
    start, stop = zero
    if stop > start:
        o_ref[:, start:stop] = jnp.zeros((o_ref.shape[0], stop - start), BF16)


def _pack_w_in(w_t, layer, copies, zero, n_cols, tk=LANES):
    _, n_src, d = w_t.shape
    return pl.pallas_call(
        functools.partial(_pack_w_in_kernel, copies=copies, zero=zero),
        grid=(d // tk,),
        in_specs=[pl.BlockSpec((None, n_src, tk), lambda i: (layer, 0, i))],
        out_specs=pl.BlockSpec((tk, n_cols), lambda i: (i, 0)),
        out_shape=jax.ShapeDtypeStruct((d, n_cols), BF16),
        compiler_params=_cparams(1),
        name="pack_w_in",
    )(w_t)


def _in_proj_kernel(xw_ref, ssq_ref, w_ref, *rest, n_cast):
    o_ref = rest[n_cast]
    _do_side_casts(rest[:n_cast] + rest[n_cast + 1:])
    o_ref[...] = _row_scale(ssq_ref, xw_ref.shape[1]) * _dot(xw_ref[...], w_ref[...])


def _in_proj(xw, ssq, w, layer, cast_ws=(), tn=MAIN_TN):
    t, d = xw.shape
    n = w.shape[1]
    tm = _pick_tm(t, WIDE_TM_CAP)
    side = _side_casts(cast_ws, layer, t // tm, n // tn) if cast_ws else None
    c_in, c_out, c_shape = side if side else ([], [], [])
    outs = pl.pallas_call(
        functools.partial(_in_proj_kernel, n_cast=len(c_in)),
        grid=(t // tm, n // tn),
        in_specs=[pl.BlockSpec((tm, d), lambda i, j: (i, 0)),
                  pl.BlockSpec((tm, LANES), lambda i, j: (i, 0)),
                  pl.BlockSpec((d, tn), lambda i, j: (0, j))] + c_in,
        out_specs=[pl.BlockSpec((tm, tn), lambda i, j: (i, j))] + c_out,
        out_shape=[jax.ShapeDtypeStruct((t, n), F32)] + c_shape,
        compiler_params=_cparams(2),
        name="in_proj",
    )(xw, ssq, w, *(cast_ws if side else ()))
    return outs[0], (tuple(outs[1:]) if side else None)


def _merge_kernel(ym_ref, yh_ref, yg_ref, wm_ref, wh_ref, wg_ref, g0_ref, g1_ref, g2_ref, o_ref, *, parts):
    wm = wm_ref[...].astype(BF16)
    wh = wh_ref[...].astype(BF16)
    wg = wg_ref[...].astype(BF16)
    for start, rows in parts:
        sl = pl.ds(start, rows)
        acc = _sigmoid(g0_ref[sl, :]) * _dot(ym_ref[sl, :], wm)
        acc = acc + _sigmoid(g1_ref[sl, :]) * _dot(yh_ref[sl, :], wh)
        acc = acc + _sigmoid(g2_ref[sl, :]) * _dot(yg_ref[sl, :], wg)
        o_ref[sl, :] = acc.astype(BF16)


def _merge(ym, yh, yg, wm, wh, wg, layer, proj, gate_off, tn=PROJ_TN):
    t, k = ym.shape
    d = wm.shape[-1]
    tm = _pick_tm(t)
    assert gate_off % tn == 0
    gb = gate_off // tn
    nb = d // tn
    yspec = pl.BlockSpec((tm, k), lambda i, j: (i, 0))
    wspec = _wspec(wm, layer, tn)

    def gspec(b):
        return pl.BlockSpec((tm, tn), lambda i, j: (i, gb + b * nb + j))

    return pl.pallas_call(
        functools.partial(_merge_kernel, parts=_row_parts(tm)),
        grid=(t // tm, nb),
        in_specs=[yspec, yspec, yspec, wspec, wspec, wspec, gspec(0), gspec(1), gspec(2)],
        out_specs=pl.BlockSpec((tm, tn), lambda i, j: (i, j)),
        out_shape=jax.ShapeDtypeStruct((t, d), BF16),
        compiler_params=_cparams(2),
        name="merge",
    )(ym, yh, yg, wm, wh, wg, proj, proj, proj)


def _final_norm_kernel(x_ref, nw_ref, o_ref):
    o_ref[...] = _rms(x_ref[...], nw_ref[...])


def _final_norm(x, nw, row0, rows, name):
    d = x.shape[1]
    tm = _pick_tm(rows)
    assert row0 % tm == 0
    return pl.pallas_call(
        _final_norm_kernel,
        grid=(rows // tm,),
        in_specs=[pl.BlockSpec((tm, d), lambda i: (row0 // tm + i, 0)),
                  pl.BlockSpec((1, d), lambda i: (0, 0))],
        out_specs=pl.BlockSpec((tm, d), lambda i: (i, 0)),
        out_shape=jax.ShapeDtypeStruct((rows, d), F32),
        compiler_params=_cparams(1),
        name=name,
    )(x, nw.reshape(1, d))


def _scan_constants(c):
    N_LEVELS = int(np.log2(c))
    assert 1 << N_LEVELS == c and N_LEVELS >= 3
    pm = np.zeros((2, c, c), np.float32)
    sg = np.zeros((N_LEVELS - 1, c, LANES), np.float32)
    mk = np.zeros((N_LEVELS + 1, c, c), np.float32)
    r = np.arange(c)
    for lvl in range(N_LEVELS):
        h = c >> (lvl + 1)
        for t in range(c):
            blk, pos = divmod(t, 2 * h)
            ridx = blk * 2 * h + h - 1
            upper = pos >= h
            if upper:
                mk[lvl, t] = ((r // (2 * h)) == blk) & ((r % (2 * h)) < h)
            if h >= 4:
                sg[lvl, t] = 1.0 if upper else -1.0
            elif h == 2:
                pm[0, t] = ((r > ridx) & (r <= t)) if upper else ((r > t) & (r <= ridx))
            else:
                sg[N_LEVELS - 2, t] = 1.0 if upper else 0.0
    pm[1] = r[None, :] <= r[:, None]
    mk[N_LEVELS] = np.eye(c)
    return jnp.asarray(pm.reshape(2 * c, c), BF16), jnp.asarray(sg), jnp.asarray(mk)


def _lower_bound(logits, layer):
    m = jnp.max(logits, axis=0, keepdims=True)
    e = jnp.exp(logits - m)
    sm = e / jnp.sum(e, axis=0, keepdims=True)
    lb = jnp.zeros_like(m)
    for i in range(1, layer + 1):
        lb = lb + sm[i:i + 1, :]
    return lb


def _gla_inputs(refs, hgrn, layer):
    if hgrn:
        q_ref, f_ref, i_ref, lbl_ref = refs
        lb = _lower_bound(lbl_ref[...], layer)
        q = _silu(q_ref[...]) * (LANES ** -0.5)
        zf = f_ref[...]
        a = (1.0 - lb) * _sigmoid(zf)
        f = lb + a
        gl = jnp.log(jnp.maximum(f, F32_TINY))
        k = (1.0 - lb) - a
        v = i_ref[...]
    else:
        q_ref, k_ref, v_ref, a_ref, wd_ref, bd_ref = refs
        q = q_ref[...] * (LANES ** -0.5)
        k = k_ref[...]
        v = v_ref[...]
        gl = _log_sigmoid(_dot_f32(a_ref[...], wd_ref[...]) + bd_ref[...]) / G_NORMALIZER
    return q, k, v, gl


def _level_factor(lvl, gl, cum, z_h2, sg_ref):
    c = gl.shape[0]
    n_levels = sg_ref.shape[0] + 1
    h = c >> (lvl + 1)
    if h >= 4:
        c3 = cum.reshape(c // (2 * h), 2 * h, LANES)
        d = (c3 - c3[:, h - 1:h, :]).reshape(c, LANES)
        return jnp.exp(sg_ref[lvl] * d)
    if h == 2:
        return jnp.exp(z_h2)
    return jnp.exp(sg_ref[n_levels - 2] * gl)


def _gla_prompt_kernel(*refs, hgrn, layer, hp, dv):
    n_in = 4 if hgrn else 6
    in_refs = refs[:n_in]
    g_ref, nw_ref, pm_ref, sg_ref, mk_ref, _, y_ref, so_ref, s_scr = refs[n_in:]
    c = pl.program_id(2)
    sub = mk_ref.shape[1]
    n_levels = mk_ref.shape[0] - 1

    @pl.when(c == 0)
    def _():
        s_scr[...] = jnp.zeros_like(s_scr)

    q_all, k_all, v_all, gl_all = _gla_inputs(in_refs, hgrn, layer)
    finals = []
    for u in range(hp):
        s = s_scr[u]
        for r0 in range(0, CHUNK, sub):
            rows = pl.ds(r0, sub)
            q, k, gl = (a[r0:r0 + sub, u * LANES:(u + 1) * LANES] for a in (q_all, k_all, gl_all))
            v = v_all[r0:r0 + sub, u * dv:(u + 1) * dv]
            zz = _dot_exact_lhs(pm_ref[...], gl)
            z_h2, cum = zz[:sub], zz[sub:]
            qb, kb = q.astype(BF16), k.astype(BF16)
            att = mk_ref[n_levels] * _dot_nt(qb, kb)
            for lvl in range(n_levels):
                eb = _level_factor(lvl, gl, cum, z_h2, sg_ref).astype(BF16)
                att = att + mk_ref[lvl] * _dot_nt(qb * eb, kb * eb)
            e_cum = jnp.exp(cum)
            e_tail = jnp.exp(cum[sub - 1:sub, :] - cum)
            vb = v.astype(BF16)
            o = _dot(att.astype(BF16), vb) + _dot((q * e_cum).astype(BF16), s.astype(BF16))
            e_last = e_cum.T[:, sub - 1:sub]
            s = e_last * s + _dot_tn((k * e_tail).astype(BF16), vb)
            gate = g_ref[rows, u * dv:(u + 1) * dv]
            y_ref[rows, u * dv:(u + 1) * dv] = (_rms(o, nw_ref[...]) * _silu(gate)).astype(BF16)
        s_scr[u] = s
        finals.append(s)

    @pl.when(c == pl.num_programs(2) - 1)
    def _():
        for u in range(hp):
            so_ref[u] = finals[u]


def _gla_prompt(proj, offs, extra, nw, consts, y_init, *, hgrn, nseq, nchunks, heads, dv, layer, hp):
    pm, sg, mk = consts
    assert heads % hp == 0

    def col(off, w):
        assert off % (hp * w) == 0
        return pl.BlockSpec((CHUNK, hp * w), lambda h, b, c: (b * nchunks + c, off // (hp * w) + h))

    def full(a):
        nd = a.ndim
        return pl.BlockSpec(a.shape, lambda h, b, c: (0,) * nd)

    if hgrn:
        oq, of, oi, og = offs
        (lbl,) = extra
        in_specs = [col(oq, LANES), col(of, LANES), col(oi, dv),
                    pl.BlockSpec((lbl.shape[0], hp * LANES), lambda h, b, c: (0, h))]
        args = [proj, proj, proj, lbl]
    else:
        oq, ok, ov, og, oa = offs
        wd, bd = extra
        in_specs = [col(oq, LANES), col(ok, LANES), col(ov, dv),
                    pl.BlockSpec((CHUNK, LANES), lambda h, b, c: (b * nchunks + c, oa // LANES)),
                    pl.BlockSpec((LANES, hp * LANES), lambda h, b, c: (0, h)),
                    pl.BlockSpec((1, hp * LANES), lambda h, b, c: (0, h))]
        args = [proj, proj, proj, proj, wd, bd]
    in_specs += [col(og, dv), full(nw), full(pm), full(sg), full(mk), pl.BlockSpec(memory_space=pl.ANY)]
    args += [proj, nw, pm, sg, mk, y_init]
    y, s = pl.pallas_call(
        functools.partial(_gla_prompt_kernel, hgrn=hgrn, layer=layer, hp=hp, dv=dv),
        grid=(heads // hp, nseq, nchunks),
        in_specs=in_specs,
        out_specs=[pl.BlockSpec((CHUNK, hp * dv), lambda h, b, c: (b * nchunks + c, h)),
                   pl.BlockSpec((None, hp, LANES, dv), lambda h, b, c: (b, h, 0, 0))],
        out_shape=[jax.ShapeDtypeStruct(y_init.shape, BF16),
                   jax.ShapeDtypeStruct((nseq, heads, LANES, dv), F32)],
        input_output_aliases={len(args) - 1: 0},
        scratch_shapes=[pltpu.VMEM((hp, LANES, dv), F32)],
        compiler_params=_cparams(3),
        name="hgrn_prompt" if hgrn else "gla_prompt",
    )(*args)
    return y, s


def _columns(x):
    nb = x.shape[0]
    if nb < LANES:
        x = jnp.concatenate([x, jnp.zeros((LANES - nb, x.shape[1]), x.dtype)], axis=0)
    return x.T


def _store_state(so_ref, layer, b, s_new, first):
    if first:
        for l in range(so_ref.shape[0]):
            so_ref[l, b] = s_new if l == layer else jnp.zeros_like(s_new)
    else:
        so_ref[b] = s_new


def _gla_sample_kernel(*refs, hgrn, nb, layer, first):
    n_in = 4 if hgrn else 6
    in_refs = refs[:n_in]
    g_ref, nw_ref, s_ref = refs[n_in:n_in + 3]
    y_ref, so_ref = refs[-2:]
    q, k, v, gl = _gla_inputs(in_refs, hgrn, layer)
    e_t = _columns(jnp.exp(gl))
    k_t = _columns(k)
    qb = q.astype(BF16)
    rowid = lax.broadcasted_iota(jnp.int32, v.shape, 0)
    o = jnp.zeros(v.shape, F32)
    for b in range(nb):
        s_new = e_t[:, b:b + 1] * s_ref[b] + k_t[:, b:b + 1] * v[b:b + 1, :]
        _store_state(so_ref, layer, b, s_new, first)
        o = jnp.where(rowid == b, _dot(qb, s_new.astype(BF16)), o)
    y_ref[...] = (_rms(o, nw_ref[...]) * _silu(g_ref[...])).astype(BF16)


def _state_specs(depth, layer, nb, dv, first):
    ispec = pl.BlockSpec((None, nb, None, LANES, dv), lambda h, i: (layer, i, h, 0, 0))
    if first:
        ospec = pl.BlockSpec((depth, nb, None, LANES, dv), lambda h, i: (0, i, h, 0, 0))
    else:
        ospec = ispec
    return ispec, ospec


def _sample_batch_block(ns, dv):
    nb = min(ns, STATE_BLOCK_BYTES // (LANES * dv * 4))
    assert ns % nb == 0
    return nb


def _gla_sample(proj, state, prev, y_all, row0, offs, extra, nw, *, hgrn, heads, dv, layer):
    depth, ns = state.shape[:2]
    nb = _sample_batch_block(ns, dv)
    rb = row0 // nb
    first = prev is None

    def col(off, w):
        return pl.BlockSpec((nb, w), lambda h, i: (rb + i, off // w + h))

    if hgrn:
        oq, of, oi, og = offs
        (lbl,) = extra
        in_specs = [col(oq, LANES), col(of, LANES), col(oi, dv),
                    pl.BlockSpec((lbl.shape[0], LANES), lambda h, i: (0, h))]
        args = [proj, proj, proj, lbl]
    else:
        oq, ok, ov, og, oa = offs
        wd, bd = extra
        in_specs = [col(oq, LANES), col(ok, LANES), col(ov, dv),
                    pl.BlockSpec((nb, LANES), lambda h, i: (rb + i, oa // LANES)),
                    pl.BlockSpec((LANES, LANES), lambda h, i: (0, h)),
                    pl.BlockSpec((1, LANES), lambda h, i: (0, h))]
        args = [proj, proj, proj, proj, wd, bd]
    ispec, ospec = _state_specs(depth, layer, nb, dv, first)
    in_specs += [col(og, dv), pl.BlockSpec(nw.shape, lambda h, i: (0, 0)), ispec,
                 pl.BlockSpec(memory_space=pl.ANY)]
    args += [proj, nw, state, y_all]
    aliases = {len(args) - 1: 0}
    if not first:
        in_specs.append(pl.BlockSpec(memory_space=pl.ANY))
        args.append(prev)
        aliases[len(args) - 1] = 1
    y, s = pl.pallas_call(
        functools.partial(_gla_sample_kernel, hgrn=hgrn, nb=nb, layer=layer, first=first),
        grid=(heads, ns // nb),
        in_specs=in_specs,
        out_specs=[pl.BlockSpec((nb, dv), lambda h, i: (rb + i, h)), ospec],
        out_shape=[jax.ShapeDtypeStruct(y_all.shape, BF16),
                   jax.ShapeDtypeStruct(state.shape, F32)],
        input_output_aliases=aliases,
        compiler_params=_cparams(2),
        name="hgrn_sample" if hgrn else "gla_sample",
    )(*args)
    return y, s


def _mamba_post(y, xs, z, dskip, nw):
    y = (y + dskip * xs) * _silu(z)
    gs = y.shape[1] // M_GROUPS
    outs = [_rms(y[:, g * gs:(g + 1) * gs], nw[:, g * gs:(g + 1) * gs]) for g in range(M_GROUPS)]
    return jnp.concatenate(outs, axis=1)


def _mamba_prompt_kernel(z_ref, x_ref, bc_ref, sm_ref, cwx_ref, cwb_ref, cbx_ref, cbb_ref,
                         dtb_ref, alog_ref, dsk_ref, nw_ref, tril_ref, mask_ref, _,
                         y_ref, cox_ref, cob_ref, so_ref, ex_scr, eb_scr, s_scr):
    c = pl.program_id(1)
    nc = pl.num_programs(1)
    tail = SUBLANES

    @pl.when(c == 0)
    def _():
        ex_scr[0:tail, :] = jnp.zeros((tail, ex_scr.shape[1]), F32)
        eb_scr[0:tail, :] = jnp.zeros((tail, eb_scr.shape[1]), F32)
        s_scr[...] = jnp.zeros_like(s_scr)

    ex_scr[tail:tail + CHUNK, :] = x_ref[...]
    eb_scr[tail:tail + CHUNK, :] = bc_ref[...]

    def conv(scr, cw_ref, cb_ref):
        xe = scr[...]
        acc = cw_ref[0:1, :] * xe
        for w in range(1, M_CONV):
            acc = pltpu.roll(acc, 1, axis=0) + cw_ref[w:w + 1, :] * xe
        return _silu(acc[tail:, :] + cb_ref[...])

    xs = conv(ex_scr, cwx_ref, cbx_ref)
    bcm = conv(eb_scr, cwb_ref, cbb_ref)

    ex_scr[0:tail, :] = ex_scr[CHUNK:CHUNK + tail, :]
    eb_scr[0:tail, :] = eb_scr[CHUNK:CHUNK + tail, :]

    dt = _softplus(sm_ref[...] + dtb_ref[...])
    a = -jnp.exp(alog_ref[...])
    cum = _dot_exact_lhs(tril_ref[...], dt * a)
    cum_t = cum.T
    cl = cum[CHUNK - 1:CHUNK, :]
    mask = mask_ref[...] > 0.5
    gw = M_GROUPS * M_STATE
    lo = lax.broadcasted_iota(jnp.int32, (CHUNK, LANES), 1) < M_HEADDIM
    lo_r = lax.broadcasted_iota(jnp.int32, (LANES, LANES), 0) < M_HEADDIM
    n_pairs = xs.shape[1] // LANES
    per_group = n_pairs // M_GROUPS
    ys = []
    for j in range(n_pairs):
        g = j // per_group
        h0, h1 = 2 * j, 2 * j + 1
        if j % per_group == 0:
            bg = bcm[:, g * M_STATE:(g + 1) * M_STATE].astype(BF16)
            cg = bcm[:, gw + g * M_STATE:gw + (g + 1) * M_STATE].astype(BF16)
            cb = jnp.where(mask, _dot_nt(cg, bg), 0.0)

        def dec(h):
            return jnp.exp(jnp.minimum(cum[:, h:h + 1] - cum_t[h:h + 1, :], 0.0))

        x2 = xs[:, j * LANES:(j + 1) * LANES]
        dt2 = jnp.where(lo, dt[:, h0:h0 + 1], dt[:, h1:h1 + 1])
        xdt = (x2 * dt2).astype(BF16)
        y_in = jnp.where(lo, _dot((cb * dec(h0)).astype(BF16), xdt),
                         _dot((cb * dec(h1)).astype(BF16), xdt))
        ec2 = jnp.where(lo, jnp.exp(cum[:, h0:h0 + 1]), jnp.exp(cum[:, h1:h1 + 1]))
        s = s_scr[j]
        ys.append(y_in + _dot_nt(cg, s.astype(BF16)) * ec2)
        w2 = dt2 * jnp.where(lo, jnp.exp(cl[:, h0:h0 + 1] - cum[:, h0:h0 + 1]),
                             jnp.exp(cl[:, h1:h1 + 1] - cum[:, h1:h1 + 1]))
        el2 = jnp.where(lo_r, jnp.exp(cl[:, h0:h0 + 1]), jnp.exp(cl[:, h1:h1 + 1]))
        s_scr[j] = el2 * s + _dot_tn((x2 * w2).astype(BF16), bg)
    y = jnp.concatenate(ys, axis=1)
    y_ref[...] = _mamba_post(y, xs, z_ref[...], dsk_ref[...], nw_ref[...]).astype(BF16)

    @pl.when(c == nc - 1)
    def _():
        so_ref[...] = s_scr[...]
        cox_ref[...] = ex_scr[tail + CHUNK - (M_CONV - 1):tail + CHUNK, :]
        cob_ref[...] = eb_scr[tail + CHUNK - (M_CONV - 1):tail + CHUNK, :]


def _mamba_prompt(proj, offs, p, consts, y_init, *, nseq, nchunks):
    oz, ox, obc, osm = offs
    mw = p["dskip"].shape[1]
    bcw = p["cwb"].shape[1]
    n_pairs = mw // LANES

    def col(off, w):
        return pl.BlockSpec((CHUNK, w), lambda b, c: (b * nchunks + c, off // w))

    def full(a):
        nd = a.ndim
        return pl.BlockSpec(a.shape, lambda b, c: (0,) * nd)

    small = [p["cwx"], p["cwb"], p["cbx"], p["cbb"], p["dtb"], p["alog"], p["dskip"], p["nw"],
             consts[0], consts[1]]
    y, cox, cob, s = pl.pallas_call(
        _mamba_prompt_kernel,
        grid=(nseq, nchunks),
        in_specs=[col(oz, mw), col(ox, mw), col(obc, bcw), col(osm, LANES)] + [full(a) for a in small]
        + [pl.BlockSpec(memory_space=pl.ANY)],
        out_specs=[pl.BlockSpec((CHUNK, mw), lambda b, c: (b * nchunks + c, 0)),
                   pl.BlockSpec((None, M_CONV - 1, mw), lambda b, c: (b, 0, 0)),
                   pl.BlockSpec((None, M_CONV - 1, bcw), lambda b, c: (b, 0, 0)),
                   pl.BlockSpec((None, n_pairs, LANES, M_STATE), lambda b, c: (b, 0, 0, 0))],
        input_output_aliases={4 + len(small): 0},
        out_shape=[jax.ShapeDtypeStruct(y_init.shape, BF16),
                   jax.ShapeDtypeStruct((nseq, M_CONV - 1, mw), F32),
                   jax.ShapeDtypeStruct((nseq, M_CONV - 1, bcw), F32),
                   jax.ShapeDtypeStruct((nseq, n_pairs, LANES, M_STATE), F32)],
        scratch_shapes=[pltpu.VMEM((CHUNK + SUBLANES, mw), F32), pltpu.VMEM((CHUNK + SUBLANES, bcw), F32),
                        pltpu.VMEM((n_pairs, LANES, M_STATE), F32)],
        compiler_params=_cparams(2),
        name="mamba_prompt",
    )(proj, proj, proj, proj, *small, y_init)
    return y, jnp.concatenate([cox, cob], axis=-1), s


def _mamba_sample_prep_kernel(cs_ref, x_ref, bc_ref, sm_ref, cw_ref, cb_ref, dtb_ref, alog_ref, exp_ref,
                              co_ref, act_ref, dte_ref, ee_ref):
    cd = cw_ref.shape[1]
    new = jnp.concatenate([x_ref[...], bc_ref[...]], axis=1)
    acc = cb_ref[...] + cw_ref[M_CONV - 1:M_CONV, :] * new
    for w in range(M_CONV - 1):
        acc = acc + cw_ref[w:w + 1, :] * cs_ref[:, w * cd:(w + 1) * cd]
    act_ref[...] = _silu(acc)
    for w in range(1, M_CONV - 1):
        co_ref[:, (w - 1) * cd:w * cd] = cs_ref[:, w * cd:(w + 1) * cd]
    co_ref[:, (M_CONV - 2) * cd:(M_CONV - 1) * cd] = new
    dt = _softplus(sm_ref[...] + dtb_ref[...])
    a = -jnp.exp(alog_ref[...])
    dte_ref[...] = _dot_exact_rhs(dt, exp_ref[...])
    ee_ref[...] = jnp.exp(_dot_exact_rhs(dt * a, exp_ref[...]))


def _mamba_sample_state_kernel(x_ref, b_ref, c_ref, dte_ref, ee_ref, s_ref, *out_refs, nb, layer, first):
    y_ref, so_ref = out_refs[-2:]
    xdt_t = _columns(x_ref[...] * dte_ref[...])
    e_t = _columns(ee_ref[...])
    bv = b_ref[...]
    cb = c_ref[...].astype(BF16)
    rowid = lax.broadcasted_iota(jnp.int32, (nb, LANES), 0)
    y = jnp.zeros((nb, LANES), F32)
    for b in range(nb):
        s_new = e_t[:, b:b + 1] * s_ref[b] + xdt_t[:, b:b + 1] * bv[b:b + 1, :]
        _store_state(so_ref, layer, b, s_new, first)
        y = jnp.where(rowid == b, _dot_nt(cb, s_new.astype(BF16)), y)
    y_ref[...] = y


def _mamba_sample_post_kernel(y_ref, x_ref, z_ref, dsk_ref, nw_ref, _, o_ref):
    o_ref[...] = _mamba_post(y_ref[...], x_ref[...], z_ref[...], dsk_ref[...], nw_ref[...]).astype(BF16)


def _mamba_sample(proj, conv_state, ssm_state, prev, y_all, row0, offs, p, expand, layer):
    oz, ox, obc, osm = offs
    depth, ns = ssm_state.shape[:2]
    nb = _sample_batch_block(ns, M_STATE)
    mw = p["dskip"].shape[1]
    bcw = p["cwb"].shape[1]
    cd = mw + bcw
    n_pairs = mw // LANES
    per_group = n_pairs // M_GROUPS
    rb = row0 // ns
    first = prev is None
    cw = jnp.concatenate([p["cwx"], p["cwb"]], axis=1)
    cb = jnp.concatenate([p["cbx"], p["cbb"]], axis=1)

    def full1(a):
        nd = a.ndim
        return pl.BlockSpec(a.shape, lambda i: (0,) * nd)

    cs2 = conv_state.reshape(ns, (M_CONV - 1) * cd)
    small = [cw, cb, p["dtb"], p["alog"], expand]
    co, act, dte, ee = pl.pallas_call(
        _mamba_sample_prep_kernel,
        grid=(1,),
        in_specs=[full1(cs2),
                  pl.BlockSpec((ns, mw), lambda i: (rb, ox // mw)),
                  pl.BlockSpec((ns, bcw), lambda i: (rb, obc // bcw)),
                  pl.BlockSpec((ns, LANES), lambda i: (rb, osm // LANES))] + [full1(a) for a in small],
        out_specs=[pl.BlockSpec((ns, (M_CONV - 1) * cd), lambda i: (0, 0)),
                   pl.BlockSpec((ns, cd), lambda i: (0, 0)),
                   pl.BlockSpec((ns, mw), lambda i: (0, 0)),
                   pl.BlockSpec((ns, mw), lambda i: (0, 0))],
        out_shape=[jax.ShapeDtypeStruct((ns, (M_CONV - 1) * cd), F32),
                   jax.ShapeDtypeStruct((ns, cd), F32),
                   jax.ShapeDtypeStruct((ns, mw), F32),
                   jax.ShapeDtypeStruct((ns, mw), F32)],
        compiler_params=_cparams(1),
        name="mamba_sample_prep",
    )(cs2, proj, proj, proj, *small)

    bblk = mw // LANES
    cblk = bblk + M_GROUPS * M_STATE // LANES
    ispec, ospec = _state_specs(depth, layer, nb, M_STATE, first)
    in_specs = [pl.BlockSpec((nb, LANES), lambda j, i: (i, j)),
                pl.BlockSpec((nb, LANES), lambda j, i: (i, bblk + j // per_group)),
                pl.BlockSpec((nb, LANES), lambda j, i: (i, cblk + j // per_group)),
                pl.BlockSpec((nb, LANES), lambda j, i: (i, j)),
                pl.BlockSpec((nb, LANES), lambda j, i: (i, j)),
                ispec]
    args = [act, act, act, dte, ee, ssm_state]
    aliases = {}
    if not first:
        in_specs.append(pl.BlockSpec(memory_space=pl.ANY))
        args.append(prev)
        aliases = {len(args) - 1: 1}
    y, s_new = pl.pallas_call(
        functools.partial(_mamba_sample_state_kernel, nb=nb, layer=layer, first=first),
        grid=(n_pairs, ns // nb),
        in_specs=in_specs,
        out_specs=[pl.BlockSpec((nb, LANES), lambda j, i: (i, j)), ospec],
        out_shape=[jax.ShapeDtypeStruct((ns, mw), F32),
                   jax.ShapeDtypeStruct(ssm_state.shape, F32)],
        input_output_aliases=aliases,
        compiler_params=_cparams(2),
        name="mamba_sample_state",
    )(*args)

    ym = pl.pallas_call(
        _mamba_sample_post_kernel,
        grid=(1,),
        in_specs=[full1(y),
                  pl.BlockSpec((ns, mw), lambda i: (0, 0)),
                  pl.BlockSpec((ns, mw), lambda i: (rb, oz // mw)),
                  full1(p["dskip"]), full1(p["nw"]), pl.BlockSpec(memory_space=pl.ANY)],
        out_specs=pl.BlockSpec((ns, mw), lambda i: (rb, 0)),
        out_shape=jax.ShapeDtypeStruct(y_all.shape, BF16),
        input_output_aliases={5: 0},
        compiler_params=_cparams(1),
        name="mamba_sample_post",
    )(y, act, proj, p["dskip"], p["nw"], y_all)
    return ym, co.reshape(ns, M_CONV - 1, cd), s_new


def _pad_lanes(v, n=LANES):
    v = v.reshape(1, -1)
    return jnp.pad(v, ((0, 0), (0, n - v.shape[1])))


def kernel(x_prompt, x_sample, state_conv, state_ssm, state_hgrn, state_gla, ffn1_norm, ffn1_w_gate_up, ffn1_w_down, mix_norm, w_in, conv_w, conv_b, dt_bias, a_log, d_skip, mamba_norm, hgrn_lb_logits, hgrn_norm, gla_w_decay, gla_b_decay, gla_norm, w_branch_mamba, w_branch_hgrn, w_branch_gla, w_out, ffn2_norm, ffn2_w_gate_up, ffn2_w_down, final_norm):
    nseq, seq, d = x_prompt.shape
    ns = x_sample.shape[0]
    depth = w_in.shape[0]
    nchunks = seq // CHUNK
    n_prompt = nseq * seq
    mw = w_branch_mamba.shape[1]
    hw = w_branch_hgrn.shape[1]
    gw = w_branch_gla.shape[1]
    gk = gla_w_decay.shape[2]
    m_heads = dt_bias.shape[1]
    h_heads = state_hgrn.shape[2]
    g_heads = state_gla.shape[2]
    g_dv = state_gla.shape[4]
    bcw = 2 * M_GROUPS * M_STATE
    n_pairs = mw // LANES
    assert seq % CHUNK == 0 and n_prompt % ns == 0 and ns % BF16_ROWS == 0
    assert m_heads <= G_RANK + m_heads <= LANES and gk // g_heads == LANES and hw // h_heads == LANES

    seg_w = {"z": mw, "xs": mw, "bc": bcw, "dt": m_heads, "hq": hw, "hf": hw, "hi": hw, "hg": hw,
             "gq": gk, "gk": gk, "gv": gw, "gg": gw, "ga": G_RANK, "gate": N_BRANCH * d}
    src_order = ("z", "xs", "bc", "dt", "hq", "hf", "hi", "hg", "gq", "gk", "gv", "gg", "ga", "gate")
    dst_order = ("z", "xs", "hq", "hf", "hi", "hg", "gv", "gg", "gq", "gk", "bc", "dt", "ga")
    src, off = {}, {}
    pos = 0
    for name in src_order:
        src[name] = pos
        pos += seg_w[name]
    pos = 0
    for name in dst_order:
        off[name] = pos
        pos += seg_w[name]
    n_used = pos
    off["gate"] = -(-n_used // MAIN_TN) * MAIN_TN
    n_cols = off["gate"] + seg_w["gate"]
    assert n_cols % MAIN_TN == 0
    copies = tuple((src[name], off[name], seg_w[name]) for name in dst_order + ("gate",))
    oz, ox, obc, osm = off["z"], off["xs"], off["bc"], off["dt"]
    assert off["ga"] == osm + m_heads and osm % LANES == 0

    consts = _scan_constants(GLA_SUB)
    tril_np = np.tril(np.ones((CHUNK, CHUNK), np.float32))
    mconsts = (jnp.asarray(tril_np, BF16), jnp.asarray(tril_np))
    expand_np = np.zeros((LANES, mw), np.float32)
    for h in range(m_heads):
        expand_np[h, h * M_HEADDIM:(h + 1) * M_HEADDIM] = 1.0
    expand = jnp.asarray(expand_np, BF16)

    x, xw, ssq = _prep(x_prompt.reshape(n_prompt, d), x_sample.reshape(ns, d), ffn1_norm[0])
    t_all = n_prompt + ns
    w_in_t = jnp.swapaxes(w_in, 1, 2)
    ssm5 = state_ssm.reshape(depth, ns, n_pairs, LANES, M_STATE)

    pc, ps, ph, pg, sc = [], [], [], [], []
    ss = sh = sg = None
    for l in range(depth):
        w_perm = _pack_w_in(w_in_t, l, copies, (n_used, off["gate"]), n_cols)
        mp = {
            "cwx": conv_w[l][:, :mw], "cwb": conv_w[l][:, mw:],
            "cbx": conv_b[l][:mw].reshape(1, mw), "cbb": conv_b[l][mw:].reshape(1, bcw),
            "dtb": _pad_lanes(dt_bias[l]), "alog": _pad_lanes(a_log[l]),
            "dskip": jnp.repeat(d_skip[l], M_HEADDIM).reshape(1, mw),
            "nw": mamba_norm[l].reshape(1, mw),
        }
        wd = jnp.zeros((LANES, gk), F32).at[m_heads:m_heads + G_RANK].set(gla_w_decay[l])
        bd = gla_b_decay[l].reshape(1, gk)
        hnw = hgrn_norm[l].reshape(1, LANES)
        gnw = gla_norm[l].reshape(1, g_dv)

        act, cast = _ffn_up(xw, ssq, ffn1_w_gate_up, l, (ffn1_w_down,))
        w_dn, tn_dn = (cast[0], PROJ_TN) if cast else (ffn1_w_down, F32_WEIGHT_TN)
        x, xw, ssq = _resid_matmul(act, w_dn, l, x, 0.5, tn_dn, "ffn_down", mix_norm[l])
        proj, cast = _in_proj(xw, ssq, w_perm, l, (w_branch_mamba, w_branch_hgrn, w_branch_gla, w_out))
        w_bm, w_bh, w_bg, w_o = cast if cast else (w_branch_mamba, w_branch_hgrn, w_branch_gla, w_out)

        moffs = (oz, ox, obc, osm)
        hoffs = (off["hq"], off["hf"], off["hi"], off["hg"])
        goffs = (off["gq"], off["gk"], off["gv"], off["gg"], osm)
        if l == 0:
            ym, yh, yg = (jnp.zeros((t_all, w), BF16) for w in (mw, hw, gw))
        ym, c1, s1 = _mamba_prompt(proj, moffs, mp, mconsts, ym, nseq=nseq, nchunks=nchunks)
        yh, h1 = _gla_prompt(proj, hoffs, (hgrn_lb_logits,), hnw, consts, yh,
                             hgrn=True, nseq=nseq, nchunks=nchunks, heads=h_heads, dv=LANES, layer=l,
                             hp=h_heads)
        yg, g1 = _gla_prompt(proj, goffs, (wd, bd), gnw, consts, yg,
                             hgrn=False, nseq=nseq, nchunks=nchunks, heads=g_heads, dv=g_dv, layer=l,
                             hp=g_heads)
        ym, c2, ss = _mamba_sample(proj, state_conv[l], ssm5, ss, ym, n_prompt, moffs, mp, expand, l)
        yh, sh = _gla_sample(proj, state_hgrn, sh, yh, n_prompt, hoffs, (hgrn_lb_logits,), hnw,
                             hgrn=True, heads=h_heads, dv=LANES, layer=l)
        yg, sg = _gla_sample(proj, state_gla, sg, yg, n_prompt, goffs, (wd, bd), gnw,
                             hgrn=False, heads=g_heads, dv=g_dv, layer=l)
        merged = _merge(ym, yh, yg, w_bm, w_bh, w_bg, l, proj, off["gate"])
        x, xw, ssq = _resid_matmul(merged, w_o, l, x, 1.0, PROJ_TN, "out_proj", ffn2_norm[l], WIDE_TM_CAP)

        act, cast = _ffn_up(xw, ssq, ffn2_w_gate_up, l, (ffn2_w_down,))
        w_dn, tn_dn = (cast[0], PROJ_TN) if cast else (ffn2_w_down, F32_WEIGHT_TN)
        if l + 1 < depth:
            x, xw, ssq = _resid_matmul(act, w_dn, l, x, 0.5, tn_dn, "ffn_down", ffn1_norm[l + 1])
        else:
            x = _resid_matmul(act, w_dn, l, x, 0.5, tn_dn, "ffn_down")

        pc.append(c1)
        ps.append(s1.reshape(nseq, m_heads, M_HEADDIM, M_STATE))
        ph.append(h1)
        pg.append(g1)
        sc.append(c2)

    y_prompt = _final_norm(x, final_norm, 0, n_prompt, "final_norm_prompt").reshape(nseq, seq, d)
    y_sample = _final_norm(x, final_norm, n_prompt, ns, "final_norm_sample").reshape(ns, 1, d)
    return (y_prompt, y_sample, jnp.stack(pc), jnp.stack(ps), jnp.stack(ph), jnp.stack(pg),
            jnp.stack(sc), ss.reshape(state_ssm.shape), sh, sg)
```

```python
import functools

import jax
import jax.numpy as jnp
import numpy as np
from jax import lax
from jax.experimental import pallas as pl
from jax.experimental.pallas import tpu as pltpu

F32 = jnp.float32
BF16 = jnp.bfloat16
EPS = 1e-6
F32_TINY = float(np.finfo(np.float32).tiny)

CHUNK = 256
LANES = 128
SUBLANES = 8
M_HEADDIM = 64
M_STATE = 128
M_GROUPS = 2
M_CONV = 4
G_RANK = 16
G_NORMALIZER = 16.0
N_BRANCH = 3
GLA_CHUNK = 512
GLA_SUB = 128
VMEM_LIMIT = 56 * 1024 * 1024
STATE_BLOCK_BYTES = 8 * 1024 * 1024
BF16_ROWS = 16
TM_CAP = 1100
WIDE_TM_CAP = 2100
MAIN_TN = 1024
PROJ_TN = 512
F32_WEIGHT_TN = 256


def _cparams(n_axes):
    return pltpu.CompilerParams(dimension_semantics=("arbitrary",) * n_axes,
                                vmem_limit_bytes=VMEM_LIMIT)


def _dot(a, b):
    return jnp.dot(a, b, preferred_element_type=F32)


def _dot_nt(a, b):
    return lax.dot_general(a, b, (((1,), (1,)), ((), ())), preferred_element_type=F32)


def _dot_tn(a, b):
    return lax.dot_general(a, b, (((0,), (0,)), ((), ())), preferred_element_type=F32)


def _split3(x):
    hi = x.astype(BF16)
    r = x - hi.astype(F32)
    mid = r.astype(BF16)
    lo = (r - mid.astype(F32)).astype(BF16)
    return hi, mid, lo


def _dot_exact_lhs(p_bf16, x):
    hi, mid, lo = _split3(x)
    return _dot(p_bf16, hi) + _dot(p_bf16, mid) + _dot(p_bf16, lo)


def _dot_exact_rhs(x, p_bf16):
    hi, mid, lo = _split3(x)
    return _dot(hi, p_bf16) + _dot(mid, p_bf16) + _dot(lo, p_bf16)


def _dot_f32(a, b):
    ah, am, _ = _split3(a)
    bh, bm, _ = _split3(b)
    return _dot(ah, bh) + _dot(ah, bm) + _dot(am, bh)


def _sigmoid(x):
    return jax.nn.sigmoid(x)


def _silu(x):
    return x * _sigmoid(x)


def _softplus(x):
    return jnp.maximum(x, 0.0) + jnp.log1p(jnp.exp(-jnp.abs(x)))


def _log_sigmoid(x):
    return jnp.minimum(x, 0.0) - jnp.log(1.0 + jnp.exp(-jnp.abs(x)))


def _rms(x, w):
    ms = jnp.mean(x * x, axis=-1, keepdims=True)
    return x * lax.rsqrt(ms + EPS) * w


def _pick_tm(t, cap=TM_CAP):
    best = BF16_ROWS
    for tm in range(BF16_ROWS, min(t, cap) + 1, BF16_ROWS):
        if t % tm == 0:
            best = tm
    return best


def _row_scale(ssq_ref, d):
    return lax.rsqrt(ssq_ref[:, 0:1] * (1.0 / d) + EPS)


def _emit_normed(xn, nw_ref, xw_ref, ssq_ref, accumulate):
    xw_ref[...] = (xn * nw_ref[...]).astype(BF16)
    part = jnp.broadcast_to(jnp.sum(xn * xn, axis=1, keepdims=True), ssq_ref.shape)
    ssq_ref[...] = ssq_ref[...] + part if accumulate else part


def _prep_kernel(xp_ref, xs_ref, nw_ref, x_ref, xw_ref, ssq_ref, *, n_prompt_blocks):
    x = jnp.where(pl.program_id(0) < n_prompt_blocks, xp_ref[...], xs_ref[...])
    x_ref[...] = x
    _emit_normed(x, nw_ref, xw_ref, ssq_ref, accumulate=False)


def _prep(xp, xs, nw):
    n_prompt, d = xp.shape
    rb = xs.shape[0]
    assert n_prompt % rb == 0
    npb = n_prompt // rb
    t = n_prompt + rb
    row = pl.BlockSpec((rb, d), lambda i: (i, 0))
    return pl.pallas_call(
        functools.partial(_prep_kernel, n_prompt_blocks=npb),
        grid=(npb + 1,),
        in_specs=[pl.BlockSpec((rb, d), lambda i: (jnp.minimum(i, npb - 1), 0)),
                  pl.BlockSpec((rb, d), lambda i: (0, 0)),
                  pl.BlockSpec((1, d), lambda i: (0, 0))],
        out_specs=[row, row, pl.BlockSpec((rb, LANES), lambda i: (i, 0))],
        out_shape=[jax.ShapeDtypeStruct((t, d), F32), jax.ShapeDtypeStruct((t, d), BF16),
                   jax.ShapeDtypeStruct((t, LANES), F32)],
        compiler_params=_cparams(1),
        name="prep",
    )(xp, xs, nw.reshape(1, d))


def _row_parts(tm):
    half = -(-(tm // 2) // BF16_ROWS) * BF16_ROWS
    return ((0, half), (half, tm - half)) if 0 < half < tm else ((0, tm),)


def _side_casts(ws, layer, n_i, nj):
    steps = n_i * nj
    ins, outs, shapes = [], [], []
    for w in ws:
        _, k, d = w.shape
        rows = k // steps
        if rows * steps != k or rows % BF16_ROWS:
            return None
        ins.append(pl.BlockSpec((None, rows, d), lambda i, j: (layer, i * nj + j, 0)))
        outs.append(pl.BlockSpec((rows, d), lambda i, j: (i * nj + j, 0)))
        shapes.append(jax.ShapeDtypeStruct((k, d), BF16))
    return ins, outs, shapes


def _do_side_casts(refs):
    n = len(refs) // 2
    for src, dst in zip(refs[:n], refs[n:]):
        dst[...] = src[...].astype(BF16)


def _ffn_up_kernel(xw_ref, ssq_ref, wg_ref, wu_ref, *rest, parts, n_cast):
    o_ref = rest[n_cast]
    _do_side_casts(rest[:n_cast] + rest[n_cast + 1:])
    wg = wg_ref[...].astype(BF16)
    wu = wu_ref[...].astype(BF16)
    d = xw_ref.shape[1]
    for start, rows in parts:
        sl = pl.ds(start, rows)
        r = lax.rsqrt(ssq_ref[sl, 0:1] * (1.0 / d) + EPS)
        h = xw_ref[sl, :]
        g = r * _dot(h, wg)
        u = r * _dot(h, wu)
        o_ref[sl, :] = (_silu(g) * u).astype(BF16)


def _ffn_up(xw, ssq, w_gu, layer, cast_ws=(), tn=PROJ_TN):
    t, d = xw.shape
    dff = w_gu.shape[2] // 2
    tm = _pick_tm(t, WIDE_TM_CAP)
    nj = dff // tn
    side = _side_casts(cast_ws, layer, t // tm, nj) if cast_ws else None
    c_in, c_out, c_shape = side if side else ([], [], [])
    outs = pl.pallas_call(
        functools.partial(_ffn_up_kernel, parts=_row_parts(tm), n_cast=len(c_in)),
        grid=(t // tm, nj),
        in_specs=[pl.BlockSpec((tm, d), lambda i, j: (i, 0)),
                  pl.BlockSpec((tm, LANES), lambda i, j: (i, 0)),
                  pl.BlockSpec((None, d, tn), lambda i, j: (layer, 0, j)),
                  pl.BlockSpec((None, d, tn), lambda i, j: (layer, 0, j + nj))] + c_in,
        out_specs=[pl.BlockSpec((tm, tn), lambda i, j: (i, j))] + c_out,
        out_shape=[jax.ShapeDtypeStruct((t, dff), BF16)] + c_shape,
        compiler_params=_cparams(2),
        name="ffn_up",
    )(xw, ssq, w_gu, w_gu, *(cast_ws if side else ()))
    return outs[0], (tuple(outs[1:]) if side else None)


def _resid_matmul_kernel(a_ref, w_ref, x_ref, *rest, scale, emit, parts):
    if emit:
        nw_ref, o_ref, xw_ref, ssq_ref = rest

        @pl.when(pl.program_id(1) == 0)
        def _():
            ssq_ref[...] = jnp.zeros_like(ssq_ref)
    else:
        (o_ref,) = rest
    wb = w_ref[...].astype(BF16)
    for start, rows in parts:
        sl = pl.ds(start, rows)
        xn = x_ref[sl, :] + scale * _dot(a_ref[sl, :], wb)
        if emit:
            _emit_normed(xn, nw_ref, xw_ref.at[sl, :], ssq_ref.at[sl, :], accumulate=True)
        o_ref[sl, :] = xn


def _wspec(w, layer, tn):
    if w.ndim == 3:
        return pl.BlockSpec((None, w.shape[1], tn), lambda i, j: (layer, 0, j))
    return pl.BlockSpec((w.shape[0], tn), lambda i, j: (0, j))


def _resid_matmul(a, w, layer, x, scale, tn, name, next_nw=None, tm_cap=TM_CAP):
    t, k = a.shape
    d = w.shape[-1]
    tm = _pick_tm(t, tm_cap)
    emit = next_nw is not None
    tile = pl.BlockSpec((tm, tn), lambda i, j: (i, j))
    in_specs = [pl.BlockSpec((tm, k), lambda i, j: (i, 0)), _wspec(w, layer, tn), tile]
    args = [a, w, x]
    out_specs = [tile]
    out_shape = [jax.ShapeDtypeStruct((t, d), F32)]
    if emit:
        in_specs.append(pl.BlockSpec((1, tn), lambda i, j: (0, j)))
        args.append(next_nw.reshape(1, d))
        out_specs += [tile, pl.BlockSpec((tm, LANES), lambda i, j: (i, 0))]
        out_shape += [jax.ShapeDtypeStruct((t, d), BF16), jax.ShapeDtypeStruct((t, LANES), F32)]
    outs = pl.pallas_call(
        functools.partial(_resid_matmul_kernel, scale=scale, emit=emit,
                          parts=_row_parts(tm) if k <= d else ((0, tm),)),
        grid=(t // tm, d // tn),
        in_specs=in_specs,
        out_specs=out_specs,
        out_shape=out_shape,
        compiler_params=_cparams(2),
        name=name,
    )(*args)
    return outs if emit else outs[0]


def _pack_w_in_kernel(w_ref, o_ref, *, copies, zero):
    for src, dst, width in copies:
        if width % LANES == 0:
            o_ref[:, dst:dst + width] = w_ref[src:src + width, :].T.astype(BF16)
        else:
            base = src // LANES * LANES
            lo = src - base
            assert dst % LANES == lo and lo + width <= LANES
            t = w_ref[base:base + LANES, :].T
            o_ref[:, dst:dst + width] = t[:, lo:lo + width].astype(BF16)
    start, stop = zero
    if stop > start:
        o_ref[:, start:stop] = jnp.zeros((o_ref.shape[0], stop - start), BF16)


def _pack_w_in(w_t, layer, copies, zero, n_cols, tk=LANES):
    _, n_src, d = w_t.shape
    return pl.pallas_call(
        functools.partial(_pack_w_in_kernel, copies=copies, zero=zero),
        grid=(d // tk,),
        in_specs=[pl.BlockSpec((None, n_src, tk), lambda i: (layer, 0, i))],
        out_specs=pl.BlockSpec((tk, n_cols), lambda i: (i, 0)),
        out_shape=jax.ShapeDtypeStruct((d, n_cols), BF16),
        compiler_params=_cparams(1),
        name="pack_w_in",
    )(w_t)


def _in_proj_kernel(xw_ref, ssq_ref, w_ref, *rest, n_cast):
    o_ref = rest[n_cast]
    _do_side_casts(rest[:n_cast] + rest[n_cast + 1:])
    o_ref[...] = _row_scale(ssq_ref, xw_ref.shape[1]) * _dot(xw_ref[...], w_ref[...])


def _in_proj(xw, ssq, w, layer, cast_ws=(), tn=MAIN_TN):
    t, d = xw.shape
    n = w.shape[1]
    tm = _pick_tm(t, WIDE_TM_CAP)
    side = _side_casts(cast_ws, layer, t // tm, n // tn) if cast_ws else None
    c_in, c_out, c_shape = side if side else ([], [], [])
    outs = pl.pallas_call(
        functools.partial(_in_proj_kernel, n_cast=len(c_in)),
        grid=(t // tm, n // tn),
        in_specs=[pl.BlockSpec((tm, d), lambda i, j: (i, 0)),
                  pl.BlockSpec((tm, LANES), lambda i, j: (i, 0)),
                  pl.BlockSpec((d, tn), lambda i, j: (0, j))] + c_in,
        out_specs=[pl.BlockSpec((tm, tn), lambda i, j: (i, j))] + c_out,
        out_shape=[jax.ShapeDtypeStruct((t, n), F32)] + c_shape,
        compiler_params=_cparams(2),
        name="in_proj",
    )(xw, ssq, w, *(cast_ws if side else ()))
    return outs[0], (tuple(outs[1:]) if side else None)


def _merge_kernel(ym_ref, yh_ref, yg_ref, wm_ref, wh_ref, wg_ref, g0_ref, g1_ref, g2_ref, o_ref, *, parts):
    wm = wm_ref[...].astype(BF16)
    wh = wh_ref[...].astype(BF16)
    wg = wg_ref[...].astype(BF16)
    for start, rows in parts:
        sl = pl.ds(start, rows)
        acc = _sigmoid(g0_ref[sl, :]) * _dot(ym_ref[sl, :], wm)
        acc = acc + _sigmoid(g1_ref[sl, :]) * _dot(yh_ref[sl, :], wh)
        acc = acc + _sigmoid(g2_ref[sl, :]) * _dot(yg_ref[sl, :], wg)
        o_ref[sl, :] = acc.astype(BF16)


def _merge(ym, yh, yg, wm, wh, wg, layer, proj, gate_off, tn=PROJ_TN):
    t, k = ym.shape
    d = wm.shape[-1]
    tm = _pick_tm(t)
    assert gate_off % tn == 0
    gb = gate_off // tn
    nb = d // tn
    yspec = pl.BlockSpec((tm, k), lambda i, j: (i, 0))
    wspec = _wspec(wm, layer, tn)

    def gspec(b):
        return pl.BlockSpec((tm, tn), lambda i, j: (i, gb + b * nb + j))

    return pl.pallas_call(
        functools.partial(_merge_kernel, parts=_row_parts(tm)),
        grid=(t // tm, nb),
        in_specs=[yspec, yspec, yspec, wspec, wspec, wspec, gspec(0), gspec(1), gspec(2)],
        out_specs=pl.BlockSpec((tm, tn), lambda i, j: (i, j)),
        out_shape=jax.ShapeDtypeStruct((t, d), BF16),
        compiler_params=_cparams(2),
        name="merge",
    )(ym, yh, yg, wm, wh, wg, proj, proj, proj)


def _final_norm_kernel(x_ref, nw_ref, o_ref):
    o_ref[...] = _rms(x_ref[...], nw_ref[...])


def _final_norm(x, nw, row0, rows, name):
    d = x.shape[1]
    tm = _pick_tm(rows)
    assert row0 % tm == 0
    return pl.pallas_call(
        _final_norm_kernel,
        grid=(rows // tm,),
        in_specs=[pl.BlockSpec((tm, d), lambda i: (row0 // tm + i, 0)),
                  pl.BlockSpec((1, d), lambda i: (0, 0))],
        out_specs=pl.BlockSpec((tm, d), lambda i: (i, 0)),
        out_shape=jax.ShapeDtypeStruct((rows, d), F32),
        compiler_params=_cparams(1),
        name=name,
    )(x, nw.reshape(1, d))


def _scan_constants(c):
    N_LEVELS = int(np.log2(c))
    assert 1 << N_LEVELS == c and N_LEVELS >= 3
    pm = np.zeros((2, c, c), np.float32)
    sg = np.zeros((N_LEVELS - 1, c, LANES), np.float32)
    mk = np.zeros((N_LEVELS + 1, c, c), np.float32)
    r = np.arange(c)
    for lvl in range(N_LEVELS):
        h = c >> (lvl + 1)
        for t in range(c):
            blk, pos = divmod(t, 2 * h)
            ridx = blk * 2 * h + h - 1
            upper = pos >= h
            if upper:
                mk[lvl, t] = ((r // (2 * h)) == blk) & ((r % (2 * h)) < h)
            if h >= 4:
                sg[lvl, t] = 1.0 if upper else -1.0
            elif h == 2:
                pm[0, t] = ((r > ridx) & (r <= t)) if upper else ((r > t) & (r <= ridx))
            else:
                sg[N_LEVELS - 2, t] = 1.0 if upper else 0.0
    pm[1] = r[None, :] <= r[:, None]
    mk[N_LEVELS] = np.eye(c)
    return jnp.asarray(pm.reshape(2 * c, c), BF16), jnp.asarray(sg), jnp.asarray(mk)


def _lower_bound(logits, layer):
    m = jnp.max(logits, axis=0, keepdims=True)
    e = jnp.exp(logits - m)
    sm = e / jnp.sum(e, axis=0, keepdims=True)
    lb = jnp.zeros_like(m)
    for i in range(1, layer + 1):
        lb = lb + sm[i:i + 1, :]
    return lb


def _gla_inputs(refs, hgrn, layer):
    if hgrn:
        q_ref, f_ref, i_ref, lbl_ref = refs
        lb = _lower_bound(lbl_ref[...], layer)
        q = _silu(q_ref[...]) * (LANES ** -0.5)
        zf = f_ref[...]
        a = (1.0 - lb) * _sigmoid(zf)
        f = lb + a
        gl = jnp.log(jnp.maximum(f, F32_TINY))
        k = (1.0 - lb) - a
        v = i_ref[...]
    else:
        q_ref, k_ref, v_ref, a_ref, wd_ref, bd_ref = refs
        q = q_ref[...] * (LANES ** -0.5)
        k = k_ref[...]
        v = v_ref[...]
        gl = _log_sigmoid(_dot_f32(a_ref[...], wd_ref[...]) + bd_ref[...]) / G_NORMALIZER
    return q, k, v, gl


def _level_factor(lvl, gl, cum, z_h2, sg_ref):
    c = gl.shape[0]
    n_levels = sg_ref.shape[0] + 1
    h = c >> (lvl + 1)
    if h >= 4:
        c3 = cum.reshape(c // (2 * h), 2 * h, LANES)
        d = (c3 - c3[:, h - 1:h, :]).reshape(c, LANES)
        return jnp.exp(sg_ref[lvl] * d)
    if h == 2:
        return jnp.exp(z_h2)
    return jnp.exp(sg_ref[n_levels - 2] * gl)


def _gla_prompt_kernel(*refs, hgrn, layer, hp, dv):
    n_in = 4 if hgrn else 6
    in_refs = refs[:n_in]
    g_ref, nw_ref, pm_ref, sg_ref, mk_ref, _, y_ref, so_ref, s_scr = refs[n_in:]
    c = pl.program_id(2)
    sub = mk_ref.shape[1]
    n_levels = mk_ref.shape[0] - 1

    @pl.when(c == 0)
    def _():
        s_scr[...] = jnp.zeros_like(s_scr)

    q_all, k_all, v_all, gl_all = _gla_inputs(in_refs, hgrn, layer)
    finals = []
    for u in range(hp):
        s = s_scr[u]
        for r0 in range(0, y_ref.shape[0], sub):
            rows = pl.ds(r0, sub)
            q, k, gl = (a[r0:r0 + sub, u * LANES:(u + 1) * LANES] for a in (q_all, k_all, gl_all))
            v = v_all[r0:r0 + sub, u * dv:(u + 1) * dv]
            zz = _dot_exact_lhs(pm_ref[...], gl)
            z_h2, cum = zz[:sub], zz[sub:]
            qb, kb = q.astype(BF16), k.astype(BF16)
            att = mk_ref[n_levels] * _dot_nt(qb, kb)
            for lvl in range(n_levels):
                eb = _level_factor(lvl, gl, cum, z_h2, sg_ref).astype(BF16)
                att = att + mk_ref[lvl] * _dot_nt(qb * eb, kb * eb)
            e_cum = jnp.exp(cum)
            e_tail = jnp.exp(cum[sub - 1:sub, :] - cum)
            vb = v.astype(BF16)
            o = _dot(att.astype(BF16), vb) + _dot((q * e_cum).astype(BF16), s.astype(BF16))
            e_last = e_cum.T[:, sub - 1:sub]
            s = e_last * s + _dot_tn((k * e_tail).astype(BF16), vb)
            gate = g_ref[rows, u * dv:(u + 1) * dv]
            y_ref[rows, u * dv:(u + 1) * dv] = (_rms(o, nw_ref[...]) * _silu(gate)).astype(BF16)
        s_scr[u] = s
        finals.append(s)

    @pl.when(c == pl.num_programs(2) - 1)
    def _():
        for u in range(hp):
            so_ref[u] = finals[u]


def _gla_prompt(proj, offs, extra, nw, consts, y_init, *, hgrn, nseq, nchunks, heads, dv, layer, hp):
    pm, sg, mk = consts
    assert heads % hp == 0

    def col(off, w):
        assert off % (hp * w) == 0
        return pl.BlockSpec((GLA_CHUNK, hp * w), lambda h, b, c: (b * nchunks + c, off // (hp * w) + h))

    def full(a):
        nd = a.ndim
        return pl.BlockSpec(a.shape, lambda h, b, c: (0,) * nd)

    if hgrn:
        oq, of, oi, og = offs
        (lbl,) = extra
        in_specs = [col(oq, LANES), col(of, LANES), col(oi, dv),
                    pl.BlockSpec((lbl.shape[0], hp * LANES), lambda h, b, c: (0, h))]
        args = [proj, proj, proj, lbl]
    else:
        oq, ok, ov, og, oa = offs
        wd, bd = extra
        in_specs = [col(oq, LANES), col(ok, LANES), col(ov, dv),
                    pl.BlockSpec((GLA_CHUNK, LANES), lambda h, b, c: (b * nchunks + c, oa // LANES)),
                    pl.BlockSpec((LANES, hp * LANES), lambda h, b, c: (0, h)),
                    pl.BlockSpec((1, hp * LANES), lambda h, b, c: (0, h))]
        args = [proj, proj, proj, proj, wd, bd]
    in_specs += [col(og, dv), full(nw), full(pm), full(sg), full(mk), pl.BlockSpec(memory_space=pl.ANY)]
    args += [proj, nw, pm, sg, mk, y_init]
    y, s = pl.pallas_call(
        functools.partial(_gla_prompt_kernel, hgrn=hgrn, layer=layer, hp=hp, dv=dv),
        grid=(heads // hp, nseq, nchunks),
        in_specs=in_specs,
        out_specs=[pl.BlockSpec((GLA_CHUNK, hp * dv), lambda h, b, c: (b * nchunks + c, h)),
                   pl.BlockSpec((None, hp, LANES, dv), lambda h, b, c: (b, h, 0, 0))],
        out_shape=[jax.ShapeDtypeStruct(y_init.shape, BF16),
                   jax.ShapeDtypeStruct((nseq, heads, LANES, dv), F32)],
        input_output_aliases={len(args) - 1: 0},
        scratch_shapes=[pltpu.VMEM((hp, LANES, dv), F32)],
        compiler_params=_cparams(3),
        name="hgrn_prompt" if hgrn else "gla_prompt",
    )(*args)
    return y, s


def _columns(x):
    nb = x.shape[0]
    if nb < LANES:
        x = jnp.concatenate([x, jnp.zeros((LANES - nb, x.shape[1]), x.dtype)], axis=0)
    return x.T


def _store_state(so_ref, layer, b, s_new, first):
    if first:
        for l in range(so_ref.shape[0]):
            so_ref[l, b] = s_new if l == layer else jnp.zeros_like(s_new)
    else:
        so_ref[b] = s_new


def _gla_sample_kernel(*refs, hgrn, nb, layer, first):
    n_in = 4 if hgrn else 6
    in_refs = refs[:n_in]
    g_ref, nw_ref, s_ref = refs[n_in:n_in + 3]
    y_ref, so_ref = refs[-2:]
    q, k, v, gl = _gla_inputs(in_refs, hgrn, layer)
    e_t = _columns(jnp.exp(gl))
    k_t = _columns(k)
    qb = q.astype(BF16)
    rowid = lax.broadcasted_iota(jnp.int32, v.shape, 0)
    o = jnp.zeros(v.shape, F32)
    for b in range(nb):
        s_new = e_t[:, b:b + 1] * s_ref[b] + k_t[:, b:b + 1] * v[b:b + 1, :]
        _store_state(so_ref, layer, b, s_new, first)
        o = jnp.where(rowid == b, _dot(qb, s_new.astype(BF16)), o)
    y_ref[...] = (_rms(o, nw_ref[...]) * _silu(g_ref[...])).astype(BF16)


def _state_specs(depth, layer, nb, dv, first):
    ispec = pl.BlockSpec((None, nb, None, LANES, dv), lambda h, i: (layer, i, h, 0, 0))
    if first:
        ospec = pl.BlockSpec((depth, nb, None, LANES, dv), lambda h, i: (0, i, h, 0, 0))
    else:
        ospec = ispec
    return ispec, ospec


def _sample_batch_block(ns, dv):
    nb = min(ns, STATE_BLOCK_BYTES // (LANES * dv * 4))
    assert ns % nb == 0
    return nb


def _gla_sample(proj, state, prev, y_all, row0, offs, extra, nw, *, hgrn, heads, dv, layer):
    depth, ns = state.shape[:2]
    nb = _sample_batch_block(ns, dv)
    rb = row0 // nb
    first = prev is None

    def col(off, w):
        return pl.BlockSpec((nb, w), lambda h, i: (rb + i, off // w + h))

    if hgrn:
        oq, of, oi, og = offs
        (lbl,) = extra
        in_specs = [col(oq, LANES), col(of, LANES), col(oi, dv),
                    pl.BlockSpec((lbl.shape[0], LANES), lambda h, i: (0, h))]
        args = [proj, proj, proj, lbl]
    else:
        oq, ok, ov, og, oa = offs
        wd, bd = extra
        in_specs = [col(oq, LANES), col(ok, LANES), col(ov, dv),
                    pl.BlockSpec((nb, LANES), lambda h, i: (rb + i, oa // LANES)),
                    pl.BlockSpec((LANES, LANES), lambda h, i: (0, h)),
                    pl.BlockSpec((1, LANES), lambda h, i: (0, h))]
        args = [proj, proj, proj, proj, wd, bd]
    ispec, ospec = _state_specs(depth, layer, nb, dv, first)
    in_specs += [col(og, dv), pl.BlockSpec(nw.shape, lambda h, i: (0, 0)), ispec,
                 pl.BlockSpec(memory_space=pl.ANY)]
    args += [proj, nw, state, y_all]
    aliases = {len(args) - 1: 0}
    if not first:
        in_specs.append(pl.BlockSpec(memory_space=pl.ANY))
        args.append(prev)
        aliases[len(args) - 1] = 1
    y, s = pl.pallas_call(
        functools.partial(_gla_sample_kernel, hgrn=hgrn, nb=nb, layer=layer, first=first),
        grid=(heads, ns // nb),
        in_specs=in_specs,
        out_specs=[pl.BlockSpec((nb, dv), lambda h, i: (rb + i, h)), ospec],
        out_shape=[jax.ShapeDtypeStruct(y_all.shape, BF16),
                   jax.ShapeDtypeStruct(state.shape, F32)],
        input_output_aliases=aliases,
        compiler_params=_cparams(2),
        name="hgrn_sample" if hgrn else "gla_sample",
    )(*args)
    return y, s


def _mamba_post(y, xs, z, dskip, nw):
    y = (y + dskip * xs) * _silu(z)
    gs = y.shape[1] // M_GROUPS
    outs = [_rms(y[:, g * gs:(g + 1) * gs], nw[:, g * gs:(g + 1) * gs]) for g in range(M_GROUPS)]
    return jnp.concatenate(outs, axis=1)


def _mamba_prompt_kernel(z_ref, x_ref, bc_ref, sm_ref, cwx_ref, cwb_ref, cbx_ref, cbb_ref,
                         dtb_ref, alog_ref, dsk_ref, nw_ref, tril_ref, mask_ref, _,
                         y_ref, cox_ref, cob_ref, so_ref, ex_scr, eb_scr, s_scr):
    c = pl.program_id(1)
    nc = pl.num_programs(1)
    tail = SUBLANES

    @pl.when(c == 0)
    def _():
        ex_scr[0:tail, :] = jnp.zeros((tail, ex_scr.shape[1]), F32)
        eb_scr[0:tail, :] = jnp.zeros((tail, eb_scr.shape[1]), F32)
        s_scr[...] = jnp.zeros_like(s_scr)

    ex_scr[tail:tail + CHUNK, :] = x_ref[...]
    eb_scr[tail:tail + CHUNK, :] = bc_ref[...]

    def conv(scr, cw_ref, cb_ref):
        xe = scr[...]
        acc = cw_ref[0:1, :] * xe
        for w in range(1, M_CONV):
            acc = pltpu.roll(acc, 1, axis=0) + cw_ref[w:w + 1, :] * xe
        return _silu(acc[tail:, :] + cb_ref[...])

    xs = conv(ex_scr, cwx_ref, cbx_ref)
    bcm = conv(eb_scr, cwb_ref, cbb_ref)

    ex_scr[0:tail, :] = ex_scr[CHUNK:CHUNK + tail, :]
    eb_scr[0:tail, :] = eb_scr[CHUNK:CHUNK + tail, :]

    dt = _softplus(sm_ref[...] + dtb_ref[...])
    a = -jnp.exp(alog_ref[...])
    cum = _dot_exact_lhs(tril_ref[...], dt * a)
    cum_t = cum.T
    cl = cum[CHUNK - 1:CHUNK, :]
    mask = mask_ref[...] > 0.5
    gw = M_GROUPS * M_STATE
    lo = lax.broadcasted_iota(jnp.int32, (CHUNK, LANES), 1) < M_HEADDIM
    lo_r = lax.broadcasted_iota(jnp.int32, (LANES, LANES), 0) < M_HEADDIM
    n_pairs = xs.shape[1] // LANES
    per_group = n_pairs // M_GROUPS
    ys = []
    for j in range(n_pairs):
        g = j // per_group
        h0, h1 = 2 * j, 2 * j + 1
        if j % per_group == 0:
            bg = bcm[:, g * M_STATE:(g + 1) * M_STATE].astype(BF16)
            cg = bcm[:, gw + g * M_STATE:gw + (g + 1) * M_STATE].astype(BF16)
            cb = jnp.where(mask, _dot_nt(cg, bg), 0.0)

        def dec(h):
            return jnp.exp(jnp.minimum(cum[:, h:h + 1] - cum_t[h:h + 1, :], 0.0))

        x2 = xs[:, j * LANES:(j + 1) * LANES]
        dt2 = jnp.where(lo, dt[:, h0:h0 + 1], dt[:, h1:h1 + 1])
        xdt = (x2 * dt2).astype(BF16)
        y_in = jnp.where(lo, _dot((cb * dec(h0)).astype(BF16), xdt),
                         _dot((cb * dec(h1)).astype(BF16), xdt))
        ec2 = jnp.where(lo, jnp.exp(cum[:, h0:h0 + 1]), jnp.exp(cum[:, h1:h1 + 1]))
        s = s_scr[j]
        ys.append(y_in + _dot_nt(cg, s.astype(BF16)) * ec2)
        w2 = dt2 * jnp.where(lo, jnp.exp(cl[:, h0:h0 + 1] - cum[:, h0:h0 + 1]),
                             jnp.exp(cl[:, h1:h1 + 1] - cum[:, h1:h1 + 1]))
        el2 = jnp.where(lo_r, jnp.exp(cl[:, h0:h0 + 1]), jnp.exp(cl[:, h1:h1 + 1]))
        s_scr[j] = el2 * s + _dot_tn((x2 * w2).astype(BF16), bg)
    y = jnp.concatenate(ys, axis=1)
    y_ref[...] = _mamba_post(y, xs, z_ref[...], dsk_ref[...], nw_ref[...]).astype(BF16)

    @pl.when(c == nc - 1)
    def _():
        so_ref[...] = s_scr[...]
        cox_ref[...] = ex_scr[tail + CHUNK - (M_CONV - 1):tail + CHUNK, :]
        cob_ref[...] = eb_scr[tail + CHUNK - (M_CONV - 1):tail + CHUNK, :]


def _mamba_prompt(proj, offs, p, consts, y_init, *, nseq, nchunks):
    oz, ox, obc, osm = offs
    mw = p["dskip"].shape[1]
    bcw = p["cwb"].shape[1]
    n_pairs = mw // LANES

    def col(off, w):
        return pl.BlockSpec((CHUNK, w), lambda b, c: (b * nchunks + c, off // w))

    def full(a):
        nd = a.ndim
        return pl.BlockSpec(a.shape, lambda b, c: (0,) * nd)

    small = [p["cwx"], p["cwb"], p["cbx"], p["cbb"], p["dtb"], p["alog"], p["dskip"], p["nw"],
             consts[0], consts[1]]
    y, cox, cob, s = pl.pallas_call(
        _mamba_prompt_kernel,
        grid=(nseq, nchunks),
        in_specs=[col(oz, mw), col(ox, mw), col(obc, bcw), col(osm, LANES)] + [full(a) for a in small]
        + [pl.BlockSpec(memory_space=pl.ANY)],
        out_specs=[pl.BlockSpec((CHUNK, mw), lambda b, c: (b * nchunks + c, 0)),
                   pl.BlockSpec((None, M_CONV - 1, mw), lambda b, c: (b, 0, 0)),
                   pl.BlockSpec((None, M_CONV - 1, bcw), lambda b, c: (b, 0, 0)),
                   pl.BlockSpec((None, n_pairs, LANES, M_STATE), lambda b, c: (b, 0, 0, 0))],
        input_output_aliases={4 + len(small): 0},
        out_shape=[jax.ShapeDtypeStruct(y_init.shape, BF16),
                   jax.ShapeDtypeStruct((nseq, M_CONV - 1, mw), F32),
                   jax.ShapeDtypeStruct((nseq, M_CONV - 1, bcw), F32),
                   jax.ShapeDtypeStruct((nseq, n_pairs, LANES, M_STATE), F32)],
        scratch_shapes=[pltpu.VMEM((CHUNK + SUBLANES, mw), F32), pltpu.VMEM((CHUNK + SUBLANES, bcw), F32),
                        pltpu.VMEM((n_pairs, LANES, M_STATE), F32)],
        compiler_params=_cparams(2),
        name="mamba_prompt",
    )(proj, proj, proj, proj, *small, y_init)
    return y, jnp.concatenate([cox, cob], axis=-1), s


def _mamba_sample_prep_kernel(cs_ref, x_ref, bc_ref, sm_ref, cw_ref, cb_ref, dtb_ref, alog_ref, exp_ref,
                              co_ref, act_ref, dte_ref, ee_ref):
    cd = cw_ref.shape[1]
    new = jnp.concatenate([x_ref[...], bc_ref[...]], axis=1)
    acc = cb_ref[...] + cw_ref[M_CONV - 1:M_CONV, :] * new
    for w in range(M_CONV - 1):
        acc = acc + cw_ref[w:w + 1, :] * cs_ref[:, w * cd:(w + 1) * cd]
    act_ref[...] = _silu(acc)
    for w in range(1, M_CONV - 1):
        co_ref[:, (w - 1) * cd:w * cd] = cs_ref[:, w * cd:(w + 1) * cd]
    co_ref[:, (M_CONV - 2) * cd:(M_CONV - 1) * cd] = new
    dt = _softplus(sm_ref[...] + dtb_ref[...])
    a = -jnp.exp(alog_ref[...])
    dte_ref[...] = _dot_exact_rhs(dt, exp_ref[...])
    ee_ref[...] = jnp.exp(_dot_exact_rhs(dt * a, exp_ref[...]))


def _mamba_sample_state_kernel(x_ref, b_ref, c_ref, dte_ref, ee_ref, s_ref, *out_refs, nb, layer, first):
    y_ref, so_ref = out_refs[-2:]
    xdt_t = _columns(x_ref[...] * dte_ref[...])
    e_t = _columns(ee_ref[...])
    bv = b_ref[...]
    cb = c_ref[...].astype(BF16)
    rowid = lax.broadcasted_iota(jnp.int32, (nb, LANES), 0)
    y = jnp.zeros((nb, LANES), F32)
    for b in range(nb):
        s_new = e_t[:, b:b + 1] * s_ref[b] + xdt_t[:, b:b + 1] * bv[b:b + 1, :]
        _store_state(so_ref, layer, b, s_new, first)
        y = jnp.where(rowid == b, _dot_nt(cb, s_new.astype(BF16)), y)
    y_ref[...] = y


def _mamba_sample_post_kernel(y_ref, x_ref, z_ref, dsk_ref, nw_ref, _, o_ref):
    o_ref[...] = _mamba_post(y_ref[...], x_ref[...], z_ref[...], dsk_ref[...], nw_ref[...]).astype(BF16)


def _mamba_sample(proj, conv_state, ssm_state, prev, y_all, row0, offs, p, expand, layer):
    oz, ox, obc, osm = offs
    depth, ns = ssm_state.shape[:2]
    nb = _sample_batch_block(ns, M_STATE)
    mw = p["dskip"].shape[1]
    bcw = p["cwb"].shape[1]
    cd = mw + bcw
    n_pairs = mw // LANES
    per_group = n_pairs // M_GROUPS
    rb = row0 // ns
    first = prev is None
    cw = jnp.concatenate([p["cwx"], p["cwb"]], axis=1)
    cb = jnp.concatenate([p["cbx"], p["cbb"]], axis=1)

    def full1(a):
        nd = a.ndim
        return pl.BlockSpec(a.shape, lambda i: (0,) * nd)

    cs2 = conv_state.reshape(ns, (M_CONV - 1) * cd)
    small = [cw, cb, p["dtb"], p["alog"], expand]
    co, act, dte, ee = pl.pallas_call(
        _mamba_sample_prep_kernel,
        grid=(1,),
        in_specs=[full1(cs2),
                  pl.BlockSpec((ns, mw), lambda i: (rb, ox // mw)),
                  pl.BlockSpec((ns, bcw), lambda i: (rb, obc // bcw)),
                  pl.BlockSpec((ns, LANES), lambda i: (rb, osm // LANES))] + [full1(a) for a in small],
        out_specs=[pl.BlockSpec((ns, (M_CONV - 1) * cd), lambda i: (0, 0)),
                   pl.BlockSpec((ns, cd), lambda i: (0, 0)),
                   pl.BlockSpec((ns, mw), lambda i: (0, 0)),
                   pl.BlockSpec((ns, mw), lambda i: (0, 0))],
        out_shape=[jax.ShapeDtypeStruct((ns, (M_CONV - 1) * cd), F32),
                   jax.ShapeDtypeStruct((ns, cd), F32),
                   jax.ShapeDtypeStruct((ns, mw), F32),
                   jax.ShapeDtypeStruct((ns, mw), F32)],
        compiler_params=_cparams(1),
        name="mamba_sample_prep",
    )(cs2, proj, proj, proj, *small)

    bblk = mw // LANES
    cblk = bblk + M_GROUPS * M_STATE // LANES
    ispec, ospec = _state_specs(depth, layer, nb, M_STATE, first)
    in_specs = [pl.BlockSpec((nb, LANES), lambda j, i: (i, j)),
                pl.BlockSpec((nb, LANES), lambda j, i: (i, bblk + j // per_group)),
                pl.BlockSpec((nb, LANES), lambda j, i: (i, cblk + j // per_group)),
                pl.BlockSpec((nb, LANES), lambda j, i: (i, j)),
                pl.BlockSpec((nb, LANES), lambda j, i: (i, j)),
                ispec]
    args = [act, act, act, dte, ee, ssm_state]
    aliases = {}
    if not first:
        in_specs.append(pl.BlockSpec(memory_space=pl.ANY))
        args.append(prev)
        aliases = {len(args) - 1: 1}
    y, s_new = pl.pallas_call(
        functools.partial(_mamba_sample_state_kernel, nb=nb, layer=layer, first=first),
        grid=(n_pairs, ns // nb),
        in_specs=in_specs,
        out_specs=[pl.BlockSpec((nb, LANES), lambda j, i: (i, j)), ospec],
        out_shape=[jax.ShapeDtypeStruct((ns, mw), F32),
                   jax.ShapeDtypeStruct(ssm_state.shape, F32)],
        input_output_aliases=aliases,
        compiler_params=_cparams(2),
        name="mamba_sample_state",
    )(*args)

    ym = pl.pallas_call(
        _mamba_sample_post_kernel,
        grid=(1,),
        in_specs=[full1(y),
                  pl.BlockSpec((ns, mw), lambda i: (0, 0)),
                  pl.BlockSpec((ns, mw), lambda i: (rb, oz // mw)),
                  full1(p["dskip"]), full1(p["nw"]), pl.BlockSpec(memory_space=pl.ANY)],
        out_specs=pl.BlockSpec((ns, mw), lambda i: (rb, 0)),
        out_shape=jax.ShapeDtypeStruct(y_all.shape, BF16),
        input_output_aliases={5: 0},
        compiler_params=_cparams(1),
        name="mamba_sample_post",
    )(y, act, proj, p["dskip"], p["nw"], y_all)
    return ym, co.reshape(ns, M_CONV - 1, cd), s_new


def _pad_lanes(v, n=LANES):
    v = v.reshape(1, -1)
    return jnp.pad(v, ((0, 0), (0, n - v.shape[1])))


def kernel(x_prompt, x_sample, state_conv, state_ssm, state_hgrn, state_gla, ffn1_norm, ffn1_w_gate_up, ffn1_w_down, mix_norm, w_in, conv_w, conv_b, dt_bias, a_log, d_skip, mamba_norm, hgrn_lb_logits, hgrn_norm, gla_w_decay, gla_b_decay, gla_norm, w_branch_mamba, w_branch_hgrn, w_branch_gla, w_out, ffn2_norm, ffn2_w_gate_up, ffn2_w_down, final_norm):
    nseq, seq, d = x_prompt.shape
    ns = x_sample.shape[0]
    depth = w_in.shape[0]
    nchunks = seq // CHUNK
    n_prompt = nseq * seq
    mw = w_branch_mamba.shape[1]
    hw = w_branch_hgrn.shape[1]
    gw = w_branch_gla.shape[1]
    gk = gla_w_decay.shape[2]
    m_heads = dt_bias.shape[1]
    h_heads = state_hgrn.shape[2]
    g_heads = state_gla.shape[2]
    g_dv = state_gla.shape[4]
    bcw = 2 * M_GROUPS * M_STATE
    n_pairs = mw // LANES
    assert seq % CHUNK == 0 and seq % GLA_CHUNK == 0 and n_prompt % ns == 0 and ns % BF16_ROWS == 0
    assert m_heads <= G_RANK + m_heads <= LANES and gk // g_heads == LANES and hw // h_heads == LANES

    seg_w = {"z": mw, "xs": mw, "bc": bcw, "dt": m_heads, "hq": hw, "hf": hw, "hi": hw, "hg": hw,
             "gq": gk, "gk": gk, "gv": gw, "gg": gw, "ga": G_RANK, "gate": N_BRANCH * d}
    src_order = ("z", "xs", "bc", "dt", "hq", "hf", "hi", "hg", "gq", "gk", "gv", "gg", "ga", "gate")
    dst_order = ("z", "xs", "hq", "hf", "hi", "hg", "gv", "gg", "gq", "gk", "bc", "dt", "ga")
    src, off = {}, {}
    pos = 0
    for name in src_order:
        src[name] = pos
        pos += seg_w[name]
    pos = 0
    for name in dst_order:
        off[name] = pos
        pos += seg_w[name]
    n_used = pos
    off["gate"] = -(-n_used // MAIN_TN) * MAIN_TN
    n_cols = off["gate"] + seg_w["gate"]
    assert n_cols % MAIN_TN == 0
    copies = tuple((src[name], off[name], seg_w[name]) for name in dst_order + ("gate",))
    oz, ox, obc, osm = off["z"], off["xs"], off["bc"], off["dt"]
    assert off["ga"] == osm + m_heads and osm % LANES == 0

    consts = _scan_constants(GLA_SUB)
    tril_np = np.tril(np.ones((CHUNK, CHUNK), np.float32))
    mconsts = (jnp.asarray(tril_np, BF16), jnp.asarray(tril_np))
    expand_np = np.zeros((LANES, mw), np.float32)
    for h in range(m_heads):
        expand_np[h, h * M_HEADDIM:(h + 1) * M_HEADDIM] = 1.0
    expand = jnp.asarray(expand_np, BF16)

    x, xw, ssq = _prep(x_prompt.reshape(n_prompt, d), x_sample.reshape(ns, d), ffn1_norm[0])
    t_all = n_prompt + ns
    w_in_t = jnp.swapaxes(w_in, 1, 2)
    ssm5 = state_ssm.reshape(depth, ns, n_pairs, LANES, M_STATE)

    pc, ps, ph, pg, sc = [], [], [], [], []
    ss = sh = sg = None
    for l in range(depth):
        w_perm = _pack_w_in(w_in_t, l, copies, (n_used, off["gate"]), n_cols)
        mp = {
            "cwx": conv_w[l][:, :mw], "cwb": conv_w[l][:, mw:],
            "cbx": conv_b[l][:mw].reshape(1, mw), "cbb": conv_b[l][mw:].reshape(1, bcw),
            "dtb": _pad_lanes(dt_bias[l]), "alog": _pad_lanes(a_log[l]),
            "dskip": jnp.repeat(d_skip[l], M_HEADDIM).reshape(1, mw),
            "nw": mamba_norm[l].reshape(1, mw),
        }
        wd = jnp.zeros((LANES, gk), F32).at[m_heads:m_heads + G_RANK].set(gla_w_decay[l])
        bd = gla_b_decay[l].reshape(1, gk)
        hnw = hgrn_norm[l].reshape(1, LANES)
        gnw = gla_norm[l].reshape(1, g_dv)

        act, cast = _ffn_up(xw, ssq, ffn1_w_gate_up, l, (ffn1_w_down,))
        w_dn, tn_dn = (cast[0], PROJ_TN) if cast else (ffn1_w_down, F32_WEIGHT_TN)
        x, xw, ssq = _resid_matmul(act, w_dn, l, x, 0.5, tn_dn, "ffn_down", mix_norm[l])
        proj, cast = _in_proj(xw, ssq, w_perm, l, (w_branch_mamba, w_branch_hgrn, w_branch_gla, w_out))
        w_bm, w_bh, w_bg, w_o = cast if cast else (w_branch_mamba, w_branch_hgrn, w_branch_gla, w_out)

        moffs = (oz, ox, obc, osm)
        hoffs = (off["hq"], off["hf"], off["hi"], off["hg"])
        goffs = (off["gq"], off["gk"], off["gv"], off["gg"], osm)
        if l == 0:
            ym, yh, yg = (jnp.zeros((t_all, w), BF16) for w in (mw, hw, gw))
        ym, c1, s1 = _mamba_prompt(proj, moffs, mp, mconsts, ym, nseq=nseq, nchunks=nchunks)
        yh, h1 = _gla_prompt(proj, hoffs, (hgrn_lb_logits,), hnw, consts, yh,
                             hgrn=True, nseq=nseq, nchunks=seq // GLA_CHUNK, heads=h_heads, dv=LANES, layer=l,
                             hp=h_heads)
        yg, g1 = _gla_prompt(proj, goffs, (wd, bd), gnw, consts, yg,
                             hgrn=False, nseq=nseq, nchunks=seq // GLA_CHUNK, heads=g_heads, dv=g_dv, layer=l,
                             hp=g_heads)
        ym, c2, ss = _mamba_sample(proj, state_conv[l], ssm5, ss, ym, n_prompt, moffs, mp, expand, l)
        yh, sh = _gla_sample(proj, state_hgrn, sh, yh, n_prompt, hoffs, (hgrn_lb_logits,), hnw,
                             hgrn=True, heads=h_heads, dv=LANES, layer=l)
        yg, sg = _gla_sample(proj, state_gla, sg, yg, n_prompt, goffs, (wd, bd), gnw,
                             hgrn=False, heads=g_heads, dv=g_dv, layer=l)
        merged = _merge(ym, yh, yg, w_bm, w_bh, w_bg, l, proj, off["gate"])
        x, xw, ssq = _resid_matmul(merged, w_o, l, x, 1.0, PROJ_TN, "out_proj", ffn2_norm[l], WIDE_TM_CAP)

        act, cast = _ffn_up(xw, ssq, ffn2_w_gate_up, l, (ffn2_w_down,))
        w_dn, tn_dn = (cast[0], PROJ_TN) if cast else (ffn2_w_down, F32_WEIGHT_TN)
        if l + 1 < depth:
            x, xw, ssq = _resid_matmul(act, w_dn, l, x, 0.5, tn_dn, "ffn_down", ffn1_norm[l + 1])
        else:
            x = _resid_matmul(act, w_dn, l, x, 0.5, tn_dn, "ffn_down")

        pc.append(c1)
        ps.append(s1.reshape(nseq, m_heads, M_HEADDIM, M_STATE))
        ph.append(h1)
        pg.append(g1)
        sc.append(c2)

    y_prompt = _final_norm(x, final_norm, 0, n_prompt, "final_norm_prompt").reshape(nseq, seq, d)
    y_sample = _final_norm(x, final_norm, n_prompt, ns, "final_norm_sample").reshape(ns, 1, d)
    return (y_prompt, y_sample, jnp.stack(pc), jnp.stack(ps), jnp.stack(ph), jnp.stack(pg),
            jnp.stack(sc), ss.reshape(state_ssm.shape), sh, sg)
```
